```python
import jax
import jax.numpy as jnp
from jax import lax
import numpy as np

D_MODEL = 1024
BATCH = 8
SEQ = 4096
DEPTH = 4

N_MIXERS = 2

SSD_D_INNER = 2 * D_MODEL
SSD_HEAD_DIM = 64
SSD_N_HEADS = SSD_D_INNER // SSD_HEAD_DIM
SSD_N_GROUPS = 4
SSD_HEADS_PER_GROUP = SSD_N_HEADS // SSD_N_GROUPS
SSD_D_STATE = 128
SSD_D_CONV = 4
SSD_CHUNK = 128
SSD_CONV_DIM = SSD_D_INNER + 2 * SSD_N_GROUPS * SSD_D_STATE
SSD_D_IN_PROJ = SSD_D_INNER + SSD_CONV_DIM + SSD_N_HEADS

ATT_HEAD_DIM = 64
ATT_HEADS_PER_GROUP = 8
ATT_PATTERNS = ((128, 1), (512, 4), (2048, 16))
ATT_N_GROUPS = len(ATT_PATTERNS)
ATT_QKV_DIM = ATT_N_GROUPS * 3 * ATT_HEADS_PER_GROUP * ATT_HEAD_DIM
ATT_OUT_DIM = ATT_HEADS_PER_GROUP * ATT_HEAD_DIM
ROPE_THETA = 500000.0
ROPE_DIM = ATT_HEAD_DIM // 4

MOE_N_GROUPS = 4
MOE_EXPERTS_PER_GROUP = 8
MOE_N_EXPERTS = MOE_N_GROUPS * MOE_EXPERTS_PER_GROUP
MOE_TOP_K = 2
MOE_HIDDEN = 512
MOE_BLOCK = 128

DEEPNORM_ALPHA = (2 * DEPTH) ** 0.25
DEEPNORM_BETA = (8 * DEPTH) ** -0.25
LN_EPS = 1e-5
RMS_EPS = 1e-5
NEG_INF = -1e30

kernel_name = 'hybrid_ssd_dilated_attn_hmoe_deepnorm'


def _layer_norm(x, g, b):
    xf = x.astype(jnp.float32)
    mu = jnp.mean(xf, -1, keepdims=True)
    var = jnp.mean(jnp.square(xf - mu), -1, keepdims=True)
    return ((xf - mu) * lax.rsqrt(var + LN_EPS) * g + b).astype(x.dtype)


def _rope_tables(positions):
    inv_freq = ROPE_THETA ** (-jnp.arange(0, ROPE_DIM, 2, dtype=jnp.float32) / ROPE_DIM)
    ang = positions.astype(jnp.float32)[..., None] * inv_freq
    return jnp.cos(ang)[:, :, None, :], jnp.sin(ang)[:, :, None, :]


def _apply_partial_rope(t, cos, sin):
    half = ROPE_DIM // 2
    t1, t2, rest = t[..., :half], t[..., half:ROPE_DIM], t[..., ROPE_DIM:]
    cos = cos.astype(t.dtype)
    sin = sin.astype(t.dtype)
    return jnp.concatenate([t1 * cos - t2 * sin, t2 * cos + t1 * sin, rest], -1)


def _causal_depthwise_conv(u, w, b):
    c = u.shape[-1]
    y = lax.conv_general_dilated(u, w.reshape(SSD_D_CONV, 1, c), window_strides=(1,),
                                 padding=((SSD_D_CONV - 1, 0),),
                                 dimension_numbers=('NWC', 'WIO', 'NWC'),
                                 feature_group_count=c)
    return y + b


def _ssd_chunked_scan(x, dt, a, b_in, c_in):
    bsz, s = x.shape[:2]
    nc = s // SSD_CHUNK
    Q, G, E, P, N = SSD_CHUNK, SSD_N_GROUPS, SSD_HEADS_PER_GROUP, SSD_HEAD_DIM, SSD_D_STATE
    xr = (x.astype(jnp.float32) * dt[..., None]).reshape(bsz, nc, Q, G, E, P)
    da = (dt * a).reshape(bsz, nc, Q, G, E).transpose(0, 1, 3, 4, 2)
    br = b_in.astype(jnp.float32).reshape(bsz, nc, Q, G, N)
    cr = c_in.astype(jnp.float32).reshape(bsz, nc, Q, G, N)
    da_cs = jnp.cumsum(da, -1)
    causal = jnp.tril(jnp.ones((Q, Q), bool))
    decay_in = jnp.exp(jnp.where(causal, da_cs[..., :, None] - da_cs[..., None, :], -jnp.inf))
    cb = jnp.einsum('bclgn,bcsgn->bcgls', cr, br)
    y_diag = jnp.einsum('bcgls,bcgels,bcsgep->bclgep', cb, decay_in, xr)
    decay_to_end = jnp.exp(da_cs[..., -1:] - da_cs)
    states = jnp.einsum('bclgn,bcgel,bclgep->bcgepn', br, decay_to_end, xr)
    chunk_decay = jnp.exp(da_cs[..., -1])

    def step(h, inp):
        s_c, dec_c = inp
        return h * dec_c[..., None, None] + s_c, h

    h0 = jnp.zeros((bsz, G, E, P, N), jnp.float32)
    _, prev = lax.scan(step, h0, (jnp.moveaxis(states, 1, 0), jnp.moveaxis(chunk_decay, 1, 0)))
    prev = jnp.moveaxis(prev, 0, 1)
    y_off = jnp.einsum('bclgn,bcgepn,bcgel->bclgep', cr, prev, jnp.exp(da_cs))
    return (y_diag + y_off).reshape(bsz, s, G * E, P)


def _ssd_mixer(x, w_in, conv_w, conv_b, dt_bias, a_log, d_skip, norm_w, w_out):
    bsz, s, _ = x.shape
    zxbcdt = x @ w_in
    z = zxbcdt[..., :SSD_D_INNER]
    xbc = zxbcdt[..., SSD_D_INNER:SSD_D_INNER + SSD_CONV_DIM]
    dt_raw = zxbcdt[..., SSD_D_INNER + SSD_CONV_DIM:]
    xbc = jax.nn.silu(_causal_depthwise_conv(xbc, conv_w, conv_b))
    gn = SSD_N_GROUPS * SSD_D_STATE
    xs = xbc[..., :SSD_D_INNER].reshape(bsz, s, SSD_N_HEADS, SSD_HEAD_DIM)
    b_in = xbc[..., SSD_D_INNER:SSD_D_INNER + gn].reshape(bsz, s, SSD_N_GROUPS, SSD_D_STATE)
    c_in = xbc[..., SSD_D_INNER + gn:].reshape(bsz, s, SSD_N_GROUPS, SSD_D_STATE)
    dt = jax.nn.softplus((dt_raw + dt_bias).astype(jnp.float32))
    a = -jnp.exp(a_log.astype(jnp.float32))
    y = _ssd_chunked_scan(xs, dt, a, b_in, c_in) + xs.astype(jnp.float32) * d_skip.astype(jnp.float32)[:, None]
    y = y.reshape(bsz, s, SSD_D_INNER) * jax.nn.silu(z.astype(jnp.float32))
    yg = y.reshape(bsz, s, SSD_N_GROUPS, -1)
    yg = yg * lax.rsqrt(jnp.mean(jnp.square(yg), -1, keepdims=True) + RMS_EPS)
    y = (yg.reshape(bsz, s, SSD_D_INNER) * norm_w).astype(x.dtype)
    return y @ w_out


def _dilated_window_attention(q, k, v, window, dilation):
    bsz, s, h, d = q.shape
    w = window // dilation
    n = s // dilation
    nb = -(-n // w)
    n_pad = nb * w - n

    def to_residues(t):
        return t.reshape(bsz, n, dilation, h, d).transpose(0, 2, 1, 3, 4)

    def key_blocks(t):
        t = jnp.pad(to_residues(t), ((0, 0), (0, 0), (w, n_pad), (0, 0), (0, 0)))
        t = t.reshape(bsz, dilation, nb + 1, w, h, d)
        return jnp.concatenate([t[:, :, :-1], t[:, :, 1:]], axis=3)

    q_r = jnp.pad(to_residues(q), ((0, 0), (0, 0), (0, n_pad), (0, 0), (0, 0)))
    q_r = q_r.reshape(bsz, dilation, nb, w, h, d).astype(jnp.float32)
    k_b = key_blocks(k).astype(jnp.float32)
    v_b = key_blocks(v).astype(jnp.float32)
    scores = jnp.einsum('brnqhd,brnkhd->brnhqk', q_r, k_b) * (ATT_HEAD_DIM ** -0.5)
    qi = jnp.arange(w)[:, None]
    ki = jnp.arange(2 * w)[None, :]
    blk = jnp.arange(nb)[:, None, None]
    valid = (ki >= qi) & (ki <= qi + w) & (blk * w + ki - w >= 0)
    scores = jnp.where(valid[None, None, :, None], scores, NEG_INF)
    m = jnp.max(scores, -1)
    p = jnp.exp(scores - m[..., None])
    l = jnp.sum(p, -1)
    m_t = m.transpose(0, 1, 2, 4, 3)
    l_t = l.transpose(0, 1, 2, 4, 3)
    o = jnp.einsum('brnhqk,brnkhd->brnqhd', p, v_b) / l_t[..., None]

    def from_residues(t):
        t = t.reshape(bsz, dilation, nb * w, *t.shape[4:])[:, :, :n]
        t = jnp.moveaxis(t, 1, 2)
        return t.reshape(bsz, s, *t.shape[3:])

    return from_residues(o), from_residues(m_t), from_residues(l_t)


def _dilated_attention_mixer(x, cos, sin, w_qkv, w_o):
    bsz, s, _ = x.shape
    qkv = (x @ w_qkv).reshape(bsz, s, ATT_N_GROUPS, 3, ATT_HEADS_PER_GROUP, ATT_HEAD_DIM)
    outs, maxes, dens = [], [], []
    for g, (window, dilation) in enumerate(ATT_PATTERNS):
        q = _apply_partial_rope(qkv[:, :, g, 0], cos, sin)
        k = _apply_partial_rope(qkv[:, :, g, 1], cos, sin)
        o, m, l = _dilated_window_attention(q, k, qkv[:, :, g, 2], window, dilation)
        outs.append(o)
        maxes.append(m)
        dens.append(l)
    o = jnp.stack(outs)
    m = jnp.stack(maxes)
    l = jnp.stack(dens)
    wgt = l * jnp.exp(m - jnp.max(m, 0, keepdims=True))
    merged = jnp.sum(wgt[..., None] * o, 0) / jnp.sum(wgt, 0)[..., None]
    return merged.reshape(bsz, s, ATT_OUT_DIM).astype(x.dtype) @ w_o


def _hierarchical_moe(x, w_rg, b_rg, w_re, b_re, w_gate, w_up, w_down):
    bsz, s, d = x.shape
    t = bsz * s
    xf = x.reshape(t, d)
    g_logits = (xf @ w_rg + b_rg).astype(jnp.float32)
    g_w, g_idx = lax.top_k(jax.nn.softmax(g_logits, axis=-1), 1)
    e_logits = (xf @ w_re + b_re).astype(jnp.float32).reshape(t, MOE_N_GROUPS, MOE_EXPERTS_PER_GROUP)
    e_logits = e_logits[jnp.arange(t), g_idx[:, 0]]
    e_w, e_idx = lax.top_k(jax.nn.softmax(e_logits, axis=-1), MOE_TOP_K)
    e_w = e_w / jnp.sum(e_w, -1, keepdims=True)
    gates = (g_w * e_w).reshape(-1)
    expert_id = (g_idx * MOE_EXPERTS_PER_GROUP + e_idx).reshape(-1).astype(jnp.int32)
    token_id = jnp.repeat(jnp.arange(t, dtype=jnp.int32), MOE_TOP_K)
    n_assign = t * MOE_TOP_K
    n_blocks = -(-n_assign // MOE_BLOCK) + MOE_N_EXPERTS
    n_rows = n_blocks * MOE_BLOCK
    order = jnp.argsort(expert_id)
    sorted_e = expert_id[order]
    counts = jnp.bincount(expert_id, length=MOE_N_EXPERTS)
    padded = (counts + MOE_BLOCK - 1) // MOE_BLOCK * MOE_BLOCK
    pad_end = jnp.cumsum(padded)
    pad_start = pad_end - padded
    cnt_start = jnp.cumsum(counts) - counts
    dest = pad_start[sorted_e] + jnp.arange(n_assign, dtype=jnp.int32) - cnt_start[sorted_e]
    row_tok = jnp.full((n_rows,), t, jnp.int32).at[dest].set(token_id[order])
    row_gate = jnp.zeros((n_rows,), x.dtype).at[dest].set(gates[order].astype(x.dtype))
    block_e = jnp.minimum(jnp.searchsorted(pad_end, jnp.arange(n_blocks) * MOE_BLOCK, side='right'),
                          MOE_N_EXPERTS - 1)
    x_rows = jnp.concatenate([xf, jnp.zeros((1, d), x.dtype)], 0)[row_tok]
    x_rows = x_rows.reshape(n_blocks, MOE_BLOCK, d)

    def expert_block(args):
        xb, e = args
        h = jax.nn.silu(xb @ w_gate[e]) * (xb @ w_up[e])
        return h @ w_down[e]

    y_rows = lax.map(expert_block, (x_rows, block_e)).reshape(n_rows, d)
    y = jax.ops.segment_sum(y_rows * row_gate[:, None], row_tok, num_segments=t + 1)[:t]
    return y.reshape(bsz, s, d)


def setup_inputs(seed: int = 0) -> dict:
    key = jax.random.key(seed)
    ks = jax.random.split(key, 24)
    f32 = jnp.float32
    n_ssd = len(range(0, DEPTH, N_MIXERS))
    n_att = len(range(1, DEPTH, N_MIXERS))

    def nrm(k, shape, scale):
        return jax.random.normal(k, shape, f32) * scale

    x = jax.random.normal(ks[0], (BATCH, SEQ, D_MODEL), f32)
    positions = jnp.broadcast_to(jnp.arange(SEQ, dtype=jnp.int32), (BATCH, SEQ))
    dt0 = jnp.exp(jax.random.uniform(ks[4], (n_ssd, SSD_N_HEADS), f32) * (np.log(0.1) - np.log(0.001)) + np.log(0.001))
    return {
        'x': x,
        'positions': positions,
        'ssd_w_in': nrm(ks[1], (n_ssd, D_MODEL, SSD_D_IN_PROJ), D_MODEL ** -0.5),
        'ssd_conv_w': nrm(ks[2], (n_ssd, SSD_D_CONV, SSD_CONV_DIM), SSD_D_CONV ** -0.5),
        'ssd_conv_b': nrm(ks[3], (n_ssd, SSD_CONV_DIM), 0.02),
        'ssd_dt_bias': dt0 + jnp.log(-jnp.expm1(-dt0)),
        'ssd_a_log': jnp.log(jax.random.uniform(ks[5], (n_ssd, SSD_N_HEADS), f32, 1.0, 16.0)),
        'ssd_d': 1.0 + nrm(ks[6], (n_ssd, SSD_N_HEADS), 0.01),
        'ssd_norm_w': 1.0 + nrm(ks[7], (n_ssd, SSD_D_INNER), 0.02),
        'ssd_w_out': nrm(ks[8], (n_ssd, SSD_D_INNER, D_MODEL), SSD_D_INNER ** -0.5 * DEEPNORM_BETA),
        'attn_w_qkv': nrm(ks[9], (n_att, D_MODEL, ATT_QKV_DIM), D_MODEL ** -0.5),
        'attn_w_o': nrm(ks[10], (n_att, ATT_OUT_DIM, D_MODEL), ATT_OUT_DIM ** -0.5 * DEEPNORM_BETA),
        'ln_g': 1.0 + nrm(ks[11], (DEPTH, 2, D_MODEL), 0.02),
        'ln_b': nrm(ks[12], (DEPTH, 2, D_MODEL), 0.02),
        'moe_w_router_group': nrm(ks[13], (DEPTH, D_MODEL, MOE_N_GROUPS), D_MODEL ** -0.5),
        'moe_b_router_group': nrm(ks[14], (DEPTH, MOE_N_GROUPS), 0.01),
        'moe_w_router_expert': nrm(ks[15], (DEPTH, D_MODEL, MOE_N_EXPERTS), D_MODEL ** -0.5),
        'moe_b_router_expert': nrm(ks[16], (DEPTH, MOE_N_EXPERTS), 0.01),
        'moe_w_gate': nrm(ks[17], (DEPTH, MOE_N_EXPERTS, D_MODEL, MOE_HIDDEN), D_MODEL ** -0.5),
        'moe_w_up': nrm(ks[18], (DEPTH, MOE_N_EXPERTS, D_MODEL, MOE_HIDDEN), D_MODEL ** -0.5),
        'moe_w_down': nrm(ks[19], (DEPTH, MOE_N_EXPERTS, MOE_HIDDEN, D_MODEL), MOE_HIDDEN ** -0.5 * DEEPNORM_BETA),
    }


def reference(x, positions, ssd_w_in, ssd_conv_w, ssd_conv_b, ssd_dt_bias, ssd_a_log, ssd_d,
              ssd_norm_w, ssd_w_out, attn_w_qkv, attn_w_o, ln_g, ln_b,
              moe_w_router_group, moe_b_router_group, moe_w_router_expert, moe_b_router_expert,
              moe_w_gate, moe_w_up, moe_w_down):
    cos, sin = _rope_tables(positions)
    for i in range(DEPTH):
        j = i // N_MIXERS
        if i % N_MIXERS == 0:
            mix = _ssd_mixer(x, ssd_w_in[j], ssd_conv_w[j], ssd_conv_b[j], ssd_dt_bias[j],
                             ssd_a_log[j], ssd_d[j], ssd_norm_w[j], ssd_w_out[j])
        else:
            mix = _dilated_attention_mixer(x, cos, sin, attn_w_qkv[j], attn_w_o[j])
        x = _layer_norm(DEEPNORM_ALPHA * x + mix, ln_g[i, 0], ln_b[i, 0])
        ffn = _hierarchical_moe(x, moe_w_router_group[i], moe_b_router_group[i],
                                moe_w_router_expert[i], moe_b_router_expert[i],
                                moe_w_gate[i], moe_w_up[i], moe_w_down[i])
        x = _layer_norm(DEEPNORM_ALPHA * x + ffn, ln_g[i, 1], ln_b[i, 1])
    return x
```

```python
import functools

import jax
import jax.numpy as jnp
from jax import lax
from jax.experimental import pallas as pl
from jax.experimental.pallas import tpu as pltpu

F32 = jnp.float32
BF16 = jnp.bfloat16

D_MODEL = 1024
DEPTH = 4
N_MIXERS = 2

SSD_D_INNER = 2048
SSD_HEAD_DIM = 64
SSD_N_HEADS = 32
SSD_N_GROUPS = 4
SSD_D_STATE = 128
SSD_D_CONV = 4
SSD_CHUNK = 128
SSD_GN = SSD_N_GROUPS * SSD_D_STATE
SSD_CONV_DIM = SSD_D_INNER + 2 * SSD_GN
SSD_GROUP_COLS = SSD_D_INNER // SSD_N_GROUPS

ATT_HEAD_DIM = 64
ATT_HEADS = 8
ATT_PATTERNS = ((128, 1), (512, 4), (2048, 16))
ATT_N_GROUPS = 3
ATT_OUT_DIM = ATT_HEADS * ATT_HEAD_DIM
ATT_QKV_DIM = ATT_N_GROUPS * 3 * ATT_OUT_DIM
ATT_BLOCK = 128
ROPE_THETA = 500000.0
ROPE_DIM = 16

MOE_N_GROUPS = 4
MOE_EPG = 8
MOE_N_EXPERTS = 32
MOE_TOP_K = 2
MOE_HIDDEN = 512
MOE_ROW_BLOCK = 256

DEEPNORM_ALPHA = (2 * DEPTH) ** 0.25
LN_EPS = 1e-5
RMS_EPS = 1e-5
NEG_INF = -1e30

LANES = 128
HALF = LANES // 2
VMEM_LIMIT = 48 * 1024 * 1024

ROW_TILE = 512


def _params(*sem):
    return pltpu.CompilerParams(dimension_semantics=sem, vmem_limit_bytes=VMEM_LIMIT)


def _silu(v):
    return v * (1.0 / (1.0 + jnp.exp(-v)))


def _layer_norm(r, g, b):
    mu = jnp.mean(r, -1, keepdims=True)
    d = r - mu
    var = jnp.mean(d * d, -1, keepdims=True)
    return d * lax.rsqrt(var + LN_EPS) * g + b


def _split3(v):
    hi = v.astype(BF16)
    r1 = v - hi.astype(F32)
    mid = r1.astype(BF16)
    lo = (r1 - mid.astype(F32)).astype(BF16)
    return hi, mid, lo


def _dot(a, b):
    return jnp.dot(a, b, preferred_element_type=F32)


def _dot_nt(a, b):
    return lax.dot_general(a, b, (((1,), (1,)), ((), ())), preferred_element_type=F32)


def _dot_tn(a, b):
    return lax.dot_general(a, b, (((0,), (0,)), ((), ())), preferred_element_type=F32)


def _pair_expand(mat, h0, lo_mask):
    rows = mat.shape[0]
    a = jnp.broadcast_to(mat[:, h0:h0 + 1], (rows, LANES))
    b = jnp.broadcast_to(mat[:, h0 + 1:h0 + 2], (rows, LANES))
    return jnp.where(lo_mask, a, b)


def _mm_kernel(x_ref, w_ref, o_ref):
    o_ref[...] = _dot(x_ref[...].astype(BF16), w_ref[...]).astype(o_ref.dtype)


def _matmul(x, w, tn, out_dtype, name):
    m, k = x.shape
    n = w.shape[1]
    return pl.pallas_call(
        _mm_kernel,
        grid=(m // ROW_TILE, n // tn),
        in_specs=[pl.BlockSpec((ROW_TILE, k), lambda i, j: (i, 0)),
                  pl.BlockSpec((k, tn), lambda i, j: (0, j))],
        out_specs=pl.BlockSpec((ROW_TILE, tn), lambda i, j: (i, j)),
        out_shape=jax.ShapeDtypeStruct((m, n), out_dtype),
        compiler_params=_params("parallel", "arbitrary"),
        name=name,
    )(x, w)


def _ssd_kernel(z_ref, xs_ref, bc_ref, dt_ref, cwx_ref, cbx_ref, cwbc_ref, cbbc_ref,
                dtb_ref, alog_ref, dsk_ref, nw_ref, o_ref, extx, extbc, state):
    q = SSD_CHUNK
    c = pl.program_id(1)

    @pl.when(c == 0)
    def _():
        extx[0:8, :] = jnp.zeros((8, SSD_D_INNER), F32)
        extbc[0:8, :] = jnp.zeros((8, 2 * SSD_GN), F32)
        state[...] = jnp.zeros(state.shape, F32)

    extx[8:8 + q, :] = xs_ref[...].astype(F32)
    extbc[8:8 + q, :] = bc_ref[...].astype(F32)

    def conv_silu(ext, w_ref, b_ref):
        acc = ext[8:8 + q, :] * w_ref[3:4, :] + b_ref[...]
        for k in range(SSD_D_CONV - 1):
            acc = acc + ext[5 + k:5 + k + q, :] * w_ref[k:k + 1, :]
        return _silu(acc)

    xs = conv_silu(extx, cwx_ref, cbx_ref)
    bc = conv_silu(extbc, cwbc_ref, cbbc_ref)
    extx[0:8, :] = extx[q:q + 8, :]
    extbc[0:8, :] = extbc[q:q + 8, :]

    pre = dt_ref[...] + dtb_ref[...]
    dt = jnp.maximum(pre, 0.0) + jnp.log(1.0 + jnp.exp(-jnp.abs(pre)))
    a = -jnp.exp(alog_ref[...])
    da = dt * a

    row = lax.broadcasted_iota(jnp.int32, (q, q), 0)
    col = lax.broadcasted_iota(jnp.int32, (q, q), 1)
    causal = row >= col
    lo_mask = col < HALF
    tri = jnp.where(causal, 1.0, 0.0).astype(BF16)
    hi, mid, lo = _split3(da)
    cs = _dot(tri, hi) + _dot(tri, mid) + _dot(tri, lo)
    cs_t = cs.T

    for g in range(SSD_N_GROUPS):
        bg = bc[:, g * SSD_D_STATE:(g + 1) * SSD_D_STATE].astype(BF16)
        cg = bc[:, SSD_GN + g * SSD_D_STATE:SSD_GN + (g + 1) * SSD_D_STATE].astype(BF16)
        cb = _dot_nt(cg, bg)
        prev = state[g]
        y_off = _dot(cg, prev.astype(BF16))
        ys, xdte, cds = [], [], []
        ssq = jnp.zeros((q, 1), F32)
        for p in range(4):
            h0 = g * 8 + 2 * p
            c0 = g * SSD_GROUP_COLS + p * LANES
            csx = _pair_expand(cs, h0, lo_mask)
            dtx = _pair_expand(dt, h0, lo_mask)
            xp = xs[:, c0:c0 + LANES]
            xdt = xp * dtx
            xdt16 = xdt.astype(BF16)
            last = csx[q - 1:q, :]
            halves = []
            for hh in range(2):
                h = h0 + hh
                diff = jnp.broadcast_to(cs[:, h:h + 1], (q, q)) - jnp.broadcast_to(cs_t[h:h + 1, :], (q, q))
                decay = jnp.exp(jnp.where(causal, diff, -jnp.inf))
                halves.append(_dot((cb * decay).astype(BF16), xdt16))
            y = jnp.where(lo_mask, halves[0], halves[1])
            y = y + y_off[:, p * LANES:(p + 1) * LANES] * jnp.exp(csx) + xp * dsk_ref[:, c0:c0 + LANES]
            y = y * _silu(z_ref[:, c0:c0 + LANES].astype(F32))
            ssq = ssq + jnp.sum(y * y, -1, keepdims=True)
            ys.append(y)
            xdte.append((xdt * jnp.exp(last - csx)).astype(BF16))
            cds.append(jnp.exp(last))
        s_new = _dot_tn(bg, jnp.concatenate(xdte, axis=1))
        state[g] = prev * jnp.concatenate(cds, axis=1) + s_new
        inv = lax.rsqrt(ssq * (1.0 / SSD_GROUP_COLS) + RMS_EPS)
        for p in range(4):
            c0 = g * SSD_GROUP_COLS + p * LANES
            o_ref[:, c0:c0 + LANES] = (ys[p] * inv * nw_ref[:, c0:c0 + LANES]).astype(o_ref.dtype)


def _ssd_scan(zxbc, dt_raw, conv_w, conv_b, dt_bias, a_log, d_skip, norm_w, bsz, seq):
    t = bsz * seq
    nc = seq // SSD_CHUNK
    q = SSD_CHUNK
    cwx = jnp.pad(conv_w[:, :SSD_D_INNER], ((0, 4), (0, 0)))
    cwbc = jnp.pad(conv_w[:, SSD_D_INNER:], ((0, 4), (0, 0)))
    cbx = conv_b[None, :SSD_D_INNER]
    cbbc = conv_b[None, SSD_D_INNER:]
    pad_h = LANES - SSD_N_HEADS
    dtb = jnp.pad(dt_bias, (0, pad_h))[None, :]
    alog = jnp.pad(a_log, (0, pad_h))[None, :]
    dsk = jnp.repeat(d_skip, SSD_HEAD_DIM)[None, :]
    nw = norm_w[None, :]

    def const(shape):
        return pl.BlockSpec(shape, lambda b, c: (0, 0))

    return pl.pallas_call(
        _ssd_kernel,
        grid=(bsz, nc),
        in_specs=[pl.BlockSpec((q, SSD_D_INNER), lambda b, c: (b * nc + c, 0)),
                  pl.BlockSpec((q, SSD_D_INNER), lambda b, c: (b * nc + c, 1)),
                  pl.BlockSpec((q, 2 * SSD_GN), lambda b, c: (b * nc + c, 4)),
                  pl.BlockSpec((q, LANES), lambda b, c: (b * nc + c, 0)),
                  const((8, SSD_D_INNER)), const((1, SSD_D_INNER)),
                  const((8, 2 * SSD_GN)), const((1, 2 * SSD_GN)),
                  const((1, LANES)), const((1, LANES)),
                  const((1, SSD_D_INNER)), const((1, SSD_D_INNER))],
        out_specs=pl.BlockSpec((q, SSD_D_INNER), lambda b, c: (b * nc + c, 0)),
        out_shape=jax.ShapeDtypeStruct((t, SSD_D_INNER), BF16),
        scratch_shapes=[pltpu.VMEM((q + 8, SSD_D_INNER), F32),
                        pltpu.VMEM((q + 8, 2 * SSD_GN), F32),
                        pltpu.VMEM((SSD_N_GROUPS, SSD_D_STATE, SSD_GROUP_COLS), F32)],
        compiler_params=_params("arbitrary", "arbitrary"),
        name="ssd_scan",
    )(zxbc, zxbc, zxbc, dt_raw, cwx, cbx, cwbc, cbbc, dtb, alog, dsk, nw)


def _proj_ln_kernel(y_ref, w_ref, x_ref, g_ref, b_ref, o_ref, o16_ref):
    mix = _dot(y_ref[...].astype(BF16), w_ref[...])
    out = _layer_norm(DEEPNORM_ALPHA * x_ref[...] + mix, g_ref[...], b_ref[...])
    o_ref[...] = out
    o16_ref[...] = out.astype(BF16)


def _proj_ln(y, w, x, g, b, name):
    t, k = y.shape
    return pl.pallas_call(
        _proj_ln_kernel,
        grid=(t // ROW_TILE,),
        in_specs=[pl.BlockSpec((ROW_TILE, k), lambda i: (i, 0)),
                  pl.BlockSpec((k, D_MODEL), lambda i: (0, 0)),
                  pl.BlockSpec((ROW_TILE, D_MODEL), lambda i: (i, 0)),
                  pl.BlockSpec((1, D_MODEL), lambda i: (0, 0)),
                  pl.BlockSpec((1, D_MODEL), lambda i: (0, 0))],
        out_specs=[pl.BlockSpec((ROW_TILE, D_MODEL), lambda i: (i, 0)),
                   pl.BlockSpec((ROW_TILE, D_MODEL), lambda i: (i, 0))],
        out_shape=[jax.ShapeDtypeStruct((t, D_MODEL), F32),
                   jax.ShapeDtypeStruct((t, D_MODEL), BF16)],
        compiler_params=_params("parallel"),
        name=name,
    )(y, w, x, g[None, :], b[None, :])


def _rope_table_kernel(pos_ref, freq_ref, c_ref, s1_ref, s2_ref):
    ang = pos_ref[...].astype(F32) * freq_ref[...]
    d = lax.broadcasted_iota(jnp.int32, ang.shape, 1) % ATT_HEAD_DIM
    cos, sin = jnp.cos(ang), jnp.sin(ang)
    half = ROPE_DIM // 2
    c_ref[...] = jnp.where(d < ROPE_DIM, cos, 1.0)
    s1_ref[...] = jnp.where(d < half, -sin, 0.0)
    s2_ref[...] = jnp.where((d >= half) & (d < ROPE_DIM), sin, 0.0)


def _rope_tables(positions):
    t = positions.size
    half = ROPE_DIM // 2
    inv_freq = ROPE_THETA ** (-jnp.arange(0, ROPE_DIM, 2, dtype=F32) / ROPE_DIM)
    d = jnp.arange(LANES) % ATT_HEAD_DIM
    freq = jnp.where(d < ROPE_DIM, inv_freq[d % half], 0.0).astype(F32)[None, :]
    tab = jax.ShapeDtypeStruct((t, LANES), F32)
    return pl.pallas_call(
        _rope_table_kernel,
        grid=(t // ROW_TILE,),
        in_specs=[pl.BlockSpec((ROW_TILE, 1), lambda i: (i, 0)),
                  pl.BlockSpec((1, LANES), lambda i: (0, 0))],
        out_specs=[pl.BlockSpec((ROW_TILE, LANES), lambda i: (i, 0))] * 3,
        out_shape=[tab, tab, tab],
        compiler_params=_params("parallel"),
        name="rope_tables",
    )(positions.reshape(t, 1), freq)


def _qkv_kernel(x_ref, w_ref, c_ref, s1_ref, s2_ref, o_ref):
    acc = _dot(x_ref[...].astype(BF16), w_ref[...])
    kind = pl.program_id(1) % 3

    @pl.when(kind < 2)
    def _():
        reps = ATT_OUT_DIM // LANES
        c = jnp.concatenate([c_ref[...]] * reps, axis=1)
        s1 = jnp.concatenate([s1_ref[...]] * reps, axis=1)
        s2 = jnp.concatenate([s2_ref[...]] * reps, axis=1)
        half = ROPE_DIM // 2
        up = pltpu.roll(acc, ATT_OUT_DIM - half, 1)
        down = pltpu.roll(acc, half, 1)
        roped = acc * c + up * s1 + down * s2
        scale = jnp.where(kind == 0, ATT_HEAD_DIM ** -0.5, 1.0)
        o_ref[...] = (roped * scale).astype(o_ref.dtype)

    @pl.when(kind == 2)
    def _():
        o_ref[...] = acc.astype(o_ref.dtype)


def _qkv_proj(x, w, tabs):
    t = x.shape[0]
    tn = ATT_OUT_DIM
    tab_spec = pl.BlockSpec((ROW_TILE, LANES), lambda i, j: (i, 0))
    return pl.pallas_call(
        _qkv_kernel,
        grid=(t // ROW_TILE, ATT_QKV_DIM // tn),
        in_specs=[pl.BlockSpec((ROW_TILE, D_MODEL), lambda i, j: (i, 0)),
                  pl.BlockSpec((D_MODEL, tn), lambda i, j: (0, j)),
                  tab_spec, tab_spec, tab_spec],
        out_specs=pl.BlockSpec((ROW_TILE, tn), lambda i, j: (i, j)),
        out_shape=jax.ShapeDtypeStruct((t, ATT_QKV_DIM), BF16),
        compiler_params=_params("parallel", "arbitrary"),
        name="qkv_rope",
    )(x, w, *tabs)


def _attn_kernel(q_ref, kp_ref, kc_ref, vp_ref, vc_ref, o_ref, st_ref):
    w = ATT_BLOCK
    i = pl.program_id(2)
    qi = lax.broadcasted_iota(jnp.int32, (w, 2 * w), 0)
    kk = lax.broadcasted_iota(jnp.int32, (w, 2 * w), 1)
    first_key = jnp.where(i > 0, 0, w)
    valid = (kk >= qi) & (kk <= qi + w) & (kk >= first_key)
    lane = lax.broadcasted_iota(jnp.int32, (w, LANES), 1)
    lo_mask = lane < HALF
    k = jnp.concatenate([kp_ref[...], kc_ref[...]], axis=0)
    v = jnp.concatenate([vp_ref[...], vc_ref[...]], axis=0)
    stats = jnp.zeros((w, LANES), F32)
    zero = jnp.zeros((), q_ref.dtype)
    for p in range(ATT_HEADS // 2):
        sl = slice(p * LANES, (p + 1) * LANES)
        qp, kp, vp = q_ref[:, sl], k[:, sl], v[:, sl]
        outs = []
        for hh in range(2):
            h = 2 * p + hh
            qm = jnp.where(lo_mask if hh == 0 else ~lo_mask, qp, zero)
            s = jnp.where(valid, _dot_nt(qm, kp), NEG_INF)
            m = jnp.max(s, -1, keepdims=True)
            pr = jnp.exp(s - m)
            l = jnp.sum(pr, -1, keepdims=True)
            outs.append(_dot(pr.astype(v.dtype), vp) / l)
            stats = jnp.where(lane == h, m, stats)
            stats = jnp.where(lane == ATT_HEADS + h, l, stats)
        o_ref[:, sl] = jnp.where(lo_mask, outs[0], outs[1]).astype(o_ref.dtype)
    st_ref[...] = stats


def _window_attention(qkv, grp, bsz, seq):
    _, dil = ATT_PATTERNS[grp]
    n = seq // dil
    nb = n // ATT_BLOCK
    w = ATT_BLOCK
    cb = ATT_QKV_DIM // ATT_OUT_DIM
    qkv_r = qkv.reshape(bsz, n, dil * ATT_QKV_DIM)

    def spec(kind, prev):
        def imap(b, r, i):
            blk = jnp.maximum(i - 1, 0) if prev else i
            return (b, blk, r * cb + grp * 3 + kind)
        return pl.BlockSpec((None, w, ATT_OUT_DIM), imap)

    o, st = pl.pallas_call(
        _attn_kernel,
        grid=(bsz, dil, nb),
        in_specs=[spec(0, False), spec(1, True), spec(1, False), spec(2, True), spec(2, False)],
        out_specs=[pl.BlockSpec((None, w, ATT_OUT_DIM), lambda b, r, i: (b, i, r)),
                   pl.BlockSpec((None, w, LANES), lambda b, r, i: (b, i, r))],
        out_shape=[jax.ShapeDtypeStruct((bsz, n, dil * ATT_OUT_DIM), BF16),
                   jax.ShapeDtypeStruct((bsz, n, dil * LANES), F32)],
        compiler_params=_params("parallel", "parallel", "arbitrary"),
        name=f"window_attn_d{dil}",
    )(qkv_r, qkv_r, qkv_r, qkv_r, qkv_r)
    return o.reshape(bsz * seq, ATT_OUT_DIM), st.reshape(bsz * seq, LANES)


def _merge_proj_ln_kernel(o1_ref, o2_ref, o3_ref, s1_ref, s2_ref, s3_ref, w_ref, x_ref, g_ref, b_ref,
                          o_ref, o16_ref):
    rows = o1_ref.shape[0]
    lane = lax.broadcasted_iota(jnp.int32, (rows, LANES), 1)
    lo_mask = lane < HALF
    sts = [s1_ref[...], s2_ref[...], s3_ref[...]]
    mx = jnp.maximum(jnp.maximum(sts[0], sts[1]), sts[2])
    wgts = [pltpu.roll(s, LANES - ATT_HEADS, 1) * jnp.exp(s - mx) for s in sts]
    den = wgts[0] + wgts[1] + wgts[2]
    den = jnp.where(lane < ATT_HEADS, den, 1.0)
    coefs = [wg / den for wg in wgts]
    outs = [o1_ref, o2_ref, o3_ref]
    parts = []
    for p in range(ATT_HEADS // 2):
        sl = slice(p * LANES, (p + 1) * LANES)
        acc = jnp.zeros((rows, LANES), F32)
        for gi in range(ATT_N_GROUPS):
            acc = acc + _pair_expand(coefs[gi], 2 * p, lo_mask) * outs[gi][:, sl].astype(F32)
        parts.append(acc.astype(BF16))
    mix = _dot(jnp.concatenate(parts, axis=1), w_ref[...])
    out = _layer_norm(DEEPNORM_ALPHA * x_ref[...] + mix, g_ref[...], b_ref[...])
    o_ref[...] = out
    o16_ref[...] = out.astype(BF16)


def _merge_proj_ln(os_, sts, w, x, g, b):
    t = x.shape[0]
    tm = ROW_TILE // 2
    ospec = pl.BlockSpec((tm, ATT_OUT_DIM), lambda i: (i, 0))
    sspec = pl.BlockSpec((tm, LANES), lambda i: (i, 0))
    xspec = pl.BlockSpec((tm, D_MODEL), lambda i: (i, 0))
    vspec = pl.BlockSpec((1, D_MODEL), lambda i: (0, 0))
    return pl.pallas_call(
        _merge_proj_ln_kernel,
        grid=(t // tm,),
        in_specs=[ospec] * 3 + [sspec] * 3 + [pl.BlockSpec((ATT_OUT_DIM, D_MODEL), lambda i: (0, 0)),
                                              xspec, vspec, vspec],
        out_specs=[xspec, xspec],
        out_shape=[jax.ShapeDtypeStruct((t, D_MODEL), F32), jax.ShapeDtypeStruct((t, D_MODEL), BF16)],
        compiler_params=_params("parallel"),
        name="attn_merge_proj_ln",
    )(*os_, *sts, w, x, g[None, :], b[None, :])


def _router_kernel(x_ref, whi_ref, wlo_ref, b_ref, o_ref):
    x = x_ref[...]
    xhi = x.astype(BF16)
    xlo = (x - xhi.astype(F32)).astype(BF16)
    logits = _dot(xhi, whi_ref[...]) + _dot(xlo, whi_ref[...]) + _dot(xhi, wlo_ref[...]) + b_ref[...]
    rows = logits.shape[0]
    lane = lax.broadcasted_iota(jnp.int32, (rows, LANES), 1)
    big = jnp.int32(LANES)

    def top1(vals, mask):
        v = jnp.where(mask, vals, -jnp.inf)
        m = jnp.max(v, -1, keepdims=True)
        idx = jnp.min(jnp.where(v == m, lane, big), -1, keepdims=True)
        return v, m, idx

    gmask = lane < MOE_N_GROUPS
    gv, gm, gidx = top1(logits, gmask)
    g_w = 1.0 / jnp.sum(jnp.exp(gv - gm), -1, keepdims=True)
    e_lo = MOE_N_GROUPS + gidx * MOE_EPG
    emask = (lane >= e_lo) & (lane < e_lo + MOE_EPG)
    ev, m1, i1 = top1(logits, emask)
    zsum = jnp.sum(jnp.exp(ev - m1), -1, keepdims=True)
    _, m2, i2 = top1(logits, emask & (lane != i1))
    p1 = 1.0 / zsum
    p2 = jnp.exp(m2 - m1) / zsum
    tot = p1 + p2
    vals = [(i1 - MOE_N_GROUPS).astype(F32), (i2 - MOE_N_GROUPS).astype(F32),
            g_w * (p1 / tot), g_w * (p2 / tot)]
    out = jnp.zeros((rows, LANES), F32)
    for j, val in enumerate(vals):
        out = jnp.where(lane == j, val, out)
    o_ref[...] = out


def _router(x, w_rg, b_rg, w_re, b_re):
    t = x.shape[0]
    n_log = MOE_N_GROUPS + MOE_N_EXPERTS
    w = jnp.pad(jnp.concatenate([w_rg, w_re], axis=1), ((0, 0), (0, LANES - n_log)))
    whi = w.astype(BF16)
    wlo = (w - whi.astype(F32)).astype(BF16)
    bias = jnp.pad(jnp.concatenate([b_rg, b_re]), (0, LANES - n_log))[None, :]
    wspec = pl.BlockSpec((D_MODEL, LANES), lambda i: (0, 0))
    return pl.pallas_call(
        _router_kernel,
        grid=(t // ROW_TILE,),
        in_specs=[pl.BlockSpec((ROW_TILE, D_MODEL), lambda i: (i, 0)), wspec, wspec,
                  pl.BlockSpec((1, LANES), lambda i: (0, 0))],
        out_specs=pl.BlockSpec((ROW_TILE, LANES), lambda i: (i, 0)),
        out_shape=jax.ShapeDtypeStruct((t, LANES), F32),
        compiler_params=_params("parallel"),
        name="moe_router",
    )(x, whi, wlo, bias)


def _expert_kernel(meta_ref, x_ref, wg_ref, wu_ref, wd_ref, o_ref):
    n_blocks = pl.num_programs(0)

    @pl.when(pl.program_id(0) < meta_ref[n_blocks])
    def _():
        x = x_ref[...]
        h = _silu(_dot(x, wg_ref[0])) * _dot(x, wu_ref[0])
        o_ref[...] = _dot(h.astype(BF16), wd_ref[0]).astype(o_ref.dtype)


def _expert_mlp(meta, x_rows, w_gate, w_up, w_down):
    n_rows = x_rows.shape[0]
    n_blocks = n_rows // MOE_ROW_BLOCK
    tb = MOE_ROW_BLOCK
    grid_spec = pltpu.PrefetchScalarGridSpec(
        num_scalar_prefetch=1,
        grid=(n_blocks,),
        in_specs=[pl.BlockSpec((tb, D_MODEL), lambda i, meta: (i, 0)),
                  pl.BlockSpec((1, D_MODEL, MOE_HIDDEN), lambda i, meta: (meta[i], 0, 0)),
                  pl.BlockSpec((1, D_MODEL, MOE_HIDDEN), lambda i, meta: (meta[i], 0, 0)),
                  pl.BlockSpec((1, MOE_HIDDEN, D_MODEL), lambda i, meta: (meta[i], 0, 0))],
        out_specs=pl.BlockSpec((tb, D_MODEL), lambda i, meta: (i, 0)),
    )
    return pl.pallas_call(
        _expert_kernel,
        grid_spec=grid_spec,
        out_shape=jax.ShapeDtypeStruct((n_rows, D_MODEL), BF16),
        compiler_params=_params("arbitrary"),
        name="moe_experts",
    )(meta, x_rows, w_gate, w_up, w_down)


def _combine_ln_kernel(y0_ref, y1_ref, r_ref, x_ref, g_ref, b_ref, o_ref):
    route = r_ref[...]
    g0 = route[:, 2:3]
    g1 = route[:, 3:4]
    ffn = g0 * y0_ref[...].astype(F32) + g1 * y1_ref[...].astype(F32)
    o_ref[...] = _layer_norm(DEEPNORM_ALPHA * x_ref[...] + ffn, g_ref[...], b_ref[...])


def _combine_ln(y0, y1, route, x, g, b):
    t = x.shape[0]
    xspec = pl.BlockSpec((ROW_TILE, D_MODEL), lambda i: (i, 0))
    vspec = pl.BlockSpec((1, D_MODEL), lambda i: (0, 0))
    return pl.pallas_call(
        _combine_ln_kernel,
        grid=(t // ROW_TILE,),
        in_specs=[xspec, xspec, pl.BlockSpec((ROW_TILE, LANES), lambda i: (i, 0)), xspec, vspec, vspec],
        out_specs=xspec,
        out_shape=jax.ShapeDtypeStruct((t, D_MODEL), F32),
        compiler_params=_params("parallel"),
        name="moe_combine_ln",
    )(y0, y1, route, x, g[None, :], b[None, :])


def _moe(x, x16, w_rg, b_rg, w_re, b_re, w_gate, w_up, w_down, g, b):
    t = x.shape[0]
    tb = MOE_ROW_BLOCK
    route = _router(x, w_rg, b_rg, w_re, b_re)
    eid = route[:, :MOE_TOP_K].astype(jnp.int32).reshape(-1)
    n_assign = t * MOE_TOP_K
    n_blocks = n_assign // tb + MOE_N_EXPERTS
    onehot = (eid[:, None] == jnp.arange(MOE_N_EXPERTS, dtype=jnp.int32)[None, :]).astype(jnp.int32)
    csum = jnp.cumsum(onehot, axis=0)
    rank = jnp.take_along_axis(csum, eid[:, None], axis=1)[:, 0] - 1
    counts = csum[-1]
    padded = (counts + tb - 1) // tb * tb
    pad_end = jnp.cumsum(padded)
    pos = (pad_end - padded)[eid] + rank
    row_tok = jnp.zeros((n_blocks * tb,), jnp.int32).at[pos].set(jnp.arange(n_assign, dtype=jnp.int32) // MOE_TOP_K)
    block_e = jnp.minimum(jnp.searchsorted(pad_end, jnp.arange(n_blocks, dtype=jnp.int32) * tb, side='right'),
                          MOE_N_EXPERTS - 1).astype(jnp.int32)
    meta = jnp.concatenate([block_e, (pad_end[-1:] // tb).astype(jnp.int32)])
    x_rows = jnp.take(x16, row_tok, axis=0)
    y_rows = _expert_mlp(meta, x_rows, w_gate, w_up, w_down)
    pos2 = pos.reshape(t, MOE_TOP_K)
    y0 = jnp.take(y_rows, pos2[:, 0], axis=0)
    y1 = jnp.take(y_rows, pos2[:, 1], axis=0)
    return _combine_ln(y0, y1, route, x, g, b)


def _ssd_layer(x, w_in, conv_w, conv_b, dt_bias, a_log, d_skip, norm_w, w_out, g, b, bsz, seq):
    w_zxbc = w_in[:, :SSD_D_INNER + SSD_CONV_DIM].astype(BF16)
    w_dt = jnp.pad(w_in[:, SSD_D_INNER + SSD_CONV_DIM:], ((0, 0), (0, LANES - SSD_N_HEADS))).astype(BF16)
    zxbc = _matmul(x, w_zxbc, 1024, BF16, "ssd_in_proj")
    dt_raw = _matmul(x, w_dt, LANES, F32, "ssd_dt_proj")
    y = _ssd_scan(zxbc, dt_raw, conv_w, conv_b, dt_bias, a_log, d_skip, norm_w, bsz, seq)
    return _proj_ln(y, w_out.astype(BF16), x, g, b, "ssd_out_proj_ln")


def _attn_layer(x, tabs, w_qkv, w_o, g, b, bsz, seq):
    qkv = _qkv_proj(x, w_qkv.astype(BF16), tabs)
    os_, sts = [], []
    for grp in range(ATT_N_GROUPS):
        o, st = _window_attention(qkv, grp, bsz, seq)
        os_.append(o)
        sts.append(st)
    return _merge_proj_ln(os_, sts, w_o.astype(BF16), x, g, b)


def kernel(x, positions, ssd_w_in, ssd_conv_w, ssd_conv_b, ssd_dt_bias, ssd_a_log, ssd_d, ssd_norm_w, ssd_w_out,
           attn_w_qkv, attn_w_o, ln_g, ln_b, moe_w_router_group, moe_b_router_group, moe_w_router_expert,
           moe_b_router_expert, moe_w_gate, moe_w_up, moe_w_down):
    bsz, seq, d = x.shape
    t = bsz * seq
    h = x.reshape(t, d)
    tabs = _rope_tables(positions)
    for i in range(DEPTH):
        j = i // N_MIXERS
        if i % N_MIXERS == 0:
            h, h16 = _ssd_layer(h, ssd_w_in[j], ssd_conv_w[j], ssd_conv_b[j], ssd_dt_bias[j], ssd_a_log[j],
                                ssd_d[j], ssd_norm_w[j], ssd_w_out[j], ln_g[i, 0], ln_b[i, 0], bsz, seq)
        else:
            h, h16 = _attn_layer(h, tabs, attn_w_qkv[j], attn_w_o[j], ln_g[i, 0], ln_b[i, 0], bsz, seq)
        h = _moe(h, h16, moe_w_router_group[i], moe_b_router_group[i], moe_w_router_expert[i],
                 moe_b_router_expert[i], moe_w_gate[i].astype(BF16), moe_w_up[i].astype(BF16),
                 moe_w_down[i].astype(BF16), ln_g[i, 1], ln_b[i, 1])
    return h.reshape(bsz, seq, d)
```

```python
import functools

import jax
import jax.numpy as jnp
from jax import lax
from jax.experimental import pallas as pl
from jax.experimental.pallas import tpu as pltpu

F32 = jnp.float32
BF16 = jnp.bfloat16

D_MODEL = 1024
DEPTH = 4
N_MIXERS = 2

SSD_D_INNER = 2048
SSD_HEAD_DIM = 64
SSD_N_HEADS = 32
SSD_N_GROUPS = 4
SSD_D_STATE = 128
SSD_D_CONV = 4
SSD_CHUNK = 128
SSD_GN = SSD_N_GROUPS * SSD_D_STATE
SSD_CONV_DIM = SSD_D_INNER + 2 * SSD_GN
SSD_GROUP_COLS = SSD_D_INNER // SSD_N_GROUPS

ATT_HEAD_DIM = 64
ATT_HEADS = 8
ATT_PATTERNS = ((128, 1), (512, 4), (2048, 16))
ATT_N_GROUPS = 3
ATT_OUT_DIM = ATT_HEADS * ATT_HEAD_DIM
ATT_QKV_DIM = ATT_N_GROUPS * 3 * ATT_OUT_DIM
ATT_BLOCK = 128
ROPE_THETA = 500000.0
ROPE_DIM = 16

MOE_N_GROUPS = 4
MOE_EPG = 8
MOE_N_EXPERTS = 32
MOE_TOP_K = 2
MOE_HIDDEN = 512
MOE_ROW_BLOCK = 256

DEEPNORM_ALPHA = (2 * DEPTH) ** 0.25
LN_EPS = 1e-5
RMS_EPS = 1e-5
NEG_INF = -1e30

LANES = 128
HALF = LANES // 2
VMEM_LIMIT = 56 * 1024 * 1024

ROW_TILE = 512


def _params(*sem):
    return pltpu.CompilerParams(dimension_semantics=sem, vmem_limit_bytes=VMEM_LIMIT)


def _silu(v):
    return v * (1.0 / (1.0 + jnp.exp(-v)))


def _layer_norm(r, g, b):
    mu = jnp.mean(r, -1, keepdims=True)
    d = r - mu
    var = jnp.mean(d * d, -1, keepdims=True)
    return d * lax.rsqrt(var + LN_EPS) * g + b


def _split3(v):
    hi = v.astype(BF16)
    r1 = v - hi.astype(F32)
    mid = r1.astype(BF16)
    lo = (r1 - mid.astype(F32)).astype(BF16)
    return hi, mid, lo


def _dot(a, b):
    return jnp.dot(a, b, preferred_element_type=F32)


def _dot_nt(a, b):
    return lax.dot_general(a, b, (((1,), (1,)), ((), ())), preferred_element_type=F32)


def _dot_tn(a, b):
    return lax.dot_general(a, b, (((0,), (0,)), ((), ())), preferred_element_type=F32)


def _pair_expand(mat, h0, lo_mask):
    rows = mat.shape[0]
    a = jnp.broadcast_to(mat[:, h0:h0 + 1], (rows, LANES))
    b = jnp.broadcast_to(mat[:, h0 + 1:h0 + 2], (rows, LANES))
    return jnp.where(lo_mask, a, b)


def _resident(shape):
    return pl.BlockSpec(shape, lambda *_: (0,) * len(shape), pipeline_mode=pl.Buffered(1))


def _in_proj_kernel(x_ref, w_ref, wdt_ref, o_ref, dt_ref):
    x = x_ref[...]
    tn = 1024
    for n in range(w_ref.shape[1] // tn):
        sl = slice(n * tn, (n + 1) * tn)
        o_ref[:, sl] = _dot(x, w_ref[:, sl]).astype(o_ref.dtype)
    dt_ref[...] = _dot(x, wdt_ref[...])


def _in_proj(x16, w_zxbc, w_dt):
    t = x16.shape[0]
    n = w_zxbc.shape[1]
    return pl.pallas_call(
        _in_proj_kernel,
        grid=(t // ROW_TILE,),
        in_specs=[pl.BlockSpec((ROW_TILE, D_MODEL), lambda i: (i, 0)),
                  _resident((D_MODEL, n)), _resident((D_MODEL, LANES))],
        out_specs=[pl.BlockSpec((ROW_TILE, n), lambda i: (i, 0)),
                   pl.BlockSpec((ROW_TILE, LANES), lambda i: (i, 0))],
        out_shape=[jax.ShapeDtypeStruct((t, n), BF16), jax.ShapeDtypeStruct((t, LANES), F32)],
        compiler_params=_params("parallel"),
        name="ssd_in_proj",
    )(x16, w_zxbc, w_dt)


def _ssd_kernel(z_ref, xs_ref, bc_ref, dt_ref, cwx_ref, cbx_ref, cwbc_ref, cbbc_ref,
                dtb_ref, alog_ref, dsk_ref, nw_ref, o_ref, extx, extbc, state):
    q = SSD_CHUNK
    c = pl.program_id(1)

    @pl.when(c == 0)
    def _():
        extx[0:8, :] = jnp.zeros((8, SSD_D_INNER), F32)
        extbc[0:8, :] = jnp.zeros((8, 2 * SSD_GN), F32)
        state[...] = jnp.zeros(state.shape, F32)

    extx[8:8 + q, :] = xs_ref[...].astype(F32)
    extbc[8:8 + q, :] = bc_ref[...].astype(F32)

    def conv_silu(ext, w_ref, b_ref):
        acc = ext[8:8 + q, :] * w_ref[3:4, :] + b_ref[...]
        for k in range(SSD_D_CONV - 1):
            acc = acc + ext[5 + k:5 + k + q, :] * w_ref[k:k + 1, :]
        return _silu(acc)

    xs = conv_silu(extx, cwx_ref, cbx_ref)
    bc = conv_silu(extbc, cwbc_ref, cbbc_ref)
    extx[0:8, :] = extx[q:q + 8, :]
    extbc[0:8, :] = extbc[q:q + 8, :]

    pre = dt_ref[...] + dtb_ref[...]
    dt = jnp.maximum(pre, 0.0) + jnp.log(1.0 + jnp.exp(-jnp.abs(pre)))
    a = -jnp.exp(alog_ref[...])
    da = dt * a

    row = lax.broadcasted_iota(jnp.int32, (q, q), 0)
    col = lax.broadcasted_iota(jnp.int32, (q, q), 1)
    causal = row >= col
    lo_mask = col < HALF
    tri = jnp.where(causal, 1.0, 0.0).astype(BF16)
    hi, mid, lo = _split3(da)
    cs = _dot(tri, hi) + _dot(tri, mid) + _dot(tri, lo)
    cs_t = cs.T

    for g in range(SSD_N_GROUPS):
        bg = bc[:, g * SSD_D_STATE:(g + 1) * SSD_D_STATE].astype(BF16)
        cg = bc[:, SSD_GN + g * SSD_D_STATE:SSD_GN + (g + 1) * SSD_D_STATE].astype(BF16)
        cb = _dot_nt(cg, bg)
        prev = state[g]
        y_off = _dot(cg, prev.astype(BF16))
        ys, xdte, cds = [], [], []
        ssq = jnp.zeros((q, 1), F32)
        for p in range(4):
            h0 = g * 8 + 2 * p
            c0 = g * SSD_GROUP_COLS + p * LANES
            csx = _pair_expand(cs, h0, lo_mask)
            dtx = _pair_expand(dt, h0, lo_mask)
            xp = xs[:, c0:c0 + LANES]
            xdt = xp * dtx
            xdt16 = xdt.astype(BF16)
            last = csx[q - 1:q, :]
            halves = []
            for hh in range(2):
                h = h0 + hh
                diff = jnp.broadcast_to(cs[:, h:h + 1], (q, q)) - jnp.broadcast_to(cs_t[h:h + 1, :], (q, q))
                decay = jnp.exp(jnp.where(causal, diff, -jnp.inf))
                halves.append(_dot((cb * decay).astype(BF16), xdt16))
            y = jnp.where(lo_mask, halves[0], halves[1])
            y = y + y_off[:, p * LANES:(p + 1) * LANES] * jnp.exp(csx) + xp * dsk_ref[:, c0:c0 + LANES]
            y = y * _silu(z_ref[:, c0:c0 + LANES].astype(F32))
            ssq = ssq + jnp.sum(y * y, -1, keepdims=True)
            ys.append(y)
            xdte.append((xdt * jnp.exp(last - csx)).astype(BF16))
            cds.append(jnp.exp(last))
        s_new = _dot_tn(bg, jnp.concatenate(xdte, axis=1))
        state[g] = prev * jnp.concatenate(cds, axis=1) + s_new
        inv = lax.rsqrt(ssq * (1.0 / SSD_GROUP_COLS) + RMS_EPS)
        for p in range(4):
            c0 = g * SSD_GROUP_COLS + p * LANES
            o_ref[:, c0:c0 + LANES] = (ys[p] * inv * nw_ref[:, c0:c0 + LANES]).astype(o_ref.dtype)


def _ssd_scan(zxbc, dt_raw, conv_w, conv_b, dt_bias, a_log, d_skip, norm_w, bsz, seq):
    t = bsz * seq
    nc = seq // SSD_CHUNK
    q = SSD_CHUNK
    cwx = jnp.pad(conv_w[:, :SSD_D_INNER], ((0, 4), (0, 0)))
    cwbc = jnp.pad(conv_w[:, SSD_D_INNER:], ((0, 4), (0, 0)))
    cbx = conv_b[None, :SSD_D_INNER]
    cbbc = conv_b[None, SSD_D_INNER:]
    pad_h = LANES - SSD_N_HEADS
    dtb = jnp.pad(dt_bias, (0, pad_h))[None, :]
    alog = jnp.pad(a_log, (0, pad_h))[None, :]
    dsk = jnp.repeat(d_skip, SSD_HEAD_DIM)[None, :]
    nw = norm_w[None, :]

    def const(shape):
        return pl.BlockSpec(shape, lambda b, c: (0, 0))

    return pl.pallas_call(
        _ssd_kernel,
        grid=(bsz, nc),
        in_specs=[pl.BlockSpec((q, SSD_D_INNER), lambda b, c: (b * nc + c, 0)),
                  pl.BlockSpec((q, SSD_D_INNER), lambda b, c: (b * nc + c, 1)),
                  pl.BlockSpec((q, 2 * SSD_GN), lambda b, c: (b * nc + c, 4)),
                  pl.BlockSpec((q, LANES), lambda b, c: (b * nc + c, 0)),
                  const((8, SSD_D_INNER)), const((1, SSD_D_INNER)),
                  const((8, 2 * SSD_GN)), const((1, 2 * SSD_GN)),
                  const((1, LANES)), const((1, LANES)),
                  const((1, SSD_D_INNER)), const((1, SSD_D_INNER))],
        out_specs=pl.BlockSpec((q, SSD_D_INNER), lambda b, c: (b * nc + c, 0)),
        out_shape=jax.ShapeDtypeStruct((t, SSD_D_INNER), BF16),
        scratch_shapes=[pltpu.VMEM((q + 8, SSD_D_INNER), F32),
                        pltpu.VMEM((q + 8, 2 * SSD_GN), F32),
                        pltpu.VMEM((SSD_N_GROUPS, SSD_D_STATE, SSD_GROUP_COLS), F32)],
        compiler_params=_params("arbitrary", "arbitrary"),
        name="ssd_scan",
    )(zxbc, zxbc, zxbc, dt_raw, cwx, cbx, cwbc, cbbc, dtb, alog, dsk, nw)


def _proj_ln_kernel(y_ref, w_ref, x_ref, g_ref, b_ref, o_ref, o16_ref):
    mix = _dot(y_ref[...].astype(BF16), w_ref[...])
    out = _layer_norm(DEEPNORM_ALPHA * x_ref[...] + mix, g_ref[...], b_ref[...])
    o_ref[...] = out
    o16_ref[...] = out.astype(BF16)


def _proj_ln(y, w, x, g, b, name):
    t, k = y.shape
    return pl.pallas_call(
        _proj_ln_kernel,
        grid=(t // ROW_TILE,),
        in_specs=[pl.BlockSpec((ROW_TILE, k), lambda i: (i, 0)),
                  _resident((k, D_MODEL)),
                  pl.BlockSpec((ROW_TILE, D_MODEL), lambda i: (i, 0)),
                  pl.BlockSpec((1, D_MODEL), lambda i: (0, 0)),
                  pl.BlockSpec((1, D_MODEL), lambda i: (0, 0))],
        out_specs=[pl.BlockSpec((ROW_TILE, D_MODEL), lambda i: (i, 0)),
                   pl.BlockSpec((ROW_TILE, D_MODEL), lambda i: (i, 0))],
        out_shape=[jax.ShapeDtypeStruct((t, D_MODEL), F32),
                   jax.ShapeDtypeStruct((t, D_MODEL), BF16)],
        compiler_params=_params("parallel"),
        name=name,
    )(y, w, x, g[None, :], b[None, :])


def _rope_table_kernel(pos_ref, freq_ref, c_ref, s1_ref, s2_ref):
    ang = pos_ref[...].astype(F32) * freq_ref[...]
    d = lax.broadcasted_iota(jnp.int32, ang.shape, 1) % ATT_HEAD_DIM
    cos, sin = jnp.cos(ang), jnp.sin(ang)
    half = ROPE_DIM // 2
    c_ref[...] = jnp.where(d < ROPE_DIM, cos, 1.0)
    s1_ref[...] = jnp.where(d < half, -sin, 0.0)
    s2_ref[...] = jnp.where((d >= half) & (d < ROPE_DIM), sin, 0.0)


def _residue_order(v, dil):
    tail = v.shape[1:]
    return v.reshape(-1, ROW_TILE // dil, dil, *tail).swapaxes(1, 2).reshape(-1, *tail)


def _perm_matrix(dil):
    src = _residue_order(jnp.arange(ROW_TILE, dtype=jnp.int32), dil)
    return (src[:, None] == jnp.arange(ROW_TILE, dtype=jnp.int32)[None, :]).astype(BF16)


def _rope_tables(positions):
    t = positions.size
    flat = positions.reshape(t)
    pos_all = jnp.concatenate([_residue_order(flat, dil) for _, dil in ATT_PATTERNS])
    half = ROPE_DIM // 2
    inv_freq = ROPE_THETA ** (-jnp.arange(0, ROPE_DIM, 2, dtype=F32) / ROPE_DIM)
    d = jnp.arange(LANES) % ATT_HEAD_DIM
    freq = jnp.where(d < ROPE_DIM, inv_freq[d % half], 0.0).astype(F32)[None, :]
    n = pos_all.size
    tab = jax.ShapeDtypeStruct((n, LANES), F32)
    tabs = pl.pallas_call(
        _rope_table_kernel,
        grid=(n // ROW_TILE,),
        in_specs=[pl.BlockSpec((ROW_TILE, 1), lambda i: (i, 0)),
                  pl.BlockSpec((1, LANES), lambda i: (0, 0))],
        out_specs=[pl.BlockSpec((ROW_TILE, LANES), lambda i: (i, 0))] * 3,
        out_shape=[tab, tab, tab],
        compiler_params=_params("parallel"),
        name="rope_tables",
    )(pos_all.reshape(n, 1), freq)
    return [tb.reshape(ATT_N_GROUPS, t, LANES) for tb in tabs]


def _qkv_kernel(x_ref, p4_ref, p16_ref, w_ref, c_ref, s1_ref, s2_ref, o_ref):
    x16 = x_ref[...]
    xs = [x16, _dot(p4_ref[...], x16).astype(BF16), _dot(p16_ref[...], x16).astype(BF16)]
    reps = ATT_OUT_DIM // LANES
    half = ROPE_DIM // 2
    for grp in range(ATT_N_GROUPS):
        c = jnp.concatenate([c_ref[grp]] * reps, axis=1)
        s1 = jnp.concatenate([s1_ref[grp]] * reps, axis=1)
        s2 = jnp.concatenate([s2_ref[grp]] * reps, axis=1)
        for kind in range(3):
            j = grp * 3 + kind
            sl = slice(j * ATT_OUT_DIM, (j + 1) * ATT_OUT_DIM)
            acc = _dot(xs[grp], w_ref[:, sl])
            if kind < 2:
                up = pltpu.roll(acc, ATT_OUT_DIM - half, 1)
                down = pltpu.roll(acc, half, 1)
                acc = acc * c + up * s1 + down * s2
            if kind == 0:
                acc = acc * (ATT_HEAD_DIM ** -0.5)
            o_ref[:, sl] = acc.astype(o_ref.dtype)


def _qkv_proj(x16, w, tabs, perms):
    t = x16.shape[0]
    tab_spec = pl.BlockSpec((ATT_N_GROUPS, ROW_TILE, LANES), lambda i: (0, i, 0))
    return pl.pallas_call(
        _qkv_kernel,
        grid=(t // ROW_TILE,),
        in_specs=[pl.BlockSpec((ROW_TILE, D_MODEL), lambda i: (i, 0)),
                  _resident((ROW_TILE, ROW_TILE)), _resident((ROW_TILE, ROW_TILE)),
                  _resident((D_MODEL, ATT_QKV_DIM)), tab_spec, tab_spec, tab_spec],
        out_specs=pl.BlockSpec((ROW_TILE, ATT_QKV_DIM), lambda i: (i, 0)),
        out_shape=jax.ShapeDtypeStruct((t, ATT_QKV_DIM), BF16),
        compiler_params=_params("parallel"),
        name="qkv_rope",
    )(x16, perms[1], perms[2], w, *tabs)


def _attn_kernel(q_ref, kp_ref, kc_ref, vp_ref, vc_ref, o_ref, st_ref):
    w = ATT_BLOCK
    i = pl.program_id(2)
    qi = lax.broadcasted_iota(jnp.int32, (w, 2 * w), 0)
    kk = lax.broadcasted_iota(jnp.int32, (w, 2 * w), 1)
    first_key = jnp.where(i > 0, 0, w)
    valid = (kk >= qi) & (kk <= qi + w) & (kk >= first_key)
    lane = lax.broadcasted_iota(jnp.int32, (w, LANES), 1)
    lo_mask = lane < HALF

    def rows(ref):
        return ref[...].reshape(w, ref.shape[-1])

    q = rows(q_ref)
    k = jnp.concatenate([rows(kp_ref), rows(kc_ref)], axis=0)
    v = jnp.concatenate([rows(vp_ref), rows(vc_ref)], axis=0)
    stats = jnp.zeros((w, LANES), F32)
    zero = jnp.zeros((), q.dtype)
    parts = []
    for p in range(ATT_HEADS // 2):
        sl = slice(p * LANES, (p + 1) * LANES)
        qp, kp, vp = q[:, sl], k[:, sl], v[:, sl]
        outs = []
        for hh in range(2):
            h = 2 * p + hh
            qm = jnp.where(lo_mask if hh == 0 else ~lo_mask, qp, zero)
            s = jnp.where(valid, _dot_nt(qm, kp), NEG_INF)
            m = jnp.max(s, -1, keepdims=True)
            pr = jnp.exp(s - m)
            l = jnp.sum(pr, -1, keepdims=True)
            outs.append(_dot(pr.astype(v.dtype), vp) / l)
            stats = jnp.where(lane == h, m, stats)
            stats = jnp.where(lane == ATT_HEADS + h, l, stats)
        parts.append(jnp.where(lo_mask, outs[0], outs[1]).astype(o_ref.dtype))
    o_ref[...] = jnp.concatenate(parts, axis=1).reshape(o_ref.shape)
    st_ref[...] = stats.reshape(st_ref.shape)


def _window_attention(qkv, grp, bsz, seq):
    _, dil = ATT_PATTERNS[grp]
    w = ATT_BLOCK
    t = bsz * seq
    chunk = ROW_TILE // dil
    tiles = w // chunk if chunk < w else 1
    span = dil * w
    nb = seq // span
    col0 = grp * 3

    if tiles == 1:
        per_b = seq // w
        stride = span // w

        def spec(width, col, prev):
            def imap(b, r, i):
                blk = jnp.maximum(i - 1, 0) if prev else i
                return (b * per_b + blk * stride + r, col)
            return pl.BlockSpec((w, width), imap)

        qkv_v, o_shape, st_shape = qkv, (t, ATT_OUT_DIM), (t, LANES)
    else:
        def spec(width, col, prev):
            def imap(b, r, i):
                blk = jnp.maximum(i - 1, 0) if prev else i
                return (b, blk, 0, r, 0, col)
            return pl.BlockSpec((None, None, tiles, None, chunk, width), imap)

        lead = (bsz, nb, tiles, dil, chunk)
        qkv_v, o_shape, st_shape = qkv.reshape(*lead, ATT_QKV_DIM), (*lead, ATT_OUT_DIM), (*lead, LANES)

    o, st = pl.pallas_call(
        _attn_kernel,
        grid=(bsz, dil, nb),
        in_specs=[spec(ATT_OUT_DIM, col0, False), spec(ATT_OUT_DIM, col0 + 1, True),
                  spec(ATT_OUT_DIM, col0 + 1, False), spec(ATT_OUT_DIM, col0 + 2, True),
                  spec(ATT_OUT_DIM, col0 + 2, False)],
        out_specs=[spec(ATT_OUT_DIM, 0, False), spec(LANES, 0, False)],
        out_shape=[jax.ShapeDtypeStruct(o_shape, BF16), jax.ShapeDtypeStruct(st_shape, F32)],
        compiler_params=_params("parallel", "parallel", "arbitrary"),
        name=f"window_attn_d{dil}",
    )(qkv_v, qkv_v, qkv_v, qkv_v, qkv_v)
    return o.reshape(t, ATT_OUT_DIM), st.reshape(t, LANES)


def _merge_proj_ln_kernel(o1_ref, o2_ref, o3_ref, s1_ref, s2_ref, s3_ref, p4t_ref, p16t_ref, w_ref, x_ref,
                          g_ref, b_ref, o_ref, o16_ref):
    rows = o1_ref.shape[0]
    lane = lax.broadcasted_iota(jnp.int32, (rows, LANES), 1)
    lo_mask = lane < HALF

    def to_token_order(pt, val):
        if val.dtype == BF16:
            return _dot(pt, val)
        return sum(_dot(pt, term) for term in _split3(val))

    p4t, p16t = p4t_ref[...], p16t_ref[...]
    outs = [o1_ref[...].astype(F32), to_token_order(p4t, o2_ref[...]), to_token_order(p16t, o3_ref[...])]
    sts = [s1_ref[...], to_token_order(p4t, s2_ref[...]), to_token_order(p16t, s3_ref[...])]
    mx = jnp.maximum(jnp.maximum(sts[0], sts[1]), sts[2])
    wgts = [pltpu.roll(s, LANES - ATT_HEADS, 1) * jnp.exp(s - mx) for s in sts]
    den = wgts[0] + wgts[1] + wgts[2]
    den = jnp.where(lane < ATT_HEADS, den, 1.0)
    coefs = [wg / den for wg in wgts]
    parts = []
    for p in range(ATT_HEADS // 2):
        sl = slice(p * LANES, (p + 1) * LANES)
        acc = jnp.zeros((rows, LANES), F32)
        for gi in range(ATT_N_GROUPS):
            acc = acc + _pair_expand(coefs[gi], 2 * p, lo_mask) * outs[gi][:, sl]
        parts.append(acc.astype(BF16))
    mix = _dot(jnp.concatenate(parts, axis=1), w_ref[...])
    out = _layer_norm(DEEPNORM_ALPHA * x_ref[...] + mix, g_ref[...], b_ref[...])
    o_ref[...] = out
    o16_ref[...] = out.astype(BF16)


def _merge_proj_ln(os_, sts, perms_t, w, x, g, b):
    t = x.shape[0]
    tm = ROW_TILE
    ospec = pl.BlockSpec((tm, ATT_OUT_DIM), lambda i: (i, 0))
    sspec = pl.BlockSpec((tm, LANES), lambda i: (i, 0))
    xspec = pl.BlockSpec((tm, D_MODEL), lambda i: (i, 0))
    vspec = pl.BlockSpec((1, D_MODEL), lambda i: (0, 0))
    pspec = _resident((tm, tm))
    return pl.pallas_call(
        _merge_proj_ln_kernel,
        grid=(t // tm,),
        in_specs=[ospec] * 3 + [sspec] * 3 + [pspec, pspec, _resident((ATT_OUT_DIM, D_MODEL)),
                                              xspec, vspec, vspec],
        out_specs=[xspec, xspec],
        out_shape=[jax.ShapeDtypeStruct((t, D_MODEL), F32), jax.ShapeDtypeStruct((t, D_MODEL), BF16)],
        compiler_params=_params("parallel"),
        name="attn_merge_proj_ln",
    )(*os_, *sts, perms_t[1], perms_t[2], w, x, g[None, :], b[None, :])


def _router_kernel(x_ref, whi_ref, wlo_ref, b_ref, o_ref, cnt_ref, carry):
    @pl.when(pl.program_id(0) == 0)
    def _():
        carry[...] = jnp.zeros(carry.shape, F32)

    x = x_ref[...]
    xhi = x.astype(BF16)
    xlo = (x - xhi.astype(F32)).astype(BF16)
    logits = _dot(xhi, whi_ref[...]) + _dot(xlo, whi_ref[...]) + _dot(xhi, wlo_ref[...]) + b_ref[...]
    rows = logits.shape[0]
    lane = lax.broadcasted_iota(jnp.int32, (rows, LANES), 1)
    big = jnp.int32(LANES)

    def top1(vals, mask):
        v = jnp.where(mask, vals, -jnp.inf)
        m = jnp.max(v, -1, keepdims=True)
        idx = jnp.min(jnp.where(v == m, lane, big), -1, keepdims=True)
        return v, m, idx

    gmask = lane < MOE_N_GROUPS
    gv, gm, gidx = top1(logits, gmask)
    g_w = 1.0 / jnp.sum(jnp.exp(gv - gm), -1, keepdims=True)
    e_lo = MOE_N_GROUPS + gidx * MOE_EPG
    emask = (lane >= e_lo) & (lane < e_lo + MOE_EPG)
    ev, m1, i1 = top1(logits, emask)
    zsum = jnp.sum(jnp.exp(ev - m1), -1, keepdims=True)
    _, m2, i2 = top1(logits, emask & (lane != i1))
    p1 = 1.0 / zsum
    p2 = jnp.exp(m2 - m1) / zsum
    tot = p1 + p2
    e1 = i1 - MOE_N_GROUPS
    e2 = i2 - MOE_N_GROUPS

    oh1 = jnp.where(lane == e1, 1.0, 0.0)
    oh2 = jnp.where(lane == e2, 1.0, 0.0)
    oh = oh1 + oh2
    ri = lax.broadcasted_iota(jnp.int32, (rows, rows), 0)
    ci = lax.broadcasted_iota(jnp.int32, (rows, rows), 1)
    strict = jnp.where(ri > ci, 1.0, 0.0).astype(BF16)
    before = _dot(strict, oh.astype(BF16)) + carry[...]
    rank1 = jnp.sum(oh1 * before, -1, keepdims=True)
    rank2 = jnp.sum(oh2 * before, -1, keepdims=True)
    carry[...] = carry[...] + jnp.sum(oh, 0, keepdims=True)
    cnt_ref[...] = carry[...]

    vals = [e1.astype(F32), e2.astype(F32), g_w * (p1 / tot), g_w * (p2 / tot), rank1, rank2]
    out = jnp.zeros((rows, LANES), F32)
    for j, val in enumerate(vals):
        out = jnp.where(lane == j, val, out)
    o_ref[...] = out


def _router(x, w_rg, b_rg, w_re, b_re):
    t = x.shape[0]
    n_log = MOE_N_GROUPS + MOE_N_EXPERTS
    w = jnp.pad(jnp.concatenate([w_rg, w_re], axis=1), ((0, 0), (0, LANES - n_log)))
    whi = w.astype(BF16)
    wlo = (w - whi.astype(F32)).astype(BF16)
    bias = jnp.pad(jnp.concatenate([b_rg, b_re]), (0, LANES - n_log))[None, :]
    wspec = pl.BlockSpec((D_MODEL, LANES), lambda i: (0, 0))
    return pl.pallas_call(
        _router_kernel,
        grid=(t // ROW_TILE,),
        in_specs=[pl.BlockSpec((ROW_TILE, D_MODEL), lambda i: (i, 0)), wspec, wspec,
                  pl.BlockSpec((1, LANES), lambda i: (0, 0))],
        out_specs=[pl.BlockSpec((ROW_TILE, LANES), lambda i: (i, 0)),
                   pl.BlockSpec((1, LANES), lambda i: (0, 0))],
        out_shape=[jax.ShapeDtypeStruct((t, LANES), F32), jax.ShapeDtypeStruct((1, LANES), F32)],
        scratch_shapes=[pltpu.VMEM((1, LANES), F32)],
        compiler_params=_params("arbitrary"),
        name="moe_router",
    )(x, whi, wlo, bias)


def _expert_kernel(meta_ref, x_ref, wg_ref, wu_ref, wd_ref, o_ref, wg16, wu16, wd16):
    i = pl.program_id(0)
    n_blocks = pl.num_programs(0)

    @pl.when(i < meta_ref[n_blocks])
    def _():
        @pl.when((i == 0) | (meta_ref[i] != meta_ref[jnp.maximum(i - 1, 0)]))
        def _():
            wg16[...] = wg_ref[0].astype(BF16)
            wu16[...] = wu_ref[0].astype(BF16)
            wd16[...] = wd_ref[0].astype(BF16)

        x = x_ref[...]
        h = _silu(_dot(x, wg16[...])) * _dot(x, wu16[...])
        o_ref[...] = _dot(h.astype(BF16), wd16[...]).astype(o_ref.dtype)


def _expert_mlp(meta, x_rows, w_gate, w_up, w_down):
    n_rows = x_rows.shape[0]
    n_blocks = n_rows // MOE_ROW_BLOCK
    tb = MOE_ROW_BLOCK
    grid_spec = pltpu.PrefetchScalarGridSpec(
        num_scalar_prefetch=1,
        grid=(n_blocks,),
        in_specs=[pl.BlockSpec((tb, D_MODEL), lambda i, meta: (i, 0)),
                  pl.BlockSpec((1, D_MODEL, MOE_HIDDEN), lambda i, meta: (meta[i], 0, 0)),
                  pl.BlockSpec((1, D_MODEL, MOE_HIDDEN), lambda i, meta: (meta[i], 0, 0)),
                  pl.BlockSpec((1, MOE_HIDDEN, D_MODEL), lambda i, meta: (meta[i], 0, 0))],
        out_specs=pl.BlockSpec((tb, D_MODEL), lambda i, meta: (i, 0)),
        scratch_shapes=[pltpu.VMEM((D_MODEL, MOE_HIDDEN), BF16), pltpu.VMEM((D_MODEL, MOE_HIDDEN), BF16),
                        pltpu.VMEM((MOE_HIDDEN, D_MODEL), BF16)],
    )
    return pl.pallas_call(
        _expert_kernel,
        grid_spec=grid_spec,
        out_shape=jax.ShapeDtypeStruct((n_rows, D_MODEL), BF16),
        compiler_params=_params("arbitrary"),
        name="moe_experts",
    )(meta, x_rows, w_gate, w_up, w_down)


def _combine_ln_kernel(y0_ref, y1_ref, r_ref, x_ref, g_ref, b_ref, o_ref, o16_ref):
    route = r_ref[...]
    g0 = route[:, 2:3]
    g1 = route[:, 3:4]
    ffn = g0 * y0_ref[...].astype(F32) + g1 * y1_ref[...].astype(F32)
    out = _layer_norm(DEEPNORM_ALPHA * x_ref[...] + ffn, g_ref[...], b_ref[...])
    o_ref[...] = out
    o16_ref[...] = out.astype(BF16)


def _combine_ln(y0, y1, route, x, g, b):
    t = x.shape[0]
    xspec = pl.BlockSpec((ROW_TILE, D_MODEL), lambda i: (i, 0))
    vspec = pl.BlockSpec((1, D_MODEL), lambda i: (0, 0))
    return pl.pallas_call(
        _combine_ln_kernel,
        grid=(t // ROW_TILE,),
        in_specs=[xspec, xspec, pl.BlockSpec((ROW_TILE, LANES), lambda i: (i, 0)), xspec, vspec, vspec],
        out_specs=[xspec, xspec],
        out_shape=[jax.ShapeDtypeStruct((t, D_MODEL), F32), jax.ShapeDtypeStruct((t, D_MODEL), BF16)],
        compiler_params=_params("parallel"),
        name="moe_combine_ln",
    )(y0, y1, route, x, g[None, :], b[None, :])


def _moe(x, x16, w_rg, b_rg, w_re, b_re, w_gate, w_up, w_down, g, b):
    t = x.shape[0]
    tb = MOE_ROW_BLOCK
    route, cnt = _router(x, w_rg, b_rg, w_re, b_re)
    n_assign = t * MOE_TOP_K
    n_blocks = n_assign // tb + MOE_N_EXPERTS
    counts = cnt[0, :MOE_N_EXPERTS].astype(jnp.int32)
    padded = (counts + tb - 1) // tb * tb
    pad_end = jnp.cumsum(padded)
    pad_start = pad_end - padded
    eid = route[:, :MOE_TOP_K].astype(jnp.int32)
    rank = route[:, 4:4 + MOE_TOP_K].astype(jnp.int32)
    sel = eid[:, :, None] == jnp.arange(MOE_N_EXPERTS, dtype=jnp.int32)
    pos = jnp.sum(jnp.where(sel, pad_start, 0), -1) + rank
    tok = jnp.broadcast_to(jnp.arange(t, dtype=jnp.int32)[:, None], (t, MOE_TOP_K))
    row_tok = jnp.zeros((n_blocks * tb,), jnp.int32).at[pos.reshape(-1)].set(tok.reshape(-1), unique_indices=True)
    block_e = jnp.minimum(jnp.searchsorted(pad_end, jnp.arange(n_blocks, dtype=jnp.int32) * tb, side='right'),
                          MOE_N_EXPERTS - 1).astype(jnp.int32)
    meta = jnp.concatenate([block_e, (pad_end[-1:] // tb).astype(jnp.int32)])
    x_rows = jnp.take(x16, row_tok, axis=0)
    y_rows = _expert_mlp(meta, x_rows, w_gate, w_up, w_down)
    y0 = jnp.take(y_rows, pos[:, 0], axis=0)
    y1 = jnp.take(y_rows, pos[:, 1], axis=0)
    return _combine_ln(y0, y1, route, x, g, b)


def _ssd_layer(x, x16, w_in, conv_w, conv_b, dt_bias, a_log, d_skip, norm_w, w_out, g, b, bsz, seq):
    w_zxbc = w_in[:, :SSD_D_INNER + SSD_CONV_DIM].astype(BF16)
    w_dt = jnp.pad(w_in[:, SSD_D_INNER + SSD_CONV_DIM:], ((0, 0), (0, LANES - SSD_N_HEADS))).astype(BF16)
    zxbc, dt_raw = _in_proj(x16, w_zxbc, w_dt)
    y = _ssd_scan(zxbc, dt_raw, conv_w, conv_b, dt_bias, a_log, d_skip, norm_w, bsz, seq)
    return _proj_ln(y, w_out.astype(BF16), x, g, b, "ssd_out_proj_ln")


def _attn_layer(x, x16, tabs, w_qkv, w_o, g, b, bsz, seq):
    perms = [_perm_matrix(dil) for _, dil in ATT_PATTERNS]
    perms_t = [p.T for p in perms]
    qkv = _qkv_proj(x16, w_qkv.astype(BF16), tabs, perms)
    os_, sts = [], []
    for grp in range(ATT_N_GROUPS):
        o, st = _window_attention(qkv, grp, bsz, seq)
        os_.append(o)
        sts.append(st)
    return _merge_proj_ln(os_, sts, perms_t, w_o.astype(BF16), x, g, b)


def kernel(x, positions, ssd_w_in, ssd_conv_w, ssd_conv_b, ssd_dt_bias, ssd_a_log, ssd_d, ssd_norm_w, ssd_w_out,
           attn_w_qkv, attn_w_o, ln_g, ln_b, moe_w_router_group, moe_b_router_group, moe_w_router_expert,
           moe_b_router_expert, moe_w_gate, moe_w_up, moe_w_down):
    bsz, seq, d = x.shape
    t = bsz * seq
    h = x.reshape(t, d)
    h16 = h.astype(BF16)
    tabs = _rope_tables(positions)
    for i in range(DEPTH):
        j = i // N_MIXERS
        if i % N_MIXERS == 0:
            h, h16 = _ssd_layer(h, h16, ssd_w_in[j], ssd_conv_w[j], ssd_conv_b[j], ssd_dt_bias[j], ssd_a_log[j],
                                ssd_d[j], ssd_norm_w[j], ssd_w_out[j], ln_g[i, 0], ln_b[i, 0], bsz, seq)
        else:
            h, h16 = _attn_layer(h, h16, tabs, attn_w_qkv[j], attn_w_o[j], ln_g[i, 0], ln_b[i, 0], bsz, seq)
        h, h16 = _moe(h, h16, moe_w_router_group[i], moe_b_router_group[i], moe_w_router_expert[i],
                      moe_b_router_expert[i], moe_w_gate[i], moe_w_up[i], moe_w_down[i], ln_g[i, 1], ln_b[i, 1])
    return h.reshape(bsz, seq, d)
```

```python
import functools

import jax
import jax.numpy as jnp
from jax import lax
from jax.experimental import pallas as pl
from jax.experimental.pallas import tpu as pltpu

F32 = jnp.float32
BF16 = jnp.bfloat16

D_MODEL = 1024
DEPTH = 4
N_MIXERS = 2

SSD_D_INNER = 2048
SSD_HEAD_DIM = 64
SSD_N_HEADS = 32
SSD_N_GROUPS = 4
SSD_D_STATE = 128
SSD_D_CONV = 4
SSD_CHUNK = 128
SSD_GN = SSD_N_GROUPS * SSD_D_STATE
SSD_CONV_DIM = SSD_D_INNER + 2 * SSD_GN
SSD_GROUP_COLS = SSD_D_INNER // SSD_N_GROUPS

ATT_HEAD_DIM = 64
ATT_HEADS = 8
ATT_PATTERNS = ((128, 1), (512, 4), (2048, 16))
ATT_N_GROUPS = 3
ATT_OUT_DIM = ATT_HEADS * ATT_HEAD_DIM
ATT_QKV_DIM = ATT_N_GROUPS * 3 * ATT_OUT_DIM
ATT_BLOCK = 128
ROPE_THETA = 500000.0
ROPE_DIM = 16

MOE_N_GROUPS = 4
MOE_EPG = 8
MOE_N_EXPERTS = 32
MOE_TOP_K = 2
MOE_HIDDEN = 512
MOE_ROW_BLOCK = 256

DEEPNORM_ALPHA = (2 * DEPTH) ** 0.25
LN_EPS = 1e-5
RMS_EPS = 1e-5
NEG_INF = -1e30

LANES = 128
HALF = LANES // 2
VMEM_LIMIT = 56 * 1024 * 1024

ROW_TILE = 512


def _params(*sem):
    return pltpu.CompilerParams(dimension_semantics=sem, vmem_limit_bytes=VMEM_LIMIT)


def _silu(v):
    return v * (1.0 / (1.0 + jnp.exp(-v)))


def _layer_norm(r, g, b):
    mu = jnp.mean(r, -1, keepdims=True)
    d = r - mu
    var = jnp.mean(d * d, -1, keepdims=True)
    return d * lax.rsqrt(var + LN_EPS) * g + b


def _split3(v):
    hi = v.astype(BF16)
    r1 = v - hi.astype(F32)
    mid = r1.astype(BF16)
    lo = (r1 - mid.astype(F32)).astype(BF16)
    return hi, mid, lo


def _dot(a, b):
    return jnp.dot(a, b, preferred_element_type=F32)


def _dot_nt(a, b):
    return lax.dot_general(a, b, (((1,), (1,)), ((), ())), preferred_element_type=F32)


def _dot_tn(a, b):
    return lax.dot_general(a, b, (((0,), (0,)), ((), ())), preferred_element_type=F32)


def _pair_expand(mat, h0, lo_mask):
    rows = mat.shape[0]
    a = jnp.broadcast_to(mat[:, h0:h0 + 1], (rows, LANES))
    b = jnp.broadcast_to(mat[:, h0 + 1:h0 + 2], (rows, LANES))
    return jnp.where(lo_mask, a, b)


def _resident(shape):
    return pl.BlockSpec(shape, lambda *_: (0,) * len(shape), pipeline_mode=pl.Buffered(1))


def _in_proj_kernel(x_ref, w_ref, wdt_ref, o_ref, dt_ref):
    x = x_ref[...]
    tn = 1024
    for n in range(w_ref.shape[1] // tn):
        sl = slice(n * tn, (n + 1) * tn)
        o_ref[:, sl] = _dot(x, w_ref[:, sl]).astype(o_ref.dtype)
    dt_ref[...] = _dot(x, wdt_ref[...])


def _in_proj(x16, w_zxbc, w_dt):
    t = x16.shape[0]
    n = w_zxbc.shape[1]
    return pl.pallas_call(
        _in_proj_kernel,
        grid=(t // ROW_TILE,),
        in_specs=[pl.BlockSpec((ROW_TILE, D_MODEL), lambda i: (i, 0)),
                  _resident((D_MODEL, n)), _resident((D_MODEL, LANES))],
        out_specs=[pl.BlockSpec((ROW_TILE, n), lambda i: (i, 0)),
                   pl.BlockSpec((ROW_TILE, LANES), lambda i: (i, 0))],
        out_shape=[jax.ShapeDtypeStruct((t, n), BF16), jax.ShapeDtypeStruct((t, LANES), F32)],
        compiler_params=_params("parallel"),
        name="ssd_in_proj",
    )(x16, w_zxbc, w_dt)


def _ssd_kernel(z_ref, xs_ref, bc_ref, dt_ref, cwx_ref, cbx_ref, cwbc_ref, cbbc_ref,
                dtb_ref, alog_ref, dsk_ref, nw_ref, o_ref, extx, extbc, state):
    q = SSD_CHUNK
    c = pl.program_id(1)

    @pl.when(c == 0)
    def _():
        extx[0:8, :] = jnp.zeros((8, SSD_D_INNER), F32)
        extbc[0:8, :] = jnp.zeros((8, 2 * SSD_GN), F32)
        state[...] = jnp.zeros(state.shape, F32)

    extx[8:8 + q, :] = xs_ref[...].astype(F32)
    extbc[8:8 + q, :] = bc_ref[...].astype(F32)

    def conv_silu(ext, w_ref, b_ref):
        acc = ext[8:8 + q, :] * w_ref[3:4, :] + b_ref[...]
        for k in range(SSD_D_CONV - 1):
            acc = acc + ext[5 + k:5 + k + q, :] * w_ref[k:k + 1, :]
        return _silu(acc)

    xs = conv_silu(extx, cwx_ref, cbx_ref)
    bc = conv_silu(extbc, cwbc_ref, cbbc_ref)
    extx[0:8, :] = extx[q:q + 8, :]
    extbc[0:8, :] = extbc[q:q + 8, :]

    pre = dt_ref[...] + dtb_ref[...]
    dt = jnp.maximum(pre, 0.0) + jnp.log(1.0 + jnp.exp(-jnp.abs(pre)))
    a = -jnp.exp(alog_ref[...])
    da = dt * a

    row = lax.broadcasted_iota(jnp.int32, (q, q), 0)
    col = lax.broadcasted_iota(jnp.int32, (q, q), 1)
    causal = row >= col
    lo_mask = col < HALF
    tri = jnp.where(causal, 1.0, 0.0).astype(BF16)
    hi, mid, lo = _split3(da)
    cs = _dot(tri, hi) + _dot(tri, mid) + _dot(tri, lo)
    cs_t = cs.T

    for g in range(SSD_N_GROUPS):
        bg = bc[:, g * SSD_D_STATE:(g + 1) * SSD_D_STATE].astype(BF16)
        cg = bc[:, SSD_GN + g * SSD_D_STATE:SSD_GN + (g + 1) * SSD_D_STATE].astype(BF16)
        cb = _dot_nt(cg, bg)
        prev = state[g]
        y_off = _dot(cg, prev.astype(BF16))
        ys, xdte, cds = [], [], []
        ssq = jnp.zeros((q, 1), F32)
        for p in range(4):
            h0 = g * 8 + 2 * p
            c0 = g * SSD_GROUP_COLS + p * LANES
            csx = _pair_expand(cs, h0, lo_mask)
            dtx = _pair_expand(dt, h0, lo_mask)
            xp = xs[:, c0:c0 + LANES]
            xdt = xp * dtx
            xdt16 = xdt.astype(BF16)
            last = csx[q - 1:q, :]
            halves = []
            for hh in range(2):
                h = h0 + hh
                diff = jnp.broadcast_to(cs[:, h:h + 1], (q, q)) - jnp.broadcast_to(cs_t[h:h + 1, :], (q, q))
                decay = jnp.exp(jnp.where(causal, diff, -jnp.inf))
                halves.append(_dot((cb * decay).astype(BF16), xdt16))
            y = jnp.where(lo_mask, halves[0], halves[1])
            y = y + y_off[:, p * LANES:(p + 1) * LANES] * jnp.exp(csx) + xp * dsk_ref[:, c0:c0 + LANES]
            y = y * _silu(z_ref[:, c0:c0 + LANES].astype(F32))
            ssq = ssq + jnp.sum(y * y, -1, keepdims=True)
            ys.append(y)
            xdte.append((xdt * jnp.exp(last - csx)).astype(BF16))
            cds.append(jnp.exp(last))
        s_new = _dot_tn(bg, jnp.concatenate(xdte, axis=1))
        state[g] = prev * jnp.concatenate(cds, axis=1) + s_new
        inv = lax.rsqrt(ssq * (1.0 / SSD_GROUP_COLS) + RMS_EPS)
        for p in range(4):
            c0 = g * SSD_GROUP_COLS + p * LANES
            o_ref[:, c0:c0 + LANES] = (ys[p] * inv * nw_ref[:, c0:c0 + LANES]).astype(o_ref.dtype)


def _ssd_scan(zxbc, dt_raw, conv_w, conv_b, dt_bias, a_log, d_skip, norm_w, bsz, seq):
    t = bsz * seq
    nc = seq // SSD_CHUNK
    q = SSD_CHUNK
    cwx = jnp.pad(conv_w[:, :SSD_D_INNER], ((0, 4), (0, 0)))
    cwbc = jnp.pad(conv_w[:, SSD_D_INNER:], ((0, 4), (0, 0)))
    cbx = conv_b[None, :SSD_D_INNER]
    cbbc = conv_b[None, SSD_D_INNER:]
    pad_h = LANES - SSD_N_HEADS
    dtb = jnp.pad(dt_bias, (0, pad_h))[None, :]
    alog = jnp.pad(a_log, (0, pad_h))[None, :]
    dsk = jnp.repeat(d_skip, SSD_HEAD_DIM)[None, :]
    nw = norm_w[None, :]

    def const(shape):
        return pl.BlockSpec(shape, lambda b, c: (0, 0))

    return pl.pallas_call(
        _ssd_kernel,
        grid=(bsz, nc),
        in_specs=[pl.BlockSpec((q, SSD_D_INNER), lambda b, c: (b * nc + c, 0)),
                  pl.BlockSpec((q, SSD_D_INNER), lambda b, c: (b * nc + c, 1)),
                  pl.BlockSpec((q, 2 * SSD_GN), lambda b, c: (b * nc + c, 4)),
                  pl.BlockSpec((q, LANES), lambda b, c: (b * nc + c, 0)),
                  const((8, SSD_D_INNER)), const((1, SSD_D_INNER)),
                  const((8, 2 * SSD_GN)), const((1, 2 * SSD_GN)),
                  const((1, LANES)), const((1, LANES)),
                  const((1, SSD_D_INNER)), const((1, SSD_D_INNER))],
        out_specs=pl.BlockSpec((q, SSD_D_INNER), lambda b, c: (b * nc + c, 0)),
        out_shape=jax.ShapeDtypeStruct((t, SSD_D_INNER), BF16),
        scratch_shapes=[pltpu.VMEM((q + 8, SSD_D_INNER), F32),
                        pltpu.VMEM((q + 8, 2 * SSD_GN), F32),
                        pltpu.VMEM((SSD_N_GROUPS, SSD_D_STATE, SSD_GROUP_COLS), F32)],
        compiler_params=_params("arbitrary", "arbitrary"),
        name="ssd_scan",
    )(zxbc, zxbc, zxbc, dt_raw, cwx, cbx, cwbc, cbbc, dtb, alog, dsk, nw)


def _proj_ln_kernel(y_ref, w_ref, x_ref, g_ref, b_ref, o_ref, o16_ref):
    mix = _dot(y_ref[...].astype(BF16), w_ref[...])
    out = _layer_norm(DEEPNORM_ALPHA * x_ref[...] + mix, g_ref[...], b_ref[...])
    o_ref[...] = out
    o16_ref[...] = out.astype(BF16)


def _proj_ln(y, w, x, g, b, name):
    t, k = y.shape
    return pl.pallas_call(
        _proj_ln_kernel,
        grid=(t // ROW_TILE,),
        in_specs=[pl.BlockSpec((ROW_TILE, k), lambda i: (i, 0)),
                  _resident((k, D_MODEL)),
                  pl.BlockSpec((ROW_TILE, D_MODEL), lambda i: (i, 0)),
                  pl.BlockSpec((1, D_MODEL), lambda i: (0, 0)),
                  pl.BlockSpec((1, D_MODEL), lambda i: (0, 0))],
        out_specs=[pl.BlockSpec((ROW_TILE, D_MODEL), lambda i: (i, 0)),
                   pl.BlockSpec((ROW_TILE, D_MODEL), lambda i: (i, 0))],
        out_shape=[jax.ShapeDtypeStruct((t, D_MODEL), F32),
                   jax.ShapeDtypeStruct((t, D_MODEL), BF16)],
        compiler_params=_params("parallel"),
        name=name,
    )(y, w, x, g[None, :], b[None, :])


def _rope_table_kernel(pos_ref, freq_ref, c_ref, s1_ref, s2_ref):
    ang = pos_ref[...].astype(F32) * freq_ref[...]
    d = lax.broadcasted_iota(jnp.int32, ang.shape, 1) % ATT_HEAD_DIM
    cos, sin = jnp.cos(ang), jnp.sin(ang)
    half = ROPE_DIM // 2
    c_ref[...] = jnp.where(d < ROPE_DIM, cos, 1.0)
    s1_ref[...] = jnp.where(d < half, -sin, 0.0)
    s2_ref[...] = jnp.where((d >= half) & (d < ROPE_DIM), sin, 0.0)


def _residue_order(v, dil):
    tail = v.shape[1:]
    return v.reshape(-1, ROW_TILE // dil, dil, *tail).swapaxes(1, 2).reshape(-1, *tail)


def _perm_matrix(dil):
    src = _residue_order(jnp.arange(ROW_TILE, dtype=jnp.int32), dil)
    return (src[:, None] == jnp.arange(ROW_TILE, dtype=jnp.int32)[None, :]).astype(BF16)


def _rope_tables(positions):
    t = positions.size
    flat = positions.reshape(t)
    pos_all = jnp.concatenate([_residue_order(flat, dil) for _, dil in ATT_PATTERNS])
    half = ROPE_DIM // 2
    inv_freq = ROPE_THETA ** (-jnp.arange(0, ROPE_DIM, 2, dtype=F32) / ROPE_DIM)
    d = jnp.arange(LANES) % ATT_HEAD_DIM
    freq = jnp.where(d < ROPE_DIM, inv_freq[d % half], 0.0).astype(F32)[None, :]
    n = pos_all.size
    tab = jax.ShapeDtypeStruct((n, LANES), F32)
    tabs = pl.pallas_call(
        _rope_table_kernel,
        grid=(n // ROW_TILE,),
        in_specs=[pl.BlockSpec((ROW_TILE, 1), lambda i: (i, 0)),
                  pl.BlockSpec((1, LANES), lambda i: (0, 0))],
        out_specs=[pl.BlockSpec((ROW_TILE, LANES), lambda i: (i, 0))] * 3,
        out_shape=[tab, tab, tab],
        compiler_params=_params("parallel"),
        name="rope_tables",
    )(pos_all.reshape(n, 1), freq)
    return [tb.reshape(ATT_N_GROUPS, t, LANES) for tb in tabs]


def _qkv_kernel(x_ref, p4_ref, p16_ref, w_ref, c_ref, s1_ref, s2_ref, o_ref):
    x16 = x_ref[...]
    xs = [x16, _dot(p4_ref[...], x16).astype(BF16), _dot(p16_ref[...], x16).astype(BF16)]
    reps = ATT_OUT_DIM // LANES
    half = ROPE_DIM // 2
    for grp in range(ATT_N_GROUPS):
        c = jnp.concatenate([c_ref[grp]] * reps, axis=1)
        s1 = jnp.concatenate([s1_ref[grp]] * reps, axis=1)
        s2 = jnp.concatenate([s2_ref[grp]] * reps, axis=1)
        for kind in range(3):
            j = grp * 3 + kind
            sl = slice(j * ATT_OUT_DIM, (j + 1) * ATT_OUT_DIM)
            acc = _dot(xs[grp], w_ref[:, sl])
            if kind < 2:
                up = pltpu.roll(acc, ATT_OUT_DIM - half, 1)
                down = pltpu.roll(acc, half, 1)
                acc = acc * c + up * s1 + down * s2
            if kind == 0:
                acc = acc * (ATT_HEAD_DIM ** -0.5)
            o_ref[:, sl] = acc.astype(o_ref.dtype)


def _qkv_proj(x16, w, tabs, perms):
    t = x16.shape[0]
    tab_spec = pl.BlockSpec((ATT_N_GROUPS, ROW_TILE, LANES), lambda i: (0, i, 0))
    return pl.pallas_call(
        _qkv_kernel,
        grid=(t // ROW_TILE,),
        in_specs=[pl.BlockSpec((ROW_TILE, D_MODEL), lambda i: (i, 0)),
                  _resident((ROW_TILE, ROW_TILE)), _resident((ROW_TILE, ROW_TILE)),
                  _resident((D_MODEL, ATT_QKV_DIM)), tab_spec, tab_spec, tab_spec],
        out_specs=pl.BlockSpec((ROW_TILE, ATT_QKV_DIM), lambda i: (i, 0)),
        out_shape=jax.ShapeDtypeStruct((t, ATT_QKV_DIM), BF16),
        compiler_params=_params("parallel"),
        name="qkv_rope",
    )(x16, perms[1], perms[2], w, *tabs)


def _attn_kernel(q_ref, kp_ref, kc_ref, vp_ref, vc_ref, o_ref, st_ref):
    w = ATT_BLOCK
    i = pl.program_id(2)
    qi = lax.broadcasted_iota(jnp.int32, (w, 2 * w), 0)
    kk = lax.broadcasted_iota(jnp.int32, (w, 2 * w), 1)
    first_key = jnp.where(i > 0, 0, w)
    valid = (kk >= qi) & (kk <= qi + w) & (kk >= first_key)
    lane = lax.broadcasted_iota(jnp.int32, (w, LANES), 1)
    lo_mask = lane < HALF

    def rows(ref):
        return ref[...].reshape(w, ref.shape[-1])

    q = rows(q_ref)
    k = jnp.concatenate([rows(kp_ref), rows(kc_ref)], axis=0)
    v = jnp.concatenate([rows(vp_ref), rows(vc_ref)], axis=0)
    stats = jnp.zeros((w, LANES), F32)
    zero = jnp.zeros((), q.dtype)
    parts = []
    for p in range(ATT_HEADS // 2):
        sl = slice(p * LANES, (p + 1) * LANES)
        qp, kp, vp = q[:, sl], k[:, sl], v[:, sl]
        outs = []
        for hh in range(2):
            h = 2 * p + hh
            qm = jnp.where(lo_mask if hh == 0 else ~lo_mask, qp, zero)
            s = jnp.where(valid, _dot_nt(qm, kp), NEG_INF)
            m = jnp.max(s, -1, keepdims=True)
            pr = jnp.exp(s - m)
            l = jnp.sum(pr, -1, keepdims=True)
            outs.append(_dot(pr.astype(v.dtype), vp) / l)
            stats = jnp.where(lane == h, m, stats)
            stats = jnp.where(lane == ATT_HEADS + h, l, stats)
        parts.append(jnp.where(lo_mask, outs[0], outs[1]).astype(o_ref.dtype))
    o_ref[...] = jnp.concatenate(parts, axis=1).reshape(o_ref.shape)
    st_ref[...] = stats.reshape(st_ref.shape)


def _window_attention(qkv, grp, bsz, seq):
    _, dil = ATT_PATTERNS[grp]
    w = ATT_BLOCK
    t = bsz * seq
    chunk = ROW_TILE // dil
    tiles = w // chunk if chunk < w else 1
    span = dil * w
    nb = seq // span
    col0 = grp * 3

    if tiles == 1:
        per_b = seq // w
        stride = span // w

        def spec(width, col, prev):
            def imap(b, r, i):
                blk = jnp.maximum(i - 1, 0) if prev else i
                return (b * per_b + blk * stride + r, col)
            return pl.BlockSpec((w, width), imap)

        qkv_v, o_shape, st_shape = qkv, (t, ATT_OUT_DIM), (t, LANES)
    else:
        def spec(width, col, prev):
            def imap(b, r, i):
                blk = jnp.maximum(i - 1, 0) if prev else i
                return (b, blk, 0, r, 0, col)
            return pl.BlockSpec((None, None, tiles, None, chunk, width), imap)

        lead = (bsz, nb, tiles, dil, chunk)
        qkv_v, o_shape, st_shape = qkv.reshape(*lead, ATT_QKV_DIM), (*lead, ATT_OUT_DIM), (*lead, LANES)

    o, st = pl.pallas_call(
        _attn_kernel,
        grid=(bsz, dil, nb),
        in_specs=[spec(ATT_OUT_DIM, col0, False), spec(ATT_OUT_DIM, col0 + 1, True),
                  spec(ATT_OUT_DIM, col0 + 1, False), spec(ATT_OUT_DIM, col0 + 2, True),
                  spec(ATT_OUT_DIM, col0 + 2, False)],
        out_specs=[spec(ATT_OUT_DIM, 0, False), spec(LANES, 0, False)],
        out_shape=[jax.ShapeDtypeStruct(o_shape, BF16), jax.ShapeDtypeStruct(st_shape, F32)],
        compiler_params=_params("parallel", "parallel", "arbitrary"),
        name=f"window_attn_d{dil}",
    )(qkv_v, qkv_v, qkv_v, qkv_v, qkv_v)
    return o.reshape(t, ATT_OUT_DIM), st.reshape(t, LANES)


def _merge_proj_ln_kernel(o1_ref, o2_ref, o3_ref, s1_ref, s2_ref, s3_ref, p4t_ref, p16t_ref, w_ref, x_ref,
                          g_ref, b_ref, o_ref, o16_ref):
    rows = o1_ref.shape[0]
    lane = lax.broadcasted_iota(jnp.int32, (rows, LANES), 1)
    lo_mask = lane < HALF

    def to_token_order(pt, val):
        if val.dtype == BF16:
            return _dot(pt, val)
        return sum(_dot(pt, term) for term in _split3(val))

    p4t, p16t = p4t_ref[...], p16t_ref[...]
    outs = [o1_ref[...].astype(F32), to_token_order(p4t, o2_ref[...]), to_token_order(p16t, o3_ref[...])]
    sts = [s1_ref[...], to_token_order(p4t, s2_ref[...]), to_token_order(p16t, s3_ref[...])]
    mx = jnp.maximum(jnp.maximum(sts[0], sts[1]), sts[2])
    wgts = [pltpu.roll(s, LANES - ATT_HEADS, 1) * jnp.exp(s - mx) for s in sts]
    den = wgts[0] + wgts[1] + wgts[2]
    den = jnp.where(lane < ATT_HEADS, den, 1.0)
    coefs = [wg / den for wg in wgts]
    parts = []
    for p in range(ATT_HEADS // 2):
        sl = slice(p * LANES, (p + 1) * LANES)
        acc = jnp.zeros((rows, LANES), F32)
        for gi in range(ATT_N_GROUPS):
            acc = acc + _pair_expand(coefs[gi], 2 * p, lo_mask) * outs[gi][:, sl]
        parts.append(acc.astype(BF16))
    mix = _dot(jnp.concatenate(parts, axis=1), w_ref[...])
    out = _layer_norm(DEEPNORM_ALPHA * x_ref[...] + mix, g_ref[...], b_ref[...])
    o_ref[...] = out
    o16_ref[...] = out.astype(BF16)


def _merge_proj_ln(os_, sts, perms_t, w, x, g, b):
    t = x.shape[0]
    tm = ROW_TILE
    ospec = pl.BlockSpec((tm, ATT_OUT_DIM), lambda i: (i, 0))
    sspec = pl.BlockSpec((tm, LANES), lambda i: (i, 0))
    xspec = pl.BlockSpec((tm, D_MODEL), lambda i: (i, 0))
    vspec = pl.BlockSpec((1, D_MODEL), lambda i: (0, 0))
    pspec = _resident((tm, tm))
    return pl.pallas_call(
        _merge_proj_ln_kernel,
        grid=(t // tm,),
        in_specs=[ospec] * 3 + [sspec] * 3 + [pspec, pspec, _resident((ATT_OUT_DIM, D_MODEL)),
                                              xspec, vspec, vspec],
        out_specs=[xspec, xspec],
        out_shape=[jax.ShapeDtypeStruct((t, D_MODEL), F32), jax.ShapeDtypeStruct((t, D_MODEL), BF16)],
        compiler_params=_params("parallel"),
        name="attn_merge_proj_ln",
    )(*os_, *sts, perms_t[1], perms_t[2], w, x, g[None, :], b[None, :])


def _router_kernel(x_ref, whi_ref, wlo_ref, b_ref, o_ref, cnt_ref, carry):
    @pl.when(pl.program_id(0) == 0)
    def _():
        carry[...] = jnp.zeros(carry.shape, F32)

    x = x_ref[...]
    xhi = x.astype(BF16)
    xlo = (x - xhi.astype(F32)).astype(BF16)
    logits = _dot(xhi, whi_ref[...]) + _dot(xlo, whi_ref[...]) + _dot(xhi, wlo_ref[...]) + b_ref[...]
    rows = logits.shape[0]
    lane = lax.broadcasted_iota(jnp.int32, (rows, LANES), 1)
    big = jnp.int32(LANES)

    def top1(vals, mask):
        v = jnp.where(mask, vals, -jnp.inf)
        m = jnp.max(v, -1, keepdims=True)
        idx = jnp.min(jnp.where(v == m, lane, big), -1, keepdims=True)
        return v, m, idx

    gmask = lane < MOE_N_GROUPS
    gv, gm, gidx = top1(logits, gmask)
    g_w = 1.0 / jnp.sum(jnp.exp(gv - gm), -1, keepdims=True)
    e_lo = MOE_N_GROUPS + gidx * MOE_EPG
    emask = (lane >= e_lo) & (lane < e_lo + MOE_EPG)
    ev, m1, i1 = top1(logits, emask)
    zsum = jnp.sum(jnp.exp(ev - m1), -1, keepdims=True)
    _, m2, i2 = top1(logits, emask & (lane != i1))
    p1 = 1.0 / zsum
    p2 = jnp.exp(m2 - m1) / zsum
    tot = p1 + p2
    e1 = i1 - MOE_N_GROUPS
    e2 = i2 - MOE_N_GROUPS

    oh1 = jnp.where(lane == e1, 1.0, 0.0)
    oh2 = jnp.where(lane == e2, 1.0, 0.0)
    oh = oh1 + oh2
    ri = lax.broadcasted_iota(jnp.int32, (rows, rows), 0)
    ci = lax.broadcasted_iota(jnp.int32, (rows, rows), 1)
    strict = jnp.where(ri > ci, 1.0, 0.0).astype(BF16)
    before = _dot(strict, oh.astype(BF16)) + carry[...]
    rank1 = jnp.sum(oh1 * before, -1, keepdims=True)
    rank2 = jnp.sum(oh2 * before, -1, keepdims=True)
    carry[...] = carry[...] + jnp.sum(oh, 0, keepdims=True)
    cnt_ref[...] = carry[...]

    vals = [e1.astype(F32), e2.astype(F32), g_w * (p1 / tot), g_w * (p2 / tot), rank1, rank2]
    out = jnp.zeros((rows, LANES), F32)
    for j, val in enumerate(vals):
        out = jnp.where(lane == j, val, out)
    o_ref[...] = out


def _router(x, w_rg, b_rg, w_re, b_re):
    t = x.shape[0]
    n_log = MOE_N_GROUPS + MOE_N_EXPERTS
    w = jnp.pad(jnp.concatenate([w_rg, w_re], axis=1), ((0, 0), (0, LANES - n_log)))
    whi = w.astype(BF16)
    wlo = (w - whi.astype(F32)).astype(BF16)
    bias = jnp.pad(jnp.concatenate([b_rg, b_re]), (0, LANES - n_log))[None, :]
    wspec = pl.BlockSpec((D_MODEL, LANES), lambda i: (0, 0))
    return pl.pallas_call(
        _router_kernel,
        grid=(t // ROW_TILE,),
        in_specs=[pl.BlockSpec((ROW_TILE, D_MODEL), lambda i: (i, 0)), wspec, wspec,
                  pl.BlockSpec((1, LANES), lambda i: (0, 0))],
        out_specs=[pl.BlockSpec((ROW_TILE, LANES), lambda i: (i, 0)),
                   pl.BlockSpec((1, LANES), lambda i: (0, 0))],
        out_shape=[jax.ShapeDtypeStruct((t, LANES), F32), jax.ShapeDtypeStruct((1, LANES), F32)],
        scratch_shapes=[pltpu.VMEM((1, LANES), F32)],
        compiler_params=_params("arbitrary"),
        name="moe_router",
    )(x, whi, wlo, bias)


def _expert_kernel(meta_ref, x_ref, wg_ref, wu_ref, wd_ref, o_ref, wg16, wu16, wd16):
    i = pl.program_id(0)
    n_blocks = pl.num_programs(0)

    @pl.when(i < meta_ref[n_blocks])
    def _():
        @pl.when((i == 0) | (meta_ref[i] != meta_ref[jnp.maximum(i - 1, 0)]))
        def _():
            wg16[...] = wg_ref[...].astype(BF16)
            wu16[...] = wu_ref[...].astype(BF16)
            wd16[...] = wd_ref[...].astype(BF16)

        x = x_ref[...]
        h = _silu(_dot(x, wg16[...])) * _dot(x, wu16[...])
        o_ref[...] = _dot(h.astype(BF16), wd16[...]).astype(o_ref.dtype)


def _expert_mlp(meta, x_rows, w_gate, w_up, w_down, layer):
    n_rows = x_rows.shape[0]
    n_blocks = n_rows // MOE_ROW_BLOCK
    tb = MOE_ROW_BLOCK
    grid_spec = pltpu.PrefetchScalarGridSpec(
        num_scalar_prefetch=1,
        grid=(n_blocks,),
        in_specs=[pl.BlockSpec((tb, D_MODEL), lambda i, meta: (i, 0)),
                  pl.BlockSpec((None, None, D_MODEL, MOE_HIDDEN), lambda i, meta: (layer, meta[i], 0, 0)),
                  pl.BlockSpec((None, None, D_MODEL, MOE_HIDDEN), lambda i, meta: (layer, meta[i], 0, 0)),
                  pl.BlockSpec((None, None, MOE_HIDDEN, D_MODEL), lambda i, meta: (layer, meta[i], 0, 0))],
        out_specs=pl.BlockSpec((tb, D_MODEL), lambda i, meta: (i, 0)),
        scratch_shapes=[pltpu.VMEM((D_MODEL, MOE_HIDDEN), BF16), pltpu.VMEM((D_MODEL, MOE_HIDDEN), BF16),
                        pltpu.VMEM((MOE_HIDDEN, D_MODEL), BF16)],
    )
    return pl.pallas_call(
        _expert_kernel,
        grid_spec=grid_spec,
        out_shape=jax.ShapeDtypeStruct((n_rows, D_MODEL), BF16),
        compiler_params=_params("arbitrary"),
        name="moe_experts",
    )(meta, x_rows, w_gate, w_up, w_down)


def _combine_ln_kernel(y0_ref, y1_ref, r_ref, x_ref, g_ref, b_ref, o_ref, o16_ref):
    route = r_ref[...]
    g0 = route[:, 2:3]
    g1 = route[:, 3:4]
    ffn = g0 * y0_ref[...].astype(F32) + g1 * y1_ref[...].astype(F32)
    out = _layer_norm(DEEPNORM_ALPHA * x_ref[...] + ffn, g_ref[...], b_ref[...])
    o_ref[...] = out
    o16_ref[...] = out.astype(BF16)


def _combine_ln(y0, y1, route, x, g, b):
    t = x.shape[0]
    xspec = pl.BlockSpec((ROW_TILE, D_MODEL), lambda i: (i, 0))
    vspec = pl.BlockSpec((1, D_MODEL), lambda i: (0, 0))
    return pl.pallas_call(
        _combine_ln_kernel,
        grid=(t // ROW_TILE,),
        in_specs=[xspec, xspec, pl.BlockSpec((ROW_TILE, LANES), lambda i: (i, 0)), xspec, vspec, vspec],
        out_specs=[xspec, xspec],
        out_shape=[jax.ShapeDtypeStruct((t, D_MODEL), F32), jax.ShapeDtypeStruct((t, D_MODEL), BF16)],
        compiler_params=_params("parallel"),
        name="moe_combine_ln",
    )(y0, y1, route, x, g[None, :], b[None, :])


def _positions_kernel(r_ref, ps_ref, o_ref):
    route = r_ref[...]
    lane = lax.broadcasted_iota(jnp.int32, route.shape, 1)
    starts = ps_ref[...]
    out = jnp.zeros(route.shape, F32)
    for k in range(MOE_TOP_K):
        eid = route[:, k:k + 1].astype(jnp.int32)
        pos = jnp.sum(jnp.where(lane == eid, starts, 0.0), -1, keepdims=True) + route[:, 4 + k:5 + k]
        out = jnp.where(lane == k, pos, out)
    o_ref[...] = out.astype(jnp.int32)


def _positions(route, pad_start):
    t = route.shape[0]
    starts = jnp.pad(pad_start.astype(F32), (0, LANES - MOE_N_EXPERTS))[None, :]
    return pl.pallas_call(
        _positions_kernel,
        grid=(t // ROW_TILE,),
        in_specs=[pl.BlockSpec((ROW_TILE, LANES), lambda i: (i, 0)), pl.BlockSpec((1, LANES), lambda i: (0, 0))],
        out_specs=pl.BlockSpec((ROW_TILE, LANES), lambda i: (i, 0)),
        out_shape=jax.ShapeDtypeStruct((t, LANES), jnp.int32),
        compiler_params=_params("parallel"),
        name="moe_positions",
    )(route, starts)


def _moe(x, x16, w_rg, b_rg, w_re, b_re, w_gate, w_up, w_down, layer, g, b):
    t = x.shape[0]
    tb = MOE_ROW_BLOCK
    route, cnt = _router(x, w_rg, b_rg, w_re, b_re)
    n_assign = t * MOE_TOP_K
    n_blocks = n_assign // tb + MOE_N_EXPERTS
    n_rows = n_blocks * tb
    counts = cnt[0, :MOE_N_EXPERTS].astype(jnp.int32)
    padded = (counts + tb - 1) // tb * tb
    pad_end = jnp.cumsum(padded)
    pos = _positions(route, pad_end - padded)
    pos0, pos1 = pos[:, 0], pos[:, 1]
    tok = jnp.arange(t, dtype=jnp.int32)
    row_tok = (jnp.arange(n_rows, dtype=jnp.int32) % t).at[jnp.concatenate([pos0, pos1])].set(
        jnp.concatenate([tok, tok]), unique_indices=True)
    block_start = jnp.arange(n_blocks, dtype=jnp.int32) * tb
    block_e = jnp.minimum(jnp.sum((pad_end[None, :] <= block_start[:, None]).astype(jnp.int32), -1),
                          MOE_N_EXPERTS - 1)
    meta = jnp.concatenate([block_e, pad_end[-1:] // tb]).astype(jnp.int32)
    x_rows = jnp.take(x16, row_tok, axis=0)
    y_rows = _expert_mlp(meta, x_rows, w_gate, w_up, w_down, layer)
    y0 = jnp.take(y_rows, pos0, axis=0)
    y1 = jnp.take(y_rows, pos1, axis=0)
    return _combine_ln(y0, y1, route, x, g, b)


def _ssd_layer(x, x16, w_in, conv_w, conv_b, dt_bias, a_log, d_skip, norm_w, w_out, g, b, bsz, seq):
    w_zxbc = w_in[:, :SSD_D_INNER + SSD_CONV_DIM].astype(BF16)
    w_dt = jnp.pad(w_in[:, SSD_D_INNER + SSD_CONV_DIM:], ((0, 0), (0, LANES - SSD_N_HEADS))).astype(BF16)
    zxbc, dt_raw = _in_proj(x16, w_zxbc, w_dt)
    y = _ssd_scan(zxbc, dt_raw, conv_w, conv_b, dt_bias, a_log, d_skip, norm_w, bsz, seq)
    return _proj_ln(y, w_out.astype(BF16), x, g, b, "ssd_out_proj_ln")


def _attn_layer(x, x16, tabs, w_qkv, w_o, g, b, bsz, seq):
    perms = [_perm_matrix(dil) for _, dil in ATT_PATTERNS]
    perms_t = [p.T for p in perms]
    qkv = _qkv_proj(x16, w_qkv.astype(BF16), tabs, perms)
    os_, sts = [], []
    for grp in range(ATT_N_GROUPS):
        o, st = _window_attention(qkv, grp, bsz, seq)
        os_.append(o)
        sts.append(st)
    return _merge_proj_ln(os_, sts, perms_t, w_o.astype(BF16), x, g, b)


def kernel(x, positions, ssd_w_in, ssd_conv_w, ssd_conv_b, ssd_dt_bias, ssd_a_log, ssd_d, ssd_norm_w, ssd_w_out,
           attn_w_qkv, attn_w_o, ln_g, ln_b, moe_w_router_group, moe_b_router_group, moe_w_router_expert,
           moe_b_router_expert, moe_w_gate, moe_w_up, moe_w_down):
    bsz, seq, d = x.shape
    t = bsz * seq
    h = x.reshape(t, d)
    h16 = h.astype(BF16)
    tabs = _rope_tables(positions)
    for i in range(DEPTH):
        j = i // N_MIXERS
        if i % N_MIXERS == 0:
            h, h16 = _ssd_layer(h, h16, ssd_w_in[j], ssd_conv_w[j], ssd_conv_b[j], ssd_dt_bias[j], ssd_a_log[j],
                                ssd_d[j], ssd_norm_w[j], ssd_w_out[j], ln_g[i, 0], ln_b[i, 0], bsz, seq)
        else:
            h, h16 = _attn_layer(h, h16, tabs, attn_w_qkv[j], attn_w_o[j], ln_g[i, 0], ln_b[i, 0], bsz, seq)
        h, h16 = _moe(h, h16, moe_w_router_group[i], moe_b_router_group[i], moe_w_router_expert[i],
                      moe_b_router_expert[i], moe_w_gate, moe_w_up, moe_w_down, i, ln_g[i, 1], ln_b[i, 1])
    return h.reshape(bsz, seq, d)
```

```python
import functools

import jax
import jax.numpy as jnp
from jax import lax
from jax.experimental import pallas as pl
from jax.experimental.pallas import tpu as pltpu
from jax.experimental.pallas import tpu_sc as plsc

F32 = jnp.float32
BF16 = jnp.bfloat16
U32 = jnp.uint32

D_MODEL = 1024
D_PACK = D_MODEL // 2
DEPTH = 4
N_MIXERS = 2

SSD_D_INNER = 2048
SSD_HEAD_DIM = 64
SSD_N_HEADS = 32
SSD_N_GROUPS = 4
SSD_D_STATE = 128
SSD_D_CONV = 4
SSD_CHUNK = 128
SSD_GN = SSD_N_GROUPS * SSD_D_STATE
SSD_CONV_DIM = SSD_D_INNER + 2 * SSD_GN
SSD_GROUP_COLS = SSD_D_INNER // SSD_N_GROUPS

ATT_HEAD_DIM = 64
ATT_HEADS = 8
ATT_PATTERNS = ((128, 1), (512, 4), (2048, 16))
ATT_N_GROUPS = 3
ATT_OUT_DIM = ATT_HEADS * ATT_HEAD_DIM
ATT_QKV_DIM = ATT_N_GROUPS * 3 * ATT_OUT_DIM
ATT_BLOCK = 128
ROPE_THETA = 500000.0
ROPE_DIM = 16

MOE_N_GROUPS = 4
MOE_EPG = 8
MOE_N_EXPERTS = 32
MOE_TOP_K = 2
MOE_HIDDEN = 512
MOE_ROW_BLOCK = 256

DEEPNORM_ALPHA = (2 * DEPTH) ** 0.25
LN_EPS = 1e-5
RMS_EPS = 1e-5
NEG_INF = -1e30

LANES = 128
HALF = LANES // 2
VMEM_LIMIT = 56 * 1024 * 1024

ROW_TILE = 512


def _params(*sem):
    return pltpu.CompilerParams(dimension_semantics=sem, vmem_limit_bytes=VMEM_LIMIT)


def _silu(v):
    h = 0.5 * v
    return h + h * jnp.tanh(h)


def _layer_norm(r, g, b):
    mu = jnp.mean(r, -1, keepdims=True)
    d = r - mu
    var = jnp.mean(d * d, -1, keepdims=True)
    return d * lax.rsqrt(var + LN_EPS) * g + b


def _split3(v):
    hi = v.astype(BF16)
    r1 = v - hi.astype(F32)
    mid = r1.astype(BF16)
    lo = (r1 - mid.astype(F32)).astype(BF16)
    return hi, mid, lo


def _dot(a, b):
    return jnp.dot(a, b, preferred_element_type=F32)


def _dot_nt(a, b):
    return lax.dot_general(a, b, (((1,), (1,)), ((), ())), preferred_element_type=F32)


def _dot_tn(a, b):
    return lax.dot_general(a, b, (((0,), (0,)), ((), ())), preferred_element_type=F32)


def _pair_expand(mat, h0, lo_mask):
    rows = mat.shape[0]
    a = jnp.broadcast_to(mat[:, h0:h0 + 1], (rows, LANES))
    b = jnp.broadcast_to(mat[:, h0 + 1:h0 + 2], (rows, LANES))
    return jnp.where(lo_mask, a, b)


def _round_bf16_bits(bits):
    odd = (bits >> 16) & jnp.uint32(1)
    return (bits + jnp.uint32(0x7FFF) + odd) & jnp.uint32(0xFFFF0000)


def _pack_words(bits):
    r = _round_bf16_bits(bits)
    return r[:, :D_PACK] | (r[:, D_PACK:] >> 16)


def _pack_rows(v):
    return _pack_words(pltpu.bitcast(v, U32))


def _unpack_rows(p):
    hi = pltpu.bitcast(p & jnp.uint32(0xFFFF0000), F32)
    lo = pltpu.bitcast(p << 16, F32)
    return jnp.concatenate([hi, lo], axis=1)


def _pack_rows_host(v):
    return _pack_words(lax.bitcast_convert_type(v, U32))


def _row_gather(data, idx):
    n = idx.shape[0]
    d = data.shape[1]
    window = LANES
    dc = d // 2
    mesh = plsc.VectorSubcoreMesh(core_axis_name="core", subcore_axis_name="subcore")

    @functools.partial(pl.kernel, out_type=jax.ShapeDtypeStruct((n, d), data.dtype), mesh=mesh)
    def gather(x_hbm, i_hbm, o_hbm):
        for c in range(d // dc):
            def body(i_vmem, o_vmem, c=c):
                pltpu.sync_copy(x_hbm.at[i_vmem.at[0], pl.ds(c * dc, dc)], o_vmem)

            pltpu.emit_pipeline(
                body,
                grid=(n // window,),
                in_specs=[pl.BlockSpec((1, window), lambda i: (0, i))],
                out_specs=[pl.BlockSpec((window, dc), lambda i, c=c: (i, c))],
                core_axis_name=("core", "subcore"),
                dimension_semantics=(pltpu.PARALLEL,),
            )(i_hbm, o_hbm)

    return gather(data, idx.reshape(1, n))


def _resident(shape):
    return pl.BlockSpec(shape, lambda *_: (0,) * len(shape), pipeline_mode=pl.Buffered(1))


def _in_proj_kernel(x_ref, w_ref, wdt_ref, o_ref, dt_ref):
    x = _unpack_rows(x_ref[...]).astype(BF16)
    tn = 1024
    for n in range(w_ref.shape[1] // tn):
        sl = slice(n * tn, (n + 1) * tn)
        o_ref[:, sl] = _dot(x, w_ref[:, sl]).astype(o_ref.dtype)
    dt_ref[...] = _dot(x, wdt_ref[...])


def _in_proj(xpk, w_zxbc, w_dt):
    t = xpk.shape[0]
    n = w_zxbc.shape[1]
    return pl.pallas_call(
        _in_proj_kernel,
        grid=(t // ROW_TILE,),
        in_specs=[pl.BlockSpec((ROW_TILE, D_PACK), lambda i: (i, 0)),
                  _resident((D_MODEL, n)), _resident((D_MODEL, LANES))],
        out_specs=[pl.BlockSpec((ROW_TILE, n), lambda i: (i, 0)),
                   pl.BlockSpec((ROW_TILE, LANES), lambda i: (i, 0))],
        out_shape=[jax.ShapeDtypeStruct((t, n), BF16), jax.ShapeDtypeStruct((t, LANES), F32)],
        compiler_params=_params("parallel"),
        name="ssd_in_proj",
    )(xpk, w_zxbc, w_dt)


def _ssd_kernel(z_ref, xs_ref, bc_ref, dt_ref, cwx_ref, cbx_ref, cwbc_ref, cbbc_ref,
                dtb_ref, alog_ref, dsk_ref, nw_ref, o_ref, extx, extbc, state):
    q = SSD_CHUNK
    c = pl.program_id(1)

    @pl.when(c == 0)
    def _():
        extx[0:8, :] = jnp.zeros((8, SSD_D_INNER), F32)
        extbc[0:8, :] = jnp.zeros((8, 2 * SSD_GN), F32)
        state[...] = jnp.zeros(state.shape, F32)

    extx[8:8 + q, :] = xs_ref[...].astype(F32)
    extbc[8:8 + q, :] = bc_ref[...].astype(F32)

    def conv_silu(ext, w_ref, b_ref):
        acc = ext[8:8 + q, :] * w_ref[3:4, :] + b_ref[...]
        for k in range(SSD_D_CONV - 1):
            acc = acc + ext[5 + k:5 + k + q, :] * w_ref[k:k + 1, :]
        return _silu(acc)

    xs = conv_silu(extx, cwx_ref, cbx_ref)
    bc = conv_silu(extbc, cwbc_ref, cbbc_ref)
    extx[0:8, :] = extx[q:q + 8, :]
    extbc[0:8, :] = extbc[q:q + 8, :]

    pre = dt_ref[...] + dtb_ref[...]
    dt = jnp.maximum(pre, 0.0) + jnp.log(1.0 + jnp.exp(-jnp.abs(pre)))
    a = -jnp.exp(alog_ref[...])
    da = dt * a

    row = lax.broadcasted_iota(jnp.int32, (q, q), 0)
    col = lax.broadcasted_iota(jnp.int32, (q, q), 1)
    causal = row >= col
    lo_mask = col < HALF
    tri = jnp.where(causal, 1.0, 0.0).astype(BF16)
    hi, mid, lo = _split3(da)
    cs = _dot(tri, hi) + _dot(tri, mid) + _dot(tri, lo)
    cs_t = cs.T

    for g in range(SSD_N_GROUPS):
        bg = bc[:, g * SSD_D_STATE:(g + 1) * SSD_D_STATE].astype(BF16)
        cg = bc[:, SSD_GN + g * SSD_D_STATE:SSD_GN + (g + 1) * SSD_D_STATE].astype(BF16)
        cb = _dot_nt(cg, bg)
        prev = state[g]
        y_off = _dot(cg, prev.astype(BF16))
        ys, xdte, cds = [], [], []
        ssq = jnp.zeros((q, 1), F32)
        for p in range(4):
            h0 = g * 8 + 2 * p
            c0 = g * SSD_GROUP_COLS + p * LANES
            csx = _pair_expand(cs, h0, lo_mask)
            dtx = _pair_expand(dt, h0, lo_mask)
            xp = xs[:, c0:c0 + LANES]
            xdt = xp * dtx
            xdt16 = xdt.astype(BF16)
            last = csx[q - 1:q, :]
            halves = []
            for hh in range(2):
                h = h0 + hh
                diff = jnp.broadcast_to(cs[:, h:h + 1], (q, q)) - jnp.broadcast_to(cs_t[h:h + 1, :], (q, q))
                decay = jnp.exp(jnp.where(causal, diff, -jnp.inf))
                halves.append(_dot((cb * decay).astype(BF16), xdt16))
            y = jnp.where(lo_mask, halves[0], halves[1])
            y = y + y_off[:, p * LANES:(p + 1) * LANES] * jnp.exp(csx) + xp * dsk_ref[:, c0:c0 + LANES]
            y = y * _silu(z_ref[:, c0:c0 + LANES].astype(F32))
            ssq = ssq + jnp.sum(y * y, -1, keepdims=True)
            ys.append(y)
            xdte.append((xdt * jnp.exp(last - csx)).astype(BF16))
            cds.append(jnp.exp(last))
        s_new = _dot_tn(bg, jnp.concatenate(xdte, axis=1))
        state[g] = prev * jnp.concatenate(cds, axis=1) + s_new
        inv = lax.rsqrt(ssq * (1.0 / SSD_GROUP_COLS) + RMS_EPS)
        for p in range(4):
            c0 = g * SSD_GROUP_COLS + p * LANES
            o_ref[:, c0:c0 + LANES] = (ys[p] * inv * nw_ref[:, c0:c0 + LANES]).astype(o_ref.dtype)


def _ssd_scan(zxbc, dt_raw, conv_w, conv_b, dt_bias, a_log, d_skip, norm_w, bsz, seq):
    t = bsz * seq
    nc = seq // SSD_CHUNK
    q = SSD_CHUNK
    cwx = jnp.pad(conv_w[:, :SSD_D_INNER], ((0, 4), (0, 0)))
    cwbc = jnp.pad(conv_w[:, SSD_D_INNER:], ((0, 4), (0, 0)))
    cbx = conv_b[None, :SSD_D_INNER]
    cbbc = conv_b[None, SSD_D_INNER:]
    pad_h = LANES - SSD_N_HEADS
    dtb = jnp.pad(dt_bias, (0, pad_h))[None, :]
    alog = jnp.pad(a_log, (0, pad_h))[None, :]
    dsk = jnp.repeat(d_skip, SSD_HEAD_DIM)[None, :]
    nw = norm_w[None, :]

    def const(shape):
        return pl.BlockSpec(shape, lambda b, c: (0, 0))

    return pl.pallas_call(
        _ssd_kernel,
        grid=(bsz, nc),
        in_specs=[pl.BlockSpec((q, SSD_D_INNER), lambda b, c: (b * nc + c, 0)),
                  pl.BlockSpec((q, SSD_D_INNER), lambda b, c: (b * nc + c, 1)),
                  pl.BlockSpec((q, 2 * SSD_GN), lambda b, c: (b * nc + c, 4)),
                  pl.BlockSpec((q, LANES), lambda b, c: (b * nc + c, 0)),
                  const((8, SSD_D_INNER)), const((1, SSD_D_INNER)),
                  const((8, 2 * SSD_GN)), const((1, 2 * SSD_GN)),
                  const((1, LANES)), const((1, LANES)),
                  const((1, SSD_D_INNER)), const((1, SSD_D_INNER))],
        out_specs=pl.BlockSpec((q, SSD_D_INNER), lambda b, c: (b * nc + c, 0)),
        out_shape=jax.ShapeDtypeStruct((t, SSD_D_INNER), BF16),
        scratch_shapes=[pltpu.VMEM((q + 8, SSD_D_INNER), F32),
                        pltpu.VMEM((q + 8, 2 * SSD_GN), F32),
                        pltpu.VMEM((SSD_N_GROUPS, SSD_D_STATE, SSD_GROUP_COLS), F32)],
        compiler_params=_params("arbitrary", "arbitrary"),
        name="ssd_scan",
    )(zxbc, zxbc, zxbc, dt_raw, cwx, cbx, cwbc, cbbc, dtb, alog, dsk, nw)


def _proj_ln_kernel(y_ref, w_ref, x_ref, g_ref, b_ref, o_ref, opk_ref):
    mix = _dot(y_ref[...].astype(BF16), w_ref[...])
    out = _layer_norm(DEEPNORM_ALPHA * x_ref[...] + mix, g_ref[...], b_ref[...])
    o_ref[...] = out
    opk_ref[...] = _pack_rows(out)


def _proj_ln(y, w, x, g, b, name):
    t, k = y.shape
    return pl.pallas_call(
        _proj_ln_kernel,
        grid=(t // ROW_TILE,),
        in_specs=[pl.BlockSpec((ROW_TILE, k), lambda i: (i, 0)),
                  _resident((k, D_MODEL)),
                  pl.BlockSpec((ROW_TILE, D_MODEL), lambda i: (i, 0)),
                  pl.BlockSpec((1, D_MODEL), lambda i: (0, 0)),
                  pl.BlockSpec((1, D_MODEL), lambda i: (0, 0))],
        out_specs=[pl.BlockSpec((ROW_TILE, D_MODEL), lambda i: (i, 0)),
                   pl.BlockSpec((ROW_TILE, D_PACK), lambda i: (i, 0))],
        out_shape=[jax.ShapeDtypeStruct((t, D_MODEL), F32),
                   jax.ShapeDtypeStruct((t, D_PACK), U32)],
        compiler_params=_params("parallel"),
        name=name,
    )(y, w, x, g[None, :], b[None, :])


def _rope_table_kernel(pos_ref, freq_ref, c_ref, s1_ref, s2_ref):
    ang = pos_ref[...].astype(F32) * freq_ref[...]
    d = lax.broadcasted_iota(jnp.int32, ang.shape, 1) % ATT_HEAD_DIM
    cos, sin = jnp.cos(ang), jnp.sin(ang)
    half = ROPE_DIM // 2
    c_ref[...] = jnp.where(d < ROPE_DIM, cos, 1.0)
    s1_ref[...] = jnp.where(d < half, -sin, 0.0)
    s2_ref[...] = jnp.where((d >= half) & (d < ROPE_DIM), sin, 0.0)


def _residue_order(v, dil):
    tail = v.shape[1:]
    return v.reshape(-1, ROW_TILE // dil, dil, *tail).swapaxes(1, 2).reshape(-1, *tail)


def _perm_matrix(dil):
    src = _residue_order(jnp.arange(ROW_TILE, dtype=jnp.int32), dil)
    return (src[:, None] == jnp.arange(ROW_TILE, dtype=jnp.int32)[None, :]).astype(BF16)


def _rope_tables(positions):
    t = positions.size
    flat = positions.reshape(t)
    pos_all = jnp.concatenate([_residue_order(flat, dil) for _, dil in ATT_PATTERNS])
    half = ROPE_DIM // 2
    inv_freq = ROPE_THETA ** (-jnp.arange(0, ROPE_DIM, 2, dtype=F32) / ROPE_DIM)
    d = jnp.arange(LANES) % ATT_HEAD_DIM
    freq = jnp.where(d < ROPE_DIM, inv_freq[d % half], 0.0).astype(F32)[None, :]
    n = pos_all.size
    tab = jax.ShapeDtypeStruct((n, LANES), F32)
    tabs = pl.pallas_call(
        _rope_table_kernel,
        grid=(n // ROW_TILE,),
        in_specs=[pl.BlockSpec((ROW_TILE, 1), lambda i: (i, 0)),
                  pl.BlockSpec((1, LANES), lambda i: (0, 0))],
        out_specs=[pl.BlockSpec((ROW_TILE, LANES), lambda i: (i, 0))] * 3,
        out_shape=[tab, tab, tab],
        compiler_params=_params("parallel"),
        name="rope_tables",
    )(pos_all.reshape(n, 1), freq)
    return [tb.reshape(ATT_N_GROUPS, t, LANES) for tb in tabs]


def _qkv_kernel(x_ref, p4_ref, p16_ref, w_ref, c_ref, s1_ref, s2_ref, o_ref):
    x16 = _unpack_rows(x_ref[...]).astype(BF16)
    xs = [x16, _dot(p4_ref[...], x16).astype(BF16), _dot(p16_ref[...], x16).astype(BF16)]
    reps = ATT_OUT_DIM // LANES
    half = ROPE_DIM // 2
    for grp in range(ATT_N_GROUPS):
        c = jnp.concatenate([c_ref[grp]] * reps, axis=1)
        s1 = jnp.concatenate([s1_ref[grp]] * reps, axis=1)
        s2 = jnp.concatenate([s2_ref[grp]] * reps, axis=1)
        for kind in range(3):
            j = grp * 3 + kind
            sl = slice(j * ATT_OUT_DIM, (j + 1) * ATT_OUT_DIM)
            acc = _dot(xs[grp], w_ref[:, sl])
            if kind < 2:
                up = pltpu.roll(acc, ATT_OUT_DIM - half, 1)
                down = pltpu.roll(acc, half, 1)
                acc = acc * c + up * s1 + down * s2
            if kind == 0:
                acc = acc * (ATT_HEAD_DIM ** -0.5)
            o_ref[:, sl] = acc.astype(o_ref.dtype)


def _qkv_proj(xpk, w, tabs, perms):
    t = xpk.shape[0]
    tab_spec = pl.BlockSpec((ATT_N_GROUPS, ROW_TILE, LANES), lambda i: (0, i, 0))
    return pl.pallas_call(
        _qkv_kernel,
        grid=(t // ROW_TILE,),
        in_specs=[pl.BlockSpec((ROW_TILE, D_PACK), lambda i: (i, 0)),
                  _resident((ROW_TILE, ROW_TILE)), _resident((ROW_TILE, ROW_TILE)),
                  _resident((D_MODEL, ATT_QKV_DIM)), tab_spec, tab_spec, tab_spec],
        out_specs=pl.BlockSpec((ROW_TILE, ATT_QKV_DIM), lambda i: (i, 0)),
        out_shape=jax.ShapeDtypeStruct((t, ATT_QKV_DIM), BF16),
        compiler_params=_params("parallel"),
        name="qkv_rope",
    )(xpk, perms[1], perms[2], w, *tabs)


def _attn_kernel(q_ref, kp_ref, kc_ref, vp_ref, vc_ref, o_ref, st_ref):
    w = ATT_BLOCK
    i = pl.program_id(2)
    qi = lax.broadcasted_iota(jnp.int32, (w, 2 * w), 0)
    kk = lax.broadcasted_iota(jnp.int32, (w, 2 * w), 1)
    first_key = jnp.where(i > 0, 0, w)
    valid = (kk >= qi) & (kk <= qi + w) & (kk >= first_key)
    lane = lax.broadcasted_iota(jnp.int32, (w, LANES), 1)
    lo_mask = lane < HALF

    def rows(ref):
        return ref[...].reshape(w, ref.shape[-1])

    q = rows(q_ref)
    k = jnp.concatenate([rows(kp_ref), rows(kc_ref)], axis=0)
    v = jnp.concatenate([rows(vp_ref), rows(vc_ref)], axis=0)
    stats = jnp.zeros((w, LANES), F32)
    zero = jnp.zeros((), q.dtype)
    parts = []
    for p in range(ATT_HEADS // 2):
        sl = slice(p * LANES, (p + 1) * LANES)
        qp, kp, vp = q[:, sl], k[:, sl], v[:, sl]
        outs = []
        for hh in range(2):
            h = 2 * p + hh
            qm = jnp.where(lo_mask if hh == 0 else ~lo_mask, qp, zero)
            s = jnp.where(valid, _dot_nt(qm, kp), NEG_INF)
            m = jnp.max(s, -1, keepdims=True)
            pr = jnp.exp(s - m)
            l = jnp.sum(pr, -1, keepdims=True)
            outs.append(_dot(pr.astype(v.dtype), vp) / l)
            stats = jnp.where(lane == h, m, stats)
            stats = jnp.where(lane == ATT_HEADS + h, l, stats)
        parts.append(jnp.where(lo_mask, outs[0], outs[1]).astype(o_ref.dtype))
    o_ref[...] = jnp.concatenate(parts, axis=1).reshape(o_ref.shape)
    st_ref[...] = stats.reshape(st_ref.shape)


def _window_attention(qkv, grp, bsz, seq):
    _, dil = ATT_PATTERNS[grp]
    w = ATT_BLOCK
    t = bsz * seq
    chunk = ROW_TILE // dil
    tiles = w // chunk if chunk < w else 1
    span = dil * w
    nb = seq // span
    col0 = grp * 3

    if tiles == 1:
        per_b = seq // w
        stride = span // w

        def spec(width, col, prev):
            def imap(b, r, i):
                blk = jnp.maximum(i - 1, 0) if prev else i
                return (b * per_b + blk * stride + r, col)
            return pl.BlockSpec((w, width), imap)

        qkv_v, o_shape, st_shape = qkv, (t, ATT_OUT_DIM), (t, LANES)
    else:
        def spec(width, col, prev):
            def imap(b, r, i):
                blk = jnp.maximum(i - 1, 0) if prev else i
                return (b, blk, 0, r, 0, col)
            return pl.BlockSpec((None, None, tiles, None, chunk, width), imap)

        lead = (bsz, nb, tiles, dil, chunk)
        qkv_v, o_shape, st_shape = qkv.reshape(*lead, ATT_QKV_DIM), (*lead, ATT_OUT_DIM), (*lead, LANES)

    o, st = pl.pallas_call(
        _attn_kernel,
        grid=(bsz, dil, nb),
        in_specs=[spec(ATT_OUT_DIM, col0, False), spec(ATT_OUT_DIM, col0 + 1, True),
                  spec(ATT_OUT_DIM, col0 + 1, False), spec(ATT_OUT_DIM, col0 + 2, True),
                  spec(ATT_OUT_DIM, col0 + 2, False)],
        out_specs=[spec(ATT_OUT_DIM, 0, False), spec(LANES, 0, False)],
        out_shape=[jax.ShapeDtypeStruct(o_shape, BF16), jax.ShapeDtypeStruct(st_shape, F32)],
        compiler_params=_params("parallel", "parallel", "arbitrary"),
        name=f"window_attn_d{dil}",
    )(qkv_v, qkv_v, qkv_v, qkv_v, qkv_v)
    return o.reshape(t, ATT_OUT_DIM), st.reshape(t, LANES)


def _merge_proj_ln_kernel(o1_ref, o2_ref, o3_ref, s1_ref, s2_ref, s3_ref, p4t_ref, p16t_ref, w_ref, x_ref,
                          g_ref, b_ref, o_ref, opk_ref):
    rows = o1_ref.shape[0]
    lane = lax.broadcasted_iota(jnp.int32, (rows, LANES), 1)
    lo_mask = lane < HALF

    def to_token_order(pt, val):
        if val.dtype == BF16:
            return _dot(pt, val)
        return sum(_dot(pt, term) for term in _split3(val))

    p4t, p16t = p4t_ref[...], p16t_ref[...]
    outs = [o1_ref[...].astype(F32), to_token_order(p4t, o2_ref[...]), to_token_order(p16t, o3_ref[...])]
    sts = [s1_ref[...], to_token_order(p4t, s2_ref[...]), to_token_order(p16t, s3_ref[...])]
    mx = jnp.maximum(jnp.maximum(sts[0], sts[1]), sts[2])
    wgts = [pltpu.roll(s, LANES - ATT_HEADS, 1) * jnp.exp(s - mx) for s in sts]
    den = wgts[0] + wgts[1] + wgts[2]
    den = jnp.where(lane < ATT_HEADS, den, 1.0)
    coefs = [wg / den for wg in wgts]
    parts = []
    for p in range(ATT_HEADS // 2):
        sl = slice(p * LANES, (p + 1) * LANES)
        acc = jnp.zeros((rows, LANES), F32)
        for gi in range(ATT_N_GROUPS):
            acc = acc + _pair_expand(coefs[gi], 2 * p, lo_mask) * outs[gi][:, sl]
        parts.append(acc.astype(BF16))
    mix = _dot(jnp.concatenate(parts, axis=1), w_ref[...])
    out = _layer_norm(DEEPNORM_ALPHA * x_ref[...] + mix, g_ref[...], b_ref[...])
    o_ref[...] = out
    opk_ref[...] = _pack_rows(out)


def _merge_proj_ln(os_, sts, perms_t, w, x, g, b):
    t = x.shape[0]
    tm = ROW_TILE
    ospec = pl.BlockSpec((tm, ATT_OUT_DIM), lambda i: (i, 0))
    sspec = pl.BlockSpec((tm, LANES), lambda i: (i, 0))
    xspec = pl.BlockSpec((tm, D_MODEL), lambda i: (i, 0))
    vspec = pl.BlockSpec((1, D_MODEL), lambda i: (0, 0))
    pspec = _resident((tm, tm))
    return pl.pallas_call(
        _merge_proj_ln_kernel,
        grid=(t // tm,),
        in_specs=[ospec] * 3 + [sspec] * 3 + [pspec, pspec, _resident((ATT_OUT_DIM, D_MODEL)),
                                              xspec, vspec, vspec],
        out_specs=[xspec, pl.BlockSpec((tm, D_PACK), lambda i: (i, 0))],
        out_shape=[jax.ShapeDtypeStruct((t, D_MODEL), F32), jax.ShapeDtypeStruct((t, D_PACK), U32)],
        compiler_params=_params("parallel"),
        name="attn_merge_proj_ln",
    )(*os_, *sts, perms_t[1], perms_t[2], w, x, g[None, :], b[None, :])


def _router_kernel(x_ref, whi_ref, wlo_ref, b_ref, o_ref, cnt_ref, carry):
    @pl.when(pl.program_id(0) == 0)
    def _():
        carry[...] = jnp.zeros(carry.shape, F32)

    x = x_ref[...]
    xhi = x.astype(BF16)
    xlo = (x - xhi.astype(F32)).astype(BF16)
    logits = _dot(xhi, whi_ref[...]) + _dot(xlo, whi_ref[...]) + _dot(xhi, wlo_ref[...]) + b_ref[...]
    rows = logits.shape[0]
    lane = lax.broadcasted_iota(jnp.int32, (rows, LANES), 1)
    big = jnp.int32(LANES)

    def top1(vals, mask):
        v = jnp.where(mask, vals, -jnp.inf)
        m = jnp.max(v, -1, keepdims=True)
        idx = jnp.min(jnp.where(v == m, lane, big), -1, keepdims=True)
        return v, m, idx

    gmask = lane < MOE_N_GROUPS
    gv, gm, gidx = top1(logits, gmask)
    g_w = 1.0 / jnp.sum(jnp.exp(gv - gm), -1, keepdims=True)
    e_lo = MOE_N_GROUPS + gidx * MOE_EPG
    emask = (lane >= e_lo) & (lane < e_lo + MOE_EPG)
    ev, m1, i1 = top1(logits, emask)
    zsum = jnp.sum(jnp.exp(ev - m1), -1, keepdims=True)
    _, m2, i2 = top1(logits, emask & (lane != i1))
    p1 = 1.0 / zsum
    p2 = jnp.exp(m2 - m1) / zsum
    tot = p1 + p2
    e1 = i1 - MOE_N_GROUPS
    e2 = i2 - MOE_N_GROUPS

    oh1 = jnp.where(lane == e1, 1.0, 0.0)
    oh2 = jnp.where(lane == e2, 1.0, 0.0)
    oh = oh1 + oh2
    ri = lax.broadcasted_iota(jnp.int32, (rows, rows), 0)
    ci = lax.broadcasted_iota(jnp.int32, (rows, rows), 1)
    strict = jnp.where(ri > ci, 1.0, 0.0).astype(BF16)
    before = _dot(strict, oh.astype(BF16)) + carry[...]
    rank1 = jnp.sum(oh1 * before, -1, keepdims=True)
    rank2 = jnp.sum(oh2 * before, -1, keepdims=True)
    carry[...] = carry[...] + jnp.sum(oh, 0, keepdims=True)
    cnt_ref[...] = carry[...]

    vals = [e1.astype(F32), e2.astype(F32), g_w * (p1 / tot), g_w * (p2 / tot), rank1, rank2]
    out = jnp.zeros((rows, LANES), F32)
    for j, val in enumerate(vals):
        out = jnp.where(lane == j, val, out)
    o_ref[...] = out


def _router(x, w_rg, b_rg, w_re, b_re):
    t = x.shape[0]
    n_log = MOE_N_GROUPS + MOE_N_EXPERTS
    w = jnp.pad(jnp.concatenate([w_rg, w_re], axis=1), ((0, 0), (0, LANES - n_log)))
    whi = w.astype(BF16)
    wlo = (w - whi.astype(F32)).astype(BF16)
    bias = jnp.pad(jnp.concatenate([b_rg, b_re]), (0, LANES - n_log))[None, :]
    wspec = pl.BlockSpec((D_MODEL, LANES), lambda i: (0, 0))
    return pl.pallas_call(
        _router_kernel,
        grid=(t // ROW_TILE,),
        in_specs=[pl.BlockSpec((ROW_TILE, D_MODEL), lambda i: (i, 0)), wspec, wspec,
                  pl.BlockSpec((1, LANES), lambda i: (0, 0))],
        out_specs=[pl.BlockSpec((ROW_TILE, LANES), lambda i: (i, 0)),
                   pl.BlockSpec((1, LANES), lambda i: (0, 0))],
        out_shape=[jax.ShapeDtypeStruct((t, LANES), F32), jax.ShapeDtypeStruct((1, LANES), F32)],
        scratch_shapes=[pltpu.VMEM((1, LANES), F32)],
        compiler_params=_params("arbitrary"),
        name="moe_router",
    )(x, whi, wlo, bias)


def _expert_kernel(meta_ref, x_ref, wg_ref, wu_ref, wd_ref, o_ref, wg16, wu16, wd16):
    i = pl.program_id(0)
    n_blocks = pl.num_programs(0)

    @pl.when(i < meta_ref[n_blocks])
    def _():
        @pl.when((i == 0) | (meta_ref[i] != meta_ref[jnp.maximum(i - 1, 0)]))
        def _():
            wg16[...] = wg_ref[...].astype(BF16)
            wu16[...] = wu_ref[...].astype(BF16)
            wd16[...] = wd_ref[...].astype(BF16)

        x = _unpack_rows(x_ref[...]).astype(BF16)
        h = _silu(_dot(x, wg16[...])) * _dot(x, wu16[...])
        o_ref[...] = _pack_rows(_dot(h.astype(BF16), wd16[...]))


def _expert_mlp(meta, x_rows, w_gate, w_up, w_down, layer):
    n_rows = x_rows.shape[0]
    n_blocks = n_rows // MOE_ROW_BLOCK
    tb = MOE_ROW_BLOCK
    grid_spec = pltpu.PrefetchScalarGridSpec(
        num_scalar_prefetch=1,
        grid=(n_blocks,),
        in_specs=[pl.BlockSpec((tb, D_PACK), lambda i, meta: (i, 0)),
                  pl.BlockSpec((None, None, D_MODEL, MOE_HIDDEN), lambda i, meta: (layer, meta[i], 0, 0)),
                  pl.BlockSpec((None, None, D_MODEL, MOE_HIDDEN), lambda i, meta: (layer, meta[i], 0, 0)),
                  pl.BlockSpec((None, None, MOE_HIDDEN, D_MODEL), lambda i, meta: (layer, meta[i], 0, 0))],
        out_specs=pl.BlockSpec((tb, D_PACK), lambda i, meta: (i, 0)),
        scratch_shapes=[pltpu.VMEM((D_MODEL, MOE_HIDDEN), BF16), pltpu.VMEM((D_MODEL, MOE_HIDDEN), BF16),
                        pltpu.VMEM((MOE_HIDDEN, D_MODEL), BF16)],
    )
    return pl.pallas_call(
        _expert_kernel,
        grid_spec=grid_spec,
        out_shape=jax.ShapeDtypeStruct((n_rows, D_PACK), U32),
        compiler_params=_params("arbitrary"),
        name="moe_experts",
    )(meta, x_rows, w_gate, w_up, w_down)


def _combine_ln_kernel(y0_ref, y1_ref, r_ref, x_ref, g_ref, b_ref, o_ref, opk_ref):
    route = r_ref[...]
    g0 = route[:, 2:3]
    g1 = route[:, 3:4]
    ffn = g0 * _unpack_rows(y0_ref[...]) + g1 * _unpack_rows(y1_ref[...])
    out = _layer_norm(DEEPNORM_ALPHA * x_ref[...] + ffn, g_ref[...], b_ref[...])
    o_ref[...] = out
    opk_ref[...] = _pack_rows(out)


def _combine_ln(y0, y1, route, x, g, b):
    t = x.shape[0]
    xspec = pl.BlockSpec((ROW_TILE, D_MODEL), lambda i: (i, 0))
    pspec = pl.BlockSpec((ROW_TILE, D_PACK), lambda i: (i, 0))
    vspec = pl.BlockSpec((1, D_MODEL), lambda i: (0, 0))
    return pl.pallas_call(
        _combine_ln_kernel,
        grid=(t // ROW_TILE,),
        in_specs=[pspec, pspec, pl.BlockSpec((ROW_TILE, LANES), lambda i: (i, 0)), xspec, vspec, vspec],
        out_specs=[xspec, pspec],
        out_shape=[jax.ShapeDtypeStruct((t, D_MODEL), F32), jax.ShapeDtypeStruct((t, D_PACK), U32)],
        compiler_params=_params("parallel"),
        name="moe_combine_ln",
    )(y0, y1, route, x, g[None, :], b[None, :])


def _positions_kernel(r_ref, ps_ref, o_ref):
    route = r_ref[...]
    lane = lax.broadcasted_iota(jnp.int32, route.shape, 1)
    starts = ps_ref[...]
    out = jnp.zeros(route.shape, F32)
    for k in range(MOE_TOP_K):
        eid = route[:, k:k + 1].astype(jnp.int32)
        pos = jnp.sum(jnp.where(lane == eid, starts, 0.0), -1, keepdims=True) + route[:, 4 + k:5 + k]
        out = jnp.where(lane == k, pos, out)
    o_ref[...] = out.astype(jnp.int32)


def _positions(route, pad_start):
    t = route.shape[0]
    starts = jnp.pad(pad_start.astype(F32), (0, LANES - MOE_N_EXPERTS))[None, :]
    return pl.pallas_call(
        _positions_kernel,
        grid=(t // ROW_TILE,),
        in_specs=[pl.BlockSpec((ROW_TILE, LANES), lambda i: (i, 0)), pl.BlockSpec((1, LANES), lambda i: (0, 0))],
        out_specs=pl.BlockSpec((ROW_TILE, LANES), lambda i: (i, 0)),
        out_shape=jax.ShapeDtypeStruct((t, LANES), jnp.int32),
        compiler_params=_params("parallel"),
        name="moe_positions",
    )(route, starts)


def _moe(x, xpk, w_rg, b_rg, w_re, b_re, w_gate, w_up, w_down, layer, g, b):
    t = x.shape[0]
    tb = MOE_ROW_BLOCK
    route, cnt = _router(x, w_rg, b_rg, w_re, b_re)
    n_assign = t * MOE_TOP_K
    n_blocks = n_assign // tb + MOE_N_EXPERTS
    n_rows = n_blocks * tb
    counts = cnt[0, :MOE_N_EXPERTS].astype(jnp.int32)
    padded = (counts + tb - 1) // tb * tb
    pad_end = jnp.cumsum(padded)
    pos = _positions(route, pad_end - padded)
    pos0, pos1 = pos[:, 0], pos[:, 1]
    tok = jnp.arange(t, dtype=jnp.int32)
    row_tok = (jnp.arange(n_rows, dtype=jnp.int32) % t).at[jnp.concatenate([pos0, pos1])].set(
        jnp.concatenate([tok, tok]), unique_indices=True)
    block_start = jnp.arange(n_blocks, dtype=jnp.int32) * tb
    block_e = jnp.minimum(jnp.sum((pad_end[None, :] <= block_start[:, None]).astype(jnp.int32), -1),
                          MOE_N_EXPERTS - 1)
    meta = jnp.concatenate([block_e, pad_end[-1:] // tb]).astype(jnp.int32)
    x_rows = _row_gather(xpk, row_tok)
    y_rows = _expert_mlp(meta, x_rows, w_gate, w_up, w_down, layer)
    y0 = _row_gather(y_rows, pos0)
    y1 = _row_gather(y_rows, pos1)
    return _combine_ln(y0, y1, route, x, g, b)


def _ssd_layer(x, xpk, w_in, conv_w, conv_b, dt_bias, a_log, d_skip, norm_w, w_out, g, b, bsz, seq):
    w_zxbc = w_in[:, :SSD_D_INNER + SSD_CONV_DIM].astype(BF16)
    w_dt = jnp.pad(w_in[:, SSD_D_INNER + SSD_CONV_DIM:], ((0, 0), (0, LANES - SSD_N_HEADS))).astype(BF16)
    zxbc, dt_raw = _in_proj(xpk, w_zxbc, w_dt)
    y = _ssd_scan(zxbc, dt_raw, conv_w, conv_b, dt_bias, a_log, d_skip, norm_w, bsz, seq)
    return _proj_ln(y, w_out.astype(BF16), x, g, b, "ssd_out_proj_ln")


def _attn_layer(x, xpk, tabs, w_qkv, w_o, g, b, bsz, seq):
    perms = [_perm_matrix(dil) for _, dil in ATT_PATTERNS]
    perms_t = [p.T for p in perms]
    qkv = _qkv_proj(xpk, w_qkv.astype(BF16), tabs, perms)
    os_, sts = [], []
    for grp in range(ATT_N_GROUPS):
        o, st = _window_attention(qkv, grp, bsz, seq)
        os_.append(o)
        sts.append(st)
    return _merge_proj_ln(os_, sts, perms_t, w_o.astype(BF16), x, g, b)


def kernel(x, positions, ssd_w_in, ssd_conv_w, ssd_conv_b, ssd_dt_bias, ssd_a_log, ssd_d, ssd_norm_w, ssd_w_out,
           attn_w_qkv, attn_w_o, ln_g, ln_b, moe_w_router_group, moe_b_router_group, moe_w_router_expert,
           moe_b_router_expert, moe_w_gate, moe_w_up, moe_w_down):
    bsz, seq, d = x.shape
    t = bsz * seq
    h = x.reshape(t, d)
    hpk = _pack_rows_host(h)
    tabs = _rope_tables(positions)
    for i in range(DEPTH):
        j = i // N_MIXERS
        if i % N_MIXERS == 0:
            h, hpk = _ssd_layer(h, hpk, ssd_w_in[j], ssd_conv_w[j], ssd_conv_b[j], ssd_dt_bias[j], ssd_a_log[j],
                                ssd_d[j], ssd_norm_w[j], ssd_w_out[j], ln_g[i, 0], ln_b[i, 0], bsz, seq)
        else:
            h, hpk = _attn_layer(h, hpk, tabs, attn_w_qkv[j], attn_w_o[j], ln_g[i, 0], ln_b[i, 0], bsz, seq)
        h, hpk = _moe(h, hpk, moe_w_router_group[i], moe_b_router_group[i], moe_w_router_expert[i],
                      moe_b_router_expert[i], moe_w_gate, moe_w_up, moe_w_down, i, ln_g[i, 1], ln_b[i, 1])
    return h.reshape(bsz, seq, d)
```

```python
import functools

import jax
import jax.numpy as jnp
from jax import lax
from jax.experimental import pallas as pl
from jax.experimental.pallas import tpu as pltpu
from jax.experimental.pallas import tpu_sc as plsc

F32 = jnp.float32
BF16 = jnp.bfloat16
U32 = jnp.uint32

D_MODEL = 1024
D_PACK = D_MODEL // 2
DEPTH = 4
N_MIXERS = 2

SSD_D_INNER = 2048
SSD_HEAD_DIM = 64
SSD_N_HEADS = 32
SSD_N_GROUPS = 4
SSD_D_STATE = 128
SSD_D_CONV = 4
SSD_CHUNK = 128
SSD_GN = SSD_N_GROUPS * SSD_D_STATE
SSD_CONV_DIM = SSD_D_INNER + 2 * SSD_GN
SSD_GROUP_COLS = SSD_D_INNER // SSD_N_GROUPS
IN_PROJ_COLS = 1024

ATT_HEAD_DIM = 64
ATT_HEADS = 8
ATT_PATTERNS = ((128, 1), (512, 4), (2048, 16))
ATT_N_GROUPS = 3
ATT_OUT_DIM = ATT_HEADS * ATT_HEAD_DIM
ATT_QKV_DIM = ATT_N_GROUPS * 3 * ATT_OUT_DIM
ATT_BLOCK = 128
ROPE_THETA = 500000.0
ROPE_DIM = 16

MOE_N_GROUPS = 4
MOE_EPG = 8
MOE_N_EXPERTS = 32
MOE_TOP_K = 2
MOE_HIDDEN = 512
MOE_ROW_BLOCK = 512

DEEPNORM_ALPHA = (2 * DEPTH) ** 0.25
LN_EPS = 1e-5
RMS_EPS = 1e-5
NEG_INF = -1e30

LANES = 128
HALF = LANES // 2
VMEM_LIMIT = 56 * 1024 * 1024

ROW_TILE = 512


def _params(*sem):
    return pltpu.CompilerParams(dimension_semantics=sem, vmem_limit_bytes=VMEM_LIMIT)


def _silu(v):
    h = 0.5 * v
    return h + h * jnp.tanh(h)


def _layer_norm(r, g, b):
    mu = jnp.mean(r, -1, keepdims=True)
    d = r - mu
    var = jnp.mean(d * d, -1, keepdims=True)
    return d * lax.rsqrt(var + LN_EPS) * g + b


def _split3(v):
    hi = v.astype(BF16)
    r1 = v - hi.astype(F32)
    mid = r1.astype(BF16)
    lo = (r1 - mid.astype(F32)).astype(BF16)
    return hi, mid, lo


def _dot(a, b):
    return jnp.dot(a, b, preferred_element_type=F32)


def _dot_nt(a, b):
    return lax.dot_general(a, b, (((1,), (1,)), ((), ())), preferred_element_type=F32)


def _dot_tn(a, b):
    return lax.dot_general(a, b, (((0,), (0,)), ((), ())), preferred_element_type=F32)


def _pair_expand(mat, h0, lo_mask):
    rows = mat.shape[0]
    a = jnp.broadcast_to(mat[:, h0:h0 + 1], (rows, LANES))
    b = jnp.broadcast_to(mat[:, h0 + 1:h0 + 2], (rows, LANES))
    return jnp.where(lo_mask, a, b)


def _round_bf16_bits(bits):
    odd = (bits >> 16) & jnp.uint32(1)
    return (bits + jnp.uint32(0x7FFF) + odd) & jnp.uint32(0xFFFF0000)


def _pack_words(bits):
    r = _round_bf16_bits(bits)
    return r[:, :D_PACK] | (r[:, D_PACK:] >> 16)


def _pack_rows(v):
    return _pack_words(pltpu.bitcast(v, U32))


def _unpack_rows(p):
    hi = pltpu.bitcast(p & jnp.uint32(0xFFFF0000), F32)
    lo = pltpu.bitcast(p << 16, F32)
    return jnp.concatenate([hi, lo], axis=1)


def _pack_rows_host(v):
    return _pack_words(lax.bitcast_convert_type(v, U32))


def _row_gather(data, idx):
    n = idx.shape[0]
    d = data.shape[1]
    window = LANES
    dc = d // 2
    mesh = plsc.VectorSubcoreMesh(core_axis_name="core", subcore_axis_name="subcore")

    @functools.partial(pl.kernel, out_type=jax.ShapeDtypeStruct((n, d), data.dtype), mesh=mesh)
    def gather(x_hbm, i_hbm, o_hbm):
        for c in range(d // dc):
            def body(i_vmem, o_vmem, c=c):
                pltpu.sync_copy(x_hbm.at[i_vmem.at[0], pl.ds(c * dc, dc)], o_vmem)

            pltpu.emit_pipeline(
                body,
                grid=(n // window,),
                in_specs=[pl.BlockSpec((1, window), lambda i: (0, i))],
                out_specs=[pl.BlockSpec((window, dc), lambda i, c=c: (i, c))],
                core_axis_name=("core", "subcore"),
                dimension_semantics=(pltpu.PARALLEL,),
            )(i_hbm, o_hbm)

    return gather(data, idx.reshape(1, n))


def _row_scatter(data, idxs, n_rows):
    t, d = data.shape
    window = LANES
    dc = d // 2
    mesh = plsc.VectorSubcoreMesh(core_axis_name="core", subcore_axis_name="subcore")

    @functools.partial(pl.kernel, out_type=jax.ShapeDtypeStruct((n_rows, d), data.dtype), mesh=mesh)
    def scatter(x_hbm, *refs):
        o_hbm = refs[-1]
        for i_hbm in refs[:-1]:
            for c in range(d // dc):
                def body(x_vmem, i_vmem, c=c):
                    pltpu.sync_copy(x_vmem, o_hbm.at[i_vmem.at[0], pl.ds(c * dc, dc)])

                pltpu.emit_pipeline(
                    body,
                    grid=(t // window,),
                    in_specs=[pl.BlockSpec((window, dc), lambda i, c=c: (i, c)),
                              pl.BlockSpec((1, window), lambda i: (0, i))],
                    out_specs=[],
                    core_axis_name=("core", "subcore"),
                    dimension_semantics=(pltpu.PARALLEL,),
                )(x_hbm, i_hbm)

    return scatter(data, *[i.reshape(1, t) for i in idxs])


def _resident(shape):
    return pl.BlockSpec(shape, lambda *_: (0,) * len(shape), pipeline_mode=pl.Buffered(1))


def _in_proj_kernel(x_ref, w_ref, wdt_ref, cw_ref, cb_ref, o_ref, dt_ref, ext, tail):
    rows = x_ref.shape[0]
    tn = IN_PROJ_COLS
    n_z = SSD_D_INNER // tn

    @pl.when(pl.program_id(1) == 0)
    def _():
        tail[...] = jnp.zeros(tail.shape, F32)

    x = _unpack_rows(x_ref[...]).astype(BF16)
    for n in range(n_z):
        sl = slice(n * tn, (n + 1) * tn)
        o_ref[:, sl] = _dot(x, w_ref[:, sl]).astype(o_ref.dtype)
    for j in range(SSD_CONV_DIM // tn):
        sl = slice((n_z + j) * tn, (n_z + j + 1) * tn)
        cl = slice(j * tn, (j + 1) * tn)
        ext[j, 0:8, :] = tail[j]
        ext[j, 8:8 + rows, :] = _dot(x, w_ref[:, sl])
        tail[j] = ext[j, rows:rows + 8, :]
        acc = ext[j, 8:8 + rows, :] * cw_ref[3:4, cl] + cb_ref[:, cl]
        for k in range(SSD_D_CONV - 1):
            acc = acc + ext[j, 5 + k:5 + k + rows, :] * cw_ref[k:k + 1, cl]
        o_ref[:, sl] = _silu(acc).astype(o_ref.dtype)
    dt_ref[...] = _dot(x, wdt_ref[...])


def _in_proj(xpk, w_zxbc, w_dt, conv_w, conv_b, bsz, seq):
    t = xpk.shape[0]
    n = w_zxbc.shape[1]
    tiles = seq // ROW_TILE
    cw = jnp.pad(conv_w, ((0, 8 - SSD_D_CONV), (0, 0)))
    n_conv = SSD_CONV_DIM // IN_PROJ_COLS
    return pl.pallas_call(
        _in_proj_kernel,
        grid=(bsz, tiles),
        in_specs=[pl.BlockSpec((ROW_TILE, D_PACK), lambda b, s: (b * tiles + s, 0)),
                  _resident((D_MODEL, n)), _resident((D_MODEL, LANES)),
                  _resident((8, SSD_CONV_DIM)), _resident((1, SSD_CONV_DIM))],
        out_specs=[pl.BlockSpec((ROW_TILE, n), lambda b, s: (b * tiles + s, 0)),
                   pl.BlockSpec((ROW_TILE, LANES), lambda b, s: (b * tiles + s, 0))],
        out_shape=[jax.ShapeDtypeStruct((t, n), BF16), jax.ShapeDtypeStruct((t, LANES), F32)],
        scratch_shapes=[pltpu.VMEM((n_conv, ROW_TILE + 8, IN_PROJ_COLS), F32),
                        pltpu.VMEM((n_conv, 8, IN_PROJ_COLS), F32)],
        compiler_params=_params("arbitrary", "arbitrary"),
        name="ssd_in_proj",
    )(xpk, w_zxbc, w_dt, cw, conv_b[None, :])


def _ssd_kernel(z_ref, xs_ref, bc_ref, dt_ref, dtb_ref, alog_ref, dsk_ref, nw_ref, o_ref, state):
    q = SSD_CHUNK

    @pl.when(pl.program_id(1) == 0)
    def _():
        state[...] = jnp.zeros(state.shape, F32)

    xs = xs_ref[...].astype(F32)
    bc = bc_ref[...]

    pre = dt_ref[...] + dtb_ref[...]
    dt = jnp.maximum(pre, 0.0) + jnp.log(1.0 + jnp.exp(-jnp.abs(pre)))
    a = -jnp.exp(alog_ref[...])
    da = dt * a

    row = lax.broadcasted_iota(jnp.int32, (q, q), 0)
    col = lax.broadcasted_iota(jnp.int32, (q, q), 1)
    causal = row >= col
    lo_mask = col < HALF
    tri = jnp.where(causal, 1.0, 0.0).astype(BF16)
    hi, mid, lo = _split3(da)
    cs = _dot(tri, hi) + _dot(tri, mid) + _dot(tri, lo)
    cs_t = cs.T

    for g in range(SSD_N_GROUPS):
        bg = bc[:, g * SSD_D_STATE:(g + 1) * SSD_D_STATE]
        cg = bc[:, SSD_GN + g * SSD_D_STATE:SSD_GN + (g + 1) * SSD_D_STATE]
        cb = _dot_nt(cg, bg)
        prev = state[g]
        y_off = _dot(cg, prev.astype(BF16))
        ys, xdte, cds = [], [], []
        ssq = jnp.zeros((q, 1), F32)
        for p in range(4):
            h0 = g * 8 + 2 * p
            c0 = g * SSD_GROUP_COLS + p * LANES
            csx = _pair_expand(cs, h0, lo_mask)
            dtx = _pair_expand(dt, h0, lo_mask)
            xp = xs[:, c0:c0 + LANES]
            xdt = xp * dtx
            xdt16 = xdt.astype(BF16)
            last = csx[q - 1:q, :]
            halves = []
            for hh in range(2):
                h = h0 + hh
                diff = jnp.broadcast_to(cs[:, h:h + 1], (q, q)) - jnp.broadcast_to(cs_t[h:h + 1, :], (q, q))
                decay = jnp.exp(jnp.where(causal, diff, -jnp.inf))
                halves.append(_dot((cb * decay).astype(BF16), xdt16))
            y = jnp.where(lo_mask, halves[0], halves[1])
            y = y + y_off[:, p * LANES:(p + 1) * LANES] * jnp.exp(csx) + xp * dsk_ref[:, c0:c0 + LANES]
            y = y * _silu(z_ref[:, c0:c0 + LANES].astype(F32))
            ssq = ssq + jnp.sum(y * y, -1, keepdims=True)
            ys.append(y)
            xdte.append((xdt * jnp.exp(last - csx)).astype(BF16))
            cds.append(jnp.exp(last))
        s_new = _dot_tn(bg, jnp.concatenate(xdte, axis=1))
        state[g] = prev * jnp.concatenate(cds, axis=1) + s_new
        inv = lax.rsqrt(ssq * (1.0 / SSD_GROUP_COLS) + RMS_EPS)
        for p in range(4):
            c0 = g * SSD_GROUP_COLS + p * LANES
            o_ref[:, c0:c0 + LANES] = (ys[p] * inv * nw_ref[:, c0:c0 + LANES]).astype(o_ref.dtype)


def _ssd_scan(zxbc, dt_raw, dt_bias, a_log, d_skip, norm_w, bsz, seq):
    t = bsz * seq
    nc = seq // SSD_CHUNK
    q = SSD_CHUNK
    pad_h = LANES - SSD_N_HEADS
    dtb = jnp.pad(dt_bias, (0, pad_h))[None, :]
    alog = jnp.pad(a_log, (0, pad_h))[None, :]
    dsk = jnp.repeat(d_skip, SSD_HEAD_DIM)[None, :]
    nw = norm_w[None, :]

    def const(shape):
        return pl.BlockSpec(shape, lambda b, c: (0, 0))

    return pl.pallas_call(
        _ssd_kernel,
        grid=(bsz, nc),
        in_specs=[pl.BlockSpec((q, SSD_D_INNER), lambda b, c: (b * nc + c, 0)),
                  pl.BlockSpec((q, SSD_D_INNER), lambda b, c: (b * nc + c, 1)),
                  pl.BlockSpec((q, 2 * SSD_GN), lambda b, c: (b * nc + c, 4)),
                  pl.BlockSpec((q, LANES), lambda b, c: (b * nc + c, 0)),
                  const((1, LANES)), const((1, LANES)),
                  const((1, SSD_D_INNER)), const((1, SSD_D_INNER))],
        out_specs=pl.BlockSpec((q, SSD_D_INNER), lambda b, c: (b * nc + c, 0)),
        out_shape=jax.ShapeDtypeStruct((t, SSD_D_INNER), BF16),
        scratch_shapes=[pltpu.VMEM((SSD_N_GROUPS, SSD_D_STATE, SSD_GROUP_COLS), F32)],
        compiler_params=_params("arbitrary", "arbitrary"),
        name="ssd_scan",
    )(zxbc, zxbc, zxbc, dt_raw, dtb, alog, dsk, nw)


def _proj_ln_kernel(y_ref, w_ref, x_ref, g_ref, b_ref, o_ref, opk_ref):
    mix = _dot(y_ref[...].astype(BF16), w_ref[...])
    out = _layer_norm(DEEPNORM_ALPHA * x_ref[...] + mix, g_ref[...], b_ref[...])
    o_ref[...] = out
    opk_ref[...] = _pack_rows(out)


def _proj_ln(y, w, x, g, b, name):
    t, k = y.shape
    return pl.pallas_call(
        _proj_ln_kernel,
        grid=(t // ROW_TILE,),
        in_specs=[pl.BlockSpec((ROW_TILE, k), lambda i: (i, 0)),
                  _resident((k, D_MODEL)),
                  pl.BlockSpec((ROW_TILE, D_MODEL), lambda i: (i, 0)),
                  pl.BlockSpec((1, D_MODEL), lambda i: (0, 0)),
                  pl.BlockSpec((1, D_MODEL), lambda i: (0, 0))],
        out_specs=[pl.BlockSpec((ROW_TILE, D_MODEL), lambda i: (i, 0)),
                   pl.BlockSpec((ROW_TILE, D_PACK), lambda i: (i, 0))],
        out_shape=[jax.ShapeDtypeStruct((t, D_MODEL), F32),
                   jax.ShapeDtypeStruct((t, D_PACK), U32)],
        compiler_params=_params("parallel"),
        name=name,
    )(y, w, x, g[None, :], b[None, :])


def _rope_table_kernel(pos_ref, freq_ref, c_ref, s1_ref, s2_ref):
    ang = pos_ref[...].astype(F32) * freq_ref[...]
    d = lax.broadcasted_iota(jnp.int32, ang.shape, 1) % ATT_HEAD_DIM
    cos, sin = jnp.cos(ang), jnp.sin(ang)
    half = ROPE_DIM // 2
    c_ref[...] = jnp.where(d < ROPE_DIM, cos, 1.0)
    s1_ref[...] = jnp.where(d < half, -sin, 0.0)
    s2_ref[...] = jnp.where((d >= half) & (d < ROPE_DIM), sin, 0.0)


def _residue_order(v, dil):
    tail = v.shape[1:]
    return v.reshape(-1, ROW_TILE // dil, dil, *tail).swapaxes(1, 2).reshape(-1, *tail)


def _perm_matrix(dil):
    src = _residue_order(jnp.arange(ROW_TILE, dtype=jnp.int32), dil)
    return (src[:, None] == jnp.arange(ROW_TILE, dtype=jnp.int32)[None, :]).astype(BF16)


def _rope_tables(positions):
    t = positions.size
    flat = positions.reshape(t)
    pos_all = jnp.concatenate([_residue_order(flat, dil) for _, dil in ATT_PATTERNS])
    half = ROPE_DIM // 2
    inv_freq = ROPE_THETA ** (-jnp.arange(0, ROPE_DIM, 2, dtype=F32) / ROPE_DIM)
    d = jnp.arange(LANES) % ATT_HEAD_DIM
    freq = jnp.where(d < ROPE_DIM, inv_freq[d % half], 0.0).astype(F32)[None, :]
    n = pos_all.size
    tab = jax.ShapeDtypeStruct((n, LANES), F32)
    tabs = pl.pallas_call(
        _rope_table_kernel,
        grid=(n // ROW_TILE,),
        in_specs=[pl.BlockSpec((ROW_TILE, 1), lambda i: (i, 0)),
                  pl.BlockSpec((1, LANES), lambda i: (0, 0))],
        out_specs=[pl.BlockSpec((ROW_TILE, LANES), lambda i: (i, 0))] * 3,
        out_shape=[tab, tab, tab],
        compiler_params=_params("parallel"),
        name="rope_tables",
    )(pos_all.reshape(n, 1), freq)
    return [tb.reshape(ATT_N_GROUPS, t, LANES) for tb in tabs]


def _qkv_kernel(x_ref, p4_ref, p16_ref, w_ref, c_ref, s1_ref, s2_ref, o_ref):
    x16 = _unpack_rows(x_ref[...]).astype(BF16)
    xs = [x16, _dot(p4_ref[...], x16).astype(BF16), _dot(p16_ref[...], x16).astype(BF16)]
    reps = ATT_OUT_DIM // LANES
    half = ROPE_DIM // 2
    for grp in range(ATT_N_GROUPS):
        c = jnp.concatenate([c_ref[grp]] * reps, axis=1)
        s1 = jnp.concatenate([s1_ref[grp]] * reps, axis=1)
        s2 = jnp.concatenate([s2_ref[grp]] * reps, axis=1)
        for kind in range(3):
            j = grp * 3 + kind
            sl = slice(j * ATT_OUT_DIM, (j + 1) * ATT_OUT_DIM)
            acc = _dot(xs[grp], w_ref[:, sl])
            if kind < 2:
                up = pltpu.roll(acc, ATT_OUT_DIM - half, 1)
                down = pltpu.roll(acc, half, 1)
                acc = acc * c + up * s1 + down * s2
            if kind == 0:
                acc = acc * (ATT_HEAD_DIM ** -0.5)
            o_ref[:, sl] = acc.astype(o_ref.dtype)


def _qkv_proj(xpk, w, tabs, perms):
    t = xpk.shape[0]
    tab_spec = pl.BlockSpec((ATT_N_GROUPS, ROW_TILE, LANES), lambda i: (0, i, 0))
    return pl.pallas_call(
        _qkv_kernel,
        grid=(t // ROW_TILE,),
        in_specs=[pl.BlockSpec((ROW_TILE, D_PACK), lambda i: (i, 0)),
                  _resident((ROW_TILE, ROW_TILE)), _resident((ROW_TILE, ROW_TILE)),
                  _resident((D_MODEL, ATT_QKV_DIM)), tab_spec, tab_spec, tab_spec],
        out_specs=pl.BlockSpec((ROW_TILE, ATT_QKV_DIM), lambda i: (i, 0)),
        out_shape=jax.ShapeDtypeStruct((t, ATT_QKV_DIM), BF16),
        compiler_params=_params("parallel"),
        name="qkv_rope",
    )(xpk, perms[1], perms[2], w, *tabs)


def _attn_kernel(q_ref, kp_ref, kc_ref, vp_ref, vc_ref, o_ref, st_ref):
    w = ATT_BLOCK
    i = pl.program_id(2)
    qi = lax.broadcasted_iota(jnp.int32, (w, 2 * w), 0)
    kk = lax.broadcasted_iota(jnp.int32, (w, 2 * w), 1)
    first_key = jnp.where(i > 0, 0, w)
    valid = (kk >= qi) & (kk <= qi + w) & (kk >= first_key)
    lane = lax.broadcasted_iota(jnp.int32, (w, LANES), 1)
    lo_mask = lane < HALF

    def rows(ref):
        return ref[...].reshape(w, ref.shape[-1])

    q = rows(q_ref)
    k = jnp.concatenate([rows(kp_ref), rows(kc_ref)], axis=0)
    v = jnp.concatenate([rows(vp_ref), rows(vc_ref)], axis=0)
    stats = jnp.zeros((w, LANES), F32)
    zero = jnp.zeros((), q.dtype)
    parts = []
    for p in range(ATT_HEADS // 2):
        sl = slice(p * LANES, (p + 1) * LANES)
        qp, kp, vp = q[:, sl], k[:, sl], v[:, sl]
        outs = []
        for hh in range(2):
            h = 2 * p + hh
            qm = jnp.where(lo_mask if hh == 0 else ~lo_mask, qp, zero)
            s = jnp.where(valid, _dot_nt(qm, kp), NEG_INF)
            m = jnp.max(s, -1, keepdims=True)
            pr = jnp.exp(s - m)
            l = jnp.sum(pr, -1, keepdims=True)
            outs.append(_dot(pr.astype(v.dtype), vp) / l)
            stats = jnp.where(lane == h, m, stats)
            stats = jnp.where(lane == ATT_HEADS + h, l, stats)
        parts.append(jnp.where(lo_mask, outs[0], outs[1]).astype(o_ref.dtype))
    o_ref[...] = jnp.concatenate(parts, axis=1).reshape(o_ref.shape)
    st_ref[...] = stats.reshape(st_ref.shape)


def _window_attention(qkv, grp, bsz, seq):
    _, dil = ATT_PATTERNS[grp]
    w = ATT_BLOCK
    t = bsz * seq
    chunk = ROW_TILE // dil
    tiles = w // chunk if chunk < w else 1
    span = dil * w
    nb = seq // span
    col0 = grp * 3

    if tiles == 1:
        per_b = seq // w
        stride = span // w

        def spec(width, col, prev):
            def imap(b, r, i):
                blk = jnp.maximum(i - 1, 0) if prev else i
                return (b * per_b + blk * stride + r, col)
            return pl.BlockSpec((w, width), imap)

        qkv_v, o_shape, st_shape = qkv, (t, ATT_OUT_DIM), (t, LANES)
    else:
        def spec(width, col, prev):
            def imap(b, r, i):
                blk = jnp.maximum(i - 1, 0) if prev else i
                return (b, blk, 0, r, 0, col)
            return pl.BlockSpec((None, None, tiles, None, chunk, width), imap)

        lead = (bsz, nb, tiles, dil, chunk)
        qkv_v, o_shape, st_shape = qkv.reshape(*lead, ATT_QKV_DIM), (*lead, ATT_OUT_DIM), (*lead, LANES)

    o, st = pl.pallas_call(
        _attn_kernel,
        grid=(bsz, dil, nb),
        in_specs=[spec(ATT_OUT_DIM, col0, False), spec(ATT_OUT_DIM, col0 + 1, True),
                  spec(ATT_OUT_DIM, col0 + 1, False), spec(ATT_OUT_DIM, col0 + 2, True),
                  spec(ATT_OUT_DIM, col0 + 2, False)],
        out_specs=[spec(ATT_OUT_DIM, 0, False), spec(LANES, 0, False)],
        out_shape=[jax.ShapeDtypeStruct(o_shape, BF16), jax.ShapeDtypeStruct(st_shape, F32)],
        compiler_params=_params("parallel", "parallel", "arbitrary"),
        name=f"window_attn_d{dil}",
    )(qkv_v, qkv_v, qkv_v, qkv_v, qkv_v)
    return o.reshape(t, ATT_OUT_DIM), st.reshape(t, LANES)


def _merge_proj_ln_kernel(o1_ref, o2_ref, o3_ref, s1_ref, s2_ref, s3_ref, p4t_ref, p16t_ref, w_ref, x_ref,
                          g_ref, b_ref, o_ref, opk_ref):
    rows = o1_ref.shape[0]
    lane = lax.broadcasted_iota(jnp.int32, (rows, LANES), 1)
    lo_mask = lane < HALF

    def to_token_order(pt, val):
        if val.dtype == BF16:
            return _dot(pt, val)
        return sum(_dot(pt, term) for term in _split3(val))

    p4t, p16t = p4t_ref[...], p16t_ref[...]
    outs = [o1_ref[...].astype(F32), to_token_order(p4t, o2_ref[...]), to_token_order(p16t, o3_ref[...])]
    sts = [s1_ref[...], to_token_order(p4t, s2_ref[...]), to_token_order(p16t, s3_ref[...])]
    mx = jnp.maximum(jnp.maximum(sts[0], sts[1]), sts[2])
    wgts = [pltpu.roll(s, LANES - ATT_HEADS, 1) * jnp.exp(s - mx) for s in sts]
    den = wgts[0] + wgts[1] + wgts[2]
    den = jnp.where(lane < ATT_HEADS, den, 1.0)
    coefs = [wg / den for wg in wgts]
    parts = []
    for p in range(ATT_HEADS // 2):
        sl = slice(p * LANES, (p + 1) * LANES)
        acc = jnp.zeros((rows, LANES), F32)
        for gi in range(ATT_N_GROUPS):
            acc = acc + _pair_expand(coefs[gi], 2 * p, lo_mask) * outs[gi][:, sl]
        parts.append(acc.astype(BF16))
    mix = _dot(jnp.concatenate(parts, axis=1), w_ref[...])
    out = _layer_norm(DEEPNORM_ALPHA * x_ref[...] + mix, g_ref[...], b_ref[...])
    o_ref[...] = out
    opk_ref[...] = _pack_rows(out)


def _merge_proj_ln(os_, sts, perms_t, w, x, g, b):
    t = x.shape[0]
    tm = ROW_TILE
    ospec = pl.BlockSpec((tm, ATT_OUT_DIM), lambda i: (i, 0))
    sspec = pl.BlockSpec((tm, LANES), lambda i: (i, 0))
    xspec = pl.BlockSpec((tm, D_MODEL), lambda i: (i, 0))
    vspec = pl.BlockSpec((1, D_MODEL), lambda i: (0, 0))
    pspec = _resident((tm, tm))
    return pl.pallas_call(
        _merge_proj_ln_kernel,
        grid=(t // tm,),
        in_specs=[ospec] * 3 + [sspec] * 3 + [pspec, pspec, _resident((ATT_OUT_DIM, D_MODEL)),
                                              xspec, vspec, vspec],
        out_specs=[xspec, pl.BlockSpec((tm, D_PACK), lambda i: (i, 0))],
        out_shape=[jax.ShapeDtypeStruct((t, D_MODEL), F32), jax.ShapeDtypeStruct((t, D_PACK), U32)],
        compiler_params=_params("parallel"),
        name="attn_merge_proj_ln",
    )(*os_, *sts, perms_t[1], perms_t[2], w, x, g[None, :], b[None, :])


def _router_kernel(x_ref, whi_ref, wlo_ref, b_ref, o_ref, cnt_ref, carry):
    @pl.when(pl.program_id(0) == 0)
    def _():
        carry[...] = jnp.zeros(carry.shape, F32)

    x = x_ref[...]
    xhi = x.astype(BF16)
    xlo = (x - xhi.astype(F32)).astype(BF16)
    logits = _dot(xhi, whi_ref[...]) + _dot(xlo, whi_ref[...]) + _dot(xhi, wlo_ref[...]) + b_ref[...]
    rows = logits.shape[0]
    lane = lax.broadcasted_iota(jnp.int32, (rows, LANES), 1)
    big = jnp.int32(LANES)

    def top1(vals, mask):
        v = jnp.where(mask, vals, -jnp.inf)
        m = jnp.max(v, -1, keepdims=True)
        idx = jnp.min(jnp.where(v == m, lane, big), -1, keepdims=True)
        return v, m, idx

    gmask = lane < MOE_N_GROUPS
    gv, gm, gidx = top1(logits, gmask)
    g_w = 1.0 / jnp.sum(jnp.exp(gv - gm), -1, keepdims=True)
    e_lo = MOE_N_GROUPS + gidx * MOE_EPG
    emask = (lane >= e_lo) & (lane < e_lo + MOE_EPG)
    ev, m1, i1 = top1(logits, emask)
    zsum = jnp.sum(jnp.exp(ev - m1), -1, keepdims=True)
    _, m2, i2 = top1(logits, emask & (lane != i1))
    p1 = 1.0 / zsum
    p2 = jnp.exp(m2 - m1) / zsum
    tot = p1 + p2
    e1 = i1 - MOE_N_GROUPS
    e2 = i2 - MOE_N_GROUPS

    oh1 = jnp.where(lane == e1, 1.0, 0.0)
    oh2 = jnp.where(lane == e2, 1.0, 0.0)
    oh = oh1 + oh2
    ri = lax.broadcasted_iota(jnp.int32, (rows, rows), 0)
    ci = lax.broadcasted_iota(jnp.int32, (rows, rows), 1)
    strict = jnp.where(ri > ci, 1.0, 0.0).astype(BF16)
    before = _dot(strict, oh.astype(BF16)) + carry[...]
    rank1 = jnp.sum(oh1 * before, -1, keepdims=True)
    rank2 = jnp.sum(oh2 * before, -1, keepdims=True)
    carry[...] = carry[...] + jnp.sum(oh, 0, keepdims=True)
    cnt_ref[...] = carry[...]

    vals = [e1.astype(F32), e2.astype(F32), g_w * (p1 / tot), g_w * (p2 / tot), rank1, rank2]
    out = jnp.zeros((rows, LANES), F32)
    for j, val in enumerate(vals):
        out = jnp.where(lane == j, val, out)
    o_ref[...] = out


def _router(x, w_rg, b_rg, w_re, b_re):
    t = x.shape[0]
    n_log = MOE_N_GROUPS + MOE_N_EXPERTS
    w = jnp.pad(jnp.concatenate([w_rg, w_re], axis=1), ((0, 0), (0, LANES - n_log)))
    whi = w.astype(BF16)
    wlo = (w - whi.astype(F32)).astype(BF16)
    bias = jnp.pad(jnp.concatenate([b_rg, b_re]), (0, LANES - n_log))[None, :]
    wspec = pl.BlockSpec((D_MODEL, LANES), lambda i: (0, 0))
    return pl.pallas_call(
        _router_kernel,
        grid=(t // ROW_TILE,),
        in_specs=[pl.BlockSpec((ROW_TILE, D_MODEL), lambda i: (i, 0)), wspec, wspec,
                  pl.BlockSpec((1, LANES), lambda i: (0, 0))],
        out_specs=[pl.BlockSpec((ROW_TILE, LANES), lambda i: (i, 0)),
                   pl.BlockSpec((1, LANES), lambda i: (0, 0))],
        out_shape=[jax.ShapeDtypeStruct((t, LANES), F32), jax.ShapeDtypeStruct((1, LANES), F32)],
        scratch_shapes=[pltpu.VMEM((1, LANES), F32)],
        compiler_params=_params("arbitrary"),
        name="moe_router",
    )(x, whi, wlo, bias)


def _expert_kernel(meta_ref, x_ref, wg_ref, wu_ref, wd_ref, o_ref, wg16, wu16, wd16):
    i = pl.program_id(0)
    n_blocks = pl.num_programs(0)

    @pl.when(i < meta_ref[n_blocks])
    def _():
        @pl.when((i == 0) | (meta_ref[i] != meta_ref[jnp.maximum(i - 1, 0)]))
        def _():
            wg16[...] = wg_ref[...].astype(BF16)
            wu16[...] = wu_ref[...].astype(BF16)
            wd16[...] = wd_ref[...].astype(BF16)

        x = _unpack_rows(x_ref[...]).astype(BF16)
        h = _silu(_dot(x, wg16[...])) * _dot(x, wu16[...])
        o_ref[...] = _pack_rows(_dot(h.astype(BF16), wd16[...]))


def _expert_mlp(meta, x_rows, w_gate, w_up, w_down, layer):
    n_rows = x_rows.shape[0]
    n_blocks = n_rows // MOE_ROW_BLOCK
    tb = MOE_ROW_BLOCK
    grid_spec = pltpu.PrefetchScalarGridSpec(
        num_scalar_prefetch=1,
        grid=(n_blocks,),
        in_specs=[pl.BlockSpec((tb, D_PACK), lambda i, meta: (i, 0)),
                  pl.BlockSpec((None, None, D_MODEL, MOE_HIDDEN), lambda i, meta: (layer, meta[i], 0, 0)),
                  pl.BlockSpec((None, None, D_MODEL, MOE_HIDDEN), lambda i, meta: (layer, meta[i], 0, 0)),
                  pl.BlockSpec((None, None, MOE_HIDDEN, D_MODEL), lambda i, meta: (layer, meta[i], 0, 0))],
        out_specs=pl.BlockSpec((tb, D_PACK), lambda i, meta: (i, 0)),
        scratch_shapes=[pltpu.VMEM((D_MODEL, MOE_HIDDEN), BF16), pltpu.VMEM((D_MODEL, MOE_HIDDEN), BF16),
                        pltpu.VMEM((MOE_HIDDEN, D_MODEL), BF16)],
    )
    return pl.pallas_call(
        _expert_kernel,
        grid_spec=grid_spec,
        out_shape=jax.ShapeDtypeStruct((n_rows, D_PACK), U32),
        compiler_params=_params("arbitrary"),
        name="moe_experts",
    )(meta, x_rows, w_gate, w_up, w_down)


def _combine_ln_kernel(y0_ref, y1_ref, r_ref, x_ref, g_ref, b_ref, o_ref, opk_ref):
    route = r_ref[...]
    g0 = route[:, 2:3]
    g1 = route[:, 3:4]
    ffn = g0 * _unpack_rows(y0_ref[...]) + g1 * _unpack_rows(y1_ref[...])
    out = _layer_norm(DEEPNORM_ALPHA * x_ref[...] + ffn, g_ref[...], b_ref[...])
    o_ref[...] = out
    opk_ref[...] = _pack_rows(out)


def _combine_ln(y0, y1, route, x, g, b):
    t = x.shape[0]
    xspec = pl.BlockSpec((ROW_TILE, D_MODEL), lambda i: (i, 0))
    pspec = pl.BlockSpec((ROW_TILE, D_PACK), lambda i: (i, 0))
    vspec = pl.BlockSpec((1, D_MODEL), lambda i: (0, 0))
    return pl.pallas_call(
        _combine_ln_kernel,
        grid=(t // ROW_TILE,),
        in_specs=[pspec, pspec, pl.BlockSpec((ROW_TILE, LANES), lambda i: (i, 0)), xspec, vspec, vspec],
        out_specs=[xspec, pspec],
        out_shape=[jax.ShapeDtypeStruct((t, D_MODEL), F32), jax.ShapeDtypeStruct((t, D_PACK), U32)],
        compiler_params=_params("parallel"),
        name="moe_combine_ln",
    )(y0, y1, route, x, g[None, :], b[None, :])


def _positions_kernel(r_ref, ps_ref, o_ref):
    route = r_ref[...]
    lane = lax.broadcasted_iota(jnp.int32, route.shape, 1)
    starts = ps_ref[...]
    out = jnp.zeros(route.shape, F32)
    for k in range(MOE_TOP_K):
        eid = route[:, k:k + 1].astype(jnp.int32)
        pos = jnp.sum(jnp.where(lane == eid, starts, 0.0), -1, keepdims=True) + route[:, 4 + k:5 + k]
        out = jnp.where(lane == k, pos, out)
    o_ref[...] = out.astype(jnp.int32)


def _positions(route, pad_start):
    t = route.shape[0]
    starts = jnp.pad(pad_start.astype(F32), (0, LANES - MOE_N_EXPERTS))[None, :]
    return pl.pallas_call(
        _positions_kernel,
        grid=(t // ROW_TILE,),
        in_specs=[pl.BlockSpec((ROW_TILE, LANES), lambda i: (i, 0)), pl.BlockSpec((1, LANES), lambda i: (0, 0))],
        out_specs=pl.BlockSpec((ROW_TILE, LANES), lambda i: (i, 0)),
        out_shape=jax.ShapeDtypeStruct((t, LANES), jnp.int32),
        compiler_params=_params("parallel"),
        name="moe_positions",
    )(route, starts)


def _moe(x, xpk, w_rg, b_rg, w_re, b_re, w_gate, w_up, w_down, layer, g, b):
    t = x.shape[0]
    tb = MOE_ROW_BLOCK
    route, cnt = _router(x, w_rg, b_rg, w_re, b_re)
    n_assign = t * MOE_TOP_K
    n_blocks = n_assign // tb + MOE_N_EXPERTS
    n_rows = n_blocks * tb
    counts = cnt[0, :MOE_N_EXPERTS].astype(jnp.int32)
    padded = (counts + tb - 1) // tb * tb
    pad_end = jnp.cumsum(padded)
    pos = _positions(route, pad_end - padded)
    pos0, pos1 = pos[:, 0], pos[:, 1]
    block_start = jnp.arange(n_blocks, dtype=jnp.int32) * tb
    block_e = jnp.minimum(jnp.sum((pad_end[None, :] <= block_start[:, None]).astype(jnp.int32), -1),
                          MOE_N_EXPERTS - 1)
    meta = jnp.concatenate([block_e, pad_end[-1:] // tb]).astype(jnp.int32)
    x_rows = _row_scatter(xpk, (pos0, pos1), n_rows)
    y_rows = _expert_mlp(meta, x_rows, w_gate, w_up, w_down, layer)
    y0 = _row_gather(y_rows, pos0)
    y1 = _row_gather(y_rows, pos1)
    return _combine_ln(y0, y1, route, x, g, b)


def _ssd_layer(x, xpk, w_in, conv_w, conv_b, dt_bias, a_log, d_skip, norm_w, w_out, g, b, bsz, seq):
    w_zxbc = w_in[:, :SSD_D_INNER + SSD_CONV_DIM].astype(BF16)
    w_dt = jnp.pad(w_in[:, SSD_D_INNER + SSD_CONV_DIM:], ((0, 0), (0, LANES - SSD_N_HEADS))).astype(BF16)
    zxbc, dt_raw = _in_proj(xpk, w_zxbc, w_dt, conv_w, conv_b, bsz, seq)
    y = _ssd_scan(zxbc, dt_raw, dt_bias, a_log, d_skip, norm_w, bsz, seq)
    return _proj_ln(y, w_out.astype(BF16), x, g, b, "ssd_out_proj_ln")


def _attn_layer(x, xpk, tabs, w_qkv, w_o, g, b, bsz, seq):
    perms = [_perm_matrix(dil) for _, dil in ATT_PATTERNS]
    perms_t = [p.T for p in perms]
    qkv = _qkv_proj(xpk, w_qkv.astype(BF16), tabs, perms)
    os_, sts = [], []
    for grp in range(ATT_N_GROUPS):
        o, st = _window_attention(qkv, grp, bsz, seq)
        os_.append(o)
        sts.append(st)
    return _merge_proj_ln(os_, sts, perms_t, w_o.astype(BF16), x, g, b)


def kernel(x, positions, ssd_w_in, ssd_conv_w, ssd_conv_b, ssd_dt_bias, ssd_a_log, ssd_d, ssd_norm_w, ssd_w_out,
           attn_w_qkv, attn_w_o, ln_g, ln_b, moe_w_router_group, moe_b_router_group, moe_w_router_expert,
           moe_b_router_expert, moe_w_gate, moe_w_up, moe_w_down):
    bsz, seq, d = x.shape
    t = bsz * seq
    h = x.reshape(t, d)
    hpk = _pack_rows_host(h)
    tabs = _rope_tables(positions)
    for i in range(DEPTH):
        j = i // N_MIXERS
        if i % N_MIXERS == 0:
            h, hpk = _ssd_layer(h, hpk, ssd_w_in[j], ssd_conv_w[j], ssd_conv_b[j], ssd_dt_bias[j], ssd_a_log[j],
                                ssd_d[j], ssd_norm_w[j], ssd_w_out[j], ln_g[i, 0], ln_b[i, 0], bsz, seq)
        else:
            h, hpk = _attn_layer(h, hpk, tabs, attn_w_qkv[j], attn_w_o[j], ln_g[i, 0], ln_b[i, 0], bsz, seq)
        h, hpk = _moe(h, hpk, moe_w_router_group[i], moe_b_router_group[i], moe_w_router_expert[i],
                      moe_b_router_expert[i], moe_w_gate, moe_w_up, moe_w_down, i, ln_g[i, 1], ln_b[i, 1])
    return h.reshape(bsz, seq, d)
```

```python
import functools

import jax
import jax.numpy as jnp
from jax import lax
from jax.experimental import pallas as pl
from jax.experimental.pallas import tpu as pltpu
from jax.experimental.pallas import tpu_sc as plsc

F32 = jnp.float32
BF16 = jnp.bfloat16
U32 = jnp.uint32

D_MODEL = 1024
D_PACK = D_MODEL // 2
DEPTH = 4
N_MIXERS = 2

SSD_D_INNER = 2048
SSD_HEAD_DIM = 64
SSD_N_HEADS = 32
SSD_N_GROUPS = 4
SSD_D_STATE = 128
SSD_D_CONV = 4
SSD_CHUNK = 128
SSD_GN = SSD_N_GROUPS * SSD_D_STATE
SSD_CONV_DIM = SSD_D_INNER + 2 * SSD_GN
SSD_GROUP_COLS = SSD_D_INNER // SSD_N_GROUPS
IN_PROJ_COLS = 1024
SSD_PAIR_UNROLL = 4

ATT_HEAD_DIM = 64
ATT_HEADS = 8
ATT_PATTERNS = ((128, 1), (512, 4), (2048, 16))
ATT_N_GROUPS = 3
ATT_OUT_DIM = ATT_HEADS * ATT_HEAD_DIM
ATT_QKV_DIM = ATT_N_GROUPS * 3 * ATT_OUT_DIM
ATT_BLOCK = 128
ROPE_THETA = 500000.0
ROPE_DIM = 16

MOE_N_GROUPS = 4
MOE_EPG = 8
MOE_N_EXPERTS = 32
MOE_TOP_K = 2
MOE_HIDDEN = 512
MOE_ROW_BLOCK = 512

DEEPNORM_ALPHA = (2 * DEPTH) ** 0.25
LN_EPS = 1e-5
RMS_EPS = 1e-5
NEG_INF = -1e30

LANES = 128
HALF = LANES // 2
VMEM_LIMIT = 56 * 1024 * 1024

ROW_TILE = 512


def _params(*sem):
    return pltpu.CompilerParams(dimension_semantics=sem, vmem_limit_bytes=VMEM_LIMIT)


def _silu(v):
    h = 0.5 * v
    return h + h * jnp.tanh(h)


def _layer_norm(r, g, b):
    mu = jnp.mean(r, -1, keepdims=True)
    d = r - mu
    var = jnp.mean(d * d, -1, keepdims=True)
    return d * lax.rsqrt(var + LN_EPS) * g + b


def _split3(v):
    hi = v.astype(BF16)
    r1 = v - hi.astype(F32)
    mid = r1.astype(BF16)
    lo = (r1 - mid.astype(F32)).astype(BF16)
    return hi, mid, lo


def _dot(a, b):
    return jnp.dot(a, b, preferred_element_type=F32)


def _dot_nt(a, b):
    return lax.dot_general(a, b, (((1,), (1,)), ((), ())), preferred_element_type=F32)


def _dot_tn(a, b):
    return lax.dot_general(a, b, (((0,), (0,)), ((), ())), preferred_element_type=F32)


def _pair_expand(mat, h0, lo_mask):
    rows = mat.shape[0]
    a = jnp.broadcast_to(mat[:, h0:h0 + 1], (rows, LANES))
    b = jnp.broadcast_to(mat[:, h0 + 1:h0 + 2], (rows, LANES))
    return jnp.where(lo_mask, a, b)


def _round_bf16_bits(bits):
    odd = (bits >> 16) & jnp.uint32(1)
    return (bits + jnp.uint32(0x7FFF) + odd) & jnp.uint32(0xFFFF0000)


def _pack_words(bits):
    r = _round_bf16_bits(bits)
    return r[:, :D_PACK] | (r[:, D_PACK:] >> 16)


def _pack_rows(v):
    return _pack_words(pltpu.bitcast(v, U32))


def _unpack_rows(p):
    hi = pltpu.bitcast(p & jnp.uint32(0xFFFF0000), F32)
    lo = pltpu.bitcast(p << 16, F32)
    return jnp.concatenate([hi, lo], axis=1)


def _pack_rows_host(v):
    return _pack_words(lax.bitcast_convert_type(v, U32))


def _row_gather(data, idx):
    n = idx.shape[0]
    d = data.shape[1]
    window = LANES
    dc = d // 2
    mesh = plsc.VectorSubcoreMesh(core_axis_name="core", subcore_axis_name="subcore")

    @functools.partial(pl.kernel, out_type=jax.ShapeDtypeStruct((n, d), data.dtype), mesh=mesh)
    def gather(x_hbm, i_hbm, o_hbm):
        for c in range(d // dc):
            def body(i_vmem, o_vmem, c=c):
                pltpu.sync_copy(x_hbm.at[i_vmem.at[0], pl.ds(c * dc, dc)], o_vmem)

            pltpu.emit_pipeline(
                body,
                grid=(n // window,),
                in_specs=[pl.BlockSpec((1, window), lambda i: (0, i))],
                out_specs=[pl.BlockSpec((window, dc), lambda i, c=c: (i, c))],
                core_axis_name=("core", "subcore"),
                dimension_semantics=(pltpu.PARALLEL,),
            )(i_hbm, o_hbm)

    return gather(data, idx.reshape(1, n))


def _row_scatter(data, idxs, n_rows):
    t, d = data.shape
    window = LANES
    dc = d // 2
    mesh = plsc.VectorSubcoreMesh(core_axis_name="core", subcore_axis_name="subcore")

    @functools.partial(pl.kernel, out_type=jax.ShapeDtypeStruct((n_rows, d), data.dtype), mesh=mesh)
    def scatter(x_hbm, *refs):
        o_hbm = refs[-1]
        for i_hbm in refs[:-1]:
            for c in range(d // dc):
                def body(x_vmem, i_vmem, c=c):
                    pltpu.sync_copy(x_vmem, o_hbm.at[i_vmem.at[0], pl.ds(c * dc, dc)])

                pltpu.emit_pipeline(
                    body,
                    grid=(t // window,),
                    in_specs=[pl.BlockSpec((window, dc), lambda i, c=c: (i, c)),
                              pl.BlockSpec((1, window), lambda i: (0, i))],
                    out_specs=[],
                    core_axis_name=("core", "subcore"),
                    dimension_semantics=(pltpu.PARALLEL,),
                )(x_hbm, i_hbm)

    return scatter(data, *[i.reshape(1, t) for i in idxs])


def _resident(shape):
    return pl.BlockSpec(shape, lambda *_: (0,) * len(shape), pipeline_mode=pl.Buffered(1))


def _in_proj_kernel(x_ref, w_ref, wdt_ref, cw_ref, cb_ref, o_ref, dt_ref, ext, tail):
    rows = x_ref.shape[0]
    tn = IN_PROJ_COLS
    n_z = SSD_D_INNER // tn

    @pl.when(pl.program_id(1) == 0)
    def _():
        tail[...] = jnp.zeros(tail.shape, F32)

    x = _unpack_rows(x_ref[...]).astype(BF16)
    for n in range(n_z):
        sl = slice(n * tn, (n + 1) * tn)
        o_ref[:, sl] = _dot(x, w_ref[:, sl]).astype(o_ref.dtype)
    for j in range(SSD_CONV_DIM // tn):
        sl = slice((n_z + j) * tn, (n_z + j + 1) * tn)
        cl = slice(j * tn, (j + 1) * tn)
        ext[j, 0:8, :] = tail[j]
        ext[j, 8:8 + rows, :] = _dot(x, w_ref[:, sl])
        tail[j] = ext[j, rows:rows + 8, :]
        acc = ext[j, 8:8 + rows, :] * cw_ref[3:4, cl] + cb_ref[:, cl]
        for k in range(SSD_D_CONV - 1):
            acc = acc + ext[j, 5 + k:5 + k + rows, :] * cw_ref[k:k + 1, cl]
        o_ref[:, sl] = _silu(acc).astype(o_ref.dtype)
    dt_ref[...] = _dot(x, wdt_ref[...])


def _in_proj(xpk, w_zxbc, w_dt, conv_w, conv_b, bsz, seq):
    t = xpk.shape[0]
    n = w_zxbc.shape[1]
    tiles = seq // ROW_TILE
    cw = jnp.pad(conv_w, ((0, 8 - SSD_D_CONV), (0, 0)))
    n_conv = SSD_CONV_DIM // IN_PROJ_COLS
    return pl.pallas_call(
        _in_proj_kernel,
        grid=(bsz, tiles),
        in_specs=[pl.BlockSpec((ROW_TILE, D_PACK), lambda b, s: (b * tiles + s, 0)),
                  _resident((D_MODEL, n)), _resident((D_MODEL, LANES)),
                  _resident((8, SSD_CONV_DIM)), _resident((1, SSD_CONV_DIM))],
        out_specs=[pl.BlockSpec((ROW_TILE, n), lambda b, s: (b * tiles + s, 0)),
                   pl.BlockSpec((ROW_TILE, LANES), lambda b, s: (b * tiles + s, 0))],
        out_shape=[jax.ShapeDtypeStruct((t, n), BF16), jax.ShapeDtypeStruct((t, LANES), F32)],
        scratch_shapes=[pltpu.VMEM((n_conv, ROW_TILE + 8, IN_PROJ_COLS), F32),
                        pltpu.VMEM((n_conv, 8, IN_PROJ_COLS), F32)],
        compiler_params=_params("arbitrary", "arbitrary"),
        name="ssd_in_proj",
    )(xpk, w_zxbc, w_dt, cw, conv_b[None, :])


def _ssd_kernel(z_ref, xs_ref, bc_ref, dt_ref, dtb_ref, alog_ref, dsk_ref, nw_ref, ex_ref, o_ref,
                state, cs_cols, cs_rows, dt_x, cb_all, y_off, y_grp, xd_all, cd_all):
    q = SSD_CHUNK
    pairs = SSD_GROUP_COLS // LANES

    @pl.when(pl.program_id(1) == 0)
    def _():
        state[...] = jnp.zeros(state.shape, F32)

    pre = dt_ref[...] + dtb_ref[...]
    dt = jnp.maximum(pre, 0.0) + jnp.log(1.0 + jnp.exp(-jnp.abs(pre)))
    a = -jnp.exp(alog_ref[...])
    row = lax.broadcasted_iota(jnp.int32, (q, q), 0)
    col = lax.broadcasted_iota(jnp.int32, (q, q), 1)
    causal = row >= col
    lo_mask = col < HALF
    tri = jnp.where(causal, 1.0, 0.0).astype(BF16)
    cs3 = _dot(tri, jnp.concatenate(_split3(dt * a), axis=1))
    cs = cs3[:, :LANES] + cs3[:, LANES:2 * LANES] + cs3[:, 2 * LANES:]
    cs_rows[...] = cs.T
    for h in range(SSD_N_HEADS):
        cs_cols[h] = jnp.broadcast_to(cs[:, h:h + 1], (q, q))
    dt3 = _dot(jnp.concatenate(_split3(dt), axis=0), ex_ref[...])
    dt_x[...] = dt3[:q] + dt3[q:2 * q] + dt3[2 * q:]
    for g in range(SSD_N_GROUPS):
        bg = bc_ref[:, g * SSD_D_STATE:(g + 1) * SSD_D_STATE]
        cg = bc_ref[:, SSD_GN + g * SSD_D_STATE:SSD_GN + (g + 1) * SSD_D_STATE]
        cb_all[g] = _dot_nt(cg, bg)
        y_off[:, g * SSD_GROUP_COLS:(g + 1) * SSD_GROUP_COLS] = _dot(cg, state[g].astype(BF16))

    def group_body(g, carry):
        cb = cb_all[g]

        def pair_body(pp, ssq):
            p = g * pairs + pp
            h0 = 2 * p
            x0 = pl.multiple_of(p * LANES, LANES)
            l0 = pl.multiple_of(pp * LANES, LANES)
            cols = (cs_cols[h0], cs_cols[h0 + 1])
            csx = jnp.where(lo_mask, cols[0], cols[1])
            xp = xs_ref[:, pl.ds(x0, LANES)].astype(F32)
            xdt = xp * dt_x[:, pl.ds(x0, LANES)]
            xdt16 = xdt.astype(BF16)
            last = csx[q - 1:q, :]
            halves = []
            for hh in range(2):
                diff = cols[hh] - cs_rows[pl.ds(h0 + hh, 1), :]
                decay = jnp.exp(jnp.where(causal, diff, -jnp.inf))
                halves.append(_dot((cb * decay).astype(BF16), xdt16))
            y = jnp.where(lo_mask, halves[0], halves[1])
            y = y + y_off[:, pl.ds(x0, LANES)] * jnp.exp(csx) + xp * dsk_ref[:, pl.ds(x0, LANES)]
            y = y * _silu(z_ref[:, pl.ds(x0, LANES)].astype(F32))
            y_grp[:, pl.ds(l0, LANES)] = y
            xd_all[:, pl.ds(x0, LANES)] = (xdt * jnp.exp(last - csx)).astype(BF16)
            cd_all[:, pl.ds(x0, LANES)] = jnp.exp(last)
            return ssq + jnp.sum(y * y, -1, keepdims=True)

        ssq = lax.fori_loop(0, pairs, pair_body, jnp.zeros((q, 1), F32), unroll=SSD_PAIR_UNROLL)
        inv = lax.rsqrt(ssq * (1.0 / SSD_GROUP_COLS) + RMS_EPS)
        g0 = pl.multiple_of(g * SSD_GROUP_COLS, SSD_GROUP_COLS)
        o_ref[:, pl.ds(g0, SSD_GROUP_COLS)] = (
            y_grp[...] * inv * nw_ref[:, pl.ds(g0, SSD_GROUP_COLS)]).astype(o_ref.dtype)
        return carry

    lax.fori_loop(0, SSD_N_GROUPS, group_body, 0)

    for g in range(SSD_N_GROUPS):
        gs = slice(g * SSD_GROUP_COLS, (g + 1) * SSD_GROUP_COLS)
        bg = bc_ref[:, g * SSD_D_STATE:(g + 1) * SSD_D_STATE]
        state[g] = state[g] * cd_all[:, gs] + _dot_tn(bg, xd_all[:, gs])


def _ssd_scan(zxbc, dt_raw, dt_bias, a_log, d_skip, norm_w, bsz, seq):
    t = bsz * seq
    nc = seq // SSD_CHUNK
    q = SSD_CHUNK
    pad_h = LANES - SSD_N_HEADS
    dtb = jnp.pad(dt_bias, (0, pad_h))[None, :]
    alog = jnp.pad(a_log, (0, pad_h))[None, :]
    dsk = jnp.repeat(d_skip, SSD_HEAD_DIM)[None, :]
    nw = norm_w[None, :]
    expand = (jnp.arange(LANES)[:, None] == jnp.arange(SSD_D_INNER)[None, :] // SSD_HEAD_DIM).astype(BF16)

    def const(shape):
        return pl.BlockSpec(shape, lambda b, c: (0, 0))

    return pl.pallas_call(
        _ssd_kernel,
        grid=(bsz, nc),
        in_specs=[pl.BlockSpec((q, SSD_D_INNER), lambda b, c: (b * nc + c, 0)),
                  pl.BlockSpec((q, SSD_D_INNER), lambda b, c: (b * nc + c, 1)),
                  pl.BlockSpec((q, 2 * SSD_GN), lambda b, c: (b * nc + c, 4)),
                  pl.BlockSpec((q, LANES), lambda b, c: (b * nc + c, 0)),
                  const((1, LANES)), const((1, LANES)),
                  const((1, SSD_D_INNER)), const((1, SSD_D_INNER)), const((LANES, SSD_D_INNER))],
        out_specs=pl.BlockSpec((q, SSD_D_INNER), lambda b, c: (b * nc + c, 0)),
        out_shape=jax.ShapeDtypeStruct((t, SSD_D_INNER), BF16),
        scratch_shapes=[pltpu.VMEM((SSD_N_GROUPS, SSD_D_STATE, SSD_GROUP_COLS), F32),
                        pltpu.VMEM((SSD_N_HEADS, q, q), F32),
                        pltpu.VMEM((LANES, q), F32),
                        pltpu.VMEM((q, SSD_D_INNER), F32),
                        pltpu.VMEM((SSD_N_GROUPS, q, q), F32),
                        pltpu.VMEM((q, SSD_D_INNER), F32),
                        pltpu.VMEM((q, SSD_GROUP_COLS), F32),
                        pltpu.VMEM((q, SSD_D_INNER), BF16),
                        pltpu.VMEM((1, SSD_D_INNER), F32)],
        compiler_params=_params("arbitrary", "arbitrary"),
        name="ssd_scan",
    )(zxbc, zxbc, zxbc, dt_raw, dtb, alog, dsk, nw, expand)


def _proj_ln_kernel(y_ref, w_ref, x_ref, g_ref, b_ref, o_ref, opk_ref):
    mix = _dot(y_ref[...].astype(BF16), w_ref[...])
    out = _layer_norm(DEEPNORM_ALPHA * x_ref[...] + mix, g_ref[...], b_ref[...])
    o_ref[...] = out
    opk_ref[...] = _pack_rows(out)


def _proj_ln(y, w, x, g, b, name):
    t, k = y.shape
    return pl.pallas_call(
        _proj_ln_kernel,
        grid=(t // ROW_TILE,),
        in_specs=[pl.BlockSpec((ROW_TILE, k), lambda i: (i, 0)),
                  _resident((k, D_MODEL)),
                  pl.BlockSpec((ROW_TILE, D_MODEL), lambda i: (i, 0)),
                  pl.BlockSpec((1, D_MODEL), lambda i: (0, 0)),
                  pl.BlockSpec((1, D_MODEL), lambda i: (0, 0))],
        out_specs=[pl.BlockSpec((ROW_TILE, D_MODEL), lambda i: (i, 0)),
                   pl.BlockSpec((ROW_TILE, D_PACK), lambda i: (i, 0))],
        out_shape=[jax.ShapeDtypeStruct((t, D_MODEL), F32),
                   jax.ShapeDtypeStruct((t, D_PACK), U32)],
        compiler_params=_params("parallel"),
        name=name,
    )(y, w, x, g[None, :], b[None, :])


def _rope_table_kernel(pos_ref, freq_ref, c_ref, s1_ref, s2_ref):
    ang = pos_ref[...].astype(F32) * freq_ref[...]
    d = lax.broadcasted_iota(jnp.int32, ang.shape, 1) % ATT_HEAD_DIM
    cos, sin = jnp.cos(ang), jnp.sin(ang)
    half = ROPE_DIM // 2
    c_ref[...] = jnp.where(d < ROPE_DIM, cos, 1.0)
    s1_ref[...] = jnp.where(d < half, -sin, 0.0)
    s2_ref[...] = jnp.where((d >= half) & (d < ROPE_DIM), sin, 0.0)


def _residue_order(v, dil):
    tail = v.shape[1:]
    return v.reshape(-1, ROW_TILE // dil, dil, *tail).swapaxes(1, 2).reshape(-1, *tail)


def _perm_matrix(dil):
    src = _residue_order(jnp.arange(ROW_TILE, dtype=jnp.int32), dil)
    return (src[:, None] == jnp.arange(ROW_TILE, dtype=jnp.int32)[None, :]).astype(BF16)


def _rope_tables(positions):
    t = positions.size
    flat = positions.reshape(t)
    pos_all = jnp.concatenate([_residue_order(flat, dil) for _, dil in ATT_PATTERNS])
    half = ROPE_DIM // 2
    inv_freq = ROPE_THETA ** (-jnp.arange(0, ROPE_DIM, 2, dtype=F32) / ROPE_DIM)
    d = jnp.arange(LANES) % ATT_HEAD_DIM
    freq = jnp.where(d < ROPE_DIM, inv_freq[d % half], 0.0).astype(F32)[None, :]
    n = pos_all.size
    tab = jax.ShapeDtypeStruct((n, LANES), F32)
    tabs = pl.pallas_call(
        _rope_table_kernel,
        grid=(n // ROW_TILE,),
        in_specs=[pl.BlockSpec((ROW_TILE, 1), lambda i: (i, 0)),
                  pl.BlockSpec((1, LANES), lambda i: (0, 0))],
        out_specs=[pl.BlockSpec((ROW_TILE, LANES), lambda i: (i, 0))] * 3,
        out_shape=[tab, tab, tab],
        compiler_params=_params("parallel"),
        name="rope_tables",
    )(pos_all.reshape(n, 1), freq)
    return [tb.reshape(ATT_N_GROUPS, t, LANES) for tb in tabs]


def _qkv_kernel(x_ref, p4_ref, p16_ref, w_ref, c_ref, s1_ref, s2_ref, o_ref):
    x16 = _unpack_rows(x_ref[...]).astype(BF16)
    xs = [x16, _dot(p4_ref[...], x16).astype(BF16), _dot(p16_ref[...], x16).astype(BF16)]
    reps = ATT_OUT_DIM // LANES
    half = ROPE_DIM // 2
    for grp in range(ATT_N_GROUPS):
        c = jnp.concatenate([c_ref[grp]] * reps, axis=1)
        s1 = jnp.concatenate([s1_ref[grp]] * reps, axis=1)
        s2 = jnp.concatenate([s2_ref[grp]] * reps, axis=1)
        for kind in range(3):
            j = grp * 3 + kind
            sl = slice(j * ATT_OUT_DIM, (j + 1) * ATT_OUT_DIM)
            acc = _dot(xs[grp], w_ref[:, sl])
            if kind < 2:
                up = pltpu.roll(acc, ATT_OUT_DIM - half, 1)
                down = pltpu.roll(acc, half, 1)
                acc = acc * c + up * s1 + down * s2
            if kind == 0:
                acc = acc * (ATT_HEAD_DIM ** -0.5)
            o_ref[:, sl] = acc.astype(o_ref.dtype)


def _qkv_proj(xpk, w, tabs, perms):
    t = xpk.shape[0]
    tab_spec = pl.BlockSpec((ATT_N_GROUPS, ROW_TILE, LANES), lambda i: (0, i, 0))
    return pl.pallas_call(
        _qkv_kernel,
        grid=(t // ROW_TILE,),
        in_specs=[pl.BlockSpec((ROW_TILE, D_PACK), lambda i: (i, 0)),
                  _resident((ROW_TILE, ROW_TILE)), _resident((ROW_TILE, ROW_TILE)),
                  _resident((D_MODEL, ATT_QKV_DIM)), tab_spec, tab_spec, tab_spec],
        out_specs=pl.BlockSpec((ROW_TILE, ATT_QKV_DIM), lambda i: (i, 0)),
        out_shape=jax.ShapeDtypeStruct((t, ATT_QKV_DIM), BF16),
        compiler_params=_params("parallel"),
        name="qkv_rope",
    )(xpk, perms[1], perms[2], w, *tabs)


def _attn_kernel(q_ref, kp_ref, kc_ref, vp_ref, vc_ref, o_ref, st_ref):
    w = ATT_BLOCK
    i = pl.program_id(2)
    qi = lax.broadcasted_iota(jnp.int32, (w, 2 * w), 0)
    kk = lax.broadcasted_iota(jnp.int32, (w, 2 * w), 1)
    first_key = jnp.where(i > 0, 0, w)
    valid = (kk >= qi) & (kk <= qi + w) & (kk >= first_key)
    lane = lax.broadcasted_iota(jnp.int32, (w, LANES), 1)
    lo_mask = lane < HALF

    def rows(ref):
        return ref[...].reshape(w, ref.shape[-1])

    q = rows(q_ref)
    k = jnp.concatenate([rows(kp_ref), rows(kc_ref)], axis=0)
    v = jnp.concatenate([rows(vp_ref), rows(vc_ref)], axis=0)
    stats = jnp.zeros((w, LANES), F32)
    zero = jnp.zeros((), q.dtype)
    parts = []
    for p in range(ATT_HEADS // 2):
        sl = slice(p * LANES, (p + 1) * LANES)
        qp, kp, vp = q[:, sl], k[:, sl], v[:, sl]
        outs = []
        for hh in range(2):
            h = 2 * p + hh
            qm = jnp.where(lo_mask if hh == 0 else ~lo_mask, qp, zero)
            s = jnp.where(valid, _dot_nt(qm, kp), NEG_INF)
            m = jnp.max(s, -1, keepdims=True)
            pr = jnp.exp(s - m)
            l = jnp.sum(pr, -1, keepdims=True)
            outs.append(_dot(pr.astype(v.dtype), vp) / l)
            stats = jnp.where(lane == h, m, stats)
            stats = jnp.where(lane == ATT_HEADS + h, l, stats)
        parts.append(jnp.where(lo_mask, outs[0], outs[1]).astype(o_ref.dtype))
    o_ref[...] = jnp.concatenate(parts, axis=1).reshape(o_ref.shape)
    st_ref[...] = stats.reshape(st_ref.shape)


def _window_attention(qkv, grp, bsz, seq):
    _, dil = ATT_PATTERNS[grp]
    w = ATT_BLOCK
    t = bsz * seq
    chunk = ROW_TILE // dil
    tiles = w // chunk if chunk < w else 1
    span = dil * w
    nb = seq // span
    col0 = grp * 3

    if tiles == 1:
        per_b = seq // w
        stride = span // w

        def spec(width, col, prev):
            def imap(b, r, i):
                blk = jnp.maximum(i - 1, 0) if prev else i
                return (b * per_b + blk * stride + r, col)
            return pl.BlockSpec((w, width), imap)

        qkv_v, o_shape, st_shape = qkv, (t, ATT_OUT_DIM), (t, LANES)
    else:
        def spec(width, col, prev):
            def imap(b, r, i):
                blk = jnp.maximum(i - 1, 0) if prev else i
                return (b, blk, 0, r, 0, col)
            return pl.BlockSpec((None, None, tiles, None, chunk, width), imap)

        lead = (bsz, nb, tiles, dil, chunk)
        qkv_v, o_shape, st_shape = qkv.reshape(*lead, ATT_QKV_DIM), (*lead, ATT_OUT_DIM), (*lead, LANES)

    o, st = pl.pallas_call(
        _attn_kernel,
        grid=(bsz, dil, nb),
        in_specs=[spec(ATT_OUT_DIM, col0, False), spec(ATT_OUT_DIM, col0 + 1, True),
                  spec(ATT_OUT_DIM, col0 + 1, False), spec(ATT_OUT_DIM, col0 + 2, True),
                  spec(ATT_OUT_DIM, col0 + 2, False)],
        out_specs=[spec(ATT_OUT_DIM, 0, False), spec(LANES, 0, False)],
        out_shape=[jax.ShapeDtypeStruct(o_shape, BF16), jax.ShapeDtypeStruct(st_shape, F32)],
        compiler_params=_params("parallel", "parallel", "arbitrary"),
        name=f"window_attn_d{dil}",
    )(qkv_v, qkv_v, qkv_v, qkv_v, qkv_v)
    return o.reshape(t, ATT_OUT_DIM), st.reshape(t, LANES)


def _merge_proj_ln_kernel(o1_ref, o2_ref, o3_ref, s1_ref, s2_ref, s3_ref, p4t_ref, p16t_ref, w_ref, x_ref,
                          g_ref, b_ref, o_ref, opk_ref):
    rows = o1_ref.shape[0]
    lane = lax.broadcasted_iota(jnp.int32, (rows, LANES), 1)
    lo_mask = lane < HALF

    def to_token_order(pt, val):
        if val.dtype == BF16:
            return _dot(pt, val)
        return sum(_dot(pt, term) for term in _split3(val))

    p4t, p16t = p4t_ref[...], p16t_ref[...]
    outs = [o1_ref[...].astype(F32), to_token_order(p4t, o2_ref[...]), to_token_order(p16t, o3_ref[...])]
    sts = [s1_ref[...], to_token_order(p4t, s2_ref[...]), to_token_order(p16t, s3_ref[...])]
    mx = jnp.maximum(jnp.maximum(sts[0], sts[1]), sts[2])
    wgts = [pltpu.roll(s, LANES - ATT_HEADS, 1) * jnp.exp(s - mx) for s in sts]
    den = wgts[0] + wgts[1] + wgts[2]
    den = jnp.where(lane < ATT_HEADS, den, 1.0)
    coefs = [wg / den for wg in wgts]
    parts = []
    for p in range(ATT_HEADS // 2):
        sl = slice(p * LANES, (p + 1) * LANES)
        acc = jnp.zeros((rows, LANES), F32)
        for gi in range(ATT_N_GROUPS):
            acc = acc + _pair_expand(coefs[gi], 2 * p, lo_mask) * outs[gi][:, sl]
        parts.append(acc.astype(BF16))
    mix = _dot(jnp.concatenate(parts, axis=1), w_ref[...])
    out = _layer_norm(DEEPNORM_ALPHA * x_ref[...] + mix, g_ref[...], b_ref[...])
    o_ref[...] = out
    opk_ref[...] = _pack_rows(out)


def _merge_proj_ln(os_, sts, perms_t, w, x, g, b):
    t = x.shape[0]
    tm = ROW_TILE
    ospec = pl.BlockSpec((tm, ATT_OUT_DIM), lambda i: (i, 0))
    sspec = pl.BlockSpec((tm, LANES), lambda i: (i, 0))
    xspec = pl.BlockSpec((tm, D_MODEL), lambda i: (i, 0))
    vspec = pl.BlockSpec((1, D_MODEL), lambda i: (0, 0))
    pspec = _resident((tm, tm))
    return pl.pallas_call(
        _merge_proj_ln_kernel,
        grid=(t // tm,),
        in_specs=[ospec] * 3 + [sspec] * 3 + [pspec, pspec, _resident((ATT_OUT_DIM, D_MODEL)),
                                              xspec, vspec, vspec],
        out_specs=[xspec, pl.BlockSpec((tm, D_PACK), lambda i: (i, 0))],
        out_shape=[jax.ShapeDtypeStruct((t, D_MODEL), F32), jax.ShapeDtypeStruct((t, D_PACK), U32)],
        compiler_params=_params("parallel"),
        name="attn_merge_proj_ln",
    )(*os_, *sts, perms_t[1], perms_t[2], w, x, g[None, :], b[None, :])


def _router_kernel(x_ref, whi_ref, wlo_ref, b_ref, o_ref, cnt_ref, carry):
    @pl.when(pl.program_id(0) == 0)
    def _():
        carry[...] = jnp.zeros(carry.shape, F32)

    x = x_ref[...]
    xhi = x.astype(BF16)
    xlo = (x - xhi.astype(F32)).astype(BF16)
    logits = _dot(xhi, whi_ref[...]) + _dot(xlo, whi_ref[...]) + _dot(xhi, wlo_ref[...]) + b_ref[...]
    rows = logits.shape[0]
    lane = lax.broadcasted_iota(jnp.int32, (rows, LANES), 1)
    big = jnp.int32(LANES)

    def top1(vals, mask):
        v = jnp.where(mask, vals, -jnp.inf)
        m = jnp.max(v, -1, keepdims=True)
        idx = jnp.min(jnp.where(v == m, lane, big), -1, keepdims=True)
        return v, m, idx

    gmask = lane < MOE_N_GROUPS
    gv, gm, gidx = top1(logits, gmask)
    g_w = 1.0 / jnp.sum(jnp.exp(gv - gm), -1, keepdims=True)
    e_lo = MOE_N_GROUPS + gidx * MOE_EPG
    emask = (lane >= e_lo) & (lane < e_lo + MOE_EPG)
    ev, m1, i1 = top1(logits, emask)
    zsum = jnp.sum(jnp.exp(ev - m1), -1, keepdims=True)
    _, m2, i2 = top1(logits, emask & (lane != i1))
    p1 = 1.0 / zsum
    p2 = jnp.exp(m2 - m1) / zsum
    tot = p1 + p2
    e1 = i1 - MOE_N_GROUPS
    e2 = i2 - MOE_N_GROUPS

    oh1 = jnp.where(lane == e1, 1.0, 0.0)
    oh2 = jnp.where(lane == e2, 1.0, 0.0)
    oh = oh1 + oh2
    ri = lax.broadcasted_iota(jnp.int32, (rows, rows), 0)
    ci = lax.broadcasted_iota(jnp.int32, (rows, rows), 1)
    strict = jnp.where(ri > ci, 1.0, 0.0).astype(BF16)
    before = _dot(strict, oh.astype(BF16)) + carry[...]
    rank1 = jnp.sum(oh1 * before, -1, keepdims=True)
    rank2 = jnp.sum(oh2 * before, -1, keepdims=True)
    carry[...] = carry[...] + jnp.sum(oh, 0, keepdims=True)
    cnt_ref[...] = carry[...]

    vals = [e1.astype(F32), e2.astype(F32), g_w * (p1 / tot), g_w * (p2 / tot), rank1, rank2]
    out = jnp.zeros((rows, LANES), F32)
    for j, val in enumerate(vals):
        out = jnp.where(lane == j, val, out)
    o_ref[...] = out


def _router(x, w_rg, b_rg, w_re, b_re):
    t = x.shape[0]
    n_log = MOE_N_GROUPS + MOE_N_EXPERTS
    w = jnp.pad(jnp.concatenate([w_rg, w_re], axis=1), ((0, 0), (0, LANES - n_log)))
    whi = w.astype(BF16)
    wlo = (w - whi.astype(F32)).astype(BF16)
    bias = jnp.pad(jnp.concatenate([b_rg, b_re]), (0, LANES - n_log))[None, :]
    wspec = pl.BlockSpec((D_MODEL, LANES), lambda i: (0, 0))
    return pl.pallas_call(
        _router_kernel,
        grid=(t // ROW_TILE,),
        in_specs=[pl.BlockSpec((ROW_TILE, D_MODEL), lambda i: (i, 0)), wspec, wspec,
                  pl.BlockSpec((1, LANES), lambda i: (0, 0))],
        out_specs=[pl.BlockSpec((ROW_TILE, LANES), lambda i: (i, 0)),
                   pl.BlockSpec((1, LANES), lambda i: (0, 0))],
        out_shape=[jax.ShapeDtypeStruct((t, LANES), F32), jax.ShapeDtypeStruct((1, LANES), F32)],
        scratch_shapes=[pltpu.VMEM((1, LANES), F32)],
        compiler_params=_params("arbitrary"),
        name="moe_router",
    )(x, whi, wlo, bias)


def _expert_kernel(meta_ref, x_ref, wg_ref, wu_ref, wd_ref, o_ref, wg16, wu16, wd16):
    i = pl.program_id(0)
    n_blocks = pl.num_programs(0)

    @pl.when(i < meta_ref[n_blocks])
    def _():
        @pl.when((i == 0) | (meta_ref[i] != meta_ref[jnp.maximum(i - 1, 0)]))
        def _():
            wg16[...] = wg_ref[...].astype(BF16)
            wu16[...] = wu_ref[...].astype(BF16)
            wd16[...] = wd_ref[...].astype(BF16)

        x = _unpack_rows(x_ref[...]).astype(BF16)
        h = _silu(_dot(x, wg16[...])) * _dot(x, wu16[...])
        o_ref[...] = _pack_rows(_dot(h.astype(BF16), wd16[...]))


def _expert_mlp(meta, x_rows, w_gate, w_up, w_down, layer):
    n_rows = x_rows.shape[0]
    n_blocks = n_rows // MOE_ROW_BLOCK
    tb = MOE_ROW_BLOCK
    grid_spec = pltpu.PrefetchScalarGridSpec(
        num_scalar_prefetch=1,
        grid=(n_blocks,),
        in_specs=[pl.BlockSpec((tb, D_PACK), lambda i, meta: (i, 0)),
                  pl.BlockSpec((None, None, D_MODEL, MOE_HIDDEN), lambda i, meta: (layer, meta[i], 0, 0)),
                  pl.BlockSpec((None, None, D_MODEL, MOE_HIDDEN), lambda i, meta: (layer, meta[i], 0, 0)),
                  pl.BlockSpec((None, None, MOE_HIDDEN, D_MODEL), lambda i, meta: (layer, meta[i], 0, 0))],
        out_specs=pl.BlockSpec((tb, D_PACK), lambda i, meta: (i, 0)),
        scratch_shapes=[pltpu.VMEM((D_MODEL, MOE_HIDDEN), BF16), pltpu.VMEM((D_MODEL, MOE_HIDDEN), BF16),
                        pltpu.VMEM((MOE_HIDDEN, D_MODEL), BF16)],
    )
    return pl.pallas_call(
        _expert_kernel,
        grid_spec=grid_spec,
        out_shape=jax.ShapeDtypeStruct((n_rows, D_PACK), U32),
        compiler_params=_params("arbitrary"),
        name="moe_experts",
    )(meta, x_rows, w_gate, w_up, w_down)


def _combine_ln_kernel(y0_ref, y1_ref, r_ref, x_ref, g_ref, b_ref, o_ref, opk_ref):
    route = r_ref[...]
    g0 = route[:, 2:3]
    g1 = route[:, 3:4]
    ffn = g0 * _unpack_rows(y0_ref[...]) + g1 * _unpack_rows(y1_ref[...])
    out = _layer_norm(DEEPNORM_ALPHA * x_ref[...] + ffn, g_ref[...], b_ref[...])
    o_ref[...] = out
    opk_ref[...] = _pack_rows(out)


def _combine_ln(y0, y1, route, x, g, b):
    t = x.shape[0]
    xspec = pl.BlockSpec((ROW_TILE, D_MODEL), lambda i: (i, 0))
    pspec = pl.BlockSpec((ROW_TILE, D_PACK), lambda i: (i, 0))
    vspec = pl.BlockSpec((1, D_MODEL), lambda i: (0, 0))
    return pl.pallas_call(
        _combine_ln_kernel,
        grid=(t // ROW_TILE,),
        in_specs=[pspec, pspec, pl.BlockSpec((ROW_TILE, LANES), lambda i: (i, 0)), xspec, vspec, vspec],
        out_specs=[xspec, pspec],
        out_shape=[jax.ShapeDtypeStruct((t, D_MODEL), F32), jax.ShapeDtypeStruct((t, D_PACK), U32)],
        compiler_params=_params("parallel"),
        name="moe_combine_ln",
    )(y0, y1, route, x, g[None, :], b[None, :])


def _positions_kernel(r_ref, ps_ref, o_ref):
    route = r_ref[...]
    lane = lax.broadcasted_iota(jnp.int32, route.shape, 1)
    starts = ps_ref[...]
    out = jnp.zeros(route.shape, F32)
    for k in range(MOE_TOP_K):
        eid = route[:, k:k + 1].astype(jnp.int32)
        pos = jnp.sum(jnp.where(lane == eid, starts, 0.0), -1, keepdims=True) + route[:, 4 + k:5 + k]
        out = jnp.where(lane == k, pos, out)
    o_ref[...] = out.astype(jnp.int32)


def _positions(route, pad_start):
    t = route.shape[0]
    starts = jnp.pad(pad_start.astype(F32), (0, LANES - MOE_N_EXPERTS))[None, :]
    return pl.pallas_call(
        _positions_kernel,
        grid=(t // ROW_TILE,),
        in_specs=[pl.BlockSpec((ROW_TILE, LANES), lambda i: (i, 0)), pl.BlockSpec((1, LANES), lambda i: (0, 0))],
        out_specs=pl.BlockSpec((ROW_TILE, LANES), lambda i: (i, 0)),
        out_shape=jax.ShapeDtypeStruct((t, LANES), jnp.int32),
        compiler_params=_params("parallel"),
        name="moe_positions",
    )(route, starts)


def _moe(x, xpk, w_rg, b_rg, w_re, b_re, w_gate, w_up, w_down, layer, g, b):
    t = x.shape[0]
    tb = MOE_ROW_BLOCK
    route, cnt = _router(x, w_rg, b_rg, w_re, b_re)
    n_assign = t * MOE_TOP_K
    n_blocks = n_assign // tb + MOE_N_EXPERTS
    n_rows = n_blocks * tb
    counts = cnt[0, :MOE_N_EXPERTS].astype(jnp.int32)
    padded = (counts + tb - 1) // tb * tb
    pad_end = jnp.cumsum(padded)
    pos = _positions(route, pad_end - padded)
    pos0, pos1 = pos[:, 0], pos[:, 1]
    block_start = jnp.arange(n_blocks, dtype=jnp.int32) * tb
    block_e = jnp.minimum(jnp.sum((pad_end[None, :] <= block_start[:, None]).astype(jnp.int32), -1),
                          MOE_N_EXPERTS - 1)
    meta = jnp.concatenate([block_e, pad_end[-1:] // tb]).astype(jnp.int32)
    x_rows = _row_scatter(xpk, (pos0, pos1), n_rows)
    y_rows = _expert_mlp(meta, x_rows, w_gate, w_up, w_down, layer)
    y0 = _row_gather(y_rows, pos0)
    y1 = _row_gather(y_rows, pos1)
    return _combine_ln(y0, y1, route, x, g, b)


def _ssd_layer(x, xpk, w_in, conv_w, conv_b, dt_bias, a_log, d_skip, norm_w, w_out, g, b, bsz, seq):
    w_zxbc = w_in[:, :SSD_D_INNER + SSD_CONV_DIM].astype(BF16)
    w_dt = jnp.pad(w_in[:, SSD_D_INNER + SSD_CONV_DIM:], ((0, 0), (0, LANES - SSD_N_HEADS))).astype(BF16)
    zxbc, dt_raw = _in_proj(xpk, w_zxbc, w_dt, conv_w, conv_b, bsz, seq)
    y = _ssd_scan(zxbc, dt_raw, dt_bias, a_log, d_skip, norm_w, bsz, seq)
    return _proj_ln(y, w_out.astype(BF16), x, g, b, "ssd_out_proj_ln")


def _attn_layer(x, xpk, tabs, w_qkv, w_o, g, b, bsz, seq):
    perms = [_perm_matrix(dil) for _, dil in ATT_PATTERNS]
    perms_t = [p.T for p in perms]
    qkv = _qkv_proj(xpk, w_qkv.astype(BF16), tabs, perms)
    os_, sts = [], []
    for grp in range(ATT_N_GROUPS):
        o, st = _window_attention(qkv, grp, bsz, seq)
        os_.append(o)
        sts.append(st)
    return _merge_proj_ln(os_, sts, perms_t, w_o.astype(BF16), x, g, b)


def kernel(x, positions, ssd_w_in, ssd_conv_w, ssd_conv_b, ssd_dt_bias, ssd_a_log, ssd_d, ssd_norm_w, ssd_w_out,
           attn_w_qkv, attn_w_o, ln_g, ln_b, moe_w_router_group, moe_b_router_group, moe_w_router_expert,
           moe_b_router_expert, moe_w_gate, moe_w_up, moe_w_down):
    bsz, seq, d = x.shape
    t = bsz * seq
    h = x.reshape(t, d)
    hpk = _pack_rows_host(h)
    tabs = _rope_tables(positions)
    for i in range(DEPTH):
        j = i // N_MIXERS
        if i % N_MIXERS == 0:
            h, hpk = _ssd_layer(h, hpk, ssd_w_in[j], ssd_conv_w[j], ssd_conv_b[j], ssd_dt_bias[j], ssd_a_log[j],
                                ssd_d[j], ssd_norm_w[j], ssd_w_out[j], ln_g[i, 0], ln_b[i, 0], bsz, seq)
        else:
            h, hpk = _attn_layer(h, hpk, tabs, attn_w_qkv[j], attn_w_o[j], ln_g[i, 0], ln_b[i, 0], bsz, seq)
        h, hpk = _moe(h, hpk, moe_w_router_group[i], moe_b_router_group[i], moe_w_router_expert[i],
                      moe_b_router_expert[i], moe_w_gate, moe_w_up, moe_w_down, i, ln_g[i, 1], ln_b[i, 1])
    return h.reshape(bsz, seq, d)
```

```python
import functools

import jax
import jax.numpy as jnp
from jax import lax
from jax.experimental import pallas as pl
from jax.experimental.pallas import tpu as pltpu
from jax.experimental.pallas import tpu_sc as plsc

F32 = jnp.float32
BF16 = jnp.bfloat16
U32 = jnp.uint32

D_MODEL = 1024
D_PACK = D_MODEL // 2
DEPTH = 4
N_MIXERS = 2

SSD_D_INNER = 2048
SSD_HEAD_DIM = 64
SSD_N_HEADS = 32
SSD_N_GROUPS = 4
SSD_D_STATE = 128
SSD_D_CONV = 4
SSD_CHUNK = 128
SSD_GN = SSD_N_GROUPS * SSD_D_STATE
SSD_CONV_DIM = SSD_D_INNER + 2 * SSD_GN
SSD_GROUP_COLS = SSD_D_INNER // SSD_N_GROUPS
IN_PROJ_COLS = 1024
SSD_PAIR_UNROLL = 4

ATT_HEAD_DIM = 64
ATT_HEADS = 8
ATT_PATTERNS = ((128, 1), (512, 4), (2048, 16))
ATT_N_GROUPS = 3
ATT_OUT_DIM = ATT_HEADS * ATT_HEAD_DIM
ATT_QKV_DIM = ATT_N_GROUPS * 3 * ATT_OUT_DIM
ATT_BLOCK = 128
ROPE_THETA = 500000.0
ROPE_DIM = 16

MOE_N_GROUPS = 4
MOE_EPG = 8
MOE_N_EXPERTS = 32
MOE_TOP_K = 2
MOE_HIDDEN = 512
MOE_ROW_BLOCK = 512

DEEPNORM_ALPHA = (2 * DEPTH) ** 0.25
LN_EPS = 1e-5
RMS_EPS = 1e-5
NEG_INF = -1e30

LANES = 128
HALF = LANES // 2
VMEM_LIMIT = 56 * 1024 * 1024

ROW_TILE = 512


def _params(*sem):
    return pltpu.CompilerParams(dimension_semantics=sem, vmem_limit_bytes=VMEM_LIMIT)


def _silu(v):
    h = 0.5 * v
    return h + h * jnp.tanh(h)


def _layer_norm(r, g, b):
    mu = jnp.mean(r, -1, keepdims=True)
    d = r - mu
    var = jnp.mean(d * d, -1, keepdims=True)
    return d * lax.rsqrt(var + LN_EPS) * g + b


def _split3(v):
    hi = v.astype(BF16)
    r1 = v - hi.astype(F32)
    mid = r1.astype(BF16)
    lo = (r1 - mid.astype(F32)).astype(BF16)
    return hi, mid, lo


def _dot(a, b):
    return jnp.dot(a, b, preferred_element_type=F32)


def _dot_nt(a, b):
    return lax.dot_general(a, b, (((1,), (1,)), ((), ())), preferred_element_type=F32)


def _dot_tn(a, b):
    return lax.dot_general(a, b, (((0,), (0,)), ((), ())), preferred_element_type=F32)


def _pair_expand(mat, h0, lo_mask):
    rows = mat.shape[0]
    a = jnp.broadcast_to(mat[:, h0:h0 + 1], (rows, LANES))
    b = jnp.broadcast_to(mat[:, h0 + 1:h0 + 2], (rows, LANES))
    return jnp.where(lo_mask, a, b)


def _round_bf16_bits(bits):
    odd = (bits >> 16) & jnp.uint32(1)
    return (bits + jnp.uint32(0x7FFF) + odd) & jnp.uint32(0xFFFF0000)


def _pack_words(bits):
    r = _round_bf16_bits(bits)
    return r[:, :D_PACK] | (r[:, D_PACK:] >> 16)


def _pack_rows(v):
    return _pack_words(pltpu.bitcast(v, U32))


def _unpack_rows(p):
    hi = pltpu.bitcast(p & jnp.uint32(0xFFFF0000), F32)
    lo = pltpu.bitcast(p << 16, F32)
    return jnp.concatenate([hi, lo], axis=1)


def _pack_rows_host(v):
    return _pack_words(lax.bitcast_convert_type(v, U32))


def _row_gather(data, idx):
    n = idx.shape[0]
    d = data.shape[1]
    window = LANES
    dc = d // 2
    mesh = plsc.VectorSubcoreMesh(core_axis_name="core", subcore_axis_name="subcore")

    @functools.partial(pl.kernel, out_type=jax.ShapeDtypeStruct((n, d), data.dtype), mesh=mesh)
    def gather(x_hbm, i_hbm, o_hbm):
        for c in range(d // dc):
            def body(i_vmem, o_vmem, c=c):
                pltpu.sync_copy(x_hbm.at[i_vmem.at[0], pl.ds(c * dc, dc)], o_vmem)

            pltpu.emit_pipeline(
                body,
                grid=(n // window,),
                in_specs=[pl.BlockSpec((1, window), lambda i: (0, i))],
                out_specs=[pl.BlockSpec((window, dc), lambda i, c=c: (i, c))],
                core_axis_name=("core", "subcore"),
                dimension_semantics=(pltpu.PARALLEL,),
            )(i_hbm, o_hbm)

    return gather(data, idx.reshape(1, n))


def _row_scatter(data, idxs, n_rows):
    t, d = data.shape
    window = LANES
    dc = d // 2
    mesh = plsc.VectorSubcoreMesh(core_axis_name="core", subcore_axis_name="subcore")

    @functools.partial(pl.kernel, out_type=jax.ShapeDtypeStruct((n_rows, d), data.dtype), mesh=mesh)
    def scatter(x_hbm, *refs):
        o_hbm = refs[-1]
        for i_hbm in refs[:-1]:
            for c in range(d // dc):
                def body(x_vmem, i_vmem, c=c):
                    pltpu.sync_copy(x_vmem, o_hbm.at[i_vmem.at[0], pl.ds(c * dc, dc)])

                pltpu.emit_pipeline(
                    body,
                    grid=(t // window,),
                    in_specs=[pl.BlockSpec((window, dc), lambda i, c=c: (i, c)),
                              pl.BlockSpec((1, window), lambda i: (0, i))],
                    out_specs=[],
                    core_axis_name=("core", "subcore"),
                    dimension_semantics=(pltpu.PARALLEL,),
                )(x_hbm, i_hbm)

    return scatter(data, *[i.reshape(1, t) for i in idxs])


def _resident(shape):
    return pl.BlockSpec(shape, lambda *_: (0,) * len(shape), pipeline_mode=pl.Buffered(1))


def _in_proj_kernel(x_ref, w_ref, wdt_ref, cw_ref, cb_ref, o_ref, dt_ref, ext, tail):
    rows = x_ref.shape[0]
    tn = IN_PROJ_COLS
    n_z = SSD_D_INNER // tn

    @pl.when(pl.program_id(1) == 0)
    def _():
        tail[...] = jnp.zeros(tail.shape, F32)

    x = _unpack_rows(x_ref[...]).astype(BF16)
    for n in range(n_z):
        sl = slice(n * tn, (n + 1) * tn)
        o_ref[:, sl] = _dot(x, w_ref[:, sl]).astype(o_ref.dtype)
    for j in range(SSD_CONV_DIM // tn):
        sl = slice((n_z + j) * tn, (n_z + j + 1) * tn)
        cl = slice(j * tn, (j + 1) * tn)
        ext[j, 0:8, :] = tail[j]
        ext[j, 8:8 + rows, :] = _dot(x, w_ref[:, sl])
        tail[j] = ext[j, rows:rows + 8, :]
        acc = ext[j, 8:8 + rows, :] * cw_ref[3:4, cl] + cb_ref[:, cl]
        for k in range(SSD_D_CONV - 1):
            acc = acc + ext[j, 5 + k:5 + k + rows, :] * cw_ref[k:k + 1, cl]
        o_ref[:, sl] = _silu(acc).astype(o_ref.dtype)
    dt_ref[...] = _dot(x, wdt_ref[...])


def _in_proj(xpk, w_zxbc, w_dt, conv_w, conv_b, bsz, seq):
    t = xpk.shape[0]
    n = w_zxbc.shape[1]
    tiles = seq // ROW_TILE
    cw = jnp.pad(conv_w, ((0, 8 - SSD_D_CONV), (0, 0)))
    n_conv = SSD_CONV_DIM // IN_PROJ_COLS
    return pl.pallas_call(
        _in_proj_kernel,
        grid=(bsz, tiles),
        in_specs=[pl.BlockSpec((ROW_TILE, D_PACK), lambda b, s: (b * tiles + s, 0)),
                  _resident((D_MODEL, n)), _resident((D_MODEL, LANES)),
                  _resident((8, SSD_CONV_DIM)), _resident((1, SSD_CONV_DIM))],
        out_specs=[pl.BlockSpec((ROW_TILE, n), lambda b, s: (b * tiles + s, 0)),
                   pl.BlockSpec((ROW_TILE, LANES), lambda b, s: (b * tiles + s, 0))],
        out_shape=[jax.ShapeDtypeStruct((t, n), BF16), jax.ShapeDtypeStruct((t, LANES), F32)],
        scratch_shapes=[pltpu.VMEM((n_conv, ROW_TILE + 8, IN_PROJ_COLS), F32),
                        pltpu.VMEM((n_conv, 8, IN_PROJ_COLS), F32)],
        compiler_params=_params("arbitrary", "arbitrary"),
        name="ssd_in_proj",
    )(xpk, w_zxbc, w_dt, cw, conv_b[None, :])


def _ssd_kernel(z_ref, xs_ref, bc_ref, dt_ref, dtb_ref, alog_ref, dsk_ref, nw_ref, ex_ref, o_ref,
                state, cs_cols, cs_rows, dt_x, cb_all, y_off, y_grp, xd_all, cd_all):
    q = SSD_CHUNK
    pairs = SSD_GROUP_COLS // LANES

    @pl.when(pl.program_id(1) == 0)
    def _():
        state[...] = jnp.zeros(state.shape, F32)

    pre = dt_ref[...] + dtb_ref[...]
    dt = jnp.maximum(pre, 0.0) + jnp.log(1.0 + jnp.exp(-jnp.abs(pre)))
    a = -jnp.exp(alog_ref[...])
    row = lax.broadcasted_iota(jnp.int32, (q, q), 0)
    col = lax.broadcasted_iota(jnp.int32, (q, q), 1)
    causal = row >= col
    lo_mask = col < HALF
    tri = jnp.where(causal, 1.0, 0.0).astype(BF16)
    cs3 = _dot(tri, jnp.concatenate(_split3(dt * a), axis=1))
    cs = cs3[:, :LANES] + cs3[:, LANES:2 * LANES] + cs3[:, 2 * LANES:]
    cs_rows[...] = cs.T
    for h in range(SSD_N_HEADS):
        cs_cols[h] = jnp.broadcast_to(cs[:, h:h + 1], (q, q))
    dt3 = _dot(jnp.concatenate(_split3(dt), axis=0), ex_ref[...])
    dt_x[...] = dt3[:q] + dt3[q:2 * q] + dt3[2 * q:]
    for g in range(SSD_N_GROUPS):
        bg = bc_ref[:, g * SSD_D_STATE:(g + 1) * SSD_D_STATE]
        cg = bc_ref[:, SSD_GN + g * SSD_D_STATE:SSD_GN + (g + 1) * SSD_D_STATE]
        cb_all[g] = _dot_nt(cg, bg)
        y_off[:, g * SSD_GROUP_COLS:(g + 1) * SSD_GROUP_COLS] = _dot(cg, state[g].astype(BF16))

    def group_body(g, carry):
        cb = cb_all[g]

        def pair_body(pp, ssq):
            p = g * pairs + pp
            h0 = 2 * p
            x0 = pl.multiple_of(p * LANES, LANES)
            l0 = pl.multiple_of(pp * LANES, LANES)
            cols = (cs_cols[h0], cs_cols[h0 + 1])
            csx = jnp.where(lo_mask, cols[0], cols[1])
            xp = xs_ref[:, pl.ds(x0, LANES)].astype(F32)
            xdt = xp * dt_x[:, pl.ds(x0, LANES)]
            xdt16 = xdt.astype(BF16)
            last = csx[q - 1:q, :]
            halves = []
            for hh in range(2):
                diff = cols[hh] - cs_rows[pl.ds(h0 + hh, 1), :]
                decay = jnp.exp(jnp.where(causal, diff, -jnp.inf))
                halves.append(_dot((cb * decay).astype(BF16), xdt16))
            y = jnp.where(lo_mask, halves[0], halves[1])
            y = y + y_off[:, pl.ds(x0, LANES)] * jnp.exp(csx) + xp * dsk_ref[:, pl.ds(x0, LANES)]
            y = y * _silu(z_ref[:, pl.ds(x0, LANES)].astype(F32))
            y_grp[:, pl.ds(l0, LANES)] = y
            xd_all[:, pl.ds(x0, LANES)] = (xdt * jnp.exp(last - csx)).astype(BF16)
            cd_all[:, pl.ds(x0, LANES)] = jnp.exp(last)
            return ssq + jnp.sum(y * y, -1, keepdims=True)

        ssq = lax.fori_loop(0, pairs, pair_body, jnp.zeros((q, 1), F32), unroll=SSD_PAIR_UNROLL)
        inv = lax.rsqrt(ssq * (1.0 / SSD_GROUP_COLS) + RMS_EPS)
        g0 = pl.multiple_of(g * SSD_GROUP_COLS, SSD_GROUP_COLS)
        o_ref[:, pl.ds(g0, SSD_GROUP_COLS)] = (
            y_grp[...] * inv * nw_ref[:, pl.ds(g0, SSD_GROUP_COLS)]).astype(o_ref.dtype)
        return carry

    lax.fori_loop(0, SSD_N_GROUPS, group_body, 0)

    for g in range(SSD_N_GROUPS):
        gs = slice(g * SSD_GROUP_COLS, (g + 1) * SSD_GROUP_COLS)
        bg = bc_ref[:, g * SSD_D_STATE:(g + 1) * SSD_D_STATE]
        state[g] = state[g] * cd_all[:, gs] + _dot_tn(bg, xd_all[:, gs])


def _ssd_scan(zxbc, dt_raw, dt_bias, a_log, d_skip, norm_w, bsz, seq):
    t = bsz * seq
    nc = seq // SSD_CHUNK
    q = SSD_CHUNK
    pad_h = LANES - SSD_N_HEADS
    dtb = jnp.pad(dt_bias, (0, pad_h))[None, :]
    alog = jnp.pad(a_log, (0, pad_h))[None, :]
    dsk = jnp.repeat(d_skip, SSD_HEAD_DIM)[None, :]
    nw = norm_w[None, :]
    expand = (jnp.arange(LANES)[:, None] == jnp.arange(SSD_D_INNER)[None, :] // SSD_HEAD_DIM).astype(BF16)

    def const(shape):
        return pl.BlockSpec(shape, lambda b, c: (0, 0))

    return pl.pallas_call(
        _ssd_kernel,
        grid=(bsz, nc),
        in_specs=[pl.BlockSpec((q, SSD_D_INNER), lambda b, c: (b * nc + c, 0)),
                  pl.BlockSpec((q, SSD_D_INNER), lambda b, c: (b * nc + c, 1)),
                  pl.BlockSpec((q, 2 * SSD_GN), lambda b, c: (b * nc + c, 4)),
                  pl.BlockSpec((q, LANES), lambda b, c: (b * nc + c, 0)),
                  const((1, LANES)), const((1, LANES)),
                  const((1, SSD_D_INNER)), const((1, SSD_D_INNER)), const((LANES, SSD_D_INNER))],
        out_specs=pl.BlockSpec((q, SSD_D_INNER), lambda b, c: (b * nc + c, 0)),
        out_shape=jax.ShapeDtypeStruct((t, SSD_D_INNER), BF16),
        scratch_shapes=[pltpu.VMEM((SSD_N_GROUPS, SSD_D_STATE, SSD_GROUP_COLS), F32),
                        pltpu.VMEM((SSD_N_HEADS, q, q), F32),
                        pltpu.VMEM((LANES, q), F32),
                        pltpu.VMEM((q, SSD_D_INNER), F32),
                        pltpu.VMEM((SSD_N_GROUPS, q, q), F32),
                        pltpu.VMEM((q, SSD_D_INNER), F32),
                        pltpu.VMEM((q, SSD_GROUP_COLS), F32),
                        pltpu.VMEM((q, SSD_D_INNER), BF16),
                        pltpu.VMEM((1, SSD_D_INNER), F32)],
        compiler_params=_params("arbitrary", "arbitrary"),
        name="ssd_scan",
    )(zxbc, zxbc, zxbc, dt_raw, dtb, alog, dsk, nw, expand)


def _proj_ln_kernel(y_ref, w_ref, x_ref, g_ref, b_ref, o_ref, opk_ref):
    mix = _dot(y_ref[...].astype(BF16), w_ref[...])
    out = _layer_norm(DEEPNORM_ALPHA * x_ref[...] + mix, g_ref[...], b_ref[...])
    o_ref[...] = out
    opk_ref[...] = _pack_rows(out)


def _proj_ln(y, w, x, g, b, name):
    t, k = y.shape
    return pl.pallas_call(
        _proj_ln_kernel,
        grid=(t // ROW_TILE,),
        in_specs=[pl.BlockSpec((ROW_TILE, k), lambda i: (i, 0)),
                  _resident((k, D_MODEL)),
                  pl.BlockSpec((ROW_TILE, D_MODEL), lambda i: (i, 0)),
                  pl.BlockSpec((1, D_MODEL), lambda i: (0, 0)),
                  pl.BlockSpec((1, D_MODEL), lambda i: (0, 0))],
        out_specs=[pl.BlockSpec((ROW_TILE, D_MODEL), lambda i: (i, 0)),
                   pl.BlockSpec((ROW_TILE, D_PACK), lambda i: (i, 0))],
        out_shape=[jax.ShapeDtypeStruct((t, D_MODEL), F32),
                   jax.ShapeDtypeStruct((t, D_PACK), U32)],
        compiler_params=_params("parallel"),
        name=name,
    )(y, w, x, g[None, :], b[None, :])


def _rope_table_kernel(pos_ref, freq_ref, c_ref, s1_ref, s2_ref, tok):
    ang = pos_ref[...].astype(F32) * freq_ref[...]
    d = lax.broadcasted_iota(jnp.int32, ang.shape, 1) % ATT_HEAD_DIM
    cos, sin = jnp.cos(ang), jnp.sin(ang)
    half = ROPE_DIM // 2
    tabs = (jnp.where(d < ROPE_DIM, cos, 1.0),
            jnp.where(d < half, -sin, 0.0),
            jnp.where((d >= half) & (d < ROPE_DIM), sin, 0.0))
    rows = ang.shape[0]
    for ti, (tab, out) in enumerate(zip(tabs, (c_ref, s1_ref, s2_ref))):
        tok[ti] = tab
        for grp, (_, dil) in enumerate(ATT_PATTERNS):
            if dil == 1:
                out[grp] = tab
            else:
                n = rows // dil
                for r in range(dil):
                    out[grp, r * n:(r + 1) * n, :] = tok[ti, pl.ds(r, n, stride=dil), :]


def _residue_order(v, dil):
    tail = v.shape[1:]
    return v.reshape(-1, ROW_TILE // dil, dil, *tail).swapaxes(1, 2).reshape(-1, *tail)


def _perm_matrix(dil):
    src = _residue_order(jnp.arange(ROW_TILE, dtype=jnp.int32), dil)
    return (src[:, None] == jnp.arange(ROW_TILE, dtype=jnp.int32)[None, :]).astype(BF16)


def _rope_tables(positions):
    t = positions.size
    half = ROPE_DIM // 2
    inv_freq = ROPE_THETA ** (-jnp.arange(0, ROPE_DIM, 2, dtype=F32) / ROPE_DIM)
    d = jnp.arange(LANES) % ATT_HEAD_DIM
    freq = jnp.where(d < ROPE_DIM, inv_freq[d % half], 0.0).astype(F32)[None, :]
    tab = jax.ShapeDtypeStruct((ATT_N_GROUPS, t, LANES), F32)
    ospec = pl.BlockSpec((ATT_N_GROUPS, ROW_TILE, LANES), lambda i: (0, i, 0))
    return pl.pallas_call(
        _rope_table_kernel,
        grid=(t // ROW_TILE,),
        in_specs=[pl.BlockSpec((ROW_TILE, 1), lambda i: (i, 0)),
                  pl.BlockSpec((1, LANES), lambda i: (0, 0))],
        out_specs=[ospec, ospec, ospec],
        out_shape=[tab, tab, tab],
        scratch_shapes=[pltpu.VMEM((3, ROW_TILE, LANES), F32)],
        compiler_params=_params("parallel"),
        name="rope_tables",
    )(positions.reshape(t, 1), freq)


def _qkv_kernel(x_ref, p4_ref, p16_ref, w_ref, c_ref, s1_ref, s2_ref, o_ref):
    x16 = _unpack_rows(x_ref[...]).astype(BF16)
    xs = [x16, _dot(p4_ref[...], x16).astype(BF16), _dot(p16_ref[...], x16).astype(BF16)]
    reps = ATT_OUT_DIM // LANES
    half = ROPE_DIM // 2
    for grp in range(ATT_N_GROUPS):
        c = jnp.concatenate([c_ref[grp]] * reps, axis=1)
        s1 = jnp.concatenate([s1_ref[grp]] * reps, axis=1)
        s2 = jnp.concatenate([s2_ref[grp]] * reps, axis=1)
        for kind in range(3):
            j = grp * 3 + kind
            sl = slice(j * ATT_OUT_DIM, (j + 1) * ATT_OUT_DIM)
            acc = _dot(xs[grp], w_ref[:, sl])
            if kind < 2:
                up = pltpu.roll(acc, ATT_OUT_DIM - half, 1)
                down = pltpu.roll(acc, half, 1)
                acc = acc * c + up * s1 + down * s2
            if kind == 0:
                acc = acc * (ATT_HEAD_DIM ** -0.5)
            o_ref[:, sl] = acc.astype(o_ref.dtype)


def _qkv_proj(xpk, w, tabs, perms):
    t = xpk.shape[0]
    tab_spec = pl.BlockSpec((ATT_N_GROUPS, ROW_TILE, LANES), lambda i: (0, i, 0))
    return pl.pallas_call(
        _qkv_kernel,
        grid=(t // ROW_TILE,),
        in_specs=[pl.BlockSpec((ROW_TILE, D_PACK), lambda i: (i, 0)),
                  _resident((ROW_TILE, ROW_TILE)), _resident((ROW_TILE, ROW_TILE)),
                  _resident((D_MODEL, ATT_QKV_DIM)), tab_spec, tab_spec, tab_spec],
        out_specs=pl.BlockSpec((ROW_TILE, ATT_QKV_DIM), lambda i: (i, 0)),
        out_shape=jax.ShapeDtypeStruct((t, ATT_QKV_DIM), BF16),
        compiler_params=_params("parallel"),
        name="qkv_rope",
    )(xpk, perms[1], perms[2], w, *tabs)


def _attn_kernel(q_ref, kp_ref, kc_ref, vp_ref, vc_ref, o_ref, st_ref):
    w = ATT_BLOCK
    i = pl.program_id(2)
    qi = lax.broadcasted_iota(jnp.int32, (w, 2 * w), 0)
    kk = lax.broadcasted_iota(jnp.int32, (w, 2 * w), 1)
    first_key = jnp.where(i > 0, 0, w)
    valid = (kk >= qi) & (kk <= qi + w) & (kk >= first_key)
    lane = lax.broadcasted_iota(jnp.int32, (w, LANES), 1)
    lo_mask = lane < HALF

    def rows(ref):
        return ref[...].reshape(w, ref.shape[-1])

    q = rows(q_ref)
    k = jnp.concatenate([rows(kp_ref), rows(kc_ref)], axis=0)
    v = jnp.concatenate([rows(vp_ref), rows(vc_ref)], axis=0)
    stats = jnp.zeros((w, LANES), F32)
    zero = jnp.zeros((), q.dtype)
    parts = []
    for p in range(ATT_HEADS // 2):
        sl = slice(p * LANES, (p + 1) * LANES)
        qp, kp, vp = q[:, sl], k[:, sl], v[:, sl]
        outs = []
        for hh in range(2):
            h = 2 * p + hh
            qm = jnp.where(lo_mask if hh == 0 else ~lo_mask, qp, zero)
            s = jnp.where(valid, _dot_nt(qm, kp), NEG_INF)
            m = jnp.max(s, -1, keepdims=True)
            pr = jnp.exp(s - m)
            l = jnp.sum(pr, -1, keepdims=True)
            outs.append(_dot(pr.astype(v.dtype), vp) / l)
            stats = jnp.where(lane == h, m, stats)
            stats = jnp.where(lane == ATT_HEADS + h, l, stats)
        parts.append(jnp.where(lo_mask, outs[0], outs[1]).astype(o_ref.dtype))
    o_ref[...] = jnp.concatenate(parts, axis=1).reshape(o_ref.shape)
    st_ref[...] = stats.reshape(st_ref.shape)


def _window_attention(qkv, grp, bsz, seq):
    _, dil = ATT_PATTERNS[grp]
    w = ATT_BLOCK
    t = bsz * seq
    chunk = ROW_TILE // dil
    tiles = w // chunk if chunk < w else 1
    span = dil * w
    nb = seq // span
    col0 = grp * 3

    if tiles == 1:
        per_b = seq // w
        stride = span // w

        def spec(width, col, prev):
            def imap(b, r, i):
                blk = jnp.maximum(i - 1, 0) if prev else i
                return (b * per_b + blk * stride + r, col)
            return pl.BlockSpec((w, width), imap)

        qkv_v, o_shape, st_shape = qkv, (t, ATT_OUT_DIM), (t, LANES)
    else:
        def spec(width, col, prev):
            def imap(b, r, i):
                blk = jnp.maximum(i - 1, 0) if prev else i
                return (b, blk, 0, r, 0, col)
            return pl.BlockSpec((None, None, tiles, None, chunk, width), imap)

        lead = (bsz, nb, tiles, dil, chunk)
        qkv_v, o_shape, st_shape = qkv.reshape(*lead, ATT_QKV_DIM), (*lead, ATT_OUT_DIM), (*lead, LANES)

    o, st = pl.pallas_call(
        _attn_kernel,
        grid=(bsz, dil, nb),
        in_specs=[spec(ATT_OUT_DIM, col0, False), spec(ATT_OUT_DIM, col0 + 1, True),
                  spec(ATT_OUT_DIM, col0 + 1, False), spec(ATT_OUT_DIM, col0 + 2, True),
                  spec(ATT_OUT_DIM, col0 + 2, False)],
        out_specs=[spec(ATT_OUT_DIM, 0, False), spec(LANES, 0, False)],
        out_shape=[jax.ShapeDtypeStruct(o_shape, BF16), jax.ShapeDtypeStruct(st_shape, F32)],
        compiler_params=_params("parallel", "parallel", "arbitrary"),
        name=f"window_attn_d{dil}",
    )(qkv_v, qkv_v, qkv_v, qkv_v, qkv_v)
    return o.reshape(t, ATT_OUT_DIM), st.reshape(t, LANES)


def _merge_proj_ln_kernel(o1_ref, o2_ref, o3_ref, s1_ref, s2_ref, s3_ref, p4t_ref, p16t_ref, w_ref, x_ref,
                          g_ref, b_ref, o_ref, opk_ref):
    rows = o1_ref.shape[0]
    lane = lax.broadcasted_iota(jnp.int32, (rows, LANES), 1)
    lo_mask = lane < HALF

    def to_token_order(pt, val):
        if val.dtype == BF16:
            return _dot(pt, val)
        return sum(_dot(pt, term) for term in _split3(val))

    p4t, p16t = p4t_ref[...], p16t_ref[...]
    outs = [o1_ref[...].astype(F32), to_token_order(p4t, o2_ref[...]), to_token_order(p16t, o3_ref[...])]
    sts = [s1_ref[...], to_token_order(p4t, s2_ref[...]), to_token_order(p16t, s3_ref[...])]
    mx = jnp.maximum(jnp.maximum(sts[0], sts[1]), sts[2])
    wgts = [pltpu.roll(s, LANES - ATT_HEADS, 1) * jnp.exp(s - mx) for s in sts]
    den = wgts[0] + wgts[1] + wgts[2]
    den = jnp.where(lane < ATT_HEADS, den, 1.0)
    coefs = [wg / den for wg in wgts]
    parts = []
    for p in range(ATT_HEADS // 2):
        sl = slice(p * LANES, (p + 1) * LANES)
        acc = jnp.zeros((rows, LANES), F32)
        for gi in range(ATT_N_GROUPS):
            acc = acc + _pair_expand(coefs[gi], 2 * p, lo_mask) * outs[gi][:, sl]
        parts.append(acc.astype(BF16))
    mix = _dot(jnp.concatenate(parts, axis=1), w_ref[...])
    out = _layer_norm(DEEPNORM_ALPHA * x_ref[...] + mix, g_ref[...], b_ref[...])
    o_ref[...] = out
    opk_ref[...] = _pack_rows(out)


def _merge_proj_ln(os_, sts, perms_t, w, x, g, b):
    t = x.shape[0]
    tm = ROW_TILE
    ospec = pl.BlockSpec((tm, ATT_OUT_DIM), lambda i: (i, 0))
    sspec = pl.BlockSpec((tm, LANES), lambda i: (i, 0))
    xspec = pl.BlockSpec((tm, D_MODEL), lambda i: (i, 0))
    vspec = pl.BlockSpec((1, D_MODEL), lambda i: (0, 0))
    pspec = _resident((tm, tm))
    return pl.pallas_call(
        _merge_proj_ln_kernel,
        grid=(t // tm,),
        in_specs=[ospec] * 3 + [sspec] * 3 + [pspec, pspec, _resident((ATT_OUT_DIM, D_MODEL)),
                                              xspec, vspec, vspec],
        out_specs=[xspec, pl.BlockSpec((tm, D_PACK), lambda i: (i, 0))],
        out_shape=[jax.ShapeDtypeStruct((t, D_MODEL), F32), jax.ShapeDtypeStruct((t, D_PACK), U32)],
        compiler_params=_params("parallel"),
        name="attn_merge_proj_ln",
    )(*os_, *sts, perms_t[1], perms_t[2], w, x, g[None, :], b[None, :])


def _router_kernel(x_ref, w2_ref, b_ref, o_ref, cnt_ref, carry):
    @pl.when(pl.program_id(0) == 0)
    def _():
        carry[...] = jnp.zeros(carry.shape, F32)

    x = x_ref[...]
    xhi = x.astype(BF16)
    xlo = (x - xhi.astype(F32)).astype(BF16)
    w2 = w2_ref[...]
    hi2 = _dot(xhi, w2)
    logits = hi2[:, :LANES] + hi2[:, LANES:] + _dot(xlo, w2[:, :LANES]) + b_ref[...]
    rows = logits.shape[0]
    lane = lax.broadcasted_iota(jnp.int32, (rows, LANES), 1)
    big = jnp.int32(LANES)

    def top1(vals, mask):
        v = jnp.where(mask, vals, -jnp.inf)
        m = jnp.max(v, -1, keepdims=True)
        idx = jnp.min(jnp.where(v == m, lane, big), -1, keepdims=True)
        return v, m, idx

    gmask = lane < MOE_N_GROUPS
    gv, gm, gidx = top1(logits, gmask)
    g_w = 1.0 / jnp.sum(jnp.exp(gv - gm), -1, keepdims=True)
    e_lo = MOE_N_GROUPS + gidx * MOE_EPG
    emask = (lane >= e_lo) & (lane < e_lo + MOE_EPG)
    ev, m1, i1 = top1(logits, emask)
    zsum = jnp.sum(jnp.exp(ev - m1), -1, keepdims=True)
    _, m2, i2 = top1(logits, emask & (lane != i1))
    p1 = 1.0 / zsum
    p2 = jnp.exp(m2 - m1) / zsum
    tot = p1 + p2
    e1 = i1 - MOE_N_GROUPS
    e2 = i2 - MOE_N_GROUPS

    oh1 = jnp.where(lane == e1, 1.0, 0.0)
    oh2 = jnp.where(lane == e2, 1.0, 0.0)
    oh = oh1 + oh2
    ri = lax.broadcasted_iota(jnp.int32, (rows, rows), 0)
    ci = lax.broadcasted_iota(jnp.int32, (rows, rows), 1)
    strict = jnp.where(ri > ci, 1.0, 0.0).astype(BF16)
    before = _dot(strict, oh.astype(BF16)) + carry[...]
    rank1 = jnp.sum(oh1 * before, -1, keepdims=True)
    rank2 = jnp.sum(oh2 * before, -1, keepdims=True)
    carry[...] = carry[...] + jnp.sum(oh, 0, keepdims=True)
    cnt_ref[...] = carry[...]

    vals = [e1.astype(F32), e2.astype(F32), g_w * (p1 / tot), g_w * (p2 / tot), rank1, rank2]
    out = jnp.zeros((rows, LANES), F32)
    for j, val in enumerate(vals):
        out = jnp.where(lane == j, val, out)
    o_ref[...] = out


def _router(x, w_rg, b_rg, w_re, b_re):
    t = x.shape[0]
    n_log = MOE_N_GROUPS + MOE_N_EXPERTS
    w = jnp.pad(jnp.concatenate([w_rg, w_re], axis=1), ((0, 0), (0, LANES - n_log)))
    whi = w.astype(BF16)
    w2 = jnp.concatenate([whi, (w - whi.astype(F32)).astype(BF16)], axis=1)
    bias = jnp.pad(jnp.concatenate([b_rg, b_re]), (0, LANES - n_log))[None, :]
    return pl.pallas_call(
        _router_kernel,
        grid=(t // ROW_TILE,),
        in_specs=[pl.BlockSpec((ROW_TILE, D_MODEL), lambda i: (i, 0)), _resident((D_MODEL, 2 * LANES)),
                  pl.BlockSpec((1, LANES), lambda i: (0, 0))],
        out_specs=[pl.BlockSpec((ROW_TILE, LANES), lambda i: (i, 0)),
                   pl.BlockSpec((1, LANES), lambda i: (0, 0))],
        out_shape=[jax.ShapeDtypeStruct((t, LANES), F32), jax.ShapeDtypeStruct((1, LANES), F32)],
        scratch_shapes=[pltpu.VMEM((1, LANES), F32)],
        compiler_params=_params("arbitrary"),
        name="moe_router",
    )(x, w2, bias)


def _expert_kernel(meta_ref, x_ref, wg_ref, wu_ref, wd_ref, o_ref, wg16, wu16, wd16):
    i = pl.program_id(0)
    n_blocks = pl.num_programs(0)

    @pl.when(i < meta_ref[n_blocks])
    def _():
        @pl.when((i == 0) | (meta_ref[i] != meta_ref[jnp.maximum(i - 1, 0)]))
        def _():
            wg16[...] = wg_ref[...].astype(BF16)
            wu16[...] = wu_ref[...].astype(BF16)
            wd16[...] = wd_ref[...].astype(BF16)

        x = _unpack_rows(x_ref[...]).astype(BF16)
        h = _silu(_dot(x, wg16[...])) * _dot(x, wu16[...])
        o_ref[...] = _pack_rows(_dot(h.astype(BF16), wd16[...]))


def _expert_mlp(meta, x_rows, w_gate, w_up, w_down, layer):
    n_rows = x_rows.shape[0]
    n_blocks = n_rows // MOE_ROW_BLOCK
    tb = MOE_ROW_BLOCK
    grid_spec = pltpu.PrefetchScalarGridSpec(
        num_scalar_prefetch=1,
        grid=(n_blocks,),
        in_specs=[pl.BlockSpec((tb, D_PACK), lambda i, meta: (i, 0)),
                  pl.BlockSpec((None, None, D_MODEL, MOE_HIDDEN), lambda i, meta: (layer, meta[i], 0, 0)),
                  pl.BlockSpec((None, None, D_MODEL, MOE_HIDDEN), lambda i, meta: (layer, meta[i], 0, 0)),
                  pl.BlockSpec((None, None, MOE_HIDDEN, D_MODEL), lambda i, meta: (layer, meta[i], 0, 0))],
        out_specs=pl.BlockSpec((tb, D_PACK), lambda i, meta: (i, 0)),
        scratch_shapes=[pltpu.VMEM((D_MODEL, MOE_HIDDEN), BF16), pltpu.VMEM((D_MODEL, MOE_HIDDEN), BF16),
                        pltpu.VMEM((MOE_HIDDEN, D_MODEL), BF16)],
    )
    return pl.pallas_call(
        _expert_kernel,
        grid_spec=grid_spec,
        out_shape=jax.ShapeDtypeStruct((n_rows, D_PACK), U32),
        compiler_params=_params("arbitrary"),
        name="moe_experts",
    )(meta, x_rows, w_gate, w_up, w_down)


def _combine_ln_kernel(y0_ref, y1_ref, r_ref, x_ref, g_ref, b_ref, o_ref, opk_ref):
    route = r_ref[...]
    g0 = route[:, 2:3]
    g1 = route[:, 3:4]
    ffn = g0 * _unpack_rows(y0_ref[...]) + g1 * _unpack_rows(y1_ref[...])
    out = _layer_norm(DEEPNORM_ALPHA * x_ref[...] + ffn, g_ref[...], b_ref[...])
    o_ref[...] = out
    opk_ref[...] = _pack_rows(out)


def _combine_ln(y0, y1, route, x, g, b):
    t = x.shape[0]
    xspec = pl.BlockSpec((ROW_TILE, D_MODEL), lambda i: (i, 0))
    pspec = pl.BlockSpec((ROW_TILE, D_PACK), lambda i: (i, 0))
    vspec = pl.BlockSpec((1, D_MODEL), lambda i: (0, 0))
    return pl.pallas_call(
        _combine_ln_kernel,
        grid=(t // ROW_TILE,),
        in_specs=[pspec, pspec, pl.BlockSpec((ROW_TILE, LANES), lambda i: (i, 0)), xspec, vspec, vspec],
        out_specs=[xspec, pspec],
        out_shape=[jax.ShapeDtypeStruct((t, D_MODEL), F32), jax.ShapeDtypeStruct((t, D_PACK), U32)],
        compiler_params=_params("parallel"),
        name="moe_combine_ln",
    )(y0, y1, route, x, g[None, :], b[None, :])


def _positions_kernel(r_ref, ps_ref, o_ref):
    route = r_ref[...]
    lane = lax.broadcasted_iota(jnp.int32, route.shape, 1)
    starts = ps_ref[...]
    out = jnp.zeros(route.shape, F32)
    for k in range(MOE_TOP_K):
        eid = route[:, k:k + 1].astype(jnp.int32)
        pos = jnp.sum(jnp.where(lane == eid, starts, 0.0), -1, keepdims=True) + route[:, 4 + k:5 + k]
        out = jnp.where(lane == k, pos, out)
    o_ref[...] = out.astype(jnp.int32)


def _positions(route, pad_start):
    t = route.shape[0]
    starts = jnp.pad(pad_start.astype(F32), (0, LANES - MOE_N_EXPERTS))[None, :]
    return pl.pallas_call(
        _positions_kernel,
        grid=(t // ROW_TILE,),
        in_specs=[pl.BlockSpec((ROW_TILE, LANES), lambda i: (i, 0)), pl.BlockSpec((1, LANES), lambda i: (0, 0))],
        out_specs=pl.BlockSpec((ROW_TILE, LANES), lambda i: (i, 0)),
        out_shape=jax.ShapeDtypeStruct((t, LANES), jnp.int32),
        compiler_params=_params("parallel"),
        name="moe_positions",
    )(route, starts)


def _moe(x, xpk, w_rg, b_rg, w_re, b_re, w_gate, w_up, w_down, layer, g, b):
    t = x.shape[0]
    tb = MOE_ROW_BLOCK
    route, cnt = _router(x, w_rg, b_rg, w_re, b_re)
    n_assign = t * MOE_TOP_K
    n_blocks = n_assign // tb + MOE_N_EXPERTS
    n_rows = n_blocks * tb
    counts = cnt[0, :MOE_N_EXPERTS].astype(jnp.int32)
    padded = (counts + tb - 1) // tb * tb
    pad_end = jnp.cumsum(padded)
    pos = _positions(route, pad_end - padded)
    pos0, pos1 = pos[:, 0], pos[:, 1]
    block_start = jnp.arange(n_blocks, dtype=jnp.int32) * tb
    block_e = jnp.minimum(jnp.sum((pad_end[None, :] <= block_start[:, None]).astype(jnp.int32), -1),
                          MOE_N_EXPERTS - 1)
    meta = jnp.concatenate([block_e, pad_end[-1:] // tb]).astype(jnp.int32)
    x_rows = _row_scatter(xpk, (pos0, pos1), n_rows)
    y_rows = _expert_mlp(meta, x_rows, w_gate, w_up, w_down, layer)
    y0 = _row_gather(y_rows, pos0)
    y1 = _row_gather(y_rows, pos1)
    return _combine_ln(y0, y1, route, x, g, b)


def _ssd_layer(x, xpk, w_in, conv_w, conv_b, dt_bias, a_log, d_skip, norm_w, w_out, g, b, bsz, seq):
    w_zxbc = w_in[:, :SSD_D_INNER + SSD_CONV_DIM].astype(BF16)
    w_dt = jnp.pad(w_in[:, SSD_D_INNER + SSD_CONV_DIM:], ((0, 0), (0, LANES - SSD_N_HEADS))).astype(BF16)
    zxbc, dt_raw = _in_proj(xpk, w_zxbc, w_dt, conv_w, conv_b, bsz, seq)
    y = _ssd_scan(zxbc, dt_raw, dt_bias, a_log, d_skip, norm_w, bsz, seq)
    return _proj_ln(y, w_out.astype(BF16), x, g, b, "ssd_out_proj_ln")


def _attn_layer(x, xpk, tabs, w_qkv, w_o, g, b, bsz, seq):
    perms = [_perm_matrix(dil) for _, dil in ATT_PATTERNS]
    perms_t = [p.T for p in perms]
    qkv = _qkv_proj(xpk, w_qkv.astype(BF16), tabs, perms)
    os_, sts = [], []
    for grp in range(ATT_N_GROUPS):
        o, st = _window_attention(qkv, grp, bsz, seq)
        os_.append(o)
        sts.append(st)
    return _merge_proj_ln(os_, sts, perms_t, w_o.astype(BF16), x, g, b)


def kernel(x, positions, ssd_w_in, ssd_conv_w, ssd_conv_b, ssd_dt_bias, ssd_a_log, ssd_d, ssd_norm_w, ssd_w_out,
           attn_w_qkv, attn_w_o, ln_g, ln_b, moe_w_router_group, moe_b_router_group, moe_w_router_expert,
           moe_b_router_expert, moe_w_gate, moe_w_up, moe_w_down):
    bsz, seq, d = x.shape
    t = bsz * seq
    h = x.reshape(t, d)
    hpk = _pack_rows_host(h)
    tabs = _rope_tables(positions)
    for i in range(DEPTH):
        j = i // N_MIXERS
        if i % N_MIXERS == 0:
            h, hpk = _ssd_layer(h, hpk, ssd_w_in[j], ssd_conv_w[j], ssd_conv_b[j], ssd_dt_bias[j], ssd_a_log[j],
                                ssd_d[j], ssd_norm_w[j], ssd_w_out[j], ln_g[i, 0], ln_b[i, 0], bsz, seq)
        else:
            h, hpk = _attn_layer(h, hpk, tabs, attn_w_qkv[j], attn_w_o[j], ln_g[i, 0], ln_b[i, 0], bsz, seq)
        h, hpk = _moe(h, hpk, moe_w_router_group[i], moe_b_router_group[i], moe_w_router_expert[i],
                      moe_b_router_expert[i], moe_w_gate, moe_w_up, moe_w_down, i, ln_g[i, 1], ln_b[i, 1])
    return h.reshape(bsz, seq, d)
```

```python
import functools

import jax
import jax.numpy as jnp
from jax import lax
from jax.experimental import pallas as pl
from jax.experimental.pallas import tpu as pltpu
from jax.experimental.pallas import tpu_sc as plsc

F32 = jnp.float32
BF16 = jnp.bfloat16
U32 = jnp.uint32

D_MODEL = 1024
D_PACK = D_MODEL // 2
DEPTH = 4
N_MIXERS = 2

SSD_D_INNER = 2048
SSD_HEAD_DIM = 64
SSD_N_HEADS = 32
SSD_N_GROUPS = 4
SSD_D_STATE = 128
SSD_D_CONV = 4
SSD_CHUNK = 128
SSD_GN = SSD_N_GROUPS * SSD_D_STATE
SSD_CONV_DIM = SSD_D_INNER + 2 * SSD_GN
SSD_GROUP_COLS = SSD_D_INNER // SSD_N_GROUPS
IN_PROJ_COLS = 1024
SSD_PAIR_UNROLL = 4

ATT_HEAD_DIM = 64
ATT_HEADS = 8
ATT_PATTERNS = ((128, 1), (512, 4), (2048, 16))
ATT_N_GROUPS = 3
ATT_OUT_DIM = ATT_HEADS * ATT_HEAD_DIM
ATT_QKV_DIM = ATT_N_GROUPS * 3 * ATT_OUT_DIM
ATT_BLOCK = 128
ROPE_THETA = 500000.0
ROPE_DIM = 16

MOE_N_GROUPS = 4
MOE_EPG = 8
MOE_N_EXPERTS = 32
MOE_TOP_K = 2
MOE_HIDDEN = 512
MOE_ROW_BLOCK = 512

DEEPNORM_ALPHA = (2 * DEPTH) ** 0.25
LN_EPS = 1e-5
RMS_EPS = 1e-5
NEG_INF = -1e30

LANES = 128
HALF = LANES // 2
VMEM_LIMIT = 56 * 1024 * 1024

ROW_TILE = 512


def _params(*sem):
    return pltpu.CompilerParams(dimension_semantics=sem, vmem_limit_bytes=VMEM_LIMIT)


def _silu(v):
    h = 0.5 * v
    return h + h * jnp.tanh(h)


def _layer_norm(r, g, b):
    mu = jnp.mean(r, -1, keepdims=True)
    d = r - mu
    var = jnp.mean(d * d, -1, keepdims=True)
    return d * lax.rsqrt(var + LN_EPS) * g + b


def _split3(v):
    hi = v.astype(BF16)
    r1 = v - hi.astype(F32)
    mid = r1.astype(BF16)
    lo = (r1 - mid.astype(F32)).astype(BF16)
    return hi, mid, lo


def _dot(a, b):
    return jnp.dot(a, b, preferred_element_type=F32)


def _dot_nt(a, b):
    return lax.dot_general(a, b, (((1,), (1,)), ((), ())), preferred_element_type=F32)


def _dot_tn(a, b):
    return lax.dot_general(a, b, (((0,), (0,)), ((), ())), preferred_element_type=F32)


def _pair_expand(mat, h0, lo_mask):
    rows = mat.shape[0]
    a = jnp.broadcast_to(mat[:, h0:h0 + 1], (rows, LANES))
    b = jnp.broadcast_to(mat[:, h0 + 1:h0 + 2], (rows, LANES))
    return jnp.where(lo_mask, a, b)


def _round_bf16_bits(bits):
    odd = (bits >> 16) & jnp.uint32(1)
    return (bits + jnp.uint32(0x7FFF) + odd) & jnp.uint32(0xFFFF0000)


def _pack_words(bits):
    r = _round_bf16_bits(bits)
    return r[:, :D_PACK] | (r[:, D_PACK:] >> 16)


def _pack_rows(v):
    return _pack_words(pltpu.bitcast(v, U32))


def _unpack_rows(p):
    hi = pltpu.bitcast(p & jnp.uint32(0xFFFF0000), F32)
    lo = pltpu.bitcast(p << 16, F32)
    return jnp.concatenate([hi, lo], axis=1)


def _pack_rows_host(v):
    return _pack_words(lax.bitcast_convert_type(v, U32))


def _row_gather(data, idx):
    n = idx.shape[0]
    d = data.shape[1]
    window = LANES
    dc = d // 2
    mesh = plsc.VectorSubcoreMesh(core_axis_name="core", subcore_axis_name="subcore")

    @functools.partial(pl.kernel, out_type=jax.ShapeDtypeStruct((n, d), data.dtype), mesh=mesh)
    def gather(x_hbm, i_hbm, o_hbm):
        for c in range(d // dc):
            def body(i_vmem, o_vmem, c=c):
                pltpu.sync_copy(x_hbm.at[i_vmem.at[0], pl.ds(c * dc, dc)], o_vmem)

            pltpu.emit_pipeline(
                body,
                grid=(n // window,),
                in_specs=[pl.BlockSpec((1, window), lambda i: (0, i))],
                out_specs=[pl.BlockSpec((window, dc), lambda i, c=c: (i, c))],
                core_axis_name=("core", "subcore"),
                dimension_semantics=(pltpu.PARALLEL,),
            )(i_hbm, o_hbm)

    return gather(data, idx.reshape(1, n))


def _row_scatter(data, idxs, n_rows):
    t, d = data.shape
    window = LANES
    dc = d // 2
    mesh = plsc.VectorSubcoreMesh(core_axis_name="core", subcore_axis_name="subcore")

    @functools.partial(pl.kernel, out_type=jax.ShapeDtypeStruct((n_rows, d), data.dtype), mesh=mesh)
    def scatter(x_hbm, *refs):
        o_hbm = refs[-1]
        for i_hbm in refs[:-1]:
            for c in range(d // dc):
                def body(x_vmem, i_vmem, c=c):
                    pltpu.sync_copy(x_vmem, o_hbm.at[i_vmem.at[0], pl.ds(c * dc, dc)])

                pltpu.emit_pipeline(
                    body,
                    grid=(t // window,),
                    in_specs=[pl.BlockSpec((window, dc), lambda i, c=c: (i, c)),
                              pl.BlockSpec((1, window), lambda i: (0, i))],
                    out_specs=[],
                    core_axis_name=("core", "subcore"),
                    dimension_semantics=(pltpu.PARALLEL,),
                )(x_hbm, i_hbm)

    return scatter(data, *[i.reshape(1, t) for i in idxs])


def _resident(shape):
    return pl.BlockSpec(shape, lambda *_: (0,) * len(shape), pipeline_mode=pl.Buffered(1))


def _in_proj_kernel(x_ref, w_ref, wdt_ref, cw_ref, cb_ref, o_ref, dt_ref, ext, tail):
    rows = x_ref.shape[0]
    tn = IN_PROJ_COLS
    n_z = SSD_D_INNER // tn

    @pl.when(pl.program_id(1) == 0)
    def _():
        tail[...] = jnp.zeros(tail.shape, F32)

    x = _unpack_rows(x_ref[...]).astype(BF16)
    for n in range(n_z):
        sl = slice(n * tn, (n + 1) * tn)
        o_ref[:, sl] = _dot(x, w_ref[:, sl]).astype(o_ref.dtype)
    for j in range(SSD_CONV_DIM // tn):
        sl = slice((n_z + j) * tn, (n_z + j + 1) * tn)
        cl = slice(j * tn, (j + 1) * tn)
        ext[j, 0:8, :] = tail[j]
        ext[j, 8:8 + rows, :] = _dot(x, w_ref[:, sl])
        tail[j] = ext[j, rows:rows + 8, :]
        acc = ext[j, 8:8 + rows, :] * cw_ref[3:4, cl] + cb_ref[:, cl]
        for k in range(SSD_D_CONV - 1):
            acc = acc + ext[j, 5 + k:5 + k + rows, :] * cw_ref[k:k + 1, cl]
        o_ref[:, sl] = _silu(acc).astype(o_ref.dtype)
    dt_ref[...] = _dot(x, wdt_ref[...])


def _in_proj(xpk, w_zxbc, w_dt, conv_w, conv_b, bsz, seq):
    t = xpk.shape[0]
    n = w_zxbc.shape[1]
    tiles = seq // ROW_TILE
    cw = jnp.pad(conv_w, ((0, 8 - SSD_D_CONV), (0, 0)))
    n_conv = SSD_CONV_DIM // IN_PROJ_COLS
    return pl.pallas_call(
        _in_proj_kernel,
        grid=(bsz, tiles),
        in_specs=[pl.BlockSpec((ROW_TILE, D_PACK), lambda b, s: (b * tiles + s, 0)),
                  _resident((D_MODEL, n)), _resident((D_MODEL, LANES)),
                  _resident((8, SSD_CONV_DIM)), _resident((1, SSD_CONV_DIM))],
        out_specs=[pl.BlockSpec((ROW_TILE, n), lambda b, s: (b * tiles + s, 0)),
                   pl.BlockSpec((ROW_TILE, LANES), lambda b, s: (b * tiles + s, 0))],
        out_shape=[jax.ShapeDtypeStruct((t, n), BF16), jax.ShapeDtypeStruct((t, LANES), F32)],
        scratch_shapes=[pltpu.VMEM((n_conv, ROW_TILE + 8, IN_PROJ_COLS), F32),
                        pltpu.VMEM((n_conv, 8, IN_PROJ_COLS), F32)],
        compiler_params=_params("arbitrary", "arbitrary"),
        name="ssd_in_proj",
    )(xpk, w_zxbc, w_dt, cw, conv_b[None, :])


def _ssd_kernel(z_ref, xs_ref, bc_ref, dt_ref, dtb_ref, alog_ref, dsk_ref, nw_ref, ex_ref, o_ref,
                state, cs_cols, cs_rows, dt_x, cb_all, y_off, y_grp, xd_all, cd_all):
    q = SSD_CHUNK
    pairs = SSD_GROUP_COLS // LANES

    @pl.when(pl.program_id(1) == 0)
    def _():
        state[...] = jnp.zeros(state.shape, F32)

    pre = dt_ref[...] + dtb_ref[...]
    dt = jnp.maximum(pre, 0.0) + jnp.log(1.0 + jnp.exp(-jnp.abs(pre)))
    a = -jnp.exp(alog_ref[...])
    row = lax.broadcasted_iota(jnp.int32, (q, q), 0)
    col = lax.broadcasted_iota(jnp.int32, (q, q), 1)
    causal = row >= col
    lo_mask = col < HALF
    tri = jnp.where(causal, 1.0, 0.0).astype(BF16)
    cs3 = _dot(tri, jnp.concatenate(_split3(dt * a), axis=1))
    cs = cs3[:, :LANES] + cs3[:, LANES:2 * LANES] + cs3[:, 2 * LANES:]
    cs_rows[...] = cs.T
    for h in range(SSD_N_HEADS):
        cs_cols[h] = jnp.broadcast_to(cs[:, h:h + 1], (q, q))
    dt3 = _dot(jnp.concatenate(_split3(dt), axis=0), ex_ref[...])
    dt_x[...] = dt3[:q] + dt3[q:2 * q] + dt3[2 * q:]
    for g in range(SSD_N_GROUPS):
        bg = bc_ref[:, g * SSD_D_STATE:(g + 1) * SSD_D_STATE]
        cg = bc_ref[:, SSD_GN + g * SSD_D_STATE:SSD_GN + (g + 1) * SSD_D_STATE]
        cb_all[g] = _dot_nt(cg, bg)
        y_off[:, g * SSD_GROUP_COLS:(g + 1) * SSD_GROUP_COLS] = _dot(cg, state[g].astype(BF16))

    def group_body(g, carry):
        cb = cb_all[g]

        def pair_body(pp, ssq):
            p = g * pairs + pp
            h0 = 2 * p
            x0 = pl.multiple_of(p * LANES, LANES)
            l0 = pl.multiple_of(pp * LANES, LANES)
            cols = (cs_cols[h0], cs_cols[h0 + 1])
            csx = jnp.where(lo_mask, cols[0], cols[1])
            xp = xs_ref[:, pl.ds(x0, LANES)].astype(F32)
            xdt = xp * dt_x[:, pl.ds(x0, LANES)]
            xdt16 = xdt.astype(BF16)
            last = csx[q - 1:q, :]
            halves = []
            for hh in range(2):
                diff = cols[hh] - cs_rows[pl.ds(h0 + hh, 1), :]
                decay = jnp.exp(jnp.where(causal, diff, -jnp.inf))
                halves.append(_dot((cb * decay).astype(BF16), xdt16))
            y = jnp.where(lo_mask, halves[0], halves[1])
            y = y + y_off[:, pl.ds(x0, LANES)] * jnp.exp(csx) + xp * dsk_ref[:, pl.ds(x0, LANES)]
            y = y * _silu(z_ref[:, pl.ds(x0, LANES)].astype(F32))
            y_grp[:, pl.ds(l0, LANES)] = y
            xd_all[:, pl.ds(x0, LANES)] = (xdt * jnp.exp(last - csx)).astype(BF16)
            cd_all[:, pl.ds(x0, LANES)] = jnp.exp(last)
            return ssq + jnp.sum(y * y, -1, keepdims=True)

        ssq = lax.fori_loop(0, pairs, pair_body, jnp.zeros((q, 1), F32), unroll=SSD_PAIR_UNROLL)
        inv = lax.rsqrt(ssq * (1.0 / SSD_GROUP_COLS) + RMS_EPS)
        g0 = pl.multiple_of(g * SSD_GROUP_COLS, SSD_GROUP_COLS)
        o_ref[:, pl.ds(g0, SSD_GROUP_COLS)] = (
            y_grp[...] * inv * nw_ref[:, pl.ds(g0, SSD_GROUP_COLS)]).astype(o_ref.dtype)
        return carry

    lax.fori_loop(0, SSD_N_GROUPS, group_body, 0)

    for g in range(SSD_N_GROUPS):
        gs = slice(g * SSD_GROUP_COLS, (g + 1) * SSD_GROUP_COLS)
        bg = bc_ref[:, g * SSD_D_STATE:(g + 1) * SSD_D_STATE]
        state[g] = state[g] * cd_all[:, gs] + _dot_tn(bg, xd_all[:, gs])


def _ssd_scan(zxbc, dt_raw, dt_bias, a_log, d_skip, norm_w, bsz, seq):
    t = bsz * seq
    nc = seq // SSD_CHUNK
    q = SSD_CHUNK
    pad_h = LANES - SSD_N_HEADS
    dtb = jnp.pad(dt_bias, (0, pad_h))[None, :]
    alog = jnp.pad(a_log, (0, pad_h))[None, :]
    dsk = jnp.repeat(d_skip, SSD_HEAD_DIM)[None, :]
    nw = norm_w[None, :]
    expand = (jnp.arange(LANES)[:, None] == jnp.arange(SSD_D_INNER)[None, :] // SSD_HEAD_DIM).astype(BF16)

    def const(shape):
        return pl.BlockSpec(shape, lambda b, c: (0, 0))

    return pl.pallas_call(
        _ssd_kernel,
        grid=(bsz, nc),
        in_specs=[pl.BlockSpec((q, SSD_D_INNER), lambda b, c: (b * nc + c, 0)),
                  pl.BlockSpec((q, SSD_D_INNER), lambda b, c: (b * nc + c, 1)),
                  pl.BlockSpec((q, 2 * SSD_GN), lambda b, c: (b * nc + c, 4)),
                  pl.BlockSpec((q, LANES), lambda b, c: (b * nc + c, 0)),
                  const((1, LANES)), const((1, LANES)),
                  const((1, SSD_D_INNER)), const((1, SSD_D_INNER)), const((LANES, SSD_D_INNER))],
        out_specs=pl.BlockSpec((q, SSD_D_INNER), lambda b, c: (b * nc + c, 0)),
        out_shape=jax.ShapeDtypeStruct((t, SSD_D_INNER), BF16),
        scratch_shapes=[pltpu.VMEM((SSD_N_GROUPS, SSD_D_STATE, SSD_GROUP_COLS), F32),
                        pltpu.VMEM((SSD_N_HEADS, q, q), F32),
                        pltpu.VMEM((LANES, q), F32),
                        pltpu.VMEM((q, SSD_D_INNER), F32),
                        pltpu.VMEM((SSD_N_GROUPS, q, q), F32),
                        pltpu.VMEM((q, SSD_D_INNER), F32),
                        pltpu.VMEM((q, SSD_GROUP_COLS), F32),
                        pltpu.VMEM((q, SSD_D_INNER), BF16),
                        pltpu.VMEM((1, SSD_D_INNER), F32)],
        compiler_params=_params("arbitrary", "arbitrary"),
        name="ssd_scan",
    )(zxbc, zxbc, zxbc, dt_raw, dtb, alog, dsk, nw, expand)


def _proj_ln_kernel(y_ref, w_ref, x_ref, g_ref, b_ref, o_ref, opk_ref):
    mix = _dot(y_ref[...].astype(BF16), w_ref[...])
    out = _layer_norm(DEEPNORM_ALPHA * x_ref[...] + mix, g_ref[...], b_ref[...])
    o_ref[...] = out
    opk_ref[...] = _pack_rows(out)


def _proj_ln(y, w, x, g, b, name):
    t, k = y.shape
    return pl.pallas_call(
        _proj_ln_kernel,
        grid=(t // ROW_TILE,),
        in_specs=[pl.BlockSpec((ROW_TILE, k), lambda i: (i, 0)),
                  _resident((k, D_MODEL)),
                  pl.BlockSpec((ROW_TILE, D_MODEL), lambda i: (i, 0)),
                  pl.BlockSpec((1, D_MODEL), lambda i: (0, 0)),
                  pl.BlockSpec((1, D_MODEL), lambda i: (0, 0))],
        out_specs=[pl.BlockSpec((ROW_TILE, D_MODEL), lambda i: (i, 0)),
                   pl.BlockSpec((ROW_TILE, D_PACK), lambda i: (i, 0))],
        out_shape=[jax.ShapeDtypeStruct((t, D_MODEL), F32),
                   jax.ShapeDtypeStruct((t, D_PACK), U32)],
        compiler_params=_params("parallel"),
        name=name,
    )(y, w, x, g[None, :], b[None, :])


def _rope_table_kernel(pos_ref, freq_ref, c_ref, s1_ref, s2_ref, tok):
    ang = pos_ref[...].astype(F32) * freq_ref[...]
    d = lax.broadcasted_iota(jnp.int32, ang.shape, 1) % ATT_HEAD_DIM
    cos, sin = jnp.cos(ang), jnp.sin(ang)
    half = ROPE_DIM // 2
    tabs = (jnp.where(d < ROPE_DIM, cos, 1.0),
            jnp.where(d < half, -sin, 0.0),
            jnp.where((d >= half) & (d < ROPE_DIM), sin, 0.0))
    rows = ang.shape[0]
    for ti, (tab, out) in enumerate(zip(tabs, (c_ref, s1_ref, s2_ref))):
        tok[ti] = tab
        for grp, (_, dil) in enumerate(ATT_PATTERNS):
            if dil == 1:
                out[grp] = tab
            else:
                n = rows // dil
                for r in range(dil):
                    out[grp, r * n:(r + 1) * n, :] = tok[ti, pl.ds(r, n, stride=dil), :]


def _residue_order(v, dil):
    tail = v.shape[1:]
    return v.reshape(-1, ROW_TILE // dil, dil, *tail).swapaxes(1, 2).reshape(-1, *tail)


def _perm_matrix(dil):
    src = _residue_order(jnp.arange(ROW_TILE, dtype=jnp.int32), dil)
    return (src[:, None] == jnp.arange(ROW_TILE, dtype=jnp.int32)[None, :]).astype(BF16)


def _rope_tables(positions):
    t = positions.size
    half = ROPE_DIM // 2
    inv_freq = ROPE_THETA ** (-jnp.arange(0, ROPE_DIM, 2, dtype=F32) / ROPE_DIM)
    d = jnp.arange(LANES) % ATT_HEAD_DIM
    freq = jnp.where(d < ROPE_DIM, inv_freq[d % half], 0.0).astype(F32)[None, :]
    tab = jax.ShapeDtypeStruct((ATT_N_GROUPS, t, LANES), F32)
    ospec = pl.BlockSpec((ATT_N_GROUPS, ROW_TILE, LANES), lambda i: (0, i, 0))
    return pl.pallas_call(
        _rope_table_kernel,
        grid=(t // ROW_TILE,),
        in_specs=[pl.BlockSpec((ROW_TILE, 1), lambda i: (i, 0)),
                  pl.BlockSpec((1, LANES), lambda i: (0, 0))],
        out_specs=[ospec, ospec, ospec],
        out_shape=[tab, tab, tab],
        scratch_shapes=[pltpu.VMEM((3, ROW_TILE, LANES), F32)],
        compiler_params=_params("parallel"),
        name="rope_tables",
    )(positions.reshape(t, 1), freq)


def _qkv_kernel(x_ref, p4_ref, p16_ref, w_ref, c_ref, s1_ref, s2_ref, o_ref):
    x16 = _unpack_rows(x_ref[...]).astype(BF16)
    xs = [x16, _dot(p4_ref[...], x16).astype(BF16), _dot(p16_ref[...], x16).astype(BF16)]
    reps = ATT_OUT_DIM // LANES
    half = ROPE_DIM // 2
    for grp in range(ATT_N_GROUPS):
        c = jnp.concatenate([c_ref[grp]] * reps, axis=1)
        s1 = jnp.concatenate([s1_ref[grp]] * reps, axis=1)
        s2 = jnp.concatenate([s2_ref[grp]] * reps, axis=1)
        for kind in range(3):
            j = grp * 3 + kind
            sl = slice(j * ATT_OUT_DIM, (j + 1) * ATT_OUT_DIM)
            acc = _dot(xs[grp], w_ref[:, sl])
            if kind < 2:
                up = pltpu.roll(acc, ATT_OUT_DIM - half, 1)
                down = pltpu.roll(acc, half, 1)
                acc = acc * c + up * s1 + down * s2
            if kind == 0:
                acc = acc * (ATT_HEAD_DIM ** -0.5)
            o_ref[:, sl] = acc.astype(o_ref.dtype)


def _qkv_proj(xpk, w, tabs, perms):
    t = xpk.shape[0]
    tab_spec = pl.BlockSpec((ATT_N_GROUPS, ROW_TILE, LANES), lambda i: (0, i, 0))
    return pl.pallas_call(
        _qkv_kernel,
        grid=(t // ROW_TILE,),
        in_specs=[pl.BlockSpec((ROW_TILE, D_PACK), lambda i: (i, 0)),
                  _resident((ROW_TILE, ROW_TILE)), _resident((ROW_TILE, ROW_TILE)),
                  _resident((D_MODEL, ATT_QKV_DIM)), tab_spec, tab_spec, tab_spec],
        out_specs=pl.BlockSpec((ROW_TILE, ATT_QKV_DIM), lambda i: (i, 0)),
        out_shape=jax.ShapeDtypeStruct((t, ATT_QKV_DIM), BF16),
        compiler_params=_params("parallel"),
        name="qkv_rope",
    )(xpk, perms[1], perms[2], w, *tabs)


def _attn_kernel(q_ref, kp_ref, kc_ref, vp_ref, vc_ref, o_ref, st_ref, *, chained):
    w = ATT_BLOCK
    i = pl.program_id(2)
    qi = lax.broadcasted_iota(jnp.int32, (w, 2 * w), 0)
    kk = lax.broadcasted_iota(jnp.int32, (w, 2 * w), 1)
    band = (kk >= qi) & (kk <= qi + w)
    lane = lax.broadcasted_iota(jnp.int32, (w, LANES), 1)
    lo_mask = lane < HALF

    def part(ref, j):
        if len(ref.shape) == 2:
            return ref[j * w:(j + 1) * w, :]
        return ref[:, j].reshape(w, ref.shape[-1])

    for j in range(2):
        if chained and j == 1:
            k, v, has_prev = kc_ref[...], vc_ref[...], True
        else:
            kprev = kp_ref[...] if chained else part(kp_ref, j)
            vprev = vp_ref[...] if chained else part(vp_ref, j)
            k = jnp.concatenate([kprev, part(kc_ref, j)], axis=0)
            v = jnp.concatenate([vprev, part(vc_ref, j)], axis=0)
            has_prev = False
        valid = band if has_prev else band & (kk >= jnp.where(i > 0, 0, w))
        q = part(q_ref, j)
        stats = jnp.zeros((w, LANES), F32)
        zero = jnp.zeros((), q.dtype)
        parts = []
        for p in range(ATT_HEADS // 2):
            sl = slice(p * LANES, (p + 1) * LANES)
            qp, kp, vp = q[:, sl], k[:, sl], v[:, sl]
            outs = []
            for hh in range(2):
                h = 2 * p + hh
                qm = jnp.where(lo_mask if hh == 0 else ~lo_mask, qp, zero)
                s = jnp.where(valid, _dot_nt(qm, kp), NEG_INF)
                m = jnp.max(s, -1, keepdims=True)
                pr = jnp.exp(s - m)
                l = jnp.sum(pr, -1, keepdims=True)
                outs.append(_dot(pr.astype(v.dtype), vp) / l)
                stats = jnp.where(lane == h, m, stats)
                stats = jnp.where(lane == ATT_HEADS + h, l, stats)
            parts.append(jnp.where(lo_mask, outs[0], outs[1]).astype(o_ref.dtype))
        out = jnp.concatenate(parts, axis=1)
        if len(o_ref.shape) == 2:
            o_ref[j * w:(j + 1) * w, :] = out
            st_ref[j * w:(j + 1) * w, :] = stats
        else:
            o_ref[:, j] = out.reshape(o_ref.shape[0], o_ref.shape[2], o_ref.shape[3])
            st_ref[:, j] = stats.reshape(st_ref.shape[0], st_ref.shape[2], st_ref.shape[3])


def _window_attention(qkv, grp, bsz, seq):
    _, dil = ATT_PATTERNS[grp]
    w = ATT_BLOCK
    t = bsz * seq
    chunk = ROW_TILE // dil
    tiles = w // chunk if chunk < w else 1
    span = dil * w
    nb = seq // span
    col0 = grp * 3
    chained = dil == 1

    if chained:
        grid = (bsz, 1, nb // 2)
        per_b = seq // (2 * w)

        def spec(width, col, prev):
            if prev:
                return pl.BlockSpec((w, width), lambda b, r, i: (b * 2 * per_b + jnp.maximum(2 * i - 1, 0), col))
            return pl.BlockSpec((2 * w, width), lambda b, r, i: (b * per_b + i, col))

        qkv_v, o_shape, st_shape = qkv, (t, ATT_OUT_DIM), (t, LANES)
    elif tiles == 1:
        grid = (bsz, dil // 2, nb)
        per_b = seq // (2 * w)
        stride = span // (2 * w)

        def spec(width, col, prev):
            def imap(b, r, i):
                blk = jnp.maximum(i - 1, 0) if prev else i
                return (b * per_b + blk * stride + r, col)
            return pl.BlockSpec((2 * w, width), imap)

        qkv_v, o_shape, st_shape = qkv, (t, ATT_OUT_DIM), (t, LANES)
    else:
        grid = (bsz, dil // 2, nb)

        def spec(width, col, prev):
            def imap(b, r, i):
                blk = jnp.maximum(i - 1, 0) if prev else i
                return (b, blk, 0, r, 0, col)
            return pl.BlockSpec((None, None, tiles, 2, chunk, width), imap)

        lead = (bsz, nb, tiles, dil, chunk)
        qkv_v, o_shape, st_shape = qkv.reshape(*lead, ATT_QKV_DIM), (*lead, ATT_OUT_DIM), (*lead, LANES)

    o, st = pl.pallas_call(
        functools.partial(_attn_kernel, chained=chained),
        grid=grid,
        in_specs=[spec(ATT_OUT_DIM, col0, False), spec(ATT_OUT_DIM, col0 + 1, True),
                  spec(ATT_OUT_DIM, col0 + 1, False), spec(ATT_OUT_DIM, col0 + 2, True),
                  spec(ATT_OUT_DIM, col0 + 2, False)],
        out_specs=[spec(ATT_OUT_DIM, 0, False), spec(LANES, 0, False)],
        out_shape=[jax.ShapeDtypeStruct(o_shape, BF16), jax.ShapeDtypeStruct(st_shape, F32)],
        compiler_params=_params("parallel", "parallel", "arbitrary"),
        name=f"window_attn_d{dil}",
    )(qkv_v, qkv_v, qkv_v, qkv_v, qkv_v)
    return o.reshape(t, ATT_OUT_DIM), st.reshape(t, LANES)


def _merge_proj_ln_kernel(o1_ref, o2_ref, o3_ref, s1_ref, s2_ref, s3_ref, p4t_ref, p16t_ref, w_ref, x_ref,
                          g_ref, b_ref, o_ref, opk_ref):
    rows = o1_ref.shape[0]
    lane = lax.broadcasted_iota(jnp.int32, (rows, LANES), 1)
    lo_mask = lane < HALF

    def to_token_order(pt, val):
        if val.dtype == BF16:
            return _dot(pt, val)
        return sum(_dot(pt, term) for term in _split3(val))

    p4t, p16t = p4t_ref[...], p16t_ref[...]
    outs = [o1_ref[...].astype(F32), to_token_order(p4t, o2_ref[...]), to_token_order(p16t, o3_ref[...])]
    sts = [s1_ref[...], to_token_order(p4t, s2_ref[...]), to_token_order(p16t, s3_ref[...])]
    mx = jnp.maximum(jnp.maximum(sts[0], sts[1]), sts[2])
    wgts = [pltpu.roll(s, LANES - ATT_HEADS, 1) * jnp.exp(s - mx) for s in sts]
    den = wgts[0] + wgts[1] + wgts[2]
    den = jnp.where(lane < ATT_HEADS, den, 1.0)
    coefs = [wg / den for wg in wgts]
    parts = []
    for p in range(ATT_HEADS // 2):
        sl = slice(p * LANES, (p + 1) * LANES)
        acc = jnp.zeros((rows, LANES), F32)
        for gi in range(ATT_N_GROUPS):
            acc = acc + _pair_expand(coefs[gi], 2 * p, lo_mask) * outs[gi][:, sl]
        parts.append(acc.astype(BF16))
    mix = _dot(jnp.concatenate(parts, axis=1), w_ref[...])
    out = _layer_norm(DEEPNORM_ALPHA * x_ref[...] + mix, g_ref[...], b_ref[...])
    o_ref[...] = out
    opk_ref[...] = _pack_rows(out)


def _merge_proj_ln(os_, sts, perms_t, w, x, g, b):
    t = x.shape[0]
    tm = ROW_TILE
    ospec = pl.BlockSpec((tm, ATT_OUT_DIM), lambda i: (i, 0))
    sspec = pl.BlockSpec((tm, LANES), lambda i: (i, 0))
    xspec = pl.BlockSpec((tm, D_MODEL), lambda i: (i, 0))
    vspec = pl.BlockSpec((1, D_MODEL), lambda i: (0, 0))
    pspec = _resident((tm, tm))
    return pl.pallas_call(
        _merge_proj_ln_kernel,
        grid=(t // tm,),
        in_specs=[ospec] * 3 + [sspec] * 3 + [pspec, pspec, _resident((ATT_OUT_DIM, D_MODEL)),
                                              xspec, vspec, vspec],
        out_specs=[xspec, pl.BlockSpec((tm, D_PACK), lambda i: (i, 0))],
        out_shape=[jax.ShapeDtypeStruct((t, D_MODEL), F32), jax.ShapeDtypeStruct((t, D_PACK), U32)],
        compiler_params=_params("parallel"),
        name="attn_merge_proj_ln",
    )(*os_, *sts, perms_t[1], perms_t[2], w, x, g[None, :], b[None, :])


def _router_kernel(x_ref, w2_ref, b_ref, o_ref, cnt_ref, carry):
    @pl.when(pl.program_id(0) == 0)
    def _():
        carry[...] = jnp.zeros(carry.shape, F32)

    x = x_ref[...]
    xhi = x.astype(BF16)
    xlo = (x - xhi.astype(F32)).astype(BF16)
    w2 = w2_ref[...]
    hi2 = _dot(xhi, w2)
    logits = hi2[:, :LANES] + hi2[:, LANES:] + _dot(xlo, w2[:, :LANES]) + b_ref[...]
    rows = logits.shape[0]
    lane = lax.broadcasted_iota(jnp.int32, (rows, LANES), 1)
    big = jnp.int32(LANES)

    def top1(vals, mask):
        v = jnp.where(mask, vals, -jnp.inf)
        m = jnp.max(v, -1, keepdims=True)
        idx = jnp.min(jnp.where(v == m, lane, big), -1, keepdims=True)
        return v, m, idx

    gmask = lane < MOE_N_GROUPS
    gv, gm, gidx = top1(logits, gmask)
    g_w = 1.0 / jnp.sum(jnp.exp(gv - gm), -1, keepdims=True)
    e_lo = MOE_N_GROUPS + gidx * MOE_EPG
    emask = (lane >= e_lo) & (lane < e_lo + MOE_EPG)
    ev, m1, i1 = top1(logits, emask)
    zsum = jnp.sum(jnp.exp(ev - m1), -1, keepdims=True)
    _, m2, i2 = top1(logits, emask & (lane != i1))
    p1 = 1.0 / zsum
    p2 = jnp.exp(m2 - m1) / zsum
    tot = p1 + p2
    e1 = i1 - MOE_N_GROUPS
    e2 = i2 - MOE_N_GROUPS

    oh1 = jnp.where(lane == e1, 1.0, 0.0)
    oh2 = jnp.where(lane == e2, 1.0, 0.0)
    oh = oh1 + oh2
    ri = lax.broadcasted_iota(jnp.int32, (rows, rows), 0)
    ci = lax.broadcasted_iota(jnp.int32, (rows, rows), 1)
    strict = jnp.where(ri > ci, 1.0, 0.0).astype(BF16)
    before = _dot(strict, oh.astype(BF16)) + carry[...]
    rank1 = jnp.sum(oh1 * before, -1, keepdims=True)
    rank2 = jnp.sum(oh2 * before, -1, keepdims=True)
    carry[...] = carry[...] + jnp.sum(oh, 0, keepdims=True)
    cnt_ref[...] = carry[...]

    vals = [e1.astype(F32), e2.astype(F32), g_w * (p1 / tot), g_w * (p2 / tot), rank1, rank2]
    out = jnp.zeros((rows, LANES), F32)
    for j, val in enumerate(vals):
        out = jnp.where(lane == j, val, out)
    o_ref[...] = out


def _router(x, w_rg, b_rg, w_re, b_re):
    t = x.shape[0]
    n_log = MOE_N_GROUPS + MOE_N_EXPERTS
    w = jnp.pad(jnp.concatenate([w_rg, w_re], axis=1), ((0, 0), (0, LANES - n_log)))
    whi = w.astype(BF16)
    w2 = jnp.concatenate([whi, (w - whi.astype(F32)).astype(BF16)], axis=1)
    bias = jnp.pad(jnp.concatenate([b_rg, b_re]), (0, LANES - n_log))[None, :]
    return pl.pallas_call(
        _router_kernel,
        grid=(t // ROW_TILE,),
        in_specs=[pl.BlockSpec((ROW_TILE, D_MODEL), lambda i: (i, 0)), _resident((D_MODEL, 2 * LANES)),
                  pl.BlockSpec((1, LANES), lambda i: (0, 0))],
        out_specs=[pl.BlockSpec((ROW_TILE, LANES), lambda i: (i, 0)),
                   pl.BlockSpec((1, LANES), lambda i: (0, 0))],
        out_shape=[jax.ShapeDtypeStruct((t, LANES), F32), jax.ShapeDtypeStruct((1, LANES), F32)],
        scratch_shapes=[pltpu.VMEM((1, LANES), F32)],
        compiler_params=_params("arbitrary"),
        name="moe_router",
    )(x, w2, bias)


def _expert_kernel(meta_ref, x_ref, wg_ref, wu_ref, wd_ref, o_ref, wg16, wu16, wd16):
    i = pl.program_id(0)
    n_blocks = pl.num_programs(0)

    @pl.when(i < meta_ref[n_blocks])
    def _():
        @pl.when((i == 0) | (meta_ref[i] != meta_ref[jnp.maximum(i - 1, 0)]))
        def _():
            wg16[...] = wg_ref[...].astype(BF16)
            wu16[...] = wu_ref[...].astype(BF16)
            wd16[...] = wd_ref[...].astype(BF16)

        x = _unpack_rows(x_ref[...]).astype(BF16)
        h = _silu(_dot(x, wg16[...])) * _dot(x, wu16[...])
        o_ref[...] = _pack_rows(_dot(h.astype(BF16), wd16[...]))


def _expert_mlp(meta, x_rows, w_gate, w_up, w_down, layer):
    n_rows = x_rows.shape[0]
    n_blocks = n_rows // MOE_ROW_BLOCK
    tb = MOE_ROW_BLOCK
    grid_spec = pltpu.PrefetchScalarGridSpec(
        num_scalar_prefetch=1,
        grid=(n_blocks,),
        in_specs=[pl.BlockSpec((tb, D_PACK), lambda i, meta: (i, 0)),
                  pl.BlockSpec((None, None, D_MODEL, MOE_HIDDEN), lambda i, meta: (layer, meta[i], 0, 0)),
                  pl.BlockSpec((None, None, D_MODEL, MOE_HIDDEN), lambda i, meta: (layer, meta[i], 0, 0)),
                  pl.BlockSpec((None, None, MOE_HIDDEN, D_MODEL), lambda i, meta: (layer, meta[i], 0, 0))],
        out_specs=pl.BlockSpec((tb, D_PACK), lambda i, meta: (i, 0)),
        scratch_shapes=[pltpu.VMEM((D_MODEL, MOE_HIDDEN), BF16), pltpu.VMEM((D_MODEL, MOE_HIDDEN), BF16),
                        pltpu.VMEM((MOE_HIDDEN, D_MODEL), BF16)],
    )
    return pl.pallas_call(
        _expert_kernel,
        grid_spec=grid_spec,
        out_shape=jax.ShapeDtypeStruct((n_rows, D_PACK), U32),
        compiler_params=_params("arbitrary"),
        name="moe_experts",
    )(meta, x_rows, w_gate, w_up, w_down)


def _combine_ln_kernel(y0_ref, y1_ref, r_ref, x_ref, g_ref, b_ref, o_ref, opk_ref):
    route = r_ref[...]
    g0 = route[:, 2:3]
    g1 = route[:, 3:4]
    ffn = g0 * _unpack_rows(y0_ref[...]) + g1 * _unpack_rows(y1_ref[...])
    out = _layer_norm(DEEPNORM_ALPHA * x_ref[...] + ffn, g_ref[...], b_ref[...])
    o_ref[...] = out
    opk_ref[...] = _pack_rows(out)


def _combine_ln(y0, y1, route, x, g, b):
    t = x.shape[0]
    xspec = pl.BlockSpec((ROW_TILE, D_MODEL), lambda i: (i, 0))
    pspec = pl.BlockSpec((ROW_TILE, D_PACK), lambda i: (i, 0))
    vspec = pl.BlockSpec((1, D_MODEL), lambda i: (0, 0))
    return pl.pallas_call(
        _combine_ln_kernel,
        grid=(t // ROW_TILE,),
        in_specs=[pspec, pspec, pl.BlockSpec((ROW_TILE, LANES), lambda i: (i, 0)), xspec, vspec, vspec],
        out_specs=[xspec, pspec],
        out_shape=[jax.ShapeDtypeStruct((t, D_MODEL), F32), jax.ShapeDtypeStruct((t, D_PACK), U32)],
        compiler_params=_params("parallel"),
        name="moe_combine_ln",
    )(y0, y1, route, x, g[None, :], b[None, :])


def _positions_kernel(r_ref, ps_ref, o_ref):
    route = r_ref[...]
    lane = lax.broadcasted_iota(jnp.int32, route.shape, 1)
    starts = ps_ref[...]
    out = jnp.zeros(route.shape, F32)
    for k in range(MOE_TOP_K):
        eid = route[:, k:k + 1].astype(jnp.int32)
        pos = jnp.sum(jnp.where(lane == eid, starts, 0.0), -1, keepdims=True) + route[:, 4 + k:5 + k]
        out = jnp.where(lane == k, pos, out)
    o_ref[...] = out.astype(jnp.int32)


def _positions(route, pad_start):
    t = route.shape[0]
    starts = jnp.pad(pad_start.astype(F32), (0, LANES - MOE_N_EXPERTS))[None, :]
    return pl.pallas_call(
        _positions_kernel,
        grid=(t // ROW_TILE,),
        in_specs=[pl.BlockSpec((ROW_TILE, LANES), lambda i: (i, 0)), pl.BlockSpec((1, LANES), lambda i: (0, 0))],
        out_specs=pl.BlockSpec((ROW_TILE, LANES), lambda i: (i, 0)),
        out_shape=jax.ShapeDtypeStruct((t, LANES), jnp.int32),
        compiler_params=_params("parallel"),
        name="moe_positions",
    )(route, starts)


def _moe(x, xpk, w_rg, b_rg, w_re, b_re, w_gate, w_up, w_down, layer, g, b):
    t = x.shape[0]
    tb = MOE_ROW_BLOCK
    route, cnt = _router(x, w_rg, b_rg, w_re, b_re)
    n_assign = t * MOE_TOP_K
    n_blocks = n_assign // tb + MOE_N_EXPERTS
    n_rows = n_blocks * tb
    counts = cnt[0, :MOE_N_EXPERTS].astype(jnp.int32)
    padded = (counts + tb - 1) // tb * tb
    pad_end = jnp.cumsum(padded)
    pos = _positions(route, pad_end - padded)
    pos0, pos1 = pos[:, 0], pos[:, 1]
    block_start = jnp.arange(n_blocks, dtype=jnp.int32) * tb
    block_e = jnp.minimum(jnp.sum((pad_end[None, :] <= block_start[:, None]).astype(jnp.int32), -1),
                          MOE_N_EXPERTS - 1)
    meta = jnp.concatenate([block_e, pad_end[-1:] // tb]).astype(jnp.int32)
    x_rows = _row_scatter(xpk, (pos0, pos1), n_rows)
    y_rows = _expert_mlp(meta, x_rows, w_gate, w_up, w_down, layer)
    y0 = _row_gather(y_rows, pos0)
    y1 = _row_gather(y_rows, pos1)
    return _combine_ln(y0, y1, route, x, g, b)


def _ssd_layer(x, xpk, w_in, conv_w, conv_b, dt_bias, a_log, d_skip, norm_w, w_out, g, b, bsz, seq):
    w_zxbc = w_in[:, :SSD_D_INNER + SSD_CONV_DIM].astype(BF16)
    w_dt = jnp.pad(w_in[:, SSD_D_INNER + SSD_CONV_DIM:], ((0, 0), (0, LANES - SSD_N_HEADS))).astype(BF16)
    zxbc, dt_raw = _in_proj(xpk, w_zxbc, w_dt, conv_w, conv_b, bsz, seq)
    y = _ssd_scan(zxbc, dt_raw, dt_bias, a_log, d_skip, norm_w, bsz, seq)
    return _proj_ln(y, w_out.astype(BF16), x, g, b, "ssd_out_proj_ln")


def _attn_layer(x, xpk, tabs, w_qkv, w_o, g, b, bsz, seq):
    perms = [_perm_matrix(dil) for _, dil in ATT_PATTERNS]
    perms_t = [p.T for p in perms]
    qkv = _qkv_proj(xpk, w_qkv.astype(BF16), tabs, perms)
    os_, sts = [], []
    for grp in range(ATT_N_GROUPS):
        o, st = _window_attention(qkv, grp, bsz, seq)
        os_.append(o)
        sts.append(st)
    return _merge_proj_ln(os_, sts, perms_t, w_o.astype(BF16), x, g, b)


def kernel(x, positions, ssd_w_in, ssd_conv_w, ssd_conv_b, ssd_dt_bias, ssd_a_log, ssd_d, ssd_norm_w, ssd_w_out,
           attn_w_qkv, attn_w_o, ln_g, ln_b, moe_w_router_group, moe_b_router_group, moe_w_router_expert,
           moe_b_router_expert, moe_w_gate, moe_w_up, moe_w_down):
    bsz, seq, d = x.shape
    t = bsz * seq
    h = x.reshape(t, d)
    hpk = _pack_rows_host(h)
    tabs = _rope_tables(positions)
    for i in range(DEPTH):
        j = i // N_MIXERS
        if i % N_MIXERS == 0:
            h, hpk = _ssd_layer(h, hpk, ssd_w_in[j], ssd_conv_w[j], ssd_conv_b[j], ssd_dt_bias[j], ssd_a_log[j],
                                ssd_d[j], ssd_norm_w[j], ssd_w_out[j], ln_g[i, 0], ln_b[i, 0], bsz, seq)
        else:
            h, hpk = _attn_layer(h, hpk, tabs, attn_w_qkv[j], attn_w_o[j], ln_g[i, 0], ln_b[i, 0], bsz, seq)
        h, hpk = _moe(h, hpk, moe_w_router_group[i], moe_b_router_group[i], moe_w_router_expert[i],
                      moe_b_router_expert[i], moe_w_gate, moe_w_up, moe_w_down, i, ln_g[i, 1], ln_b[i, 1])
    return h.reshape(bsz, seq, d)
```

```python
import functools

import jax
import jax.numpy as jnp
from jax import lax
from jax.experimental import pallas as pl
from jax.experimental.pallas import tpu as pltpu
from jax.experimental.pallas import tpu_sc as plsc

F32 = jnp.float32
BF16 = jnp.bfloat16
U32 = jnp.uint32

D_MODEL = 1024
D_PACK = D_MODEL // 2
DEPTH = 4
N_MIXERS = 2

SSD_D_INNER = 2048
SSD_HEAD_DIM = 64
SSD_N_HEADS = 32
SSD_N_GROUPS = 4
SSD_D_STATE = 128
SSD_D_CONV = 4
SSD_CHUNK = 128
SSD_GN = SSD_N_GROUPS * SSD_D_STATE
SSD_CONV_DIM = SSD_D_INNER + 2 * SSD_GN
SSD_GROUP_COLS = SSD_D_INNER // SSD_N_GROUPS
IN_PROJ_COLS = 1024
SSD_PAIR_UNROLL = 4

ATT_HEAD_DIM = 64
ATT_HEADS = 8
ATT_PATTERNS = ((128, 1), (512, 4), (2048, 16))
ATT_N_GROUPS = 3
ATT_OUT_DIM = ATT_HEADS * ATT_HEAD_DIM
ATT_QKV_DIM = ATT_N_GROUPS * 3 * ATT_OUT_DIM
ATT_BLOCK = 128
ATT_PROBLEMS = 4
ROPE_THETA = 500000.0
ROPE_DIM = 16

MOE_N_GROUPS = 4
MOE_EPG = 8
MOE_N_EXPERTS = 32
MOE_TOP_K = 2
MOE_HIDDEN = 512
MOE_ROW_BLOCK = 512

DEEPNORM_ALPHA = (2 * DEPTH) ** 0.25
LN_EPS = 1e-5
RMS_EPS = 1e-5
NEG_INF = -1e30

LANES = 128
HALF = LANES // 2
VMEM_LIMIT = 56 * 1024 * 1024

ROW_TILE = 512


def _params(*sem):
    return pltpu.CompilerParams(dimension_semantics=sem, vmem_limit_bytes=VMEM_LIMIT)


def _silu(v):
    h = 0.5 * v
    return h + h * jnp.tanh(h)


def _layer_norm(r, g, b):
    mu = jnp.mean(r, -1, keepdims=True)
    d = r - mu
    var = jnp.mean(d * d, -1, keepdims=True)
    return d * lax.rsqrt(var + LN_EPS) * g + b


def _split3(v):
    hi = v.astype(BF16)
    r1 = v - hi.astype(F32)
    mid = r1.astype(BF16)
    lo = (r1 - mid.astype(F32)).astype(BF16)
    return hi, mid, lo


def _dot(a, b):
    return jnp.dot(a, b, preferred_element_type=F32)


def _dot_nt(a, b):
    return lax.dot_general(a, b, (((1,), (1,)), ((), ())), preferred_element_type=F32)


def _dot_tn(a, b):
    return lax.dot_general(a, b, (((0,), (0,)), ((), ())), preferred_element_type=F32)


def _pair_expand(mat, h0, lo_mask):
    rows = mat.shape[0]
    a = jnp.broadcast_to(mat[:, h0:h0 + 1], (rows, LANES))
    b = jnp.broadcast_to(mat[:, h0 + 1:h0 + 2], (rows, LANES))
    return jnp.where(lo_mask, a, b)


def _round_bf16_bits(bits):
    odd = (bits >> 16) & jnp.uint32(1)
    return (bits + jnp.uint32(0x7FFF) + odd) & jnp.uint32(0xFFFF0000)


def _pack_words(bits):
    r = _round_bf16_bits(bits)
    return r[:, :D_PACK] | (r[:, D_PACK:] >> 16)


def _pack_rows(v):
    return _pack_words(pltpu.bitcast(v, U32))


def _unpack_rows(p):
    hi = pltpu.bitcast(p & jnp.uint32(0xFFFF0000), F32)
    lo = pltpu.bitcast(p << 16, F32)
    return jnp.concatenate([hi, lo], axis=1)


def _pack_rows_host(v):
    return _pack_words(lax.bitcast_convert_type(v, U32))


def _row_gather(data, idx):
    n = idx.shape[0]
    d = data.shape[1]
    window = LANES
    dc = d // 2
    mesh = plsc.VectorSubcoreMesh(core_axis_name="core", subcore_axis_name="subcore")

    @functools.partial(pl.kernel, out_type=jax.ShapeDtypeStruct((n, d), data.dtype), mesh=mesh)
    def gather(x_hbm, i_hbm, o_hbm):
        for c in range(d // dc):
            def body(i_vmem, o_vmem, c=c):
                pltpu.sync_copy(x_hbm.at[i_vmem.at[0], pl.ds(c * dc, dc)], o_vmem)

            pltpu.emit_pipeline(
                body,
                grid=(n // window,),
                in_specs=[pl.BlockSpec((1, window), lambda i: (0, i))],
                out_specs=[pl.BlockSpec((window, dc), lambda i, c=c: (i, c))],
                core_axis_name=("core", "subcore"),
                dimension_semantics=(pltpu.PARALLEL,),
            )(i_hbm, o_hbm)

    return gather(data, idx.reshape(1, n))


def _row_scatter(data, idxs, n_rows):
    t, d = data.shape
    window = LANES
    dc = d // 2
    mesh = plsc.VectorSubcoreMesh(core_axis_name="core", subcore_axis_name="subcore")

    @functools.partial(pl.kernel, out_type=jax.ShapeDtypeStruct((n_rows, d), data.dtype), mesh=mesh)
    def scatter(x_hbm, *refs):
        o_hbm = refs[-1]
        for i_hbm in refs[:-1]:
            for c in range(d // dc):
                def body(x_vmem, i_vmem, c=c):
                    pltpu.sync_copy(x_vmem, o_hbm.at[i_vmem.at[0], pl.ds(c * dc, dc)])

                pltpu.emit_pipeline(
                    body,
                    grid=(t // window,),
                    in_specs=[pl.BlockSpec((window, dc), lambda i, c=c: (i, c)),
                              pl.BlockSpec((1, window), lambda i: (0, i))],
                    out_specs=[],
                    core_axis_name=("core", "subcore"),
                    dimension_semantics=(pltpu.PARALLEL,),
                )(x_hbm, i_hbm)

    return scatter(data, *[i.reshape(1, t) for i in idxs])


def _resident(shape):
    return pl.BlockSpec(shape, lambda *_: (0,) * len(shape), pipeline_mode=pl.Buffered(1))


def _in_proj_kernel(x_ref, w_ref, wdt_ref, cw_ref, cb_ref, o_ref, dt_ref, ext, tail):
    rows = x_ref.shape[0]
    tn = IN_PROJ_COLS
    n_z = SSD_D_INNER // tn

    @pl.when(pl.program_id(1) == 0)
    def _():
        tail[...] = jnp.zeros(tail.shape, F32)

    x = _unpack_rows(x_ref[...]).astype(BF16)
    for n in range(n_z):
        sl = slice(n * tn, (n + 1) * tn)
        o_ref[:, sl] = _dot(x, w_ref[:, sl]).astype(o_ref.dtype)
    for j in range(SSD_CONV_DIM // tn):
        sl = slice((n_z + j) * tn, (n_z + j + 1) * tn)
        cl = slice(j * tn, (j + 1) * tn)
        ext[j, 0:8, :] = tail[j]
        ext[j, 8:8 + rows, :] = _dot(x, w_ref[:, sl])
        tail[j] = ext[j, rows:rows + 8, :]
        acc = ext[j, 8:8 + rows, :] * cw_ref[3:4, cl] + cb_ref[:, cl]
        for k in range(SSD_D_CONV - 1):
            acc = acc + ext[j, 5 + k:5 + k + rows, :] * cw_ref[k:k + 1, cl]
        o_ref[:, sl] = _silu(acc).astype(o_ref.dtype)
    dt_ref[...] = _dot(x, wdt_ref[...])


def _in_proj(xpk, w_zxbc, w_dt, conv_w, conv_b, bsz, seq):
    t = xpk.shape[0]
    n = w_zxbc.shape[1]
    tiles = seq // ROW_TILE
    cw = jnp.pad(conv_w, ((0, 8 - SSD_D_CONV), (0, 0)))
    n_conv = SSD_CONV_DIM // IN_PROJ_COLS
    return pl.pallas_call(
        _in_proj_kernel,
        grid=(bsz, tiles),
        in_specs=[pl.BlockSpec((ROW_TILE, D_PACK), lambda b, s: (b * tiles + s, 0)),
                  _resident((D_MODEL, n)), _resident((D_MODEL, LANES)),
                  _resident((8, SSD_CONV_DIM)), _resident((1, SSD_CONV_DIM))],
        out_specs=[pl.BlockSpec((ROW_TILE, n), lambda b, s: (b * tiles + s, 0)),
                   pl.BlockSpec((ROW_TILE, LANES), lambda b, s: (b * tiles + s, 0))],
        out_shape=[jax.ShapeDtypeStruct((t, n), BF16), jax.ShapeDtypeStruct((t, LANES), F32)],
        scratch_shapes=[pltpu.VMEM((n_conv, ROW_TILE + 8, IN_PROJ_COLS), F32),
                        pltpu.VMEM((n_conv, 8, IN_PROJ_COLS), F32)],
        compiler_params=_params("arbitrary", "arbitrary"),
        name="ssd_in_proj",
    )(xpk, w_zxbc, w_dt, cw, conv_b[None, :])


def _ssd_kernel(z_ref, xs_ref, bc_ref, dt_ref, dtb_ref, alog_ref, dsk_ref, nw_ref, ex_ref, o_ref,
                state, cs_cols, cs_rows, dt_x, cb_all, y_off, y_grp, xd_all, cd_all):
    q = SSD_CHUNK
    pairs = SSD_GROUP_COLS // LANES

    @pl.when(pl.program_id(1) == 0)
    def _():
        state[...] = jnp.zeros(state.shape, F32)

    pre = dt_ref[...] + dtb_ref[...]
    dt = jnp.maximum(pre, 0.0) + jnp.log(1.0 + jnp.exp(-jnp.abs(pre)))
    a = -jnp.exp(alog_ref[...])
    row = lax.broadcasted_iota(jnp.int32, (q, q), 0)
    col = lax.broadcasted_iota(jnp.int32, (q, q), 1)
    causal = row >= col
    lo_mask = col < HALF
    tri = jnp.where(causal, 1.0, 0.0).astype(BF16)
    cs3 = _dot(tri, jnp.concatenate(_split3(dt * a), axis=1))
    cs = cs3[:, :LANES] + cs3[:, LANES:2 * LANES] + cs3[:, 2 * LANES:]
    cs_rows[...] = cs.T
    for h in range(SSD_N_HEADS):
        cs_cols[h] = jnp.broadcast_to(cs[:, h:h + 1], (q, q))
    dt3 = _dot(jnp.concatenate(_split3(dt), axis=0), ex_ref[...])
    dt_x[...] = dt3[:q] + dt3[q:2 * q] + dt3[2 * q:]
    for g in range(SSD_N_GROUPS):
        bg = bc_ref[:, g * SSD_D_STATE:(g + 1) * SSD_D_STATE]
        cg = bc_ref[:, SSD_GN + g * SSD_D_STATE:SSD_GN + (g + 1) * SSD_D_STATE]
        cb_all[g] = _dot_nt(cg, bg)
        y_off[:, g * SSD_GROUP_COLS:(g + 1) * SSD_GROUP_COLS] = _dot(cg, state[g].astype(BF16))

    def group_body(g, carry):
        cb = cb_all[g]

        def pair_body(pp, ssq):
            p = g * pairs + pp
            h0 = 2 * p
            x0 = pl.multiple_of(p * LANES, LANES)
            l0 = pl.multiple_of(pp * LANES, LANES)
            cols = (cs_cols[h0], cs_cols[h0 + 1])
            csx = jnp.where(lo_mask, cols[0], cols[1])
            xp = xs_ref[:, pl.ds(x0, LANES)].astype(F32)
            xdt = xp * dt_x[:, pl.ds(x0, LANES)]
            xdt16 = xdt.astype(BF16)
            last = csx[q - 1:q, :]
            halves = []
            for hh in range(2):
                diff = cols[hh] - cs_rows[pl.ds(h0 + hh, 1), :]
                decay = jnp.exp(jnp.where(causal, diff, -jnp.inf))
                halves.append(_dot((cb * decay).astype(BF16), xdt16))
            y = jnp.where(lo_mask, halves[0], halves[1])
            y = y + y_off[:, pl.ds(x0, LANES)] * jnp.exp(csx) + xp * dsk_ref[:, pl.ds(x0, LANES)]
            y = y * _silu(z_ref[:, pl.ds(x0, LANES)].astype(F32))
            y_grp[:, pl.ds(l0, LANES)] = y
            xd_all[:, pl.ds(x0, LANES)] = (xdt * jnp.exp(last - csx)).astype(BF16)
            cd_all[:, pl.ds(x0, LANES)] = jnp.exp(last)
            return ssq + jnp.sum(y * y, -1, keepdims=True)

        ssq = lax.fori_loop(0, pairs, pair_body, jnp.zeros((q, 1), F32), unroll=SSD_PAIR_UNROLL)
        inv = lax.rsqrt(ssq * (1.0 / SSD_GROUP_COLS) + RMS_EPS)
        g0 = pl.multiple_of(g * SSD_GROUP_COLS, SSD_GROUP_COLS)
        o_ref[:, pl.ds(g0, SSD_GROUP_COLS)] = (
            y_grp[...] * inv * nw_ref[:, pl.ds(g0, SSD_GROUP_COLS)]).astype(o_ref.dtype)
        return carry

    lax.fori_loop(0, SSD_N_GROUPS, group_body, 0)

    for g in range(SSD_N_GROUPS):
        gs = slice(g * SSD_GROUP_COLS, (g + 1) * SSD_GROUP_COLS)
        bg = bc_ref[:, g * SSD_D_STATE:(g + 1) * SSD_D_STATE]
        state[g] = state[g] * cd_all[:, gs] + _dot_tn(bg, xd_all[:, gs])


def _ssd_scan(zxbc, dt_raw, dt_bias, a_log, d_skip, norm_w, bsz, seq):
    t = bsz * seq
    nc = seq // SSD_CHUNK
    q = SSD_CHUNK
    pad_h = LANES - SSD_N_HEADS
    dtb = jnp.pad(dt_bias, (0, pad_h))[None, :]
    alog = jnp.pad(a_log, (0, pad_h))[None, :]
    dsk = jnp.repeat(d_skip, SSD_HEAD_DIM)[None, :]
    nw = norm_w[None, :]
    expand = (jnp.arange(LANES)[:, None] == jnp.arange(SSD_D_INNER)[None, :] // SSD_HEAD_DIM).astype(BF16)

    def const(shape):
        return pl.BlockSpec(shape, lambda b, c: (0, 0))

    return pl.pallas_call(
        _ssd_kernel,
        grid=(bsz, nc),
        in_specs=[pl.BlockSpec((q, SSD_D_INNER), lambda b, c: (b * nc + c, 0)),
                  pl.BlockSpec((q, SSD_D_INNER), lambda b, c: (b * nc + c, 1)),
                  pl.BlockSpec((q, 2 * SSD_GN), lambda b, c: (b * nc + c, 4)),
                  pl.BlockSpec((q, LANES), lambda b, c: (b * nc + c, 0)),
                  const((1, LANES)), const((1, LANES)),
                  const((1, SSD_D_INNER)), const((1, SSD_D_INNER)), const((LANES, SSD_D_INNER))],
        out_specs=pl.BlockSpec((q, SSD_D_INNER), lambda b, c: (b * nc + c, 0)),
        out_shape=jax.ShapeDtypeStruct((t, SSD_D_INNER), BF16),
        scratch_shapes=[pltpu.VMEM((SSD_N_GROUPS, SSD_D_STATE, SSD_GROUP_COLS), F32),
                        pltpu.VMEM((SSD_N_HEADS, q, q), F32),
                        pltpu.VMEM((LANES, q), F32),
                        pltpu.VMEM((q, SSD_D_INNER), F32),
                        pltpu.VMEM((SSD_N_GROUPS, q, q), F32),
                        pltpu.VMEM((q, SSD_D_INNER), F32),
                        pltpu.VMEM((q, SSD_GROUP_COLS), F32),
                        pltpu.VMEM((q, SSD_D_INNER), BF16),
                        pltpu.VMEM((1, SSD_D_INNER), F32)],
        compiler_params=_params("arbitrary", "arbitrary"),
        name="ssd_scan",
    )(zxbc, zxbc, zxbc, dt_raw, dtb, alog, dsk, nw, expand)


def _proj_ln_kernel(y_ref, w_ref, x_ref, g_ref, b_ref, w2_ref, rb_ref, o_ref, opk_ref, route_ref, cnt_ref, carry):
    mix = _dot(y_ref[...].astype(BF16), w_ref[...])
    _ln_route_epilogue(DEEPNORM_ALPHA * x_ref[...] + mix, g_ref, b_ref, w2_ref, rb_ref,
                       o_ref, opk_ref, route_ref, cnt_ref, carry)


def _route_specs(t):
    ins = [_resident((D_MODEL, 2 * LANES)), pl.BlockSpec((1, LANES), lambda i: (0, 0))]
    outs = [pl.BlockSpec((ROW_TILE, LANES), lambda i: (i, 0)), pl.BlockSpec((1, LANES), lambda i: (0, 0))]
    shapes = [jax.ShapeDtypeStruct((t, LANES), F32), jax.ShapeDtypeStruct((1, LANES), F32)]
    return ins, outs, shapes, [pltpu.VMEM((1, LANES), F32)]


def _proj_ln(y, w, x, g, b, router_w, name):
    t, k = y.shape
    r_in, r_out, r_shape, r_scratch = _route_specs(t)
    return pl.pallas_call(
        _proj_ln_kernel,
        grid=(t // ROW_TILE,),
        in_specs=[pl.BlockSpec((ROW_TILE, k), lambda i: (i, 0)),
                  _resident((k, D_MODEL)),
                  pl.BlockSpec((ROW_TILE, D_MODEL), lambda i: (i, 0)),
                  pl.BlockSpec((1, D_MODEL), lambda i: (0, 0)),
                  pl.BlockSpec((1, D_MODEL), lambda i: (0, 0))] + r_in,
        out_specs=[pl.BlockSpec((ROW_TILE, D_MODEL), lambda i: (i, 0)),
                   pl.BlockSpec((ROW_TILE, D_PACK), lambda i: (i, 0))] + r_out,
        out_shape=[jax.ShapeDtypeStruct((t, D_MODEL), F32),
                   jax.ShapeDtypeStruct((t, D_PACK), U32)] + r_shape,
        scratch_shapes=r_scratch,
        compiler_params=_params("arbitrary"),
        name=name,
    )(y, w, x, g[None, :], b[None, :], *router_w)


def _rope_table_kernel(pos_ref, freq_ref, c_ref, s1_ref, s2_ref, tok):
    ang = pos_ref[...].astype(F32) * freq_ref[...]
    d = lax.broadcasted_iota(jnp.int32, ang.shape, 1) % ATT_HEAD_DIM
    cos, sin = jnp.cos(ang), jnp.sin(ang)
    half = ROPE_DIM // 2
    tabs = (jnp.where(d < ROPE_DIM, cos, 1.0),
            jnp.where(d < half, -sin, 0.0),
            jnp.where((d >= half) & (d < ROPE_DIM), sin, 0.0))
    rows = ang.shape[0]
    for ti, (tab, out) in enumerate(zip(tabs, (c_ref, s1_ref, s2_ref))):
        tok[ti] = tab
        for grp, (_, dil) in enumerate(ATT_PATTERNS):
            if dil == 1:
                out[grp] = tab
            else:
                n = rows // dil
                for r in range(dil):
                    out[grp, r * n:(r + 1) * n, :] = tok[ti, pl.ds(r, n, stride=dil), :]


def _residue_order(v, dil):
    tail = v.shape[1:]
    return v.reshape(-1, ROW_TILE // dil, dil, *tail).swapaxes(1, 2).reshape(-1, *tail)


def _perm_matrix(dil):
    src = _residue_order(jnp.arange(ROW_TILE, dtype=jnp.int32), dil)
    return (src[:, None] == jnp.arange(ROW_TILE, dtype=jnp.int32)[None, :]).astype(BF16)


def _rope_tables(positions):
    t = positions.size
    half = ROPE_DIM // 2
    inv_freq = ROPE_THETA ** (-jnp.arange(0, ROPE_DIM, 2, dtype=F32) / ROPE_DIM)
    d = jnp.arange(LANES) % ATT_HEAD_DIM
    freq = jnp.where(d < ROPE_DIM, inv_freq[d % half], 0.0).astype(F32)[None, :]
    tab = jax.ShapeDtypeStruct((ATT_N_GROUPS, t, LANES), F32)
    ospec = pl.BlockSpec((ATT_N_GROUPS, ROW_TILE, LANES), lambda i: (0, i, 0))
    return pl.pallas_call(
        _rope_table_kernel,
        grid=(t // ROW_TILE,),
        in_specs=[pl.BlockSpec((ROW_TILE, 1), lambda i: (i, 0)),
                  pl.BlockSpec((1, LANES), lambda i: (0, 0))],
        out_specs=[ospec, ospec, ospec],
        out_shape=[tab, tab, tab],
        scratch_shapes=[pltpu.VMEM((3, ROW_TILE, LANES), F32)],
        compiler_params=_params("parallel"),
        name="rope_tables",
    )(positions.reshape(t, 1), freq)


def _qkv_kernel(x_ref, p4_ref, p16_ref, w_ref, c_ref, s1_ref, s2_ref, o_ref):
    x16 = _unpack_rows(x_ref[...]).astype(BF16)
    xs = [x16, _dot(p4_ref[...], x16).astype(BF16), _dot(p16_ref[...], x16).astype(BF16)]
    reps = ATT_OUT_DIM // LANES
    half = ROPE_DIM // 2
    for grp in range(ATT_N_GROUPS):
        c = jnp.concatenate([c_ref[grp]] * reps, axis=1)
        s1 = jnp.concatenate([s1_ref[grp]] * reps, axis=1)
        s2 = jnp.concatenate([s2_ref[grp]] * reps, axis=1)
        for kind in range(3):
            j = grp * 3 + kind
            sl = slice(j * ATT_OUT_DIM, (j + 1) * ATT_OUT_DIM)
            acc = _dot(xs[grp], w_ref[:, sl])
            if kind < 2:
                up = pltpu.roll(acc, ATT_OUT_DIM - half, 1)
                down = pltpu.roll(acc, half, 1)
                acc = acc * c + up * s1 + down * s2
            if kind == 0:
                acc = acc * (ATT_HEAD_DIM ** -0.5)
            o_ref[:, sl] = acc.astype(o_ref.dtype)


def _qkv_proj(xpk, w, tabs, perms):
    t = xpk.shape[0]
    tab_spec = pl.BlockSpec((ATT_N_GROUPS, ROW_TILE, LANES), lambda i: (0, i, 0))
    return pl.pallas_call(
        _qkv_kernel,
        grid=(t // ROW_TILE,),
        in_specs=[pl.BlockSpec((ROW_TILE, D_PACK), lambda i: (i, 0)),
                  _resident((ROW_TILE, ROW_TILE)), _resident((ROW_TILE, ROW_TILE)),
                  _resident((D_MODEL, ATT_QKV_DIM)), tab_spec, tab_spec, tab_spec],
        out_specs=pl.BlockSpec((ROW_TILE, ATT_QKV_DIM), lambda i: (i, 0)),
        out_shape=jax.ShapeDtypeStruct((t, ATT_QKV_DIM), BF16),
        compiler_params=_params("parallel"),
        name="qkv_rope",
    )(xpk, perms[1], perms[2], w, *tabs)


def _attn_kernel(q_ref, kp_ref, kc_ref, vp_ref, vc_ref, o_ref, st_ref, *, chained):
    w = ATT_BLOCK
    i = pl.program_id(2)
    qi = lax.broadcasted_iota(jnp.int32, (w, 2 * w), 0)
    kk = lax.broadcasted_iota(jnp.int32, (w, 2 * w), 1)
    band = (kk >= qi) & (kk <= qi + w)
    lane = lax.broadcasted_iota(jnp.int32, (w, LANES), 1)
    lo_mask = lane < HALF

    def part(ref, j):
        if len(ref.shape) == 2:
            return ref[j * w:(j + 1) * w, :]
        return ref[:, j].reshape(w, ref.shape[-1])

    for j in range(ATT_PROBLEMS):
        if chained and j >= 1:
            k, v, has_prev = kc_ref[(j - 1) * w:(j + 1) * w, :], vc_ref[(j - 1) * w:(j + 1) * w, :], True
        else:
            kprev = kp_ref[...] if chained else part(kp_ref, j)
            vprev = vp_ref[...] if chained else part(vp_ref, j)
            k = jnp.concatenate([kprev, part(kc_ref, j)], axis=0)
            v = jnp.concatenate([vprev, part(vc_ref, j)], axis=0)
            has_prev = False
        valid = band if has_prev else band & (kk >= jnp.where(i > 0, 0, w))
        q = part(q_ref, j)
        stats = jnp.zeros((w, LANES), F32)
        zero = jnp.zeros((), q.dtype)
        parts = []
        for p in range(ATT_HEADS // 2):
            sl = slice(p * LANES, (p + 1) * LANES)
            qp, kp, vp = q[:, sl], k[:, sl], v[:, sl]
            outs = []
            for hh in range(2):
                h = 2 * p + hh
                qm = jnp.where(lo_mask if hh == 0 else ~lo_mask, qp, zero)
                s = jnp.where(valid, _dot_nt(qm, kp), NEG_INF)
                m = jnp.max(s, -1, keepdims=True)
                pr = jnp.exp(s - m)
                l = jnp.sum(pr, -1, keepdims=True)
                outs.append(_dot(pr.astype(v.dtype), vp) / l)
                stats = jnp.where(lane == h, m, stats)
                stats = jnp.where(lane == ATT_HEADS + h, l, stats)
            parts.append(jnp.where(lo_mask, outs[0], outs[1]).astype(o_ref.dtype))
        out = jnp.concatenate(parts, axis=1)
        if len(o_ref.shape) == 2:
            o_ref[j * w:(j + 1) * w, :] = out
            st_ref[j * w:(j + 1) * w, :] = stats
        else:
            o_ref[:, j] = out.reshape(o_ref.shape[0], o_ref.shape[2], o_ref.shape[3])
            st_ref[:, j] = stats.reshape(st_ref.shape[0], st_ref.shape[2], st_ref.shape[3])


def _window_attention(qkv, grp, bsz, seq):
    _, dil = ATT_PATTERNS[grp]
    w = ATT_BLOCK
    t = bsz * seq
    chunk = ROW_TILE // dil
    tiles = w // chunk if chunk < w else 1
    span = dil * w
    nb = seq // span
    col0 = grp * 3
    chained = dil == 1
    g = ATT_PROBLEMS

    if chained:
        grid = (bsz, 1, nb // g)
        per_b = seq // (g * w)

        def spec(width, col, prev):
            if prev:
                return pl.BlockSpec((w, width), lambda b, r, i: (b * g * per_b + jnp.maximum(g * i - 1, 0), col))
            return pl.BlockSpec((g * w, width), lambda b, r, i: (b * per_b + i, col))

        qkv_v, o_shape, st_shape = qkv, (t, ATT_OUT_DIM), (t, LANES)
    elif tiles == 1:
        grid = (bsz, dil // g, nb)
        per_b = seq // (g * w)
        stride = span // (g * w)

        def spec(width, col, prev):
            def imap(b, r, i):
                blk = jnp.maximum(i - 1, 0) if prev else i
                return (b * per_b + blk * stride + r, col)
            return pl.BlockSpec((g * w, width), imap)

        qkv_v, o_shape, st_shape = qkv, (t, ATT_OUT_DIM), (t, LANES)
    else:
        grid = (bsz, dil // g, nb)

        def spec(width, col, prev):
            def imap(b, r, i):
                blk = jnp.maximum(i - 1, 0) if prev else i
                return (b, blk, 0, r, 0, col)
            return pl.BlockSpec((None, None, tiles, g, chunk, width), imap)

        lead = (bsz, nb, tiles, dil, chunk)
        qkv_v, o_shape, st_shape = qkv.reshape(*lead, ATT_QKV_DIM), (*lead, ATT_OUT_DIM), (*lead, LANES)

    o, st = pl.pallas_call(
        functools.partial(_attn_kernel, chained=chained),
        grid=grid,
        in_specs=[spec(ATT_OUT_DIM, col0, False), spec(ATT_OUT_DIM, col0 + 1, True),
                  spec(ATT_OUT_DIM, col0 + 1, False), spec(ATT_OUT_DIM, col0 + 2, True),
                  spec(ATT_OUT_DIM, col0 + 2, False)],
        out_specs=[spec(ATT_OUT_DIM, 0, False), spec(LANES, 0, False)],
        out_shape=[jax.ShapeDtypeStruct(o_shape, BF16), jax.ShapeDtypeStruct(st_shape, F32)],
        compiler_params=_params("parallel", "parallel", "arbitrary"),
        name=f"window_attn_d{dil}",
    )(qkv_v, qkv_v, qkv_v, qkv_v, qkv_v)
    return o.reshape(t, ATT_OUT_DIM), st.reshape(t, LANES)


def _merge_proj_ln_kernel(o1_ref, o2_ref, o3_ref, s1_ref, s2_ref, s3_ref, p4t_ref, p16t_ref, w_ref, x_ref,
                          g_ref, b_ref, w2_ref, rb_ref, o_ref, opk_ref, route_ref, cnt_ref, carry):
    rows = o1_ref.shape[0]
    lane = lax.broadcasted_iota(jnp.int32, (rows, LANES), 1)
    lo_mask = lane < HALF

    def to_token_order(pt, val):
        if val.dtype == BF16:
            return _dot(pt, val)
        return sum(_dot(pt, term) for term in _split3(val))

    p4t, p16t = p4t_ref[...], p16t_ref[...]
    outs = [o1_ref[...].astype(F32), to_token_order(p4t, o2_ref[...]), to_token_order(p16t, o3_ref[...])]
    sts = [s1_ref[...], to_token_order(p4t, s2_ref[...]), to_token_order(p16t, s3_ref[...])]
    mx = jnp.maximum(jnp.maximum(sts[0], sts[1]), sts[2])
    wgts = [pltpu.roll(s, LANES - ATT_HEADS, 1) * jnp.exp(s - mx) for s in sts]
    den = wgts[0] + wgts[1] + wgts[2]
    den = jnp.where(lane < ATT_HEADS, den, 1.0)
    coefs = [wg / den for wg in wgts]
    parts = []
    for p in range(ATT_HEADS // 2):
        sl = slice(p * LANES, (p + 1) * LANES)
        acc = jnp.zeros((rows, LANES), F32)
        for gi in range(ATT_N_GROUPS):
            acc = acc + _pair_expand(coefs[gi], 2 * p, lo_mask) * outs[gi][:, sl]
        parts.append(acc.astype(BF16))
    mix = _dot(jnp.concatenate(parts, axis=1), w_ref[...])
    _ln_route_epilogue(DEEPNORM_ALPHA * x_ref[...] + mix, g_ref, b_ref, w2_ref, rb_ref,
                       o_ref, opk_ref, route_ref, cnt_ref, carry)


def _merge_proj_ln(os_, sts, perms_t, w, x, g, b, router_w):
    t = x.shape[0]
    tm = ROW_TILE
    r_in, r_out, r_shape, r_scratch = _route_specs(t)
    ospec = pl.BlockSpec((tm, ATT_OUT_DIM), lambda i: (i, 0))
    sspec = pl.BlockSpec((tm, LANES), lambda i: (i, 0))
    xspec = pl.BlockSpec((tm, D_MODEL), lambda i: (i, 0))
    vspec = pl.BlockSpec((1, D_MODEL), lambda i: (0, 0))
    pspec = _resident((tm, tm))
    return pl.pallas_call(
        _merge_proj_ln_kernel,
        grid=(t // tm,),
        in_specs=[ospec] * 3 + [sspec] * 3 + [pspec, pspec, _resident((ATT_OUT_DIM, D_MODEL)),
                                              xspec, vspec, vspec] + r_in,
        out_specs=[xspec, pl.BlockSpec((tm, D_PACK), lambda i: (i, 0))] + r_out,
        out_shape=[jax.ShapeDtypeStruct((t, D_MODEL), F32), jax.ShapeDtypeStruct((t, D_PACK), U32)] + r_shape,
        scratch_shapes=r_scratch,
        compiler_params=_params("arbitrary"),
        name="attn_merge_proj_ln",
    )(*os_, *sts, perms_t[1], perms_t[2], w, x, g[None, :], b[None, :], *router_w)


def _route_rows(x, w2_ref, b_ref, carry):
    xhi = x.astype(BF16)
    xlo = (x - xhi.astype(F32)).astype(BF16)
    w2 = w2_ref[...]
    hi2 = _dot(xhi, w2)
    logits = hi2[:, :LANES] + hi2[:, LANES:] + _dot(xlo, w2[:, :LANES]) + b_ref[...]
    rows = logits.shape[0]
    lane = lax.broadcasted_iota(jnp.int32, (rows, LANES), 1)
    big = jnp.int32(LANES)

    def top1(vals, mask):
        v = jnp.where(mask, vals, -jnp.inf)
        m = jnp.max(v, -1, keepdims=True)
        idx = jnp.min(jnp.where(v == m, lane, big), -1, keepdims=True)
        return v, m, idx

    gmask = lane < MOE_N_GROUPS
    gv, gm, gidx = top1(logits, gmask)
    g_w = 1.0 / jnp.sum(jnp.exp(gv - gm), -1, keepdims=True)
    e_lo = MOE_N_GROUPS + gidx * MOE_EPG
    emask = (lane >= e_lo) & (lane < e_lo + MOE_EPG)
    ev, m1, i1 = top1(logits, emask)
    zsum = jnp.sum(jnp.exp(ev - m1), -1, keepdims=True)
    _, m2, i2 = top1(logits, emask & (lane != i1))
    p1 = 1.0 / zsum
    p2 = jnp.exp(m2 - m1) / zsum
    tot = p1 + p2
    e1 = i1 - MOE_N_GROUPS
    e2 = i2 - MOE_N_GROUPS

    oh1 = jnp.where(lane == e1, 1.0, 0.0)
    oh2 = jnp.where(lane == e2, 1.0, 0.0)
    oh = oh1 + oh2
    ri = lax.broadcasted_iota(jnp.int32, (rows, rows), 0)
    ci = lax.broadcasted_iota(jnp.int32, (rows, rows), 1)
    strict = jnp.where(ri > ci, 1.0, 0.0).astype(BF16)
    before = _dot(strict, oh.astype(BF16)) + carry[...]
    rank1 = jnp.sum(oh1 * before, -1, keepdims=True)
    rank2 = jnp.sum(oh2 * before, -1, keepdims=True)
    carry[...] = carry[...] + jnp.sum(oh, 0, keepdims=True)

    vals = [e1.astype(F32), e2.astype(F32), g_w * (p1 / tot), g_w * (p2 / tot), rank1, rank2]
    out = jnp.zeros((rows, LANES), F32)
    for j, val in enumerate(vals):
        out = jnp.where(lane == j, val, out)
    return out


def _router_weights(w_rg, b_rg, w_re, b_re):
    n_log = MOE_N_GROUPS + MOE_N_EXPERTS
    w = jnp.pad(jnp.concatenate([w_rg, w_re], axis=1), ((0, 0), (0, LANES - n_log)))
    whi = w.astype(BF16)
    w2 = jnp.concatenate([whi, (w - whi.astype(F32)).astype(BF16)], axis=1)
    bias = jnp.pad(jnp.concatenate([b_rg, b_re]), (0, LANES - n_log))[None, :]
    return w2, bias


def _ln_route_epilogue(r, g_ref, b_ref, w2_ref, rb_ref, o_ref, opk_ref, route_ref, cnt_ref, carry):
    @pl.when(pl.program_id(0) == 0)
    def _():
        carry[...] = jnp.zeros(carry.shape, F32)

    out = _layer_norm(r, g_ref[...], b_ref[...])
    o_ref[...] = out
    opk_ref[...] = _pack_rows(out)
    route_ref[...] = _route_rows(out, w2_ref, rb_ref, carry)
    cnt_ref[...] = carry[...]


def _expert_kernel(meta_ref, x_ref, wg_ref, wu_ref, wd_ref, o_ref, wg16, wu16, wd16):
    i = pl.program_id(0)
    n_blocks = pl.num_programs(0)

    @pl.when(i < meta_ref[n_blocks])
    def _():
        @pl.when((i == 0) | (meta_ref[i] != meta_ref[jnp.maximum(i - 1, 0)]))
        def _():
            wg16[...] = wg_ref[...].astype(BF16)
            wu16[...] = wu_ref[...].astype(BF16)
            wd16[...] = wd_ref[...].astype(BF16)

        x = _unpack_rows(x_ref[...]).astype(BF16)
        h = _silu(_dot(x, wg16[...])) * _dot(x, wu16[...])
        o_ref[...] = _pack_rows(_dot(h.astype(BF16), wd16[...]))


def _expert_mlp(meta, x_rows, w_gate, w_up, w_down, layer):
    n_rows = x_rows.shape[0]
    n_blocks = n_rows // MOE_ROW_BLOCK
    tb = MOE_ROW_BLOCK
    grid_spec = pltpu.PrefetchScalarGridSpec(
        num_scalar_prefetch=1,
        grid=(n_blocks,),
        in_specs=[pl.BlockSpec((tb, D_PACK), lambda i, meta: (i, 0)),
                  pl.BlockSpec((None, None, D_MODEL, MOE_HIDDEN), lambda i, meta: (layer, meta[i], 0, 0)),
                  pl.BlockSpec((None, None, D_MODEL, MOE_HIDDEN), lambda i, meta: (layer, meta[i], 0, 0)),
                  pl.BlockSpec((None, None, MOE_HIDDEN, D_MODEL), lambda i, meta: (layer, meta[i], 0, 0))],
        out_specs=pl.BlockSpec((tb, D_PACK), lambda i, meta: (i, 0)),
        scratch_shapes=[pltpu.VMEM((D_MODEL, MOE_HIDDEN), BF16), pltpu.VMEM((D_MODEL, MOE_HIDDEN), BF16),
                        pltpu.VMEM((MOE_HIDDEN, D_MODEL), BF16)],
    )
    return pl.pallas_call(
        _expert_kernel,
        grid_spec=grid_spec,
        out_shape=jax.ShapeDtypeStruct((n_rows, D_PACK), U32),
        compiler_params=_params("arbitrary"),
        name="moe_experts",
    )(meta, x_rows, w_gate, w_up, w_down)


def _combine_ln_kernel(y0_ref, y1_ref, r_ref, x_ref, g_ref, b_ref, o_ref, opk_ref):
    route = r_ref[...]
    g0 = route[:, 2:3]
    g1 = route[:, 3:4]
    ffn = g0 * _unpack_rows(y0_ref[...]) + g1 * _unpack_rows(y1_ref[...])
    out = _layer_norm(DEEPNORM_ALPHA * x_ref[...] + ffn, g_ref[...], b_ref[...])
    o_ref[...] = out
    opk_ref[...] = _pack_rows(out)


def _combine_ln(y0, y1, route, x, g, b):
    t = x.shape[0]
    xspec = pl.BlockSpec((ROW_TILE, D_MODEL), lambda i: (i, 0))
    pspec = pl.BlockSpec((ROW_TILE, D_PACK), lambda i: (i, 0))
    vspec = pl.BlockSpec((1, D_MODEL), lambda i: (0, 0))
    return pl.pallas_call(
        _combine_ln_kernel,
        grid=(t // ROW_TILE,),
        in_specs=[pspec, pspec, pl.BlockSpec((ROW_TILE, LANES), lambda i: (i, 0)), xspec, vspec, vspec],
        out_specs=[xspec, pspec],
        out_shape=[jax.ShapeDtypeStruct((t, D_MODEL), F32), jax.ShapeDtypeStruct((t, D_PACK), U32)],
        compiler_params=_params("parallel"),
        name="moe_combine_ln",
    )(y0, y1, route, x, g[None, :], b[None, :])


def _positions_kernel(r_ref, ps_ref, o_ref):
    route = r_ref[...]
    lane = lax.broadcasted_iota(jnp.int32, route.shape, 1)
    starts = ps_ref[...]
    out = jnp.zeros(route.shape, F32)
    for k in range(MOE_TOP_K):
        eid = route[:, k:k + 1].astype(jnp.int32)
        pos = jnp.sum(jnp.where(lane == eid, starts, 0.0), -1, keepdims=True) + route[:, 4 + k:5 + k]
        out = jnp.where(lane == k, pos, out)
    o_ref[...] = out.astype(jnp.int32)


def _positions(route, pad_start):
    t = route.shape[0]
    starts = jnp.pad(pad_start.astype(F32), (0, LANES - MOE_N_EXPERTS))[None, :]
    return pl.pallas_call(
        _positions_kernel,
        grid=(t // ROW_TILE,),
        in_specs=[pl.BlockSpec((ROW_TILE, LANES), lambda i: (i, 0)), pl.BlockSpec((1, LANES), lambda i: (0, 0))],
        out_specs=pl.BlockSpec((ROW_TILE, LANES), lambda i: (i, 0)),
        out_shape=jax.ShapeDtypeStruct((t, LANES), jnp.int32),
        compiler_params=_params("parallel"),
        name="moe_positions",
    )(route, starts)


def _moe(x, xpk, route, cnt, w_gate, w_up, w_down, layer, g, b):
    t = x.shape[0]
    tb = MOE_ROW_BLOCK
    n_assign = t * MOE_TOP_K
    n_blocks = n_assign // tb + MOE_N_EXPERTS
    n_rows = n_blocks * tb
    counts = cnt[0, :MOE_N_EXPERTS].astype(jnp.int32)
    padded = (counts + tb - 1) // tb * tb
    pad_end = jnp.cumsum(padded)
    pos = _positions(route, pad_end - padded)
    pos0, pos1 = pos[:, 0], pos[:, 1]
    block_start = jnp.arange(n_blocks, dtype=jnp.int32) * tb
    block_e = jnp.minimum(jnp.sum((pad_end[None, :] <= block_start[:, None]).astype(jnp.int32), -1),
                          MOE_N_EXPERTS - 1)
    meta = jnp.concatenate([block_e, pad_end[-1:] // tb]).astype(jnp.int32)
    x_rows = _row_scatter(xpk, (pos0, pos1), n_rows)
    y_rows = _expert_mlp(meta, x_rows, w_gate, w_up, w_down, layer)
    y0 = _row_gather(y_rows, pos0)
    y1 = _row_gather(y_rows, pos1)
    return _combine_ln(y0, y1, route, x, g, b)


def _ssd_layer(x, xpk, w_in, conv_w, conv_b, dt_bias, a_log, d_skip, norm_w, w_out, g, b, router_w, bsz, seq):
    w_zxbc = w_in[:, :SSD_D_INNER + SSD_CONV_DIM].astype(BF16)
    w_dt = jnp.pad(w_in[:, SSD_D_INNER + SSD_CONV_DIM:], ((0, 0), (0, LANES - SSD_N_HEADS))).astype(BF16)
    zxbc, dt_raw = _in_proj(xpk, w_zxbc, w_dt, conv_w, conv_b, bsz, seq)
    y = _ssd_scan(zxbc, dt_raw, dt_bias, a_log, d_skip, norm_w, bsz, seq)
    return _proj_ln(y, w_out.astype(BF16), x, g, b, router_w, "ssd_out_proj_ln")


def _attn_layer(x, xpk, tabs, w_qkv, w_o, g, b, router_w, bsz, seq):
    perms = [_perm_matrix(dil) for _, dil in ATT_PATTERNS]
    perms_t = [p.T for p in perms]
    qkv = _qkv_proj(xpk, w_qkv.astype(BF16), tabs, perms)
    os_, sts = [], []
    for grp in range(ATT_N_GROUPS):
        o, st = _window_attention(qkv, grp, bsz, seq)
        os_.append(o)
        sts.append(st)
    return _merge_proj_ln(os_, sts, perms_t, w_o.astype(BF16), x, g, b, router_w)


def kernel(x, positions, ssd_w_in, ssd_conv_w, ssd_conv_b, ssd_dt_bias, ssd_a_log, ssd_d, ssd_norm_w, ssd_w_out,
           attn_w_qkv, attn_w_o, ln_g, ln_b, moe_w_router_group, moe_b_router_group, moe_w_router_expert,
           moe_b_router_expert, moe_w_gate, moe_w_up, moe_w_down):
    bsz, seq, d = x.shape
    t = bsz * seq
    h = x.reshape(t, d)
    hpk = _pack_rows_host(h)
    tabs = _rope_tables(positions)
    for i in range(DEPTH):
        j = i // N_MIXERS
        router_w = _router_weights(moe_w_router_group[i], moe_b_router_group[i], moe_w_router_expert[i],
                                   moe_b_router_expert[i])
        if i % N_MIXERS == 0:
            h, hpk, route, cnt = _ssd_layer(h, hpk, ssd_w_in[j], ssd_conv_w[j], ssd_conv_b[j], ssd_dt_bias[j],
                                            ssd_a_log[j], ssd_d[j], ssd_norm_w[j], ssd_w_out[j], ln_g[i, 0],
                                            ln_b[i, 0], router_w, bsz, seq)
        else:
            h, hpk, route, cnt = _attn_layer(h, hpk, tabs, attn_w_qkv[j], attn_w_o[j], ln_g[i, 0], ln_b[i, 0],
                                             router_w, bsz, seq)
        h, hpk = _moe(h, hpk, route, cnt, moe_w_gate, moe_w_up, moe_w_down, i, ln_g[i, 1], ln_b[i, 1])
    return h.reshape(bsz, seq, d)
```

```python
import functools

import jax
import jax.numpy as jnp
from jax import lax
from jax.experimental import pallas as pl
from jax.experimental.pallas import tpu as pltpu
from jax.experimental.pallas import tpu_sc as plsc

F32 = jnp.float32
BF16 = jnp.bfloat16
U32 = jnp.uint32

D_MODEL = 1024
D_PACK = D_MODEL // 2
DEPTH = 4
N_MIXERS = 2

SSD_D_INNER = 2048
SSD_HEAD_DIM = 64
SSD_N_HEADS = 32
SSD_N_GROUPS = 4
SSD_D_STATE = 128
SSD_D_CONV = 4
SSD_CHUNK = 128
SSD_GN = SSD_N_GROUPS * SSD_D_STATE
SSD_CONV_DIM = SSD_D_INNER + 2 * SSD_GN
SSD_GROUP_COLS = SSD_D_INNER // SSD_N_GROUPS
IN_PROJ_COLS = 1024
SSD_PAIR_UNROLL = 4

ATT_HEAD_DIM = 64
ATT_HEADS = 8
ATT_PATTERNS = ((128, 1), (512, 4), (2048, 16))
ATT_N_GROUPS = 3
ATT_OUT_DIM = ATT_HEADS * ATT_HEAD_DIM
ATT_QKV_DIM = ATT_N_GROUPS * 3 * ATT_OUT_DIM
ATT_BLOCK = 128
ATT_PROBLEMS = 4
ROPE_THETA = 500000.0
ROPE_DIM = 16

MOE_N_GROUPS = 4
MOE_EPG = 8
MOE_N_EXPERTS = 32
MOE_TOP_K = 2
MOE_HIDDEN = 512
MOE_ROW_BLOCK = 512

DEEPNORM_ALPHA = (2 * DEPTH) ** 0.25
LN_EPS = 1e-5
RMS_EPS = 1e-5
NEG_INF = -1e30

LANES = 128
HALF = LANES // 2
VMEM_LIMIT = 56 * 1024 * 1024

ROW_TILE = 512


def _params(*sem):
    return pltpu.CompilerParams(dimension_semantics=sem, vmem_limit_bytes=VMEM_LIMIT)


def _silu(v):
    h = 0.5 * v
    return h + h * jnp.tanh(h)


def _layer_norm(r, g, b):
    mu = jnp.mean(r, -1, keepdims=True)
    d = r - mu
    var = jnp.mean(d * d, -1, keepdims=True)
    return d * lax.rsqrt(var + LN_EPS) * g + b


def _split3(v):
    hi = v.astype(BF16)
    r1 = v - hi.astype(F32)
    mid = r1.astype(BF16)
    lo = (r1 - mid.astype(F32)).astype(BF16)
    return hi, mid, lo


def _dot(a, b):
    return jnp.dot(a, b, preferred_element_type=F32)


def _dot_nt(a, b):
    return lax.dot_general(a, b, (((1,), (1,)), ((), ())), preferred_element_type=F32)


def _dot_tn(a, b):
    return lax.dot_general(a, b, (((0,), (0,)), ((), ())), preferred_element_type=F32)


def _pair_expand(mat, h0, lo_mask):
    rows = mat.shape[0]
    a = jnp.broadcast_to(mat[:, h0:h0 + 1], (rows, LANES))
    b = jnp.broadcast_to(mat[:, h0 + 1:h0 + 2], (rows, LANES))
    return jnp.where(lo_mask, a, b)


def _pack_rows(v):
    r = pltpu.bitcast(v.astype(BF16).astype(F32), U32)
    return r[:, :D_PACK] | (r[:, D_PACK:] >> 16)


def _unpack_rows(p):
    hi = pltpu.bitcast(p & jnp.uint32(0xFFFF0000), F32)
    lo = pltpu.bitcast(p << 16, F32)
    return jnp.concatenate([hi, lo], axis=1)


def _row_gather(data, idx):
    n = idx.shape[0]
    d = data.shape[1]
    window = LANES
    dc = d // 2
    mesh = plsc.VectorSubcoreMesh(core_axis_name="core", subcore_axis_name="subcore")

    @functools.partial(pl.kernel, out_type=jax.ShapeDtypeStruct((n, d), data.dtype), mesh=mesh)
    def gather(x_hbm, i_hbm, o_hbm):
        for c in range(d // dc):
            def body(i_vmem, o_vmem, c=c):
                pltpu.sync_copy(x_hbm.at[i_vmem.at[0], pl.ds(c * dc, dc)], o_vmem)

            pltpu.emit_pipeline(
                body,
                grid=(n // window,),
                in_specs=[pl.BlockSpec((1, window), lambda i: (0, i))],
                out_specs=[pl.BlockSpec((window, dc), lambda i, c=c: (i, c))],
                core_axis_name=("core", "subcore"),
                dimension_semantics=(pltpu.PARALLEL,),
            )(i_hbm, o_hbm)

    return gather(data, idx.reshape(1, n))


def _row_scatter(data, idxs, n_rows):
    t, d = data.shape
    window = LANES
    dc = d // 2
    mesh = plsc.VectorSubcoreMesh(core_axis_name="core", subcore_axis_name="subcore")

    @functools.partial(pl.kernel, out_type=jax.ShapeDtypeStruct((n_rows, d), data.dtype), mesh=mesh)
    def scatter(x_hbm, *refs):
        o_hbm = refs[-1]
        for i_hbm in refs[:-1]:
            for c in range(d // dc):
                def body(x_vmem, i_vmem, c=c):
                    pltpu.sync_copy(x_vmem, o_hbm.at[i_vmem.at[0], pl.ds(c * dc, dc)])

                pltpu.emit_pipeline(
                    body,
                    grid=(t // window,),
                    in_specs=[pl.BlockSpec((window, dc), lambda i, c=c: (i, c)),
                              pl.BlockSpec((1, window), lambda i: (0, i))],
                    out_specs=[],
                    core_axis_name=("core", "subcore"),
                    dimension_semantics=(pltpu.PARALLEL,),
                )(x_hbm, i_hbm)

    return scatter(data, *[i.reshape(1, t) for i in idxs])


def _resident(shape):
    return pl.BlockSpec(shape, lambda *_: (0,) * len(shape), pipeline_mode=pl.Buffered(1))


def _in_proj_kernel(x_ref, w_ref, wdt_ref, cw_ref, cb_ref, o_ref, dt_ref, ext, tail):
    rows = x_ref.shape[0]
    tn = IN_PROJ_COLS
    n_z = SSD_D_INNER // tn

    @pl.when(pl.program_id(1) == 0)
    def _():
        tail[...] = jnp.zeros(tail.shape, F32)

    x = x_ref[...]
    x = (_unpack_rows(x) if x.dtype == U32 else x).astype(BF16)
    for n in range(n_z):
        sl = slice(n * tn, (n + 1) * tn)
        o_ref[:, sl] = _dot(x, w_ref[:, sl]).astype(o_ref.dtype)
    for j in range(SSD_CONV_DIM // tn):
        sl = slice((n_z + j) * tn, (n_z + j + 1) * tn)
        cl = slice(j * tn, (j + 1) * tn)
        ext[j, 0:8, :] = tail[j]
        ext[j, 8:8 + rows, :] = _dot(x, w_ref[:, sl])
        tail[j] = ext[j, rows:rows + 8, :]
        acc = ext[j, 8:8 + rows, :] * cw_ref[3:4, cl] + cb_ref[:, cl]
        for k in range(SSD_D_CONV - 1):
            acc = acc + ext[j, 5 + k:5 + k + rows, :] * cw_ref[k:k + 1, cl]
        o_ref[:, sl] = _silu(acc).astype(o_ref.dtype)
    dt_ref[...] = _dot(x, wdt_ref[...])


def _in_proj(xpk, w_zxbc, w_dt, conv_w, conv_b, bsz, seq):
    t = xpk.shape[0]
    n = w_zxbc.shape[1]
    tiles = seq // ROW_TILE
    cw = jnp.pad(conv_w, ((0, 8 - SSD_D_CONV), (0, 0)))
    n_conv = SSD_CONV_DIM // IN_PROJ_COLS
    return pl.pallas_call(
        _in_proj_kernel,
        grid=(bsz, tiles),
        in_specs=[pl.BlockSpec((ROW_TILE, xpk.shape[1]), lambda b, s: (b * tiles + s, 0)),
                  _resident((D_MODEL, n)), _resident((D_MODEL, LANES)),
                  _resident((8, SSD_CONV_DIM)), _resident((1, SSD_CONV_DIM))],
        out_specs=[pl.BlockSpec((ROW_TILE, n), lambda b, s: (b * tiles + s, 0)),
                   pl.BlockSpec((ROW_TILE, LANES), lambda b, s: (b * tiles + s, 0))],
        out_shape=[jax.ShapeDtypeStruct((t, n), BF16), jax.ShapeDtypeStruct((t, LANES), F32)],
        scratch_shapes=[pltpu.VMEM((n_conv, ROW_TILE + 8, IN_PROJ_COLS), F32),
                        pltpu.VMEM((n_conv, 8, IN_PROJ_COLS), F32)],
        compiler_params=_params("arbitrary", "arbitrary"),
        name="ssd_in_proj",
    )(xpk, w_zxbc, w_dt, cw, conv_b[None, :])


def _ssd_kernel(z_ref, xs_ref, bc_ref, dt_ref, dtb_ref, alog_ref, dsk_ref, nw_ref, ex_ref, o_ref,
                state, cs_cols, cs_rows, dt_x, cb_all, y_off, y_grp, xd_all, cd_all):
    q = SSD_CHUNK
    pairs = SSD_GROUP_COLS // LANES

    @pl.when(pl.program_id(1) == 0)
    def _():
        state[...] = jnp.zeros(state.shape, F32)

    pre = dt_ref[...] + dtb_ref[...]
    dt = jnp.maximum(pre, 0.0) + jnp.log(1.0 + jnp.exp(-jnp.abs(pre)))
    a = -jnp.exp(alog_ref[...])
    row = lax.broadcasted_iota(jnp.int32, (q, q), 0)
    col = lax.broadcasted_iota(jnp.int32, (q, q), 1)
    causal = row >= col
    lo_mask = col < HALF
    tri = jnp.where(causal, 1.0, 0.0).astype(BF16)
    cs3 = _dot(tri, jnp.concatenate(_split3(dt * a), axis=1))
    cs = cs3[:, :LANES] + cs3[:, LANES:2 * LANES] + cs3[:, 2 * LANES:]
    cs_rows[...] = cs.T
    for h in range(SSD_N_HEADS):
        cs_cols[h] = jnp.broadcast_to(cs[:, h:h + 1], (q, q))
    dt3 = _dot(jnp.concatenate(_split3(dt), axis=0), ex_ref[...])
    dt_x[...] = dt3[:q] + dt3[q:2 * q] + dt3[2 * q:]
    for g in range(SSD_N_GROUPS):
        bg = bc_ref[:, g * SSD_D_STATE:(g + 1) * SSD_D_STATE]
        cg = bc_ref[:, SSD_GN + g * SSD_D_STATE:SSD_GN + (g + 1) * SSD_D_STATE]
        cb_all[g] = _dot_nt(cg, bg)
        y_off[:, g * SSD_GROUP_COLS:(g + 1) * SSD_GROUP_COLS] = _dot(cg, state[g].astype(BF16))

    def group_body(g, carry):
        cb = cb_all[g]

        def pair_body(pp, ssq):
            p = g * pairs + pp
            h0 = 2 * p
            x0 = pl.multiple_of(p * LANES, LANES)
            l0 = pl.multiple_of(pp * LANES, LANES)
            cols = (cs_cols[h0], cs_cols[h0 + 1])
            csx = jnp.where(lo_mask, cols[0], cols[1])
            xp = xs_ref[:, pl.ds(x0, LANES)].astype(F32)
            xdt = xp * dt_x[:, pl.ds(x0, LANES)]
            xdt16 = xdt.astype(BF16)
            last = csx[q - 1:q, :]
            halves = []
            for hh in range(2):
                diff = cols[hh] - cs_rows[pl.ds(h0 + hh, 1), :]
                decay = jnp.exp(jnp.where(causal, diff, -jnp.inf))
                halves.append(_dot((cb * decay).astype(BF16), xdt16))
            y = jnp.where(lo_mask, halves[0], halves[1])
            y = y + y_off[:, pl.ds(x0, LANES)] * jnp.exp(csx) + xp * dsk_ref[:, pl.ds(x0, LANES)]
            y = y * _silu(z_ref[:, pl.ds(x0, LANES)].astype(F32))
            y_grp[:, pl.ds(l0, LANES)] = y
            xd_all[:, pl.ds(x0, LANES)] = (xdt * jnp.exp(last - csx)).astype(BF16)
            cd_all[:, pl.ds(x0, LANES)] = jnp.exp(last)
            return ssq + jnp.sum(y * y, -1, keepdims=True)

        ssq = lax.fori_loop(0, pairs, pair_body, jnp.zeros((q, 1), F32), unroll=SSD_PAIR_UNROLL)
        inv = lax.rsqrt(ssq * (1.0 / SSD_GROUP_COLS) + RMS_EPS)
        g0 = pl.multiple_of(g * SSD_GROUP_COLS, SSD_GROUP_COLS)
        o_ref[:, pl.ds(g0, SSD_GROUP_COLS)] = (
            y_grp[...] * inv * nw_ref[:, pl.ds(g0, SSD_GROUP_COLS)]).astype(o_ref.dtype)
        return carry

    lax.fori_loop(0, SSD_N_GROUPS, group_body, 0)

    for g in range(SSD_N_GROUPS):
        gs = slice(g * SSD_GROUP_COLS, (g + 1) * SSD_GROUP_COLS)
        bg = bc_ref[:, g * SSD_D_STATE:(g + 1) * SSD_D_STATE]
        state[g] = state[g] * cd_all[:, gs] + _dot_tn(bg, xd_all[:, gs])


def _ssd_scan(zxbc, dt_raw, dt_bias, a_log, d_skip, norm_w, bsz, seq):
    t = bsz * seq
    nc = seq // SSD_CHUNK
    q = SSD_CHUNK
    pad_h = LANES - SSD_N_HEADS
    dtb = jnp.pad(dt_bias, (0, pad_h))[None, :]
    alog = jnp.pad(a_log, (0, pad_h))[None, :]
    dsk = jnp.repeat(d_skip, SSD_HEAD_DIM)[None, :]
    nw = norm_w[None, :]
    expand = (jnp.arange(LANES)[:, None] == jnp.arange(SSD_D_INNER)[None, :] // SSD_HEAD_DIM).astype(BF16)

    def const(shape):
        return pl.BlockSpec(shape, lambda b, c: (0, 0))

    return pl.pallas_call(
        _ssd_kernel,
        grid=(bsz, nc),
        in_specs=[pl.BlockSpec((q, SSD_D_INNER), lambda b, c: (b * nc + c, 0)),
                  pl.BlockSpec((q, SSD_D_INNER), lambda b, c: (b * nc + c, 1)),
                  pl.BlockSpec((q, 2 * SSD_GN), lambda b, c: (b * nc + c, 4)),
                  pl.BlockSpec((q, LANES), lambda b, c: (b * nc + c, 0)),
                  const((1, LANES)), const((1, LANES)),
                  const((1, SSD_D_INNER)), const((1, SSD_D_INNER)), const((LANES, SSD_D_INNER))],
        out_specs=pl.BlockSpec((q, SSD_D_INNER), lambda b, c: (b * nc + c, 0)),
        out_shape=jax.ShapeDtypeStruct((t, SSD_D_INNER), BF16),
        scratch_shapes=[pltpu.VMEM((SSD_N_GROUPS, SSD_D_STATE, SSD_GROUP_COLS), F32),
                        pltpu.VMEM((SSD_N_HEADS, q, q), F32),
                        pltpu.VMEM((LANES, q), F32),
                        pltpu.VMEM((q, SSD_D_INNER), F32),
                        pltpu.VMEM((SSD_N_GROUPS, q, q), F32),
                        pltpu.VMEM((q, SSD_D_INNER), F32),
                        pltpu.VMEM((q, SSD_GROUP_COLS), F32),
                        pltpu.VMEM((q, SSD_D_INNER), BF16),
                        pltpu.VMEM((1, SSD_D_INNER), F32)],
        compiler_params=_params("arbitrary", "arbitrary"),
        name="ssd_scan",
    )(zxbc, zxbc, zxbc, dt_raw, dtb, alog, dsk, nw, expand)


def _proj_ln_kernel(y_ref, w_ref, x_ref, g_ref, b_ref, w2_ref, rb_ref, o_ref, opk_ref, route_ref, cnt_ref, carry):
    mix = _dot(y_ref[...].astype(BF16), w_ref[...])
    _ln_route_epilogue(DEEPNORM_ALPHA * x_ref[...] + mix, g_ref, b_ref, w2_ref, rb_ref,
                       o_ref, opk_ref, route_ref, cnt_ref, carry)


def _route_specs(t):
    ins = [_resident((D_MODEL, 2 * LANES)), pl.BlockSpec((1, LANES), lambda i: (0, 0))]
    outs = [pl.BlockSpec((ROW_TILE, LANES), lambda i: (i, 0)), pl.BlockSpec((1, LANES), lambda i: (0, 0))]
    shapes = [jax.ShapeDtypeStruct((t, LANES), F32), jax.ShapeDtypeStruct((1, LANES), F32)]
    return ins, outs, shapes, [pltpu.VMEM((1, LANES), F32)]


def _proj_ln(y, w, x, g, b, router_w, name):
    t, k = y.shape
    r_in, r_out, r_shape, r_scratch = _route_specs(t)
    return pl.pallas_call(
        _proj_ln_kernel,
        grid=(t // ROW_TILE,),
        in_specs=[pl.BlockSpec((ROW_TILE, k), lambda i: (i, 0)),
                  _resident((k, D_MODEL)),
                  pl.BlockSpec((ROW_TILE, D_MODEL), lambda i: (i, 0)),
                  pl.BlockSpec((1, D_MODEL), lambda i: (0, 0)),
                  pl.BlockSpec((1, D_MODEL), lambda i: (0, 0))] + r_in,
        out_specs=[pl.BlockSpec((ROW_TILE, D_MODEL), lambda i: (i, 0)),
                   pl.BlockSpec((ROW_TILE, D_PACK), lambda i: (i, 0))] + r_out,
        out_shape=[jax.ShapeDtypeStruct((t, D_MODEL), F32),
                   jax.ShapeDtypeStruct((t, D_PACK), U32)] + r_shape,
        scratch_shapes=r_scratch,
        compiler_params=_params("arbitrary"),
        name=name,
    )(y, w, x, g[None, :], b[None, :], *router_w)


def _rope_table_kernel(pos_ref, freq_ref, c_ref, s1_ref, s2_ref, tok):
    ang = pos_ref[...].astype(F32) * freq_ref[...]
    d = lax.broadcasted_iota(jnp.int32, ang.shape, 1) % ATT_HEAD_DIM
    cos, sin = jnp.cos(ang), jnp.sin(ang)
    half = ROPE_DIM // 2
    tabs = (jnp.where(d < ROPE_DIM, cos, 1.0),
            jnp.where(d < half, -sin, 0.0),
            jnp.where((d >= half) & (d < ROPE_DIM), sin, 0.0))
    rows = ang.shape[0]
    for ti, (tab, out) in enumerate(zip(tabs, (c_ref, s1_ref, s2_ref))):
        tok[ti] = tab
        for grp, (_, dil) in enumerate(ATT_PATTERNS):
            if dil == 1:
                out[grp] = tab
            else:
                n = rows // dil
                for r in range(dil):
                    out[grp, r * n:(r + 1) * n, :] = tok[ti, pl.ds(r, n, stride=dil), :]


def _residue_order(v, dil):
    tail = v.shape[1:]
    return v.reshape(-1, ROW_TILE // dil, dil, *tail).swapaxes(1, 2).reshape(-1, *tail)


def _perm_matrix(dil):
    src = _residue_order(jnp.arange(ROW_TILE, dtype=jnp.int32), dil)
    return (src[:, None] == jnp.arange(ROW_TILE, dtype=jnp.int32)[None, :]).astype(BF16)


def _rope_tables(positions):
    t = positions.size
    half = ROPE_DIM // 2
    inv_freq = ROPE_THETA ** (-jnp.arange(0, ROPE_DIM, 2, dtype=F32) / ROPE_DIM)
    d = jnp.arange(LANES) % ATT_HEAD_DIM
    freq = jnp.where(d < ROPE_DIM, inv_freq[d % half], 0.0).astype(F32)[None, :]
    tab = jax.ShapeDtypeStruct((ATT_N_GROUPS, t, LANES), F32)
    ospec = pl.BlockSpec((ATT_N_GROUPS, ROW_TILE, LANES), lambda i: (0, i, 0))
    return pl.pallas_call(
        _rope_table_kernel,
        grid=(t // ROW_TILE,),
        in_specs=[pl.BlockSpec((ROW_TILE, 1), lambda i: (i, 0)),
                  pl.BlockSpec((1, LANES), lambda i: (0, 0))],
        out_specs=[ospec, ospec, ospec],
        out_shape=[tab, tab, tab],
        scratch_shapes=[pltpu.VMEM((3, ROW_TILE, LANES), F32)],
        compiler_params=_params("parallel"),
        name="rope_tables",
    )(positions.reshape(t, 1), freq)


def _qkv_kernel(x_ref, p4_ref, p16_ref, w_ref, c_ref, s1_ref, s2_ref, o_ref):
    x16 = _unpack_rows(x_ref[...]).astype(BF16)
    xs = [x16, _dot(p4_ref[...], x16).astype(BF16), _dot(p16_ref[...], x16).astype(BF16)]
    reps = ATT_OUT_DIM // LANES
    half = ROPE_DIM // 2
    for grp in range(ATT_N_GROUPS):
        c = jnp.concatenate([c_ref[grp]] * reps, axis=1)
        s1 = jnp.concatenate([s1_ref[grp]] * reps, axis=1)
        s2 = jnp.concatenate([s2_ref[grp]] * reps, axis=1)
        for kind in range(3):
            j = grp * 3 + kind
            sl = slice(j * ATT_OUT_DIM, (j + 1) * ATT_OUT_DIM)
            acc = _dot(xs[grp], w_ref[:, sl])
            if kind < 2:
                up = pltpu.roll(acc, ATT_OUT_DIM - half, 1)
                down = pltpu.roll(acc, half, 1)
                acc = acc * c + up * s1 + down * s2
            if kind == 0:
                acc = acc * (ATT_HEAD_DIM ** -0.5)
            o_ref[:, sl] = acc.astype(o_ref.dtype)


def _qkv_proj(xpk, w, tabs, perms):
    t = xpk.shape[0]
    tab_spec = pl.BlockSpec((ATT_N_GROUPS, ROW_TILE, LANES), lambda i: (0, i, 0))
    return pl.pallas_call(
        _qkv_kernel,
        grid=(t // ROW_TILE,),
        in_specs=[pl.BlockSpec((ROW_TILE, D_PACK), lambda i: (i, 0)),
                  _resident((ROW_TILE, ROW_TILE)), _resident((ROW_TILE, ROW_TILE)),
                  _resident((D_MODEL, ATT_QKV_DIM)), tab_spec, tab_spec, tab_spec],
        out_specs=pl.BlockSpec((ROW_TILE, ATT_QKV_DIM), lambda i: (i, 0)),
        out_shape=jax.ShapeDtypeStruct((t, ATT_QKV_DIM), BF16),
        compiler_params=_params("parallel"),
        name="qkv_rope",
    )(xpk, perms[1], perms[2], w, *tabs)


def _attn_kernel(q_ref, kp_ref, kc_ref, vp_ref, vc_ref, o_ref, st_ref, *, chained):
    w = ATT_BLOCK
    i = pl.program_id(2)
    qi = lax.broadcasted_iota(jnp.int32, (w, 2 * w), 0)
    kk = lax.broadcasted_iota(jnp.int32, (w, 2 * w), 1)
    band = (kk >= qi) & (kk <= qi + w)
    lane = lax.broadcasted_iota(jnp.int32, (w, LANES), 1)
    lo_mask = lane < HALF

    def part(ref, j):
        if len(ref.shape) == 2:
            return ref[j * w:(j + 1) * w, :]
        return ref[:, j].reshape(w, ref.shape[-1])

    for j in range(ATT_PROBLEMS):
        if chained and j >= 1:
            k, v, has_prev = kc_ref[(j - 1) * w:(j + 1) * w, :], vc_ref[(j - 1) * w:(j + 1) * w, :], True
        else:
            kprev = kp_ref[...] if chained else part(kp_ref, j)
            vprev = vp_ref[...] if chained else part(vp_ref, j)
            k = jnp.concatenate([kprev, part(kc_ref, j)], axis=0)
            v = jnp.concatenate([vprev, part(vc_ref, j)], axis=0)
            has_prev = False
        valid = band if has_prev else band & (kk >= jnp.where(i > 0, 0, w))
        q = part(q_ref, j)
        stats = jnp.zeros((w, LANES), F32)
        zero = jnp.zeros((), q.dtype)
        parts = []
        for p in range(ATT_HEADS // 2):
            sl = slice(p * LANES, (p + 1) * LANES)
            qp, kp, vp = q[:, sl], k[:, sl], v[:, sl]
            outs = []
            for hh in range(2):
                h = 2 * p + hh
                qm = jnp.where(lo_mask if hh == 0 else ~lo_mask, qp, zero)
                s = jnp.where(valid, _dot_nt(qm, kp), NEG_INF)
                m = jnp.max(s, -1, keepdims=True)
                pr = jnp.exp(s - m)
                l = jnp.sum(pr, -1, keepdims=True)
                outs.append(_dot(pr.astype(v.dtype), vp) / l)
                stats = jnp.where(lane == h, m, stats)
                stats = jnp.where(lane == ATT_HEADS + h, l, stats)
            parts.append(jnp.where(lo_mask, outs[0], outs[1]).astype(o_ref.dtype))
        out = jnp.concatenate(parts, axis=1)
        if len(o_ref.shape) == 2:
            o_ref[j * w:(j + 1) * w, :] = out
            st_ref[j * w:(j + 1) * w, :] = stats
        else:
            o_ref[:, j] = out.reshape(o_ref.shape[0], o_ref.shape[2], o_ref.shape[3])
            st_ref[:, j] = stats.reshape(st_ref.shape[0], st_ref.shape[2], st_ref.shape[3])


def _window_attention(qkv, grp, bsz, seq):
    _, dil = ATT_PATTERNS[grp]
    w = ATT_BLOCK
    t = bsz * seq
    chunk = ROW_TILE // dil
    tiles = w // chunk if chunk < w else 1
    span = dil * w
    nb = seq // span
    col0 = grp * 3
    chained = dil == 1
    g = ATT_PROBLEMS

    if chained:
        grid = (bsz, 1, nb // g)
        per_b = seq // (g * w)

        def spec(width, col, prev):
            if prev:
                return pl.BlockSpec((w, width), lambda b, r, i: (b * g * per_b + jnp.maximum(g * i - 1, 0), col))
            return pl.BlockSpec((g * w, width), lambda b, r, i: (b * per_b + i, col))

        qkv_v, o_shape, st_shape = qkv, (t, ATT_OUT_DIM), (t, LANES)
    elif tiles == 1:
        grid = (bsz, dil // g, nb)
        per_b = seq // (g * w)
        stride = span // (g * w)

        def spec(width, col, prev):
            def imap(b, r, i):
                blk = jnp.maximum(i - 1, 0) if prev else i
                return (b * per_b + blk * stride + r, col)
            return pl.BlockSpec((g * w, width), imap)

        qkv_v, o_shape, st_shape = qkv, (t, ATT_OUT_DIM), (t, LANES)
    else:
        grid = (bsz, dil // g, nb)

        def spec(width, col, prev):
            def imap(b, r, i):
                blk = jnp.maximum(i - 1, 0) if prev else i
                return (b, blk, 0, r, 0, col)
            return pl.BlockSpec((None, None, tiles, g, chunk, width), imap)

        lead = (bsz, nb, tiles, dil, chunk)
        qkv_v, o_shape, st_shape = qkv.reshape(*lead, ATT_QKV_DIM), (*lead, ATT_OUT_DIM), (*lead, LANES)

    o, st = pl.pallas_call(
        functools.partial(_attn_kernel, chained=chained),
        grid=grid,
        in_specs=[spec(ATT_OUT_DIM, col0, False), spec(ATT_OUT_DIM, col0 + 1, True),
                  spec(ATT_OUT_DIM, col0 + 1, False), spec(ATT_OUT_DIM, col0 + 2, True),
                  spec(ATT_OUT_DIM, col0 + 2, False)],
        out_specs=[spec(ATT_OUT_DIM, 0, False), spec(LANES, 0, False)],
        out_shape=[jax.ShapeDtypeStruct(o_shape, BF16), jax.ShapeDtypeStruct(st_shape, F32)],
        compiler_params=_params("parallel", "parallel", "arbitrary"),
        name=f"window_attn_d{dil}",
    )(qkv_v, qkv_v, qkv_v, qkv_v, qkv_v)
    return o.reshape(t, ATT_OUT_DIM), st.reshape(t, LANES)


def _merge_proj_ln_kernel(o1_ref, o2_ref, o3_ref, s1_ref, s2_ref, s3_ref, p4t_ref, p16t_ref, w_ref, x_ref,
                          g_ref, b_ref, w2_ref, rb_ref, o_ref, opk_ref, route_ref, cnt_ref, carry):
    rows = o1_ref.shape[0]
    lane = lax.broadcasted_iota(jnp.int32, (rows, LANES), 1)
    lo_mask = lane < HALF

    def to_token_order(pt, val):
        if val.dtype == BF16:
            return _dot(pt, val)
        return sum(_dot(pt, term) for term in _split3(val))

    p4t, p16t = p4t_ref[...], p16t_ref[...]
    outs = [o1_ref[...].astype(F32), to_token_order(p4t, o2_ref[...]), to_token_order(p16t, o3_ref[...])]
    sts = [s1_ref[...], to_token_order(p4t, s2_ref[...]), to_token_order(p16t, s3_ref[...])]
    mx = jnp.maximum(jnp.maximum(sts[0], sts[1]), sts[2])
    wgts = [pltpu.roll(s, LANES - ATT_HEADS, 1) * jnp.exp(s - mx) for s in sts]
    den = wgts[0] + wgts[1] + wgts[2]
    den = jnp.where(lane < ATT_HEADS, den, 1.0)
    coefs = [wg / den for wg in wgts]
    parts = []
    for p in range(ATT_HEADS // 2):
        sl = slice(p * LANES, (p + 1) * LANES)
        acc = jnp.zeros((rows, LANES), F32)
        for gi in range(ATT_N_GROUPS):
            acc = acc + _pair_expand(coefs[gi], 2 * p, lo_mask) * outs[gi][:, sl]
        parts.append(acc.astype(BF16))
    mix = _dot(jnp.concatenate(parts, axis=1), w_ref[...])
    _ln_route_epilogue(DEEPNORM_ALPHA * x_ref[...] + mix, g_ref, b_ref, w2_ref, rb_ref,
                       o_ref, opk_ref, route_ref, cnt_ref, carry)


def _merge_proj_ln(os_, sts, perms_t, w, x, g, b, router_w):
    t = x.shape[0]
    tm = ROW_TILE
    r_in, r_out, r_shape, r_scratch = _route_specs(t)
    ospec = pl.BlockSpec((tm, ATT_OUT_DIM), lambda i: (i, 0))
    sspec = pl.BlockSpec((tm, LANES), lambda i: (i, 0))
    xspec = pl.BlockSpec((tm, D_MODEL), lambda i: (i, 0))
    vspec = pl.BlockSpec((1, D_MODEL), lambda i: (0, 0))
    pspec = _resident((tm, tm))
    return pl.pallas_call(
        _merge_proj_ln_kernel,
        grid=(t // tm,),
        in_specs=[ospec] * 3 + [sspec] * 3 + [pspec, pspec, _resident((ATT_OUT_DIM, D_MODEL)),
                                              xspec, vspec, vspec] + r_in,
        out_specs=[xspec, pl.BlockSpec((tm, D_PACK), lambda i: (i, 0))] + r_out,
        out_shape=[jax.ShapeDtypeStruct((t, D_MODEL), F32), jax.ShapeDtypeStruct((t, D_PACK), U32)] + r_shape,
        scratch_shapes=r_scratch,
        compiler_params=_params("arbitrary"),
        name="attn_merge_proj_ln",
    )(*os_, *sts, perms_t[1], perms_t[2], w, x, g[None, :], b[None, :], *router_w)


def _route_rows(x, w2_ref, b_ref, carry):
    xhi = x.astype(BF16)
    xlo = (x - xhi.astype(F32)).astype(BF16)
    w2 = w2_ref[...]
    hi2 = _dot(xhi, w2)
    logits = hi2[:, :LANES] + hi2[:, LANES:] + _dot(xlo, w2[:, :LANES]) + b_ref[...]
    rows = logits.shape[0]
    lane = lax.broadcasted_iota(jnp.int32, (rows, LANES), 1)
    big = jnp.int32(LANES)

    def top1(vals, mask):
        v = jnp.where(mask, vals, -jnp.inf)
        m = jnp.max(v, -1, keepdims=True)
        idx = jnp.min(jnp.where(v == m, lane, big), -1, keepdims=True)
        return v, m, idx

    gmask = lane < MOE_N_GROUPS
    gv, gm, gidx = top1(logits, gmask)
    g_w = 1.0 / jnp.sum(jnp.exp(gv - gm), -1, keepdims=True)
    e_lo = MOE_N_GROUPS + gidx * MOE_EPG
    emask = (lane >= e_lo) & (lane < e_lo + MOE_EPG)
    ev, m1, i1 = top1(logits, emask)
    zsum = jnp.sum(jnp.exp(ev - m1), -1, keepdims=True)
    _, m2, i2 = top1(logits, emask & (lane != i1))
    p1 = 1.0 / zsum
    p2 = jnp.exp(m2 - m1) / zsum
    tot = p1 + p2
    e1 = i1 - MOE_N_GROUPS
    e2 = i2 - MOE_N_GROUPS

    oh1 = jnp.where(lane == e1, 1.0, 0.0)
    oh2 = jnp.where(lane == e2, 1.0, 0.0)
    oh = oh1 + oh2
    ri = lax.broadcasted_iota(jnp.int32, (rows, rows), 0)
    ci = lax.broadcasted_iota(jnp.int32, (rows, rows), 1)
    strict = jnp.where(ri > ci, 1.0, 0.0).astype(BF16)
    before = _dot(strict, oh.astype(BF16)) + carry[...]
    rank1 = jnp.sum(oh1 * before, -1, keepdims=True)
    rank2 = jnp.sum(oh2 * before, -1, keepdims=True)
    carry[...] = carry[...] + jnp.sum(oh, 0, keepdims=True)

    vals = [e1.astype(F32), e2.astype(F32), g_w * (p1 / tot), g_w * (p2 / tot), rank1, rank2]
    out = jnp.zeros((rows, LANES), F32)
    for j, val in enumerate(vals):
        out = jnp.where(lane == j, val, out)
    return out


def _router_weights(w_rg, b_rg, w_re, b_re):
    n_log = MOE_N_GROUPS + MOE_N_EXPERTS
    w = jnp.pad(jnp.concatenate([w_rg, w_re], axis=1), ((0, 0), (0, LANES - n_log)))
    whi = w.astype(BF16)
    w2 = jnp.concatenate([whi, (w - whi.astype(F32)).astype(BF16)], axis=1)
    bias = jnp.pad(jnp.concatenate([b_rg, b_re]), (0, LANES - n_log))[None, :]
    return w2, bias


def _ln_route_epilogue(r, g_ref, b_ref, w2_ref, rb_ref, o_ref, opk_ref, route_ref, cnt_ref, carry):
    @pl.when(pl.program_id(0) == 0)
    def _():
        carry[...] = jnp.zeros(carry.shape, F32)

    out = _layer_norm(r, g_ref[...], b_ref[...])
    o_ref[...] = out
    opk_ref[...] = _pack_rows(out)
    route_ref[...] = _route_rows(out, w2_ref, rb_ref, carry)
    cnt_ref[...] = carry[...]


def _expert_kernel(meta_ref, x_ref, wg_ref, wu_ref, wd_ref, o_ref, wg16, wu16, wd16):
    i = pl.program_id(0)
    n_blocks = pl.num_programs(0)

    @pl.when(i < meta_ref[n_blocks])
    def _():
        @pl.when((i == 0) | (meta_ref[i] != meta_ref[jnp.maximum(i - 1, 0)]))
        def _():
            wg16[...] = wg_ref[...].astype(BF16)
            wu16[...] = wu_ref[...].astype(BF16)
            wd16[...] = wd_ref[...].astype(BF16)

        x = _unpack_rows(x_ref[...]).astype(BF16)
        h = _silu(_dot(x, wg16[...])) * _dot(x, wu16[...])
        o_ref[...] = _pack_rows(_dot(h.astype(BF16), wd16[...]))


def _expert_mlp(meta, x_rows, w_gate, w_up, w_down, layer):
    n_rows = x_rows.shape[0]
    n_blocks = n_rows // MOE_ROW_BLOCK
    tb = MOE_ROW_BLOCK
    grid_spec = pltpu.PrefetchScalarGridSpec(
        num_scalar_prefetch=1,
        grid=(n_blocks,),
        in_specs=[pl.BlockSpec((tb, D_PACK), lambda i, meta: (i, 0)),
                  pl.BlockSpec((None, None, D_MODEL, MOE_HIDDEN), lambda i, meta: (layer, meta[i], 0, 0)),
                  pl.BlockSpec((None, None, D_MODEL, MOE_HIDDEN), lambda i, meta: (layer, meta[i], 0, 0)),
                  pl.BlockSpec((None, None, MOE_HIDDEN, D_MODEL), lambda i, meta: (layer, meta[i], 0, 0))],
        out_specs=pl.BlockSpec((tb, D_PACK), lambda i, meta: (i, 0)),
        scratch_shapes=[pltpu.VMEM((D_MODEL, MOE_HIDDEN), BF16), pltpu.VMEM((D_MODEL, MOE_HIDDEN), BF16),
                        pltpu.VMEM((MOE_HIDDEN, D_MODEL), BF16)],
    )
    return pl.pallas_call(
        _expert_kernel,
        grid_spec=grid_spec,
        out_shape=jax.ShapeDtypeStruct((n_rows, D_PACK), U32),
        compiler_params=_params("arbitrary"),
        name="moe_experts",
    )(meta, x_rows, w_gate, w_up, w_down)


def _combine_ln_kernel(y0_ref, y1_ref, r_ref, x_ref, g_ref, b_ref, o_ref, opk_ref):
    route = r_ref[...]
    g0 = route[:, 2:3]
    g1 = route[:, 3:4]
    ffn = g0 * _unpack_rows(y0_ref[...]) + g1 * _unpack_rows(y1_ref[...])
    out = _layer_norm(DEEPNORM_ALPHA * x_ref[...] + ffn, g_ref[...], b_ref[...])
    o_ref[...] = out
    opk_ref[...] = _pack_rows(out)


def _combine_ln(y0, y1, route, x, g, b):
    t = x.shape[0]
    xspec = pl.BlockSpec((ROW_TILE, D_MODEL), lambda i: (i, 0))
    pspec = pl.BlockSpec((ROW_TILE, D_PACK), lambda i: (i, 0))
    vspec = pl.BlockSpec((1, D_MODEL), lambda i: (0, 0))
    return pl.pallas_call(
        _combine_ln_kernel,
        grid=(t // ROW_TILE,),
        in_specs=[pspec, pspec, pl.BlockSpec((ROW_TILE, LANES), lambda i: (i, 0)), xspec, vspec, vspec],
        out_specs=[xspec, pspec],
        out_shape=[jax.ShapeDtypeStruct((t, D_MODEL), F32), jax.ShapeDtypeStruct((t, D_PACK), U32)],
        compiler_params=_params("parallel"),
        name="moe_combine_ln",
    )(y0, y1, route, x, g[None, :], b[None, :])


def _positions_kernel(r_ref, ps_ref, o_ref):
    route = r_ref[...]
    lane = lax.broadcasted_iota(jnp.int32, route.shape, 1)
    starts = ps_ref[...]
    out = jnp.zeros(route.shape, F32)
    for k in range(MOE_TOP_K):
        eid = route[:, k:k + 1].astype(jnp.int32)
        pos = jnp.sum(jnp.where(lane == eid, starts, 0.0), -1, keepdims=True) + route[:, 4 + k:5 + k]
        out = jnp.where(lane == k, pos, out)
    o_ref[...] = out.astype(jnp.int32)


def _positions(route, pad_start):
    t = route.shape[0]
    starts = jnp.pad(pad_start.astype(F32), (0, LANES - MOE_N_EXPERTS))[None, :]
    return pl.pallas_call(
        _positions_kernel,
        grid=(t // ROW_TILE,),
        in_specs=[pl.BlockSpec((ROW_TILE, LANES), lambda i: (i, 0)), pl.BlockSpec((1, LANES), lambda i: (0, 0))],
        out_specs=pl.BlockSpec((ROW_TILE, LANES), lambda i: (i, 0)),
        out_shape=jax.ShapeDtypeStruct((t, LANES), jnp.int32),
        compiler_params=_params("parallel"),
        name="moe_positions",
    )(route, starts)


def _moe(x, xpk, route, cnt, w_gate, w_up, w_down, layer, g, b):
    t = x.shape[0]
    tb = MOE_ROW_BLOCK
    n_assign = t * MOE_TOP_K
    n_blocks = n_assign // tb + MOE_N_EXPERTS
    n_rows = n_blocks * tb
    counts = cnt[0, :MOE_N_EXPERTS].astype(jnp.int32)
    padded = (counts + tb - 1) // tb * tb
    pad_end = jnp.cumsum(padded)
    pos = _positions(route, pad_end - padded)
    pos0, pos1 = pos[:, 0], pos[:, 1]
    block_start = jnp.arange(n_blocks, dtype=jnp.int32) * tb
    block_e = jnp.minimum(jnp.sum((pad_end[None, :] <= block_start[:, None]).astype(jnp.int32), -1),
                          MOE_N_EXPERTS - 1)
    meta = jnp.concatenate([block_e, pad_end[-1:] // tb]).astype(jnp.int32)
    x_rows = _row_scatter(xpk, (pos0, pos1), n_rows)
    y_rows = _expert_mlp(meta, x_rows, w_gate, w_up, w_down, layer)
    y0 = _row_gather(y_rows, pos0)
    y1 = _row_gather(y_rows, pos1)
    return _combine_ln(y0, y1, route, x, g, b)


def _ssd_layer(x, xpk, w_in, conv_w, conv_b, dt_bias, a_log, d_skip, norm_w, w_out, g, b, router_w, bsz, seq):
    w_zxbc = w_in[:, :SSD_D_INNER + SSD_CONV_DIM].astype(BF16)
    w_dt = jnp.pad(w_in[:, SSD_D_INNER + SSD_CONV_DIM:], ((0, 0), (0, LANES - SSD_N_HEADS))).astype(BF16)
    zxbc, dt_raw = _in_proj(xpk, w_zxbc, w_dt, conv_w, conv_b, bsz, seq)
    y = _ssd_scan(zxbc, dt_raw, dt_bias, a_log, d_skip, norm_w, bsz, seq)
    return _proj_ln(y, w_out.astype(BF16), x, g, b, router_w, "ssd_out_proj_ln")


def _attn_layer(x, xpk, tabs, w_qkv, w_o, g, b, router_w, bsz, seq):
    perms = [_perm_matrix(dil) for _, dil in ATT_PATTERNS]
    perms_t = [p.T for p in perms]
    qkv = _qkv_proj(xpk, w_qkv.astype(BF16), tabs, perms)
    os_, sts = [], []
    for grp in range(ATT_N_GROUPS):
        o, st = _window_attention(qkv, grp, bsz, seq)
        os_.append(o)
        sts.append(st)
    return _merge_proj_ln(os_, sts, perms_t, w_o.astype(BF16), x, g, b, router_w)


def kernel(x, positions, ssd_w_in, ssd_conv_w, ssd_conv_b, ssd_dt_bias, ssd_a_log, ssd_d, ssd_norm_w, ssd_w_out,
           attn_w_qkv, attn_w_o, ln_g, ln_b, moe_w_router_group, moe_b_router_group, moe_w_router_expert,
           moe_b_router_expert, moe_w_gate, moe_w_up, moe_w_down):
    bsz, seq, d = x.shape
    t = bsz * seq
    h = x.reshape(t, d)
    hpk = h
    tabs = _rope_tables(positions)
    for i in range(DEPTH):
        j = i // N_MIXERS
        router_w = _router_weights(moe_w_router_group[i], moe_b_router_group[i], moe_w_router_expert[i],
                                   moe_b_router_expert[i])
        if i % N_MIXERS == 0:
            h, hpk, route, cnt = _ssd_layer(h, hpk, ssd_w_in[j], ssd_conv_w[j], ssd_conv_b[j], ssd_dt_bias[j],
                                            ssd_a_log[j], ssd_d[j], ssd_norm_w[j], ssd_w_out[j], ln_g[i, 0],
                                            ln_b[i, 0], router_w, bsz, seq)
        else:
            h, hpk, route, cnt = _attn_layer(h, hpk, tabs, attn_w_qkv[j], attn_w_o[j], ln_g[i, 0], ln_b[i, 0],
                                             router_w, bsz, seq)
        h, hpk = _moe(h, hpk, route, cnt, moe_w_gate, moe_w_up, moe_w_down, i, ln_g[i, 1], ln_b[i, 1])
    return h.reshape(bsz, seq, d)
```

```python
import functools

import jax
import jax.numpy as jnp
from jax import lax
from jax.experimental import pallas as pl
from jax.experimental.pallas import tpu as pltpu
from jax.experimental.pallas import tpu_sc as plsc

F32 = jnp.float32
BF16 = jnp.bfloat16
U32 = jnp.uint32

D_MODEL = 1024
D_PACK = D_MODEL // 2
DEPTH = 4
N_MIXERS = 2

SSD_D_INNER = 2048
SSD_HEAD_DIM = 64
SSD_N_HEADS = 32
SSD_N_GROUPS = 4
SSD_D_STATE = 128
SSD_D_CONV = 4
SSD_CHUNK = 128
SSD_GN = SSD_N_GROUPS * SSD_D_STATE
SSD_CONV_DIM = SSD_D_INNER + 2 * SSD_GN
SSD_GROUP_COLS = SSD_D_INNER // SSD_N_GROUPS
IN_PROJ_COLS = 1024
SSD_PAIR_UNROLL = 4

ATT_HEAD_DIM = 64
ATT_HEADS = 8
ATT_PATTERNS = ((128, 1), (512, 4), (2048, 16))
ATT_N_GROUPS = 3
ATT_OUT_DIM = ATT_HEADS * ATT_HEAD_DIM
ATT_QKV_DIM = ATT_N_GROUPS * 3 * ATT_OUT_DIM
ATT_BLOCK = 128
ATT_PROBLEMS = 8
ROPE_THETA = 500000.0
ROPE_DIM = 16

MOE_N_GROUPS = 4
MOE_EPG = 8
MOE_N_EXPERTS = 32
MOE_TOP_K = 2
MOE_HIDDEN = 512
MOE_ROW_BLOCK = 512

DEEPNORM_ALPHA = (2 * DEPTH) ** 0.25
LN_EPS = 1e-5
RMS_EPS = 1e-5
NEG_INF = -1e30

LANES = 128
HALF = LANES // 2
VMEM_LIMIT = 56 * 1024 * 1024

ROW_TILE = 512


def _params(*sem):
    return pltpu.CompilerParams(dimension_semantics=sem, vmem_limit_bytes=VMEM_LIMIT)


def _silu(v):
    h = 0.5 * v
    return h + h * jnp.tanh(h)


def _layer_norm(r, g, b):
    mu = jnp.mean(r, -1, keepdims=True)
    d = r - mu
    var = jnp.mean(d * d, -1, keepdims=True)
    return d * lax.rsqrt(var + LN_EPS) * g + b


def _split3(v):
    hi = v.astype(BF16)
    r1 = v - hi.astype(F32)
    mid = r1.astype(BF16)
    lo = (r1 - mid.astype(F32)).astype(BF16)
    return hi, mid, lo


def _dot(a, b):
    return jnp.dot(a, b, preferred_element_type=F32)


def _dot_nt(a, b):
    return lax.dot_general(a, b, (((1,), (1,)), ((), ())), preferred_element_type=F32)


def _dot_tn(a, b):
    return lax.dot_general(a, b, (((0,), (0,)), ((), ())), preferred_element_type=F32)


def _pair_expand(mat, h0, lo_mask):
    rows = mat.shape[0]
    a = jnp.broadcast_to(mat[:, h0:h0 + 1], (rows, LANES))
    b = jnp.broadcast_to(mat[:, h0 + 1:h0 + 2], (rows, LANES))
    return jnp.where(lo_mask, a, b)


def _pack_rows(v):
    r = pltpu.bitcast(v.astype(BF16).astype(F32), U32)
    return r[:, :D_PACK] | (r[:, D_PACK:] >> 16)


def _unpack_rows(p):
    hi = pltpu.bitcast(p & jnp.uint32(0xFFFF0000), F32)
    lo = pltpu.bitcast(p << 16, F32)
    return jnp.concatenate([hi, lo], axis=1)


def _row_gather(data, idx):
    n = idx.shape[1]
    d = data.shape[1]
    window = LANES
    dc = d // 2
    mesh = plsc.VectorSubcoreMesh(core_axis_name="core", subcore_axis_name="subcore")

    @functools.partial(pl.kernel, out_type=jax.ShapeDtypeStruct((n, d), data.dtype), mesh=mesh)
    def gather(x_hbm, i_hbm, o_hbm):
        for c in range(d // dc):
            def body(i_vmem, o_vmem, c=c):
                pltpu.sync_copy(x_hbm.at[i_vmem.at[0], pl.ds(c * dc, dc)], o_vmem)

            pltpu.emit_pipeline(
                body,
                grid=(n // window,),
                in_specs=[pl.BlockSpec((1, window), lambda i: (0, i))],
                out_specs=[pl.BlockSpec((window, dc), lambda i, c=c: (i, c))],
                core_axis_name=("core", "subcore"),
                dimension_semantics=(pltpu.PARALLEL,),
            )(i_hbm, o_hbm)

    return gather(data, idx)


def _row_scatter(data, idxs, n_rows):
    t, d = data.shape
    window = LANES
    dc = d // 2
    mesh = plsc.VectorSubcoreMesh(core_axis_name="core", subcore_axis_name="subcore")

    @functools.partial(pl.kernel, out_type=jax.ShapeDtypeStruct((n_rows, d), data.dtype), mesh=mesh)
    def scatter(x_hbm, *refs):
        o_hbm = refs[-1]
        for i_hbm in refs[:-1]:
            for c in range(d // dc):
                def body(x_vmem, i_vmem, c=c):
                    pltpu.sync_copy(x_vmem, o_hbm.at[i_vmem.at[0], pl.ds(c * dc, dc)])

                pltpu.emit_pipeline(
                    body,
                    grid=(t // window,),
                    in_specs=[pl.BlockSpec((window, dc), lambda i, c=c: (i, c)),
                              pl.BlockSpec((1, window), lambda i: (0, i))],
                    out_specs=[],
                    core_axis_name=("core", "subcore"),
                    dimension_semantics=(pltpu.PARALLEL,),
                )(x_hbm, i_hbm)

    return scatter(data, *idxs)


def _resident(shape):
    return pl.BlockSpec(shape, lambda *_: (0,) * len(shape), pipeline_mode=pl.Buffered(1))


def _in_proj_kernel(x_ref, w_ref, wdt_ref, cw_ref, cb_ref, o_ref, dt_ref, ext, tail):
    rows = x_ref.shape[0]
    tn = IN_PROJ_COLS
    n_z = SSD_D_INNER // tn

    @pl.when(pl.program_id(1) == 0)
    def _():
        tail[...] = jnp.zeros(tail.shape, F32)

    x = x_ref[...]
    x = (_unpack_rows(x) if x.dtype == U32 else x).astype(BF16)
    for n in range(n_z):
        sl = slice(n * tn, (n + 1) * tn)
        o_ref[:, sl] = _dot(x, w_ref[:, sl]).astype(o_ref.dtype)
    for j in range(SSD_CONV_DIM // tn):
        sl = slice((n_z + j) * tn, (n_z + j + 1) * tn)
        cl = slice(j * tn, (j + 1) * tn)
        ext[j, 0:8, :] = tail[j]
        ext[j, 8:8 + rows, :] = _dot(x, w_ref[:, sl])
        tail[j] = ext[j, rows:rows + 8, :]
        acc = ext[j, 8:8 + rows, :] * cw_ref[3:4, cl] + cb_ref[:, cl]
        for k in range(SSD_D_CONV - 1):
            acc = acc + ext[j, 5 + k:5 + k + rows, :] * cw_ref[k:k + 1, cl]
        o_ref[:, sl] = _silu(acc).astype(o_ref.dtype)
    dt_ref[...] = _dot(x, wdt_ref[...])


def _in_proj(xpk, w_zxbc, w_dt, conv_w, conv_b, bsz, seq):
    t = xpk.shape[0]
    n = w_zxbc.shape[1]
    tiles = seq // ROW_TILE
    cw = jnp.pad(conv_w, ((0, 8 - SSD_D_CONV), (0, 0)))
    n_conv = SSD_CONV_DIM // IN_PROJ_COLS
    return pl.pallas_call(
        _in_proj_kernel,
        grid=(bsz, tiles),
        in_specs=[pl.BlockSpec((ROW_TILE, xpk.shape[1]), lambda b, s: (b * tiles + s, 0)),
                  _resident((D_MODEL, n)), _resident((D_MODEL, LANES)),
                  _resident((8, SSD_CONV_DIM)), _resident((1, SSD_CONV_DIM))],
        out_specs=[pl.BlockSpec((ROW_TILE, n), lambda b, s: (b * tiles + s, 0)),
                   pl.BlockSpec((ROW_TILE, LANES), lambda b, s: (b * tiles + s, 0))],
        out_shape=[jax.ShapeDtypeStruct((t, n), BF16), jax.ShapeDtypeStruct((t, LANES), F32)],
        scratch_shapes=[pltpu.VMEM((n_conv, ROW_TILE + 8, IN_PROJ_COLS), F32),
                        pltpu.VMEM((n_conv, 8, IN_PROJ_COLS), F32)],
        compiler_params=_params("arbitrary", "arbitrary"),
        name="ssd_in_proj",
    )(xpk, w_zxbc, w_dt, cw, conv_b[None, :])


def _ssd_kernel(z_ref, xs_ref, bc_ref, dt_ref, dtb_ref, alog_ref, dsk_ref, nw_ref, ex_ref, o_ref,
                state, cs_cols, cs_rows, dt_x, cb_all, y_off, y_grp, xd_all, cd_all):
    q = SSD_CHUNK
    pairs = SSD_GROUP_COLS // LANES

    @pl.when(pl.program_id(1) == 0)
    def _():
        state[...] = jnp.zeros(state.shape, F32)

    pre = dt_ref[...] + dtb_ref[...]
    dt = jnp.maximum(pre, 0.0) + jnp.log(1.0 + jnp.exp(-jnp.abs(pre)))
    a = -jnp.exp(alog_ref[...])
    row = lax.broadcasted_iota(jnp.int32, (q, q), 0)
    col = lax.broadcasted_iota(jnp.int32, (q, q), 1)
    causal = row >= col
    lo_mask = col < HALF
    tri = jnp.where(causal, 1.0, 0.0).astype(BF16)
    cs3 = _dot(tri, jnp.concatenate(_split3(dt * a), axis=1))
    cs = cs3[:, :LANES] + cs3[:, LANES:2 * LANES] + cs3[:, 2 * LANES:]
    cs_rows[...] = cs.T
    for h in range(SSD_N_HEADS):
        cs_cols[h] = jnp.broadcast_to(cs[:, h:h + 1], (q, q))
    dt3 = _dot(jnp.concatenate(_split3(dt), axis=0), ex_ref[...])
    dt_x[...] = dt3[:q] + dt3[q:2 * q] + dt3[2 * q:]
    for g in range(SSD_N_GROUPS):
        bg = bc_ref[:, g * SSD_D_STATE:(g + 1) * SSD_D_STATE]
        cg = bc_ref[:, SSD_GN + g * SSD_D_STATE:SSD_GN + (g + 1) * SSD_D_STATE]
        cb_all[g] = _dot_nt(cg, bg)
        y_off[:, g * SSD_GROUP_COLS:(g + 1) * SSD_GROUP_COLS] = _dot(cg, state[g].astype(BF16))

    def group_body(g, carry):
        cb = cb_all[g]

        def pair_body(pp, ssq):
            p = g * pairs + pp
            h0 = 2 * p
            x0 = pl.multiple_of(p * LANES, LANES)
            l0 = pl.multiple_of(pp * LANES, LANES)
            cols = (cs_cols[h0], cs_cols[h0 + 1])
            csx = jnp.where(lo_mask, cols[0], cols[1])
            xp = xs_ref[:, pl.ds(x0, LANES)].astype(F32)
            xdt = xp * dt_x[:, pl.ds(x0, LANES)]
            xdt16 = xdt.astype(BF16)
            last = csx[q - 1:q, :]
            halves = []
            for hh in range(2):
                diff = cols[hh] - cs_rows[pl.ds(h0 + hh, 1), :]
                decay = jnp.exp(jnp.where(causal, diff, -jnp.inf))
                halves.append(_dot((cb * decay).astype(BF16), xdt16))
            y = jnp.where(lo_mask, halves[0], halves[1])
            y = y + y_off[:, pl.ds(x0, LANES)] * jnp.exp(csx) + xp * dsk_ref[:, pl.ds(x0, LANES)]
            y = y * _silu(z_ref[:, pl.ds(x0, LANES)].astype(F32))
            y_grp[:, pl.ds(l0, LANES)] = y
            xd_all[:, pl.ds(x0, LANES)] = (xdt * jnp.exp(last - csx)).astype(BF16)
            cd_all[:, pl.ds(x0, LANES)] = jnp.exp(last)
            return ssq + jnp.sum(y * y, -1, keepdims=True)

        ssq = lax.fori_loop(0, pairs, pair_body, jnp.zeros((q, 1), F32), unroll=SSD_PAIR_UNROLL)
        inv = lax.rsqrt(ssq * (1.0 / SSD_GROUP_COLS) + RMS_EPS)
        g0 = pl.multiple_of(g * SSD_GROUP_COLS, SSD_GROUP_COLS)
        o_ref[:, pl.ds(g0, SSD_GROUP_COLS)] = (
            y_grp[...] * inv * nw_ref[:, pl.ds(g0, SSD_GROUP_COLS)]).astype(o_ref.dtype)
        return carry

    lax.fori_loop(0, SSD_N_GROUPS, group_body, 0)

    for g in range(SSD_N_GROUPS):
        gs = slice(g * SSD_GROUP_COLS, (g + 1) * SSD_GROUP_COLS)
        bg = bc_ref[:, g * SSD_D_STATE:(g + 1) * SSD_D_STATE]
        state[g] = state[g] * cd_all[:, gs] + _dot_tn(bg, xd_all[:, gs])


def _ssd_scan(zxbc, dt_raw, dt_bias, a_log, d_skip, norm_w, bsz, seq):
    t = bsz * seq
    nc = seq // SSD_CHUNK
    q = SSD_CHUNK
    pad_h = LANES - SSD_N_HEADS
    dtb = jnp.pad(dt_bias, (0, pad_h))[None, :]
    alog = jnp.pad(a_log, (0, pad_h))[None, :]
    dsk = jnp.repeat(d_skip, SSD_HEAD_DIM)[None, :]
    nw = norm_w[None, :]
    expand = (jnp.arange(LANES)[:, None] == jnp.arange(SSD_D_INNER)[None, :] // SSD_HEAD_DIM).astype(BF16)

    def const(shape):
        return pl.BlockSpec(shape, lambda b, c: (0, 0))

    return pl.pallas_call(
        _ssd_kernel,
        grid=(bsz, nc),
        in_specs=[pl.BlockSpec((q, SSD_D_INNER), lambda b, c: (b * nc + c, 0)),
                  pl.BlockSpec((q, SSD_D_INNER), lambda b, c: (b * nc + c, 1)),
                  pl.BlockSpec((q, 2 * SSD_GN), lambda b, c: (b * nc + c, 4)),
                  pl.BlockSpec((q, LANES), lambda b, c: (b * nc + c, 0)),
                  const((1, LANES)), const((1, LANES)),
                  const((1, SSD_D_INNER)), const((1, SSD_D_INNER)), const((LANES, SSD_D_INNER))],
        out_specs=pl.BlockSpec((q, SSD_D_INNER), lambda b, c: (b * nc + c, 0)),
        out_shape=jax.ShapeDtypeStruct((t, SSD_D_INNER), BF16),
        scratch_shapes=[pltpu.VMEM((SSD_N_GROUPS, SSD_D_STATE, SSD_GROUP_COLS), F32),
                        pltpu.VMEM((SSD_N_HEADS, q, q), F32),
                        pltpu.VMEM((LANES, q), F32),
                        pltpu.VMEM((q, SSD_D_INNER), F32),
                        pltpu.VMEM((SSD_N_GROUPS, q, q), F32),
                        pltpu.VMEM((q, SSD_D_INNER), F32),
                        pltpu.VMEM((q, SSD_GROUP_COLS), F32),
                        pltpu.VMEM((q, SSD_D_INNER), BF16),
                        pltpu.VMEM((1, SSD_D_INNER), F32)],
        compiler_params=_params("arbitrary", "arbitrary"),
        name="ssd_scan",
    )(zxbc, zxbc, zxbc, dt_raw, dtb, alog, dsk, nw, expand)


def _proj_ln_kernel(y_ref, w_ref, x_ref, g_ref, b_ref, w2_ref, rb_ref, o_ref, opk_ref, route_ref, cnt_ref, carry):
    mix = _dot(y_ref[...].astype(BF16), w_ref[...])
    _ln_route_epilogue(DEEPNORM_ALPHA * x_ref[...] + mix, g_ref, b_ref, w2_ref, rb_ref,
                       o_ref, opk_ref, route_ref, cnt_ref, carry)


def _route_specs(t):
    ins = [_resident((D_MODEL, 2 * LANES)), pl.BlockSpec((1, LANES), lambda i: (0, 0))]
    outs = [pl.BlockSpec((ROW_TILE, LANES), lambda i: (i, 0)), pl.BlockSpec((1, LANES), lambda i: (0, 0))]
    shapes = [jax.ShapeDtypeStruct((t, LANES), F32), jax.ShapeDtypeStruct((1, LANES), F32)]
    return ins, outs, shapes, [pltpu.VMEM((1, LANES), F32)]


def _proj_ln(y, w, x, g, b, router_w, name):
    t, k = y.shape
    r_in, r_out, r_shape, r_scratch = _route_specs(t)
    return pl.pallas_call(
        _proj_ln_kernel,
        grid=(t // ROW_TILE,),
        in_specs=[pl.BlockSpec((ROW_TILE, k), lambda i: (i, 0)),
                  _resident((k, D_MODEL)),
                  pl.BlockSpec((ROW_TILE, D_MODEL), lambda i: (i, 0)),
                  pl.BlockSpec((1, D_MODEL), lambda i: (0, 0)),
                  pl.BlockSpec((1, D_MODEL), lambda i: (0, 0))] + r_in,
        out_specs=[pl.BlockSpec((ROW_TILE, D_MODEL), lambda i: (i, 0)),
                   pl.BlockSpec((ROW_TILE, D_PACK), lambda i: (i, 0))] + r_out,
        out_shape=[jax.ShapeDtypeStruct((t, D_MODEL), F32),
                   jax.ShapeDtypeStruct((t, D_PACK), U32)] + r_shape,
        scratch_shapes=r_scratch,
        compiler_params=_params("arbitrary"),
        name=name,
    )(y, w, x, g[None, :], b[None, :], *router_w)


def _rope_table_kernel(pos_ref, freq_ref, c_ref, s1_ref, s2_ref, tok):
    ang = pos_ref[...].astype(F32) * freq_ref[...]
    d = lax.broadcasted_iota(jnp.int32, ang.shape, 1) % ATT_HEAD_DIM
    cos, sin = jnp.cos(ang), jnp.sin(ang)
    half = ROPE_DIM // 2
    tabs = (jnp.where(d < ROPE_DIM, cos, 1.0),
            jnp.where(d < half, -sin, 0.0),
            jnp.where((d >= half) & (d < ROPE_DIM), sin, 0.0))
    rows = ang.shape[0]
    for ti, (tab, out) in enumerate(zip(tabs, (c_ref, s1_ref, s2_ref))):
        tok[ti] = tab
        for grp, (_, dil) in enumerate(ATT_PATTERNS):
            if dil == 1:
                out[grp] = tab
            else:
                n = rows // dil
                for r in range(dil):
                    out[grp, r * n:(r + 1) * n, :] = tok[ti, pl.ds(r, n, stride=dil), :]


def _residue_order(v, dil):
    tail = v.shape[1:]
    return v.reshape(-1, ROW_TILE // dil, dil, *tail).swapaxes(1, 2).reshape(-1, *tail)


def _perm_matrix(dil):
    src = _residue_order(jnp.arange(ROW_TILE, dtype=jnp.int32), dil)
    return (src[:, None] == jnp.arange(ROW_TILE, dtype=jnp.int32)[None, :]).astype(BF16)


def _rope_tables(positions):
    t = positions.size
    half = ROPE_DIM // 2
    inv_freq = ROPE_THETA ** (-jnp.arange(0, ROPE_DIM, 2, dtype=F32) / ROPE_DIM)
    d = jnp.arange(LANES) % ATT_HEAD_DIM
    freq = jnp.where(d < ROPE_DIM, inv_freq[d % half], 0.0).astype(F32)[None, :]
    tab = jax.ShapeDtypeStruct((ATT_N_GROUPS, t, LANES), F32)
    ospec = pl.BlockSpec((ATT_N_GROUPS, ROW_TILE, LANES), lambda i: (0, i, 0))
    return pl.pallas_call(
        _rope_table_kernel,
        grid=(t // ROW_TILE,),
        in_specs=[pl.BlockSpec((ROW_TILE, 1), lambda i: (i, 0)),
                  pl.BlockSpec((1, LANES), lambda i: (0, 0))],
        out_specs=[ospec, ospec, ospec],
        out_shape=[tab, tab, tab],
        scratch_shapes=[pltpu.VMEM((3, ROW_TILE, LANES), F32)],
        compiler_params=_params("parallel"),
        name="rope_tables",
    )(positions.reshape(t, 1), freq)


def _qkv_kernel(x_ref, p4_ref, p16_ref, w_ref, c_ref, s1_ref, s2_ref, o_ref):
    x16 = _unpack_rows(x_ref[...]).astype(BF16)
    xs = [x16, _dot(p4_ref[...], x16).astype(BF16), _dot(p16_ref[...], x16).astype(BF16)]
    reps = ATT_OUT_DIM // LANES
    half = ROPE_DIM // 2
    for grp in range(ATT_N_GROUPS):
        c = jnp.concatenate([c_ref[grp]] * reps, axis=1)
        s1 = jnp.concatenate([s1_ref[grp]] * reps, axis=1)
        s2 = jnp.concatenate([s2_ref[grp]] * reps, axis=1)
        for kind in range(3):
            j = grp * 3 + kind
            sl = slice(j * ATT_OUT_DIM, (j + 1) * ATT_OUT_DIM)
            acc = _dot(xs[grp], w_ref[:, sl])
            if kind < 2:
                up = pltpu.roll(acc, ATT_OUT_DIM - half, 1)
                down = pltpu.roll(acc, half, 1)
                acc = acc * c + up * s1 + down * s2
            if kind == 0:
                acc = acc * (ATT_HEAD_DIM ** -0.5)
            o_ref[:, sl] = acc.astype(o_ref.dtype)


def _qkv_proj(xpk, w, tabs, perms):
    t = xpk.shape[0]
    tab_spec = pl.BlockSpec((ATT_N_GROUPS, ROW_TILE, LANES), lambda i: (0, i, 0))
    return pl.pallas_call(
        _qkv_kernel,
        grid=(t // ROW_TILE,),
        in_specs=[pl.BlockSpec((ROW_TILE, D_PACK), lambda i: (i, 0)),
                  _resident((ROW_TILE, ROW_TILE)), _resident((ROW_TILE, ROW_TILE)),
                  _resident((D_MODEL, ATT_QKV_DIM)), tab_spec, tab_spec, tab_spec],
        out_specs=pl.BlockSpec((ROW_TILE, ATT_QKV_DIM), lambda i: (i, 0)),
        out_shape=jax.ShapeDtypeStruct((t, ATT_QKV_DIM), BF16),
        compiler_params=_params("parallel"),
        name="qkv_rope",
    )(xpk, perms[1], perms[2], w, *tabs)


def _attn_kernel(q_ref, kp_ref, kc_ref, vp_ref, vc_ref, o_ref, st_ref, *, chained, problems):
    w = ATT_BLOCK
    i = pl.program_id(2)
    qi = lax.broadcasted_iota(jnp.int32, (w, 2 * w), 0)
    kk = lax.broadcasted_iota(jnp.int32, (w, 2 * w), 1)
    band = (kk >= qi) & (kk <= qi + w)
    lane = lax.broadcasted_iota(jnp.int32, (w, LANES), 1)
    lo_mask = lane < HALF

    def part(ref, j):
        if len(ref.shape) == 2:
            return ref[j * w:(j + 1) * w, :]
        return ref[:, j].reshape(w, ref.shape[-1])

    for j in range(problems):
        if chained and j >= 1:
            k, v, has_prev = kc_ref[(j - 1) * w:(j + 1) * w, :], vc_ref[(j - 1) * w:(j + 1) * w, :], True
        else:
            kprev = kp_ref[...] if chained else part(kp_ref, j)
            vprev = vp_ref[...] if chained else part(vp_ref, j)
            k = jnp.concatenate([kprev, part(kc_ref, j)], axis=0)
            v = jnp.concatenate([vprev, part(vc_ref, j)], axis=0)
            has_prev = False
        valid = band if has_prev else band & (kk >= jnp.where(i > 0, 0, w))
        q = part(q_ref, j)
        stats = jnp.zeros((w, LANES), F32)
        zero = jnp.zeros((), q.dtype)
        parts = []
        for p in range(ATT_HEADS // 2):
            sl = slice(p * LANES, (p + 1) * LANES)
            qp, kp, vp = q[:, sl], k[:, sl], v[:, sl]
            outs = []
            for hh in range(2):
                h = 2 * p + hh
                qm = jnp.where(lo_mask if hh == 0 else ~lo_mask, qp, zero)
                s = jnp.where(valid, _dot_nt(qm, kp), NEG_INF)
                m = jnp.max(s, -1, keepdims=True)
                pr = jnp.exp(s - m)
                l = jnp.sum(pr, -1, keepdims=True)
                outs.append(_dot(pr.astype(v.dtype), vp) / l)
                stats = jnp.where(lane == h, m, stats)
                stats = jnp.where(lane == ATT_HEADS + h, l, stats)
            parts.append(jnp.where(lo_mask, outs[0], outs[1]).astype(o_ref.dtype))
        out = jnp.concatenate(parts, axis=1)
        if len(o_ref.shape) == 2:
            o_ref[j * w:(j + 1) * w, :] = out
            st_ref[j * w:(j + 1) * w, :] = stats
        else:
            o_ref[:, j] = out.reshape(o_ref.shape[0], o_ref.shape[2], o_ref.shape[3])
            st_ref[:, j] = stats.reshape(st_ref.shape[0], st_ref.shape[2], st_ref.shape[3])


def _window_attention(qkv, grp, bsz, seq):
    _, dil = ATT_PATTERNS[grp]
    w = ATT_BLOCK
    t = bsz * seq
    chunk = ROW_TILE // dil
    tiles = w // chunk if chunk < w else 1
    span = dil * w
    nb = seq // span
    col0 = grp * 3
    chained = dil == 1
    g = ATT_PROBLEMS if chained else min(ATT_PROBLEMS, dil)

    if chained:
        grid = (bsz, 1, nb // g)
        per_b = seq // (g * w)

        def spec(width, col, prev):
            if prev:
                return pl.BlockSpec((w, width), lambda b, r, i: (b * g * per_b + jnp.maximum(g * i - 1, 0), col))
            return pl.BlockSpec((g * w, width), lambda b, r, i: (b * per_b + i, col))

        qkv_v, o_shape, st_shape = qkv, (t, ATT_OUT_DIM), (t, LANES)
    elif tiles == 1:
        grid = (bsz, dil // g, nb)
        per_b = seq // (g * w)
        stride = span // (g * w)

        def spec(width, col, prev):
            def imap(b, r, i):
                blk = jnp.maximum(i - 1, 0) if prev else i
                return (b * per_b + blk * stride + r, col)
            return pl.BlockSpec((g * w, width), imap)

        qkv_v, o_shape, st_shape = qkv, (t, ATT_OUT_DIM), (t, LANES)
    else:
        grid = (bsz, dil // g, nb)

        def spec(width, col, prev):
            def imap(b, r, i):
                blk = jnp.maximum(i - 1, 0) if prev else i
                return (b, blk, 0, r, 0, col)
            return pl.BlockSpec((None, None, tiles, g, chunk, width), imap)

        lead = (bsz, nb, tiles, dil, chunk)
        qkv_v, o_shape, st_shape = qkv.reshape(*lead, ATT_QKV_DIM), (*lead, ATT_OUT_DIM), (*lead, LANES)

    o, st = pl.pallas_call(
        functools.partial(_attn_kernel, chained=chained, problems=g),
        grid=grid,
        in_specs=[spec(ATT_OUT_DIM, col0, False), spec(ATT_OUT_DIM, col0 + 1, True),
                  spec(ATT_OUT_DIM, col0 + 1, False), spec(ATT_OUT_DIM, col0 + 2, True),
                  spec(ATT_OUT_DIM, col0 + 2, False)],
        out_specs=[spec(ATT_OUT_DIM, 0, False), spec(LANES, 0, False)],
        out_shape=[jax.ShapeDtypeStruct(o_shape, BF16), jax.ShapeDtypeStruct(st_shape, F32)],
        compiler_params=_params("parallel", "parallel", "arbitrary"),
        name=f"window_attn_d{dil}",
    )(qkv_v, qkv_v, qkv_v, qkv_v, qkv_v)
    return o.reshape(t, ATT_OUT_DIM), st.reshape(t, LANES)


def _merge_proj_ln_kernel(o1_ref, o2_ref, o3_ref, s1_ref, s2_ref, s3_ref, p4t_ref, p16t_ref, w_ref, x_ref,
                          g_ref, b_ref, w2_ref, rb_ref, o_ref, opk_ref, route_ref, cnt_ref, carry):
    rows = o1_ref.shape[0]
    lane = lax.broadcasted_iota(jnp.int32, (rows, LANES), 1)
    lo_mask = lane < HALF

    def to_token_order(pt, val):
        if val.dtype == BF16:
            return _dot(pt, val)
        return sum(_dot(pt, term) for term in _split3(val))

    p4t, p16t = p4t_ref[...], p16t_ref[...]
    outs = [o1_ref[...].astype(F32), to_token_order(p4t, o2_ref[...]), to_token_order(p16t, o3_ref[...])]
    sts = [s1_ref[...], to_token_order(p4t, s2_ref[...]), to_token_order(p16t, s3_ref[...])]
    mx = jnp.maximum(jnp.maximum(sts[0], sts[1]), sts[2])
    wgts = [pltpu.roll(s, LANES - ATT_HEADS, 1) * jnp.exp(s - mx) for s in sts]
    den = wgts[0] + wgts[1] + wgts[2]
    den = jnp.where(lane < ATT_HEADS, den, 1.0)
    coefs = [wg / den for wg in wgts]
    parts = []
    for p in range(ATT_HEADS // 2):
        sl = slice(p * LANES, (p + 1) * LANES)
        acc = jnp.zeros((rows, LANES), F32)
        for gi in range(ATT_N_GROUPS):
            acc = acc + _pair_expand(coefs[gi], 2 * p, lo_mask) * outs[gi][:, sl]
        parts.append(acc.astype(BF16))
    mix = _dot(jnp.concatenate(parts, axis=1), w_ref[...])
    _ln_route_epilogue(DEEPNORM_ALPHA * x_ref[...] + mix, g_ref, b_ref, w2_ref, rb_ref,
                       o_ref, opk_ref, route_ref, cnt_ref, carry)


def _merge_proj_ln(os_, sts, perms_t, w, x, g, b, router_w):
    t = x.shape[0]
    tm = ROW_TILE
    r_in, r_out, r_shape, r_scratch = _route_specs(t)
    ospec = pl.BlockSpec((tm, ATT_OUT_DIM), lambda i: (i, 0))
    sspec = pl.BlockSpec((tm, LANES), lambda i: (i, 0))
    xspec = pl.BlockSpec((tm, D_MODEL), lambda i: (i, 0))
    vspec = pl.BlockSpec((1, D_MODEL), lambda i: (0, 0))
    pspec = _resident((tm, tm))
    return pl.pallas_call(
        _merge_proj_ln_kernel,
        grid=(t // tm,),
        in_specs=[ospec] * 3 + [sspec] * 3 + [pspec, pspec, _resident((ATT_OUT_DIM, D_MODEL)),
                                              xspec, vspec, vspec] + r_in,
        out_specs=[xspec, pl.BlockSpec((tm, D_PACK), lambda i: (i, 0))] + r_out,
        out_shape=[jax.ShapeDtypeStruct((t, D_MODEL), F32), jax.ShapeDtypeStruct((t, D_PACK), U32)] + r_shape,
        scratch_shapes=r_scratch,
        compiler_params=_params("arbitrary"),
        name="attn_merge_proj_ln",
    )(*os_, *sts, perms_t[1], perms_t[2], w, x, g[None, :], b[None, :], *router_w)


def _route_rows(x, w2_ref, b_ref, carry):
    xhi = x.astype(BF16)
    xlo = (x - xhi.astype(F32)).astype(BF16)
    w2 = w2_ref[...]
    hi2 = _dot(xhi, w2)
    logits = hi2[:, :LANES] + hi2[:, LANES:] + _dot(xlo, w2[:, :LANES]) + b_ref[...]
    rows = logits.shape[0]
    lane = lax.broadcasted_iota(jnp.int32, (rows, LANES), 1)
    big = jnp.int32(LANES)

    def top1(vals, mask):
        v = jnp.where(mask, vals, -jnp.inf)
        m = jnp.max(v, -1, keepdims=True)
        idx = jnp.min(jnp.where(v == m, lane, big), -1, keepdims=True)
        return v, m, idx

    gmask = lane < MOE_N_GROUPS
    gv, gm, gidx = top1(logits, gmask)
    g_w = 1.0 / jnp.sum(jnp.exp(gv - gm), -1, keepdims=True)
    e_lo = MOE_N_GROUPS + gidx * MOE_EPG
    emask = (lane >= e_lo) & (lane < e_lo + MOE_EPG)
    ev, m1, i1 = top1(logits, emask)
    zsum = jnp.sum(jnp.exp(ev - m1), -1, keepdims=True)
    _, m2, i2 = top1(logits, emask & (lane != i1))
    p1 = 1.0 / zsum
    p2 = jnp.exp(m2 - m1) / zsum
    tot = p1 + p2
    e1 = i1 - MOE_N_GROUPS
    e2 = i2 - MOE_N_GROUPS

    oh1 = jnp.where(lane == e1, 1.0, 0.0)
    oh2 = jnp.where(lane == e2, 1.0, 0.0)
    oh = oh1 + oh2
    ri = lax.broadcasted_iota(jnp.int32, (rows, rows), 0)
    ci = lax.broadcasted_iota(jnp.int32, (rows, rows), 1)
    strict = jnp.where(ri > ci, 1.0, 0.0).astype(BF16)
    before = _dot(strict, oh.astype(BF16)) + carry[...]
    rank1 = jnp.sum(oh1 * before, -1, keepdims=True)
    rank2 = jnp.sum(oh2 * before, -1, keepdims=True)
    carry[...] = carry[...] + jnp.sum(oh, 0, keepdims=True)

    vals = [e1.astype(F32), e2.astype(F32), g_w * (p1 / tot), g_w * (p2 / tot), rank1, rank2]
    out = jnp.zeros((rows, LANES), F32)
    for j, val in enumerate(vals):
        out = jnp.where(lane == j, val, out)
    return out


def _router_weights(w_rg, b_rg, w_re, b_re):
    n_log = MOE_N_GROUPS + MOE_N_EXPERTS
    w = jnp.pad(jnp.concatenate([w_rg, w_re], axis=1), ((0, 0), (0, LANES - n_log)))
    whi = w.astype(BF16)
    w2 = jnp.concatenate([whi, (w - whi.astype(F32)).astype(BF16)], axis=1)
    bias = jnp.pad(jnp.concatenate([b_rg, b_re]), (0, LANES - n_log))[None, :]
    return w2, bias


def _ln_route_epilogue(r, g_ref, b_ref, w2_ref, rb_ref, o_ref, opk_ref, route_ref, cnt_ref, carry):
    @pl.when(pl.program_id(0) == 0)
    def _():
        carry[...] = jnp.zeros(carry.shape, F32)

    out = _layer_norm(r, g_ref[...], b_ref[...])
    o_ref[...] = out
    opk_ref[...] = _pack_rows(out)
    route_ref[...] = _route_rows(out, w2_ref, rb_ref, carry)
    cnt_ref[...] = carry[...]


def _expert_kernel(meta_ref, x_ref, wg_ref, wu_ref, wd_ref, o_ref, wg16, wu16, wd16):
    i = pl.program_id(0)
    n_blocks = pl.num_programs(0)

    @pl.when(i < meta_ref[n_blocks])
    def _():
        @pl.when((i == 0) | (meta_ref[i] != meta_ref[jnp.maximum(i - 1, 0)]))
        def _():
            wg16[...] = wg_ref[...].astype(BF16)
            wu16[...] = wu_ref[...].astype(BF16)
            wd16[...] = wd_ref[...].astype(BF16)

        x = _unpack_rows(x_ref[...]).astype(BF16)
        h = _silu(_dot(x, wg16[...])) * _dot(x, wu16[...])
        o_ref[...] = _pack_rows(_dot(h.astype(BF16), wd16[...]))


def _expert_mlp(meta, x_rows, w_gate, w_up, w_down, layer):
    n_rows = x_rows.shape[0]
    n_blocks = n_rows // MOE_ROW_BLOCK
    tb = MOE_ROW_BLOCK
    grid_spec = pltpu.PrefetchScalarGridSpec(
        num_scalar_prefetch=1,
        grid=(n_blocks,),
        in_specs=[pl.BlockSpec((tb, D_PACK), lambda i, meta: (i, 0)),
                  pl.BlockSpec((None, None, D_MODEL, MOE_HIDDEN), lambda i, meta: (layer, meta[i], 0, 0)),
                  pl.BlockSpec((None, None, D_MODEL, MOE_HIDDEN), lambda i, meta: (layer, meta[i], 0, 0)),
                  pl.BlockSpec((None, None, MOE_HIDDEN, D_MODEL), lambda i, meta: (layer, meta[i], 0, 0))],
        out_specs=pl.BlockSpec((tb, D_PACK), lambda i, meta: (i, 0)),
        scratch_shapes=[pltpu.VMEM((D_MODEL, MOE_HIDDEN), BF16), pltpu.VMEM((D_MODEL, MOE_HIDDEN), BF16),
                        pltpu.VMEM((MOE_HIDDEN, D_MODEL), BF16)],
    )
    return pl.pallas_call(
        _expert_kernel,
        grid_spec=grid_spec,
        out_shape=jax.ShapeDtypeStruct((n_rows, D_PACK), U32),
        compiler_params=_params("arbitrary"),
        name="moe_experts",
    )(meta, x_rows, w_gate, w_up, w_down)


def _combine_ln_kernel(y0_ref, y1_ref, r_ref, x_ref, g_ref, b_ref, o_ref, opk_ref):
    route = r_ref[...]
    g0 = route[:, 2:3]
    g1 = route[:, 3:4]
    ffn = g0 * _unpack_rows(y0_ref[...]) + g1 * _unpack_rows(y1_ref[...])
    out = _layer_norm(DEEPNORM_ALPHA * x_ref[...] + ffn, g_ref[...], b_ref[...])
    o_ref[...] = out
    opk_ref[...] = _pack_rows(out)


def _combine_ln(y0, y1, route, x, g, b):
    t = x.shape[0]
    xspec = pl.BlockSpec((ROW_TILE, D_MODEL), lambda i: (i, 0))
    pspec = pl.BlockSpec((ROW_TILE, D_PACK), lambda i: (i, 0))
    vspec = pl.BlockSpec((1, D_MODEL), lambda i: (0, 0))
    return pl.pallas_call(
        _combine_ln_kernel,
        grid=(t // ROW_TILE,),
        in_specs=[pspec, pspec, pl.BlockSpec((ROW_TILE, LANES), lambda i: (i, 0)), xspec, vspec, vspec],
        out_specs=[xspec, pspec],
        out_shape=[jax.ShapeDtypeStruct((t, D_MODEL), F32), jax.ShapeDtypeStruct((t, D_PACK), U32)],
        compiler_params=_params("parallel"),
        name="moe_combine_ln",
    )(y0, y1, route, x, g[None, :], b[None, :])


def _positions_kernel(r_ref, ps_ref, p0_ref, p1_ref):
    route = r_ref[...]
    lane = lax.broadcasted_iota(jnp.int32, route.shape, 1)
    starts = ps_ref[...]
    out = jnp.zeros(route.shape, F32)
    for k in range(MOE_TOP_K):
        eid = route[:, k:k + 1].astype(jnp.int32)
        pos = jnp.sum(jnp.where(lane == eid, starts, 0.0), -1, keepdims=True) + route[:, 4 + k:5 + k]
        out = jnp.where(lane == k, pos, out)
    for c in range(route.shape[0] // LANES):
        tile = out[c * LANES:(c + 1) * LANES, :].T.astype(jnp.int32)
        p0_ref[:, c * LANES:(c + 1) * LANES] = tile[0:1, :]
        p1_ref[:, c * LANES:(c + 1) * LANES] = tile[1:2, :]


def _positions(route, pad_start):
    t = route.shape[0]
    rows = 4 * ROW_TILE
    starts = jnp.pad(pad_start.astype(F32), (0, LANES - MOE_N_EXPERTS))[None, :]
    ospec = pl.BlockSpec((1, rows), lambda i: (0, i))
    return pl.pallas_call(
        _positions_kernel,
        grid=(t // rows,),
        in_specs=[pl.BlockSpec((rows, LANES), lambda i: (i, 0)), pl.BlockSpec((1, LANES), lambda i: (0, 0))],
        out_specs=[ospec, ospec],
        out_shape=[jax.ShapeDtypeStruct((1, t), jnp.int32)] * 2,
        compiler_params=_params("parallel"),
        name="moe_positions",
    )(route, starts)


def _moe(x, xpk, route, cnt, w_gate, w_up, w_down, layer, g, b):
    t = x.shape[0]
    tb = MOE_ROW_BLOCK
    n_assign = t * MOE_TOP_K
    n_blocks = n_assign // tb + MOE_N_EXPERTS
    n_rows = n_blocks * tb
    counts = cnt[0, :MOE_N_EXPERTS].astype(jnp.int32)
    padded = (counts + tb - 1) // tb * tb
    pad_end = jnp.cumsum(padded)
    pos0, pos1 = _positions(route, pad_end - padded)
    block_start = jnp.arange(n_blocks, dtype=jnp.int32) * tb
    block_e = jnp.minimum(jnp.sum((pad_end[None, :] <= block_start[:, None]).astype(jnp.int32), -1),
                          MOE_N_EXPERTS - 1)
    meta = jnp.concatenate([block_e, pad_end[-1:] // tb]).astype(jnp.int32)
    x_rows = _row_scatter(xpk, (pos0, pos1), n_rows)
    y_rows = _expert_mlp(meta, x_rows, w_gate, w_up, w_down, layer)
    y0 = _row_gather(y_rows, pos0)
    y1 = _row_gather(y_rows, pos1)
    return _combine_ln(y0, y1, route, x, g, b)


def _ssd_layer(x, xpk, w_in, conv_w, conv_b, dt_bias, a_log, d_skip, norm_w, w_out, g, b, router_w, bsz, seq):
    w_zxbc = w_in[:, :SSD_D_INNER + SSD_CONV_DIM].astype(BF16)
    w_dt = jnp.pad(w_in[:, SSD_D_INNER + SSD_CONV_DIM:], ((0, 0), (0, LANES - SSD_N_HEADS))).astype(BF16)
    zxbc, dt_raw = _in_proj(xpk, w_zxbc, w_dt, conv_w, conv_b, bsz, seq)
    y = _ssd_scan(zxbc, dt_raw, dt_bias, a_log, d_skip, norm_w, bsz, seq)
    return _proj_ln(y, w_out.astype(BF16), x, g, b, router_w, "ssd_out_proj_ln")


def _attn_layer(x, xpk, tabs, w_qkv, w_o, g, b, router_w, bsz, seq):
    perms = [_perm_matrix(dil) for _, dil in ATT_PATTERNS]
    perms_t = [p.T for p in perms]
    qkv = _qkv_proj(xpk, w_qkv.astype(BF16), tabs, perms)
    os_, sts = [], []
    for grp in range(ATT_N_GROUPS):
        o, st = _window_attention(qkv, grp, bsz, seq)
        os_.append(o)
        sts.append(st)
    return _merge_proj_ln(os_, sts, perms_t, w_o.astype(BF16), x, g, b, router_w)


def kernel(x, positions, ssd_w_in, ssd_conv_w, ssd_conv_b, ssd_dt_bias, ssd_a_log, ssd_d, ssd_norm_w, ssd_w_out,
           attn_w_qkv, attn_w_o, ln_g, ln_b, moe_w_router_group, moe_b_router_group, moe_w_router_expert,
           moe_b_router_expert, moe_w_gate, moe_w_up, moe_w_down):
    bsz, seq, d = x.shape
    t = bsz * seq
    h = x.reshape(t, d)
    hpk = h
    tabs = _rope_tables(positions)
    for i in range(DEPTH):
        j = i // N_MIXERS
        router_w = _router_weights(moe_w_router_group[i], moe_b_router_group[i], moe_w_router_expert[i],
                                   moe_b_router_expert[i])
        if i % N_MIXERS == 0:
            h, hpk, route, cnt = _ssd_layer(h, hpk, ssd_w_in[j], ssd_conv_w[j], ssd_conv_b[j], ssd_dt_bias[j],
                                            ssd_a_log[j], ssd_d[j], ssd_norm_w[j], ssd_w_out[j], ln_g[i, 0],
                                            ln_b[i, 0], router_w, bsz, seq)
        else:
            h, hpk, route, cnt = _attn_layer(h, hpk, tabs, attn_w_qkv[j], attn_w_o[j], ln_g[i, 0], ln_b[i, 0],
                                             router_w, bsz, seq)
        h, hpk = _moe(h, hpk, route, cnt, moe_w_gate, moe_w_up, moe_w_down, i, ln_g[i, 1], ln_b[i, 1])
    return h.reshape(bsz, seq, d)
```

```python
import functools

import jax
import jax.numpy as jnp
from jax import lax
from jax.experimental import pallas as pl
from jax.experimental.pallas import tpu as pltpu
from jax.experimental.pallas import tpu_sc as plsc

F32 = jnp.float32
BF16 = jnp.bfloat16
U32 = jnp.uint32

D_MODEL = 1024
D_PACK = D_MODEL // 2
DEPTH = 4
N_MIXERS = 2

SSD_D_INNER = 2048
SSD_HEAD_DIM = 64
SSD_N_HEADS = 32
SSD_N_GROUPS = 4
SSD_D_STATE = 128
SSD_D_CONV = 4
SSD_CHUNK = 128
SSD_GN = SSD_N_GROUPS * SSD_D_STATE
SSD_CONV_DIM = SSD_D_INNER + 2 * SSD_GN
SSD_GROUP_COLS = SSD_D_INNER // SSD_N_GROUPS
IN_PROJ_COLS = 1024
SSD_PAIR_UNROLL = 4

ATT_HEAD_DIM = 64
ATT_HEADS = 8
ATT_PATTERNS = ((128, 1), (512, 4), (2048, 16))
ATT_N_GROUPS = 3
ATT_OUT_DIM = ATT_HEADS * ATT_HEAD_DIM
ATT_QKV_DIM = ATT_N_GROUPS * 3 * ATT_OUT_DIM
ATT_BLOCK = 128
ATT_PROBLEMS = 8
ROPE_THETA = 500000.0
ROPE_DIM = 16

MOE_N_GROUPS = 4
MOE_EPG = 8
MOE_N_EXPERTS = 32
MOE_TOP_K = 2
MOE_HIDDEN = 512
MOE_ROW_BLOCK = 512

DEEPNORM_ALPHA = (2 * DEPTH) ** 0.25
LN_EPS = 1e-5
RMS_EPS = 1e-5
NEG_INF = -1e30

LANES = 128
HALF = LANES // 2
VMEM_LIMIT = 56 * 1024 * 1024

ROW_TILE = 512


def _params(*sem):
    return pltpu.CompilerParams(dimension_semantics=sem, vmem_limit_bytes=VMEM_LIMIT)


def _silu(v):
    h = 0.5 * v
    return h + h * jnp.tanh(h)


def _layer_norm(r, g, b):
    mu = jnp.mean(r, -1, keepdims=True)
    d = r - mu
    var = jnp.mean(d * d, -1, keepdims=True)
    return d * lax.rsqrt(var + LN_EPS) * g + b


def _split3(v):
    hi = v.astype(BF16)
    r1 = v - hi.astype(F32)
    mid = r1.astype(BF16)
    lo = (r1 - mid.astype(F32)).astype(BF16)
    return hi, mid, lo


def _dot(a, b):
    return jnp.dot(a, b, preferred_element_type=F32)


def _dot_nt(a, b):
    return lax.dot_general(a, b, (((1,), (1,)), ((), ())), preferred_element_type=F32)


def _dot_tn(a, b):
    return lax.dot_general(a, b, (((0,), (0,)), ((), ())), preferred_element_type=F32)


def _pair_expand(mat, h0, lo_mask):
    rows = mat.shape[0]
    a = jnp.broadcast_to(mat[:, h0:h0 + 1], (rows, LANES))
    b = jnp.broadcast_to(mat[:, h0 + 1:h0 + 2], (rows, LANES))
    return jnp.where(lo_mask, a, b)


def _pack_rows(v):
    r = pltpu.bitcast(v.astype(BF16).astype(F32), U32)
    return r[:, :D_PACK] | (r[:, D_PACK:] >> 16)


def _unpack_rows(p):
    hi = pltpu.bitcast(p & jnp.uint32(0xFFFF0000), F32)
    lo = pltpu.bitcast(p << 16, F32)
    return jnp.concatenate([hi, lo], axis=1)


def _row_gather(data, idx):
    n = idx.shape[1]
    d = data.shape[1]
    window = LANES
    dc = d // 2
    mesh = plsc.VectorSubcoreMesh(core_axis_name="core", subcore_axis_name="subcore")

    @functools.partial(pl.kernel, out_type=jax.ShapeDtypeStruct((n, d), data.dtype), mesh=mesh)
    def gather(x_hbm, i_hbm, o_hbm):
        for c in range(d // dc):
            def body(i_vmem, o_vmem, c=c):
                pltpu.sync_copy(x_hbm.at[i_vmem.at[0], pl.ds(c * dc, dc)], o_vmem)

            pltpu.emit_pipeline(
                body,
                grid=(n // window,),
                in_specs=[pl.BlockSpec((1, window), lambda i: (0, i))],
                out_specs=[pl.BlockSpec((window, dc), lambda i, c=c: (i, c))],
                core_axis_name=("core", "subcore"),
                dimension_semantics=(pltpu.PARALLEL,),
            )(i_hbm, o_hbm)

    return gather(data, idx)


def _row_scatter(data, idxs, n_rows):
    t, d = data.shape
    window = LANES
    dc = d // 2
    mesh = plsc.VectorSubcoreMesh(core_axis_name="core", subcore_axis_name="subcore")

    @functools.partial(pl.kernel, out_type=jax.ShapeDtypeStruct((n_rows, d), data.dtype), mesh=mesh)
    def scatter(x_hbm, *refs):
        o_hbm = refs[-1]
        for i_hbm in refs[:-1]:
            for c in range(d // dc):
                def body(x_vmem, i_vmem, c=c):
                    pltpu.sync_copy(x_vmem, o_hbm.at[i_vmem.at[0], pl.ds(c * dc, dc)])

                pltpu.emit_pipeline(
                    body,
                    grid=(t // window,),
                    in_specs=[pl.BlockSpec((window, dc), lambda i, c=c: (i, c)),
                              pl.BlockSpec((1, window), lambda i: (0, i))],
                    out_specs=[],
                    core_axis_name=("core", "subcore"),
                    dimension_semantics=(pltpu.PARALLEL,),
                )(x_hbm, i_hbm)

    return scatter(data, *idxs)


def _resident(shape):
    return pl.BlockSpec(shape, lambda *_: (0,) * len(shape), pipeline_mode=pl.Buffered(1))


def _in_proj_kernel(x_ref, w_ref, wdt_ref, cw_ref, cb_ref, o_ref, dt_ref, ext, tail):
    rows = x_ref.shape[0]
    tn = IN_PROJ_COLS
    n_z = SSD_D_INNER // tn

    @pl.when(pl.program_id(1) == 0)
    def _():
        tail[...] = jnp.zeros(tail.shape, F32)

    x = x_ref[...]
    x = (_unpack_rows(x) if x.dtype == U32 else x).astype(BF16)
    for n in range(n_z):
        sl = slice(n * tn, (n + 1) * tn)
        o_ref[:, sl] = _dot(x, w_ref[:, sl]).astype(o_ref.dtype)
    for n in range(n_z + SSD_D_INNER // tn, w_ref.shape[1] // tn):
        sl = slice(n * tn, (n + 1) * tn)
        o_ref[:, sl] = _dot(x, w_ref[:, sl]).astype(o_ref.dtype)
    for j in range(SSD_D_INNER // tn):
        sl = slice((n_z + j) * tn, (n_z + j + 1) * tn)
        cl = slice(j * tn, (j + 1) * tn)
        ext[j, 0:8, :] = tail[j]
        ext[j, 8:8 + rows, :] = _dot(x, w_ref[:, sl])
        tail[j] = ext[j, rows:rows + 8, :]
        acc = ext[j, 8:8 + rows, :] * cw_ref[3:4, cl] + cb_ref[:, cl]
        for k in range(SSD_D_CONV - 1):
            acc = acc + ext[j, 5 + k:5 + k + rows, :] * cw_ref[k:k + 1, cl]
        o_ref[:, sl] = _silu(acc).astype(o_ref.dtype)
    dt_ref[...] = _dot(x, wdt_ref[...])


def _in_proj(xpk, w_zxbc, w_dt, conv_w, conv_b, bsz, seq):
    t = xpk.shape[0]
    n = w_zxbc.shape[1]
    tiles = seq // ROW_TILE
    cw = jnp.pad(conv_w[:, :SSD_D_INNER], ((0, 8 - SSD_D_CONV), (0, 0)))
    conv_b = conv_b[:SSD_D_INNER]
    n_conv = SSD_D_INNER // IN_PROJ_COLS
    return pl.pallas_call(
        _in_proj_kernel,
        grid=(bsz, tiles),
        in_specs=[pl.BlockSpec((ROW_TILE, xpk.shape[1]), lambda b, s: (b * tiles + s, 0)),
                  _resident((D_MODEL, n)), _resident((D_MODEL, LANES)),
                  _resident((8, SSD_D_INNER)), _resident((1, SSD_D_INNER))],
        out_specs=[pl.BlockSpec((ROW_TILE, n), lambda b, s: (b * tiles + s, 0)),
                   pl.BlockSpec((ROW_TILE, LANES), lambda b, s: (b * tiles + s, 0))],
        out_shape=[jax.ShapeDtypeStruct((t, n), BF16), jax.ShapeDtypeStruct((t, LANES), F32)],
        scratch_shapes=[pltpu.VMEM((n_conv, ROW_TILE + 8, IN_PROJ_COLS), F32),
                        pltpu.VMEM((n_conv, 8, IN_PROJ_COLS), F32)],
        compiler_params=_params("arbitrary", "arbitrary"),
        name="ssd_in_proj",
    )(xpk, w_zxbc, w_dt, cw, conv_b[None, :])


def _ssd_kernel(z_ref, xs_ref, bc_ref, dt_ref, dtb_ref, alog_ref, dsk_ref, nw_ref, ex_ref, cw_ref, cbias_ref, o_ref,
                state, cs_cols, cs_rows, dt_x, cb_all, y_off, y_grp, xd_all, cd_all, bc_ext, bc_s):
    q = SSD_CHUNK
    pairs = SSD_GROUP_COLS // LANES

    @pl.when(pl.program_id(1) == 0)
    def _():
        state[...] = jnp.zeros(state.shape, F32)
        bc_ext[0:8, :] = jnp.zeros((8, 2 * SSD_GN), F32)

    bc_ext[8:8 + q, :] = bc_ref[...].astype(F32)
    acc = bc_ext[8:8 + q, :] * cw_ref[3:4, :] + cbias_ref[...]
    for k in range(SSD_D_CONV - 1):
        acc = acc + bc_ext[5 + k:5 + k + q, :] * cw_ref[k:k + 1, :]
    bc_s[...] = _silu(acc).astype(BF16)
    bc_ext[0:8, :] = bc_ext[q:q + 8, :]

    pre = dt_ref[...] + dtb_ref[...]
    dt = jnp.maximum(pre, 0.0) + jnp.log(1.0 + jnp.exp(-jnp.abs(pre)))
    a = -jnp.exp(alog_ref[...])
    row = lax.broadcasted_iota(jnp.int32, (q, q), 0)
    col = lax.broadcasted_iota(jnp.int32, (q, q), 1)
    causal = row >= col
    lo_mask = col < HALF
    tri = jnp.where(causal, 1.0, 0.0).astype(BF16)
    cs3 = _dot(tri, jnp.concatenate(_split3(dt * a), axis=1))
    cs = cs3[:, :LANES] + cs3[:, LANES:2 * LANES] + cs3[:, 2 * LANES:]
    cs_rows[...] = cs.T
    for h in range(SSD_N_HEADS):
        cs_cols[h] = jnp.broadcast_to(cs[:, h:h + 1], (q, q))
    dt3 = _dot(jnp.concatenate(_split3(dt), axis=0), ex_ref[...])
    dt_x[...] = dt3[:q] + dt3[q:2 * q] + dt3[2 * q:]
    for g in range(SSD_N_GROUPS):
        bg = bc_s[:, g * SSD_D_STATE:(g + 1) * SSD_D_STATE]
        cg = bc_s[:, SSD_GN + g * SSD_D_STATE:SSD_GN + (g + 1) * SSD_D_STATE]
        cb_all[g] = _dot_nt(cg, bg)
        y_off[:, g * SSD_GROUP_COLS:(g + 1) * SSD_GROUP_COLS] = _dot(cg, state[g].astype(BF16))

    def group_body(g, carry):
        cb = cb_all[g]

        def pair_body(pp, ssq):
            p = g * pairs + pp
            h0 = 2 * p
            x0 = pl.multiple_of(p * LANES, LANES)
            l0 = pl.multiple_of(pp * LANES, LANES)
            cols = (cs_cols[h0], cs_cols[h0 + 1])
            csx = jnp.where(lo_mask, cols[0], cols[1])
            xp = xs_ref[:, pl.ds(x0, LANES)].astype(F32)
            xdt = xp * dt_x[:, pl.ds(x0, LANES)]
            xdt16 = xdt.astype(BF16)
            last = csx[q - 1:q, :]
            halves = []
            for hh in range(2):
                diff = cols[hh] - cs_rows[pl.ds(h0 + hh, 1), :]
                decay = jnp.exp(jnp.where(causal, diff, -jnp.inf))
                halves.append(_dot((cb * decay).astype(BF16), xdt16))
            y = jnp.where(lo_mask, halves[0], halves[1])
            y = y + y_off[:, pl.ds(x0, LANES)] * jnp.exp(csx) + xp * dsk_ref[:, pl.ds(x0, LANES)]
            y = y * _silu(z_ref[:, pl.ds(x0, LANES)].astype(F32))
            y_grp[:, pl.ds(l0, LANES)] = y
            xd_all[:, pl.ds(x0, LANES)] = (xdt * jnp.exp(last - csx)).astype(BF16)
            cd_all[:, pl.ds(x0, LANES)] = jnp.exp(last)
            return ssq + jnp.sum(y * y, -1, keepdims=True)

        ssq = lax.fori_loop(0, pairs, pair_body, jnp.zeros((q, 1), F32), unroll=SSD_PAIR_UNROLL)
        inv = lax.rsqrt(ssq * (1.0 / SSD_GROUP_COLS) + RMS_EPS)
        g0 = pl.multiple_of(g * SSD_GROUP_COLS, SSD_GROUP_COLS)
        o_ref[:, pl.ds(g0, SSD_GROUP_COLS)] = (
            y_grp[...] * inv * nw_ref[:, pl.ds(g0, SSD_GROUP_COLS)]).astype(o_ref.dtype)
        return carry

    lax.fori_loop(0, SSD_N_GROUPS, group_body, 0)

    for g in range(SSD_N_GROUPS):
        gs = slice(g * SSD_GROUP_COLS, (g + 1) * SSD_GROUP_COLS)
        bg = bc_s[:, g * SSD_D_STATE:(g + 1) * SSD_D_STATE]
        state[g] = state[g] * cd_all[:, gs] + _dot_tn(bg, xd_all[:, gs])


def _ssd_scan(zxbc, dt_raw, conv_w, conv_b, dt_bias, a_log, d_skip, norm_w, bsz, seq):
    t = bsz * seq
    nc = seq // SSD_CHUNK
    q = SSD_CHUNK
    pad_h = LANES - SSD_N_HEADS
    dtb = jnp.pad(dt_bias, (0, pad_h))[None, :]
    alog = jnp.pad(a_log, (0, pad_h))[None, :]
    dsk = jnp.repeat(d_skip, SSD_HEAD_DIM)[None, :]
    nw = norm_w[None, :]
    expand = (jnp.arange(LANES)[:, None] == jnp.arange(SSD_D_INNER)[None, :] // SSD_HEAD_DIM).astype(BF16)
    cw_bc = jnp.pad(conv_w[:, SSD_D_INNER:], ((0, 8 - SSD_D_CONV), (0, 0)))
    cb_bc = conv_b[None, SSD_D_INNER:]

    def const(shape):
        return pl.BlockSpec(shape, lambda b, c: (0, 0))

    return pl.pallas_call(
        _ssd_kernel,
        grid=(bsz, nc),
        in_specs=[pl.BlockSpec((q, SSD_D_INNER), lambda b, c: (b * nc + c, 0)),
                  pl.BlockSpec((q, SSD_D_INNER), lambda b, c: (b * nc + c, 1)),
                  pl.BlockSpec((q, 2 * SSD_GN), lambda b, c: (b * nc + c, 4)),
                  pl.BlockSpec((q, LANES), lambda b, c: (b * nc + c, 0)),
                  const((1, LANES)), const((1, LANES)),
                  const((1, SSD_D_INNER)), const((1, SSD_D_INNER)), const((LANES, SSD_D_INNER)),
                  const((8, 2 * SSD_GN)), const((1, 2 * SSD_GN))],
        out_specs=pl.BlockSpec((q, SSD_D_INNER), lambda b, c: (b * nc + c, 0)),
        out_shape=jax.ShapeDtypeStruct((t, SSD_D_INNER), BF16),
        scratch_shapes=[pltpu.VMEM((SSD_N_GROUPS, SSD_D_STATE, SSD_GROUP_COLS), F32),
                        pltpu.VMEM((SSD_N_HEADS, q, q), F32),
                        pltpu.VMEM((LANES, q), F32),
                        pltpu.VMEM((q, SSD_D_INNER), F32),
                        pltpu.VMEM((SSD_N_GROUPS, q, q), F32),
                        pltpu.VMEM((q, SSD_D_INNER), F32),
                        pltpu.VMEM((q, SSD_GROUP_COLS), F32),
                        pltpu.VMEM((q, SSD_D_INNER), BF16),
                        pltpu.VMEM((1, SSD_D_INNER), F32),
                        pltpu.VMEM((q + 8, 2 * SSD_GN), F32),
                        pltpu.VMEM((q, 2 * SSD_GN), BF16)],
        compiler_params=_params("arbitrary", "arbitrary"),
        name="ssd_scan",
    )(zxbc, zxbc, zxbc, dt_raw, dtb, alog, dsk, nw, expand, cw_bc, cb_bc)


def _proj_ln_kernel(y_ref, w_ref, x_ref, g_ref, b_ref, w2_ref, rb_ref, o_ref, opk_ref, route_ref, cnt_ref, carry):
    mix = _dot(y_ref[...].astype(BF16), w_ref[...])
    _ln_route_epilogue(DEEPNORM_ALPHA * x_ref[...] + mix, g_ref, b_ref, w2_ref, rb_ref,
                       o_ref, opk_ref, route_ref, cnt_ref, carry)


def _route_specs(t):
    ins = [_resident((D_MODEL, 2 * LANES)), pl.BlockSpec((1, LANES), lambda i: (0, 0))]
    outs = [pl.BlockSpec((ROW_TILE, LANES), lambda i: (i, 0)), pl.BlockSpec((1, LANES), lambda i: (0, 0))]
    shapes = [jax.ShapeDtypeStruct((t, LANES), F32), jax.ShapeDtypeStruct((1, LANES), F32)]
    return ins, outs, shapes, [pltpu.VMEM((1, LANES), F32)]


def _proj_ln(y, w, x, g, b, router_w, name):
    t, k = y.shape
    r_in, r_out, r_shape, r_scratch = _route_specs(t)
    return pl.pallas_call(
        _proj_ln_kernel,
        grid=(t // ROW_TILE,),
        in_specs=[pl.BlockSpec((ROW_TILE, k), lambda i: (i, 0)),
                  _resident((k, D_MODEL)),
                  pl.BlockSpec((ROW_TILE, D_MODEL), lambda i: (i, 0)),
                  pl.BlockSpec((1, D_MODEL), lambda i: (0, 0)),
                  pl.BlockSpec((1, D_MODEL), lambda i: (0, 0))] + r_in,
        out_specs=[pl.BlockSpec((ROW_TILE, D_MODEL), lambda i: (i, 0)),
                   pl.BlockSpec((ROW_TILE, D_PACK), lambda i: (i, 0))] + r_out,
        out_shape=[jax.ShapeDtypeStruct((t, D_MODEL), F32),
                   jax.ShapeDtypeStruct((t, D_PACK), U32)] + r_shape,
        scratch_shapes=r_scratch,
        compiler_params=_params("arbitrary"),
        name=name,
    )(y, w, x, g[None, :], b[None, :], *router_w)


def _rope_table_kernel(pos_ref, freq_ref, c_ref, s1_ref, s2_ref, tok):
    ang = pos_ref[...].astype(F32) * freq_ref[...]
    d = lax.broadcasted_iota(jnp.int32, ang.shape, 1) % ATT_HEAD_DIM
    cos, sin = jnp.cos(ang), jnp.sin(ang)
    half = ROPE_DIM // 2
    tabs = (jnp.where(d < ROPE_DIM, cos, 1.0),
            jnp.where(d < half, -sin, 0.0),
            jnp.where((d >= half) & (d < ROPE_DIM), sin, 0.0))
    rows = ang.shape[0]
    for ti, (tab, out) in enumerate(zip(tabs, (c_ref, s1_ref, s2_ref))):
        tok[ti] = tab
        for grp, (_, dil) in enumerate(ATT_PATTERNS):
            if dil == 1:
                out[grp] = tab
            else:
                n = rows // dil
                for r in range(dil):
                    out[grp, r * n:(r + 1) * n, :] = tok[ti, pl.ds(r, n, stride=dil), :]


def _residue_order(v, dil):
    tail = v.shape[1:]
    return v.reshape(-1, ROW_TILE // dil, dil, *tail).swapaxes(1, 2).reshape(-1, *tail)


def _perm_matrix(dil):
    src = _residue_order(jnp.arange(ROW_TILE, dtype=jnp.int32), dil)
    return (src[:, None] == jnp.arange(ROW_TILE, dtype=jnp.int32)[None, :]).astype(BF16)


def _rope_tables(positions):
    t = positions.size
    half = ROPE_DIM // 2
    inv_freq = ROPE_THETA ** (-jnp.arange(0, ROPE_DIM, 2, dtype=F32) / ROPE_DIM)
    d = jnp.arange(LANES) % ATT_HEAD_DIM
    freq = jnp.where(d < ROPE_DIM, inv_freq[d % half], 0.0).astype(F32)[None, :]
    tab = jax.ShapeDtypeStruct((ATT_N_GROUPS, t, LANES), F32)
    ospec = pl.BlockSpec((ATT_N_GROUPS, ROW_TILE, LANES), lambda i: (0, i, 0))
    return pl.pallas_call(
        _rope_table_kernel,
        grid=(t // ROW_TILE,),
        in_specs=[pl.BlockSpec((ROW_TILE, 1), lambda i: (i, 0)),
                  pl.BlockSpec((1, LANES), lambda i: (0, 0))],
        out_specs=[ospec, ospec, ospec],
        out_shape=[tab, tab, tab],
        scratch_shapes=[pltpu.VMEM((3, ROW_TILE, LANES), F32)],
        compiler_params=_params("parallel"),
        name="rope_tables",
    )(positions.reshape(t, 1), freq)


def _qkv_kernel(x_ref, p4_ref, p16_ref, w_ref, c_ref, s1_ref, s2_ref, o_ref):
    x16 = _unpack_rows(x_ref[...]).astype(BF16)
    xs = [x16, _dot(p4_ref[...], x16).astype(BF16), _dot(p16_ref[...], x16).astype(BF16)]
    reps = ATT_OUT_DIM // LANES
    half = ROPE_DIM // 2
    for grp in range(ATT_N_GROUPS):
        c = jnp.concatenate([c_ref[grp]] * reps, axis=1)
        s1 = jnp.concatenate([s1_ref[grp]] * reps, axis=1)
        s2 = jnp.concatenate([s2_ref[grp]] * reps, axis=1)
        for kind in range(3):
            j = grp * 3 + kind
            sl = slice(j * ATT_OUT_DIM, (j + 1) * ATT_OUT_DIM)
            acc = _dot(xs[grp], w_ref[:, sl])
            if kind < 2:
                up = pltpu.roll(acc, ATT_OUT_DIM - half, 1)
                down = pltpu.roll(acc, half, 1)
                acc = acc * c + up * s1 + down * s2
            if kind == 0:
                acc = acc * (ATT_HEAD_DIM ** -0.5)
            o_ref[:, sl] = acc.astype(o_ref.dtype)


def _qkv_proj(xpk, w, tabs, perms):
    t = xpk.shape[0]
    tab_spec = pl.BlockSpec((ATT_N_GROUPS, ROW_TILE, LANES), lambda i: (0, i, 0))
    return pl.pallas_call(
        _qkv_kernel,
        grid=(t // ROW_TILE,),
        in_specs=[pl.BlockSpec((ROW_TILE, D_PACK), lambda i: (i, 0)),
                  _resident((ROW_TILE, ROW_TILE)), _resident((ROW_TILE, ROW_TILE)),
                  _resident((D_MODEL, ATT_QKV_DIM)), tab_spec, tab_spec, tab_spec],
        out_specs=pl.BlockSpec((ROW_TILE, ATT_QKV_DIM), lambda i: (i, 0)),
        out_shape=jax.ShapeDtypeStruct((t, ATT_QKV_DIM), BF16),
        compiler_params=_params("parallel"),
        name="qkv_rope",
    )(xpk, perms[1], perms[2], w, *tabs)


def _attn_kernel(q_ref, kp_ref, kc_ref, vp_ref, vc_ref, o_ref, st_ref, *, chained, problems):
    w = ATT_BLOCK
    i = pl.program_id(2)
    qi = lax.broadcasted_iota(jnp.int32, (w, 2 * w), 0)
    kk = lax.broadcasted_iota(jnp.int32, (w, 2 * w), 1)
    band = (kk >= qi) & (kk <= qi + w)
    lane = lax.broadcasted_iota(jnp.int32, (w, LANES), 1)
    lo_mask = lane < HALF

    def part(ref, j):
        if len(ref.shape) == 2:
            return ref[j * w:(j + 1) * w, :]
        return ref[:, j].reshape(w, ref.shape[-1])

    for j in range(problems):
        if chained and j >= 1:
            k, v, has_prev = kc_ref[(j - 1) * w:(j + 1) * w, :], vc_ref[(j - 1) * w:(j + 1) * w, :], True
        else:
            kprev = kp_ref[...] if chained else part(kp_ref, j)
            vprev = vp_ref[...] if chained else part(vp_ref, j)
            k = jnp.concatenate([kprev, part(kc_ref, j)], axis=0)
            v = jnp.concatenate([vprev, part(vc_ref, j)], axis=0)
            has_prev = False
        valid = band if has_prev else band & (kk >= jnp.where(i > 0, 0, w))
        q = part(q_ref, j)
        stats = jnp.zeros((w, LANES), F32)
        zero = jnp.zeros((), q.dtype)
        parts = []
        for p in range(ATT_HEADS // 2):
            sl = slice(p * LANES, (p + 1) * LANES)
            qp, kp, vp = q[:, sl], k[:, sl], v[:, sl]
            outs = []
            for hh in range(2):
                h = 2 * p + hh
                qm = jnp.where(lo_mask if hh == 0 else ~lo_mask, qp, zero)
                s = jnp.where(valid, _dot_nt(qm, kp), NEG_INF)
                m = jnp.max(s, -1, keepdims=True)
                pr = jnp.exp(s - m)
                l = jnp.sum(pr, -1, keepdims=True)
                outs.append(_dot(pr.astype(v.dtype), vp) / l)
                stats = jnp.where(lane == h, m, stats)
                stats = jnp.where(lane == ATT_HEADS + h, l, stats)
            parts.append(jnp.where(lo_mask, outs[0], outs[1]).astype(o_ref.dtype))
        out = jnp.concatenate(parts, axis=1)
        if len(o_ref.shape) == 2:
            o_ref[j * w:(j + 1) * w, :] = out
            st_ref[j * w:(j + 1) * w, :] = stats
        else:
            o_ref[:, j] = out.reshape(o_ref.shape[0], o_ref.shape[2], o_ref.shape[3])
            st_ref[:, j] = stats.reshape(st_ref.shape[0], st_ref.shape[2], st_ref.shape[3])


def _window_attention(qkv, grp, bsz, seq):
    _, dil = ATT_PATTERNS[grp]
    w = ATT_BLOCK
    t = bsz * seq
    chunk = ROW_TILE // dil
    tiles = w // chunk if chunk < w else 1
    span = dil * w
    nb = seq // span
    col0 = grp * 3
    chained = dil == 1
    g = ATT_PROBLEMS if chained else min(ATT_PROBLEMS, dil)

    if chained:
        grid = (bsz, 1, nb // g)
        per_b = seq // (g * w)

        def spec(width, col, prev):
            if prev:
                return pl.BlockSpec((w, width), lambda b, r, i: (b * g * per_b + jnp.maximum(g * i - 1, 0), col))
            return pl.BlockSpec((g * w, width), lambda b, r, i: (b * per_b + i, col))

        qkv_v, o_shape, st_shape = qkv, (t, ATT_OUT_DIM), (t, LANES)
    elif tiles == 1:
        grid = (bsz, dil // g, nb)
        per_b = seq // (g * w)
        stride = span // (g * w)

        def spec(width, col, prev):
            def imap(b, r, i):
                blk = jnp.maximum(i - 1, 0) if prev else i
                return (b * per_b + blk * stride + r, col)
            return pl.BlockSpec((g * w, width), imap)

        qkv_v, o_shape, st_shape = qkv, (t, ATT_OUT_DIM), (t, LANES)
    else:
        grid = (bsz, dil // g, nb)

        def spec(width, col, prev):
            def imap(b, r, i):
                blk = jnp.maximum(i - 1, 0) if prev else i
                return (b, blk, 0, r, 0, col)
            return pl.BlockSpec((None, None, tiles, g, chunk, width), imap)

        lead = (bsz, nb, tiles, dil, chunk)
        qkv_v, o_shape, st_shape = qkv.reshape(*lead, ATT_QKV_DIM), (*lead, ATT_OUT_DIM), (*lead, LANES)

    o, st = pl.pallas_call(
        functools.partial(_attn_kernel, chained=chained, problems=g),
        grid=grid,
        in_specs=[spec(ATT_OUT_DIM, col0, False), spec(ATT_OUT_DIM, col0 + 1, True),
                  spec(ATT_OUT_DIM, col0 + 1, False), spec(ATT_OUT_DIM, col0 + 2, True),
                  spec(ATT_OUT_DIM, col0 + 2, False)],
        out_specs=[spec(ATT_OUT_DIM, 0, False), spec(LANES, 0, False)],
        out_shape=[jax.ShapeDtypeStruct(o_shape, BF16), jax.ShapeDtypeStruct(st_shape, F32)],
        compiler_params=_params("parallel", "parallel", "arbitrary"),
        name=f"window_attn_d{dil}",
    )(qkv_v, qkv_v, qkv_v, qkv_v, qkv_v)
    return o.reshape(t, ATT_OUT_DIM), st.reshape(t, LANES)


def _merge_proj_ln_kernel(o1_ref, o2_ref, o3_ref, s1_ref, s2_ref, s3_ref, p4t_ref, p16t_ref, w_ref, x_ref,
                          g_ref, b_ref, w2_ref, rb_ref, o_ref, opk_ref, route_ref, cnt_ref, carry):
    rows = o1_ref.shape[0]
    lane = lax.broadcasted_iota(jnp.int32, (rows, LANES), 1)
    lo_mask = lane < HALF

    def to_token_order(pt, val):
        if val.dtype == BF16:
            return _dot(pt, val)
        return sum(_dot(pt, term) for term in _split3(val))

    p4t, p16t = p4t_ref[...], p16t_ref[...]
    outs = [o1_ref[...].astype(F32), to_token_order(p4t, o2_ref[...]), to_token_order(p16t, o3_ref[...])]
    sts = [s1_ref[...], to_token_order(p4t, s2_ref[...]), to_token_order(p16t, s3_ref[...])]
    mx = jnp.maximum(jnp.maximum(sts[0], sts[1]), sts[2])
    wgts = [pltpu.roll(s, LANES - ATT_HEADS, 1) * jnp.exp(s - mx) for s in sts]
    den = wgts[0] + wgts[1] + wgts[2]
    den = jnp.where(lane < ATT_HEADS, den, 1.0)
    coefs = [wg / den for wg in wgts]
    parts = []
    for p in range(ATT_HEADS // 2):
        sl = slice(p * LANES, (p + 1) * LANES)
        acc = jnp.zeros((rows, LANES), F32)
        for gi in range(ATT_N_GROUPS):
            acc = acc + _pair_expand(coefs[gi], 2 * p, lo_mask) * outs[gi][:, sl]
        parts.append(acc.astype(BF16))
    mix = _dot(jnp.concatenate(parts, axis=1), w_ref[...])
    _ln_route_epilogue(DEEPNORM_ALPHA * x_ref[...] + mix, g_ref, b_ref, w2_ref, rb_ref,
                       o_ref, opk_ref, route_ref, cnt_ref, carry)


def _merge_proj_ln(os_, sts, perms_t, w, x, g, b, router_w):
    t = x.shape[0]
    tm = ROW_TILE
    r_in, r_out, r_shape, r_scratch = _route_specs(t)
    ospec = pl.BlockSpec((tm, ATT_OUT_DIM), lambda i: (i, 0))
    sspec = pl.BlockSpec((tm, LANES), lambda i: (i, 0))
    xspec = pl.BlockSpec((tm, D_MODEL), lambda i: (i, 0))
    vspec = pl.BlockSpec((1, D_MODEL), lambda i: (0, 0))
    pspec = _resident((tm, tm))
    return pl.pallas_call(
        _merge_proj_ln_kernel,
        grid=(t // tm,),
        in_specs=[ospec] * 3 + [sspec] * 3 + [pspec, pspec, _resident((ATT_OUT_DIM, D_MODEL)),
                                              xspec, vspec, vspec] + r_in,
        out_specs=[xspec, pl.BlockSpec((tm, D_PACK), lambda i: (i, 0))] + r_out,
        out_shape=[jax.ShapeDtypeStruct((t, D_MODEL), F32), jax.ShapeDtypeStruct((t, D_PACK), U32)] + r_shape,
        scratch_shapes=r_scratch,
        compiler_params=_params("arbitrary"),
        name="attn_merge_proj_ln",
    )(*os_, *sts, perms_t[1], perms_t[2], w, x, g[None, :], b[None, :], *router_w)


def _route_rows(x, w2_ref, b_ref, carry):
    xhi = x.astype(BF16)
    xlo = (x - xhi.astype(F32)).astype(BF16)
    w2 = w2_ref[...]
    hi2 = _dot(xhi, w2)
    logits = hi2[:, :LANES] + hi2[:, LANES:] + _dot(xlo, w2[:, :LANES]) + b_ref[...]
    rows = logits.shape[0]
    lane = lax.broadcasted_iota(jnp.int32, (rows, LANES), 1)
    big = jnp.int32(LANES)

    def top1(vals, mask):
        v = jnp.where(mask, vals, -jnp.inf)
        m = jnp.max(v, -1, keepdims=True)
        idx = jnp.min(jnp.where(v == m, lane, big), -1, keepdims=True)
        return v, m, idx

    gmask = lane < MOE_N_GROUPS
    gv, gm, gidx = top1(logits, gmask)
    g_w = 1.0 / jnp.sum(jnp.exp(gv - gm), -1, keepdims=True)
    e_lo = MOE_N_GROUPS + gidx * MOE_EPG
    emask = (lane >= e_lo) & (lane < e_lo + MOE_EPG)
    ev, m1, i1 = top1(logits, emask)
    zsum = jnp.sum(jnp.exp(ev - m1), -1, keepdims=True)
    _, m2, i2 = top1(logits, emask & (lane != i1))
    p1 = 1.0 / zsum
    p2 = jnp.exp(m2 - m1) / zsum
    tot = p1 + p2
    e1 = i1 - MOE_N_GROUPS
    e2 = i2 - MOE_N_GROUPS

    oh1 = jnp.where(lane == e1, 1.0, 0.0)
    oh2 = jnp.where(lane == e2, 1.0, 0.0)
    oh = oh1 + oh2
    ri = lax.broadcasted_iota(jnp.int32, (rows, rows), 0)
    ci = lax.broadcasted_iota(jnp.int32, (rows, rows), 1)
    strict = jnp.where(ri > ci, 1.0, 0.0).astype(BF16)
    before = _dot(strict, oh.astype(BF16)) + carry[...]
    rank1 = jnp.sum(oh1 * before, -1, keepdims=True)
    rank2 = jnp.sum(oh2 * before, -1, keepdims=True)
    carry[...] = carry[...] + jnp.sum(oh, 0, keepdims=True)

    vals = [e1.astype(F32), e2.astype(F32), g_w * (p1 / tot), g_w * (p2 / tot), rank1, rank2]
    out = jnp.zeros((rows, LANES), F32)
    for j, val in enumerate(vals):
        out = jnp.where(lane == j, val, out)
    return out


def _router_weights(w_rg, b_rg, w_re, b_re):
    n_log = MOE_N_GROUPS + MOE_N_EXPERTS
    w = jnp.pad(jnp.concatenate([w_rg, w_re], axis=1), ((0, 0), (0, LANES - n_log)))
    whi = w.astype(BF16)
    w2 = jnp.concatenate([whi, (w - whi.astype(F32)).astype(BF16)], axis=1)
    bias = jnp.pad(jnp.concatenate([b_rg, b_re]), (0, LANES - n_log))[None, :]
    return w2, bias


def _ln_route_epilogue(r, g_ref, b_ref, w2_ref, rb_ref, o_ref, opk_ref, route_ref, cnt_ref, carry):
    @pl.when(pl.program_id(0) == 0)
    def _():
        carry[...] = jnp.zeros(carry.shape, F32)

    out = _layer_norm(r, g_ref[...], b_ref[...])
    o_ref[...] = out
    opk_ref[...] = _pack_rows(out)
    route_ref[...] = _route_rows(out, w2_ref, rb_ref, carry)
    cnt_ref[...] = carry[...]


def _expert_kernel(meta_ref, x_ref, wg_ref, wu_ref, wd_ref, o_ref, wg16, wu16, wd16):
    i = pl.program_id(0)
    n_blocks = pl.num_programs(0)

    @pl.when(i < meta_ref[n_blocks])
    def _():
        @pl.when((i == 0) | (meta_ref[i] != meta_ref[jnp.maximum(i - 1, 0)]))
        def _():
            wg16[...] = wg_ref[...].astype(BF16)
            wu16[...] = wu_ref[...].astype(BF16)
            wd16[...] = wd_ref[...].astype(BF16)

        rows = lax.broadcasted_iota(jnp.int32, x_ref.shape, 0)
        xw = jnp.where(rows < meta_ref[n_blocks + 1 + i], x_ref[...], jnp.uint32(0))
        x = _unpack_rows(xw).astype(BF16)
        h = _silu(_dot(x, wg16[...])) * _dot(x, wu16[...])
        o_ref[...] = _pack_rows(_dot(h.astype(BF16), wd16[...]))


def _expert_mlp(meta, x_rows, w_gate, w_up, w_down, layer):
    n_rows = x_rows.shape[0]
    n_blocks = n_rows // MOE_ROW_BLOCK
    tb = MOE_ROW_BLOCK
    grid_spec = pltpu.PrefetchScalarGridSpec(
        num_scalar_prefetch=1,
        grid=(n_blocks,),
        in_specs=[pl.BlockSpec((tb, D_PACK), lambda i, meta: (i, 0)),
                  pl.BlockSpec((None, None, D_MODEL, MOE_HIDDEN), lambda i, meta: (layer, meta[i], 0, 0)),
                  pl.BlockSpec((None, None, D_MODEL, MOE_HIDDEN), lambda i, meta: (layer, meta[i], 0, 0)),
                  pl.BlockSpec((None, None, MOE_HIDDEN, D_MODEL), lambda i, meta: (layer, meta[i], 0, 0))],
        out_specs=pl.BlockSpec((tb, D_PACK), lambda i, meta: (i, 0)),
        scratch_shapes=[pltpu.VMEM((D_MODEL, MOE_HIDDEN), BF16), pltpu.VMEM((D_MODEL, MOE_HIDDEN), BF16),
                        pltpu.VMEM((MOE_HIDDEN, D_MODEL), BF16)],
    )
    return pl.pallas_call(
        _expert_kernel,
        grid_spec=grid_spec,
        out_shape=jax.ShapeDtypeStruct((n_rows, D_PACK), U32),
        compiler_params=_params("arbitrary"),
        name="moe_experts",
    )(meta, x_rows, w_gate, w_up, w_down)


def _combine_ln_kernel(y0_ref, y1_ref, r_ref, x_ref, g_ref, b_ref, o_ref, opk_ref):
    route = r_ref[...]
    g0 = route[:, 2:3]
    g1 = route[:, 3:4]
    ffn = g0 * _unpack_rows(y0_ref[...]) + g1 * _unpack_rows(y1_ref[...])
    out = _layer_norm(DEEPNORM_ALPHA * x_ref[...] + ffn, g_ref[...], b_ref[...])
    o_ref[...] = out
    opk_ref[...] = _pack_rows(out)


def _combine_ln(y0, y1, route, x, g, b):
    t = x.shape[0]
    xspec = pl.BlockSpec((ROW_TILE, D_MODEL), lambda i: (i, 0))
    pspec = pl.BlockSpec((ROW_TILE, D_PACK), lambda i: (i, 0))
    vspec = pl.BlockSpec((1, D_MODEL), lambda i: (0, 0))
    return pl.pallas_call(
        _combine_ln_kernel,
        grid=(t // ROW_TILE,),
        in_specs=[pspec, pspec, pl.BlockSpec((ROW_TILE, LANES), lambda i: (i, 0)), xspec, vspec, vspec],
        out_specs=[xspec, pspec],
        out_shape=[jax.ShapeDtypeStruct((t, D_MODEL), F32), jax.ShapeDtypeStruct((t, D_PACK), U32)],
        compiler_params=_params("parallel"),
        name="moe_combine_ln",
    )(y0, y1, route, x, g[None, :], b[None, :])


def _positions_kernel(r_ref, ps_ref, p0_ref, p1_ref):
    route = r_ref[...]
    lane = lax.broadcasted_iota(jnp.int32, route.shape, 1)
    starts = ps_ref[...]
    out = jnp.zeros(route.shape, F32)
    for k in range(MOE_TOP_K):
        eid = route[:, k:k + 1].astype(jnp.int32)
        pos = jnp.sum(jnp.where(lane == eid, starts, 0.0), -1, keepdims=True) + route[:, 4 + k:5 + k]
        out = jnp.where(lane == k, pos, out)
    for c in range(route.shape[0] // LANES):
        tile = out[c * LANES:(c + 1) * LANES, :].T.astype(jnp.int32)
        p0_ref[:, c * LANES:(c + 1) * LANES] = tile[0:1, :]
        p1_ref[:, c * LANES:(c + 1) * LANES] = tile[1:2, :]


def _positions(route, pad_start):
    t = route.shape[0]
    rows = 4 * ROW_TILE
    starts = jnp.pad(pad_start.astype(F32), (0, LANES - MOE_N_EXPERTS))[None, :]
    ospec = pl.BlockSpec((1, rows), lambda i: (0, i))
    return pl.pallas_call(
        _positions_kernel,
        grid=(t // rows,),
        in_specs=[pl.BlockSpec((rows, LANES), lambda i: (i, 0)), pl.BlockSpec((1, LANES), lambda i: (0, 0))],
        out_specs=[ospec, ospec],
        out_shape=[jax.ShapeDtypeStruct((1, t), jnp.int32)] * 2,
        compiler_params=_params("parallel"),
        name="moe_positions",
    )(route, starts)


def _moe(x, xpk, route, cnt, w_gate, w_up, w_down, layer, g, b):
    t = x.shape[0]
    tb = MOE_ROW_BLOCK
    n_assign = t * MOE_TOP_K
    n_blocks = n_assign // tb + MOE_N_EXPERTS
    n_rows = n_blocks * tb
    counts = cnt[0, :MOE_N_EXPERTS].astype(jnp.int32)
    padded = (counts + tb - 1) // tb * tb
    pad_end = jnp.cumsum(padded)
    pos0, pos1 = _positions(route, pad_end - padded)
    block_start = jnp.arange(n_blocks, dtype=jnp.int32) * tb
    block_e = jnp.minimum(jnp.sum((pad_end[None, :] <= block_start[:, None]).astype(jnp.int32), -1),
                          MOE_N_EXPERTS - 1)
    valid = jnp.clip(pad_end[block_e] - padded[block_e] + counts[block_e] - block_start, 0, tb)
    meta = jnp.concatenate([block_e, pad_end[-1:] // tb, valid]).astype(jnp.int32)
    x_rows = _row_scatter(xpk, (pos0, pos1), n_rows)
    y_rows = _expert_mlp(meta, x_rows, w_gate, w_up, w_down, layer)
    y0 = _row_gather(y_rows, pos0)
    y1 = _row_gather(y_rows, pos1)
    return _combine_ln(y0, y1, route, x, g, b)


def _ssd_layer(x, xpk, w_in, conv_w, conv_b, dt_bias, a_log, d_skip, norm_w, w_out, g, b, router_w, bsz, seq):
    w_zxbc = w_in[:, :SSD_D_INNER + SSD_CONV_DIM].astype(BF16)
    w_dt = jnp.pad(w_in[:, SSD_D_INNER + SSD_CONV_DIM:], ((0, 0), (0, LANES - SSD_N_HEADS))).astype(BF16)
    zxbc, dt_raw = _in_proj(xpk, w_zxbc, w_dt, conv_w, conv_b, bsz, seq)
    y = _ssd_scan(zxbc, dt_raw, conv_w, conv_b, dt_bias, a_log, d_skip, norm_w, bsz, seq)
    return _proj_ln(y, w_out.astype(BF16), x, g, b, router_w, "ssd_out_proj_ln")


def _attn_layer(x, xpk, tabs, w_qkv, w_o, g, b, router_w, bsz, seq):
    perms = [_perm_matrix(dil) for _, dil in ATT_PATTERNS]
    perms_t = [p.T for p in perms]
    qkv = _qkv_proj(xpk, w_qkv.astype(BF16), tabs, perms)
    os_, sts = [], []
    for grp in range(ATT_N_GROUPS):
        o, st = _window_attention(qkv, grp, bsz, seq)
        os_.append(o)
        sts.append(st)
    return _merge_proj_ln(os_, sts, perms_t, w_o.astype(BF16), x, g, b, router_w)


def kernel(x, positions, ssd_w_in, ssd_conv_w, ssd_conv_b, ssd_dt_bias, ssd_a_log, ssd_d, ssd_norm_w, ssd_w_out,
           attn_w_qkv, attn_w_o, ln_g, ln_b, moe_w_router_group, moe_b_router_group, moe_w_router_expert,
           moe_b_router_expert, moe_w_gate, moe_w_up, moe_w_down):
    bsz, seq, d = x.shape
    t = bsz * seq
    h = x.reshape(t, d)
    hpk = h
    tabs = _rope_tables(positions)
    for i in range(DEPTH):
        j = i // N_MIXERS
        router_w = _router_weights(moe_w_router_group[i], moe_b_router_group[i], moe_w_router_expert[i],
                                   moe_b_router_expert[i])
        if i % N_MIXERS == 0:
            h, hpk, route, cnt = _ssd_layer(h, hpk, ssd_w_in[j], ssd_conv_w[j], ssd_conv_b[j], ssd_dt_bias[j],
                                            ssd_a_log[j], ssd_d[j], ssd_norm_w[j], ssd_w_out[j], ln_g[i, 0],
                                            ln_b[i, 0], router_w, bsz, seq)
        else:
            h, hpk, route, cnt = _attn_layer(h, hpk, tabs, attn_w_qkv[j], attn_w_o[j], ln_g[i, 0], ln_b[i, 0],
                                             router_w, bsz, seq)
        h, hpk = _moe(h, hpk, route, cnt, moe_w_gate, moe_w_up, moe_w_down, i, ln_g[i, 1], ln_b[i, 1])
    return h.reshape(bsz, seq, d)
```

```python
import functools

import jax
import jax.numpy as jnp
from jax import lax
from jax.experimental import pallas as pl
from jax.experimental.pallas import tpu as pltpu
from jax.experimental.pallas import tpu_sc as plsc

F32 = jnp.float32
BF16 = jnp.bfloat16
U32 = jnp.uint32

D_MODEL = 1024
D_PACK = D_MODEL // 2
DEPTH = 4
N_MIXERS = 2

SSD_D_INNER = 2048
SSD_HEAD_DIM = 64
SSD_N_HEADS = 32
SSD_N_GROUPS = 4
SSD_D_STATE = 128
SSD_D_CONV = 4
SSD_CHUNK = 128
SSD_GN = SSD_N_GROUPS * SSD_D_STATE
SSD_CONV_DIM = SSD_D_INNER + 2 * SSD_GN
SSD_GROUP_COLS = SSD_D_INNER // SSD_N_GROUPS
IN_PROJ_COLS = 1024
SSD_PAIR_UNROLL = 4

ATT_HEAD_DIM = 64
ATT_HEADS = 8
ATT_PATTERNS = ((128, 1), (512, 4), (2048, 16))
ATT_N_GROUPS = 3
ATT_OUT_DIM = ATT_HEADS * ATT_HEAD_DIM
ATT_QKV_DIM = ATT_N_GROUPS * 3 * ATT_OUT_DIM
ATT_BLOCK = 128
ATT_PROBLEMS = 8
ROPE_THETA = 500000.0
ROPE_DIM = 16

MOE_N_GROUPS = 4
MOE_EPG = 8
MOE_N_EXPERTS = 32
MOE_TOP_K = 2
MOE_HIDDEN = 512
MOE_ROW_BLOCK = 512

DEEPNORM_ALPHA = (2 * DEPTH) ** 0.25
LN_EPS = 1e-5
RMS_EPS = 1e-5
NEG_INF = -1e30

LANES = 128
HALF = LANES // 2
VMEM_LIMIT = 56 * 1024 * 1024

ROW_TILE = 512


def _params(*sem):
    return pltpu.CompilerParams(dimension_semantics=sem, vmem_limit_bytes=VMEM_LIMIT)


def _silu(v):
    h = 0.5 * v
    return h + h * jnp.tanh(h)


def _layer_norm(r, g, b):
    mu = jnp.mean(r, -1, keepdims=True)
    d = r - mu
    var = jnp.mean(d * d, -1, keepdims=True)
    return d * lax.rsqrt(var + LN_EPS) * g + b


def _split3(v):
    hi = v.astype(BF16)
    r1 = v - hi.astype(F32)
    mid = r1.astype(BF16)
    lo = (r1 - mid.astype(F32)).astype(BF16)
    return hi, mid, lo


def _dot(a, b):
    return jnp.dot(a, b, preferred_element_type=F32)


def _dot_nt(a, b):
    return lax.dot_general(a, b, (((1,), (1,)), ((), ())), preferred_element_type=F32)


def _dot_tn(a, b):
    return lax.dot_general(a, b, (((0,), (0,)), ((), ())), preferred_element_type=F32)


def _pair_expand(mat, h0, lo_mask):
    rows = mat.shape[0]
    a = jnp.broadcast_to(mat[:, h0:h0 + 1], (rows, LANES))
    b = jnp.broadcast_to(mat[:, h0 + 1:h0 + 2], (rows, LANES))
    return jnp.where(lo_mask, a, b)


def _pack_rows(v):
    r = pltpu.bitcast(v.astype(BF16).astype(F32), U32)
    return r[:, :D_PACK] | (r[:, D_PACK:] >> 16)


def _unpack_rows(p):
    hi = pltpu.bitcast(p & jnp.uint32(0xFFFF0000), F32)
    lo = pltpu.bitcast(p << 16, F32)
    return jnp.concatenate([hi, lo], axis=1)


def _row_gather(data, idx):
    n = idx.shape[1]
    d = data.shape[1]
    window = LANES
    dc = d // 2
    mesh = plsc.VectorSubcoreMesh(core_axis_name="core", subcore_axis_name="subcore")

    @functools.partial(pl.kernel, out_type=jax.ShapeDtypeStruct((n, d), data.dtype), mesh=mesh)
    def gather(x_hbm, i_hbm, o_hbm):
        for c in range(d // dc):
            def body(i_vmem, o_vmem, c=c):
                pltpu.sync_copy(x_hbm.at[i_vmem.at[0], pl.ds(c * dc, dc)], o_vmem)

            pltpu.emit_pipeline(
                body,
                grid=(n // window,),
                in_specs=[pl.BlockSpec((1, window), lambda i: (0, i))],
                out_specs=[pl.BlockSpec((window, dc), lambda i, c=c: (i, c))],
                core_axis_name=("core", "subcore"),
                dimension_semantics=(pltpu.PARALLEL,),
            )(i_hbm, o_hbm)

    return gather(data, idx)


def _row_scatter(data, idxs, n_rows):
    t, d = data.shape
    window = LANES
    dc = d // 2
    mesh = plsc.VectorSubcoreMesh(core_axis_name="core", subcore_axis_name="subcore")

    @functools.partial(pl.kernel, out_type=jax.ShapeDtypeStruct((n_rows, d), data.dtype), mesh=mesh)
    def scatter(x_hbm, *refs):
        o_hbm = refs[-1]
        for i_hbm in refs[:-1]:
            for c in range(d // dc):
                def body(x_vmem, i_vmem, c=c):
                    pltpu.sync_copy(x_vmem, o_hbm.at[i_vmem.at[0], pl.ds(c * dc, dc)])

                pltpu.emit_pipeline(
                    body,
                    grid=(t // window,),
                    in_specs=[pl.BlockSpec((window, dc), lambda i, c=c: (i, c)),
                              pl.BlockSpec((1, window), lambda i: (0, i))],
                    out_specs=[],
                    core_axis_name=("core", "subcore"),
                    dimension_semantics=(pltpu.PARALLEL,),
                )(x_hbm, i_hbm)

    return scatter(data, *idxs)


def _resident(shape):
    return pl.BlockSpec(shape, lambda *_: (0,) * len(shape), pipeline_mode=pl.Buffered(1))


def _in_proj_kernel(x_ref, w_ref, wdt_ref, cw_ref, cb_ref, o_ref, dt_ref, ext, tail):
    rows = x_ref.shape[0]
    tn = IN_PROJ_COLS
    n_z = SSD_D_INNER // tn

    @pl.when(pl.program_id(1) == 0)
    def _():
        tail[...] = jnp.zeros(tail.shape, F32)

    x = x_ref[...]
    x = (_unpack_rows(x) if x.dtype == U32 else x).astype(BF16)
    for n in range(n_z):
        sl = slice(n * tn, (n + 1) * tn)
        o_ref[:, sl] = _dot(x, w_ref[:, sl]).astype(o_ref.dtype)
    for n in range(n_z + SSD_D_INNER // tn, w_ref.shape[1] // tn):
        sl = slice(n * tn, (n + 1) * tn)
        o_ref[:, sl] = _dot(x, w_ref[:, sl]).astype(o_ref.dtype)
    for j in range(SSD_D_INNER // tn):
        sl = slice((n_z + j) * tn, (n_z + j + 1) * tn)
        cl = slice(j * tn, (j + 1) * tn)
        ext[j, 0:8, :] = tail[j]
        ext[j, 8:8 + rows, :] = _dot(x, w_ref[:, sl])
        tail[j] = ext[j, rows:rows + 8, :]
        acc = ext[j, 8:8 + rows, :] * cw_ref[3:4, cl] + cb_ref[:, cl]
        for k in range(SSD_D_CONV - 1):
            acc = acc + ext[j, 5 + k:5 + k + rows, :] * cw_ref[k:k + 1, cl]
        o_ref[:, sl] = _silu(acc).astype(o_ref.dtype)
    dt_ref[...] = _dot(x, wdt_ref[...])


def _in_proj(xpk, w_zxbc, w_dt, conv_w, conv_b, bsz, seq):
    t = xpk.shape[0]
    n = w_zxbc.shape[1]
    tiles = seq // ROW_TILE
    cw = jnp.pad(conv_w[:, :SSD_D_INNER], ((0, 8 - SSD_D_CONV), (0, 0)))
    conv_b = conv_b[:SSD_D_INNER]
    n_conv = SSD_D_INNER // IN_PROJ_COLS
    return pl.pallas_call(
        _in_proj_kernel,
        grid=(bsz, tiles),
        in_specs=[pl.BlockSpec((ROW_TILE, xpk.shape[1]), lambda b, s: (b * tiles + s, 0)),
                  _resident((D_MODEL, n)), _resident((D_MODEL, LANES)),
                  _resident((8, SSD_D_INNER)), _resident((1, SSD_D_INNER))],
        out_specs=[pl.BlockSpec((ROW_TILE, n), lambda b, s: (b * tiles + s, 0)),
                   pl.BlockSpec((ROW_TILE, LANES), lambda b, s: (b * tiles + s, 0))],
        out_shape=[jax.ShapeDtypeStruct((t, n), BF16), jax.ShapeDtypeStruct((t, LANES), F32)],
        scratch_shapes=[pltpu.VMEM((n_conv, ROW_TILE + 8, IN_PROJ_COLS), F32),
                        pltpu.VMEM((n_conv, 8, IN_PROJ_COLS), F32)],
        compiler_params=_params("arbitrary", "arbitrary"),
        name="ssd_in_proj",
    )(xpk, w_zxbc, w_dt, cw, conv_b[None, :])


def _ssd_kernel(z_ref, xs_ref, bc_ref, dt_ref, dtb_ref, alog_ref, dsk_ref, nw_ref, ex_ref, cw_ref, cbias_ref, o_ref,
                state, cs_cols, cs_rows, dt_x, cb_all, y_off, y_grp, xd_all, cd_all, bc_ext, bc_s):
    q = SSD_CHUNK
    pairs = SSD_GROUP_COLS // LANES

    @pl.when(pl.program_id(1) == 0)
    def _():
        state[...] = jnp.zeros(state.shape, F32)
        bc_ext[0:8, :] = jnp.zeros((8, 2 * SSD_GN), F32)

    bc_ext[8:8 + q, :] = bc_ref[...].astype(F32)
    acc = bc_ext[8:8 + q, :] * cw_ref[3:4, :] + cbias_ref[...]
    for k in range(SSD_D_CONV - 1):
        acc = acc + bc_ext[5 + k:5 + k + q, :] * cw_ref[k:k + 1, :]
    bc_s[...] = _silu(acc).astype(BF16)
    bc_ext[0:8, :] = bc_ext[q:q + 8, :]

    pre = dt_ref[...] + dtb_ref[...]
    dt = jnp.maximum(pre, 0.0) + jnp.log(1.0 + jnp.exp(-jnp.abs(pre)))
    a = -jnp.exp(alog_ref[...])
    row = lax.broadcasted_iota(jnp.int32, (q, q), 0)
    col = lax.broadcasted_iota(jnp.int32, (q, q), 1)
    causal = row >= col
    lo_mask = col < HALF
    tri = jnp.where(causal, 1.0, 0.0).astype(BF16)
    cs3 = _dot(tri, jnp.concatenate(_split3(dt * a), axis=1))
    cs = cs3[:, :LANES] + cs3[:, LANES:2 * LANES] + cs3[:, 2 * LANES:]
    cs_rows[...] = cs.T
    for h in range(SSD_N_HEADS):
        cs_cols[h] = jnp.broadcast_to(cs[:, h:h + 1], (q, q))
    dt3 = _dot(jnp.concatenate(_split3(dt), axis=0), ex_ref[...])
    dt_x[...] = dt3[:q] + dt3[q:2 * q] + dt3[2 * q:]
    for g in range(SSD_N_GROUPS):
        bg = bc_s[:, g * SSD_D_STATE:(g + 1) * SSD_D_STATE]
        cg = bc_s[:, SSD_GN + g * SSD_D_STATE:SSD_GN + (g + 1) * SSD_D_STATE]
        cb_all[g] = _dot_nt(cg, bg)
        y_off[:, g * SSD_GROUP_COLS:(g + 1) * SSD_GROUP_COLS] = _dot(cg, state[g].astype(BF16))

    def group_body(g, carry):
        cb = cb_all[g]

        def pair_body(pp, ssq):
            p = g * pairs + pp
            h0 = 2 * p
            x0 = pl.multiple_of(p * LANES, LANES)
            l0 = pl.multiple_of(pp * LANES, LANES)
            cols = (cs_cols[h0], cs_cols[h0 + 1])
            csx = jnp.where(lo_mask, cols[0], cols[1])
            xp = xs_ref[:, pl.ds(x0, LANES)].astype(F32)
            xdt = xp * dt_x[:, pl.ds(x0, LANES)]
            xdt16 = xdt.astype(BF16)
            last = csx[q - 1:q, :]
            halves = []
            for hh in range(2):
                diff = cols[hh] - cs_rows[pl.ds(h0 + hh, 1), :]
                decay = jnp.exp(jnp.where(causal, diff, -jnp.inf))
                halves.append(_dot((cb * decay).astype(BF16), xdt16))
            y = jnp.where(lo_mask, halves[0], halves[1])
            y = y + y_off[:, pl.ds(x0, LANES)] * jnp.exp(csx) + xp * dsk_ref[:, pl.ds(x0, LANES)]
            y = y * _silu(z_ref[:, pl.ds(x0, LANES)].astype(F32))
            y_grp[:, pl.ds(l0, LANES)] = y
            xd_all[:, pl.ds(x0, LANES)] = (xdt * jnp.exp(last - csx)).astype(BF16)
            cd_all[:, pl.ds(x0, LANES)] = jnp.exp(last)
            return ssq + jnp.sum(y * y, -1, keepdims=True)

        ssq = lax.fori_loop(0, pairs, pair_body, jnp.zeros((q, 1), F32), unroll=SSD_PAIR_UNROLL)
        inv = lax.rsqrt(ssq * (1.0 / SSD_GROUP_COLS) + RMS_EPS)
        g0 = pl.multiple_of(g * SSD_GROUP_COLS, SSD_GROUP_COLS)
        o_ref[:, pl.ds(g0, SSD_GROUP_COLS)] = (
            y_grp[...] * inv * nw_ref[:, pl.ds(g0, SSD_GROUP_COLS)]).astype(o_ref.dtype)
        return carry

    lax.fori_loop(0, SSD_N_GROUPS, group_body, 0)

    for g in range(SSD_N_GROUPS):
        gs = slice(g * SSD_GROUP_COLS, (g + 1) * SSD_GROUP_COLS)
        bg = bc_s[:, g * SSD_D_STATE:(g + 1) * SSD_D_STATE]
        state[g] = state[g] * cd_all[:, gs] + _dot_tn(bg, xd_all[:, gs])


def _ssd_scan(zxbc, dt_raw, conv_w, conv_b, dt_bias, a_log, d_skip, norm_w, bsz, seq):
    t = bsz * seq
    nc = seq // SSD_CHUNK
    q = SSD_CHUNK
    pad_h = LANES - SSD_N_HEADS
    dtb = jnp.pad(dt_bias, (0, pad_h))[None, :]
    alog = jnp.pad(a_log, (0, pad_h))[None, :]
    dsk = jnp.repeat(d_skip, SSD_HEAD_DIM)[None, :]
    nw = norm_w[None, :]
    expand = (jnp.arange(LANES)[:, None] == jnp.arange(SSD_D_INNER)[None, :] // SSD_HEAD_DIM).astype(BF16)
    cw_bc = jnp.pad(conv_w[:, SSD_D_INNER:], ((0, 8 - SSD_D_CONV), (0, 0)))
    cb_bc = conv_b[None, SSD_D_INNER:]

    def const(shape):
        return pl.BlockSpec(shape, lambda b, c: (0, 0))

    return pl.pallas_call(
        _ssd_kernel,
        grid=(bsz, nc),
        in_specs=[pl.BlockSpec((q, SSD_D_INNER), lambda b, c: (b * nc + c, 0)),
                  pl.BlockSpec((q, SSD_D_INNER), lambda b, c: (b * nc + c, 1)),
                  pl.BlockSpec((q, 2 * SSD_GN), lambda b, c: (b * nc + c, 4)),
                  pl.BlockSpec((q, LANES), lambda b, c: (b * nc + c, 0)),
                  const((1, LANES)), const((1, LANES)),
                  const((1, SSD_D_INNER)), const((1, SSD_D_INNER)), const((LANES, SSD_D_INNER)),
                  const((8, 2 * SSD_GN)), const((1, 2 * SSD_GN))],
        out_specs=pl.BlockSpec((q, SSD_D_INNER), lambda b, c: (b * nc + c, 0)),
        out_shape=jax.ShapeDtypeStruct((t, SSD_D_INNER), BF16),
        scratch_shapes=[pltpu.VMEM((SSD_N_GROUPS, SSD_D_STATE, SSD_GROUP_COLS), F32),
                        pltpu.VMEM((SSD_N_HEADS, q, q), F32),
                        pltpu.VMEM((LANES, q), F32),
                        pltpu.VMEM((q, SSD_D_INNER), F32),
                        pltpu.VMEM((SSD_N_GROUPS, q, q), F32),
                        pltpu.VMEM((q, SSD_D_INNER), F32),
                        pltpu.VMEM((q, SSD_GROUP_COLS), F32),
                        pltpu.VMEM((q, SSD_D_INNER), BF16),
                        pltpu.VMEM((1, SSD_D_INNER), F32),
                        pltpu.VMEM((q + 8, 2 * SSD_GN), F32),
                        pltpu.VMEM((q, 2 * SSD_GN), BF16)],
        compiler_params=_params("arbitrary", "arbitrary"),
        name="ssd_scan",
    )(zxbc, zxbc, zxbc, dt_raw, dtb, alog, dsk, nw, expand, cw_bc, cb_bc)


def _proj_ln_kernel(y_ref, w_ref, x_ref, g_ref, b_ref, w2_ref, rb_ref, o_ref, opk_ref, route_ref, cnt_ref, carry):
    mix = _dot(y_ref[...].astype(BF16), w_ref[...])
    _ln_route_epilogue(DEEPNORM_ALPHA * x_ref[...] + mix, g_ref, b_ref, w2_ref, rb_ref,
                       o_ref, opk_ref, route_ref, cnt_ref, carry)


def _route_specs(t):
    ins = [_resident((D_MODEL, 2 * LANES)), pl.BlockSpec((1, LANES), lambda i: (0, 0))]
    outs = [pl.BlockSpec((ROW_TILE, LANES), lambda i: (i, 0)), pl.BlockSpec((1, LANES), lambda i: (0, 0))]
    shapes = [jax.ShapeDtypeStruct((t, LANES), F32), jax.ShapeDtypeStruct((1, LANES), F32)]
    return ins, outs, shapes, [pltpu.VMEM((1, LANES), F32)]


def _proj_ln(y, w, x, g, b, router_w, name):
    t, k = y.shape
    r_in, r_out, r_shape, r_scratch = _route_specs(t)
    return pl.pallas_call(
        _proj_ln_kernel,
        grid=(t // ROW_TILE,),
        in_specs=[pl.BlockSpec((ROW_TILE, k), lambda i: (i, 0)),
                  _resident((k, D_MODEL)),
                  pl.BlockSpec((ROW_TILE, D_MODEL), lambda i: (i, 0)),
                  pl.BlockSpec((1, D_MODEL), lambda i: (0, 0)),
                  pl.BlockSpec((1, D_MODEL), lambda i: (0, 0))] + r_in,
        out_specs=[pl.BlockSpec((ROW_TILE, D_MODEL), lambda i: (i, 0)),
                   pl.BlockSpec((ROW_TILE, D_PACK), lambda i: (i, 0))] + r_out,
        out_shape=[jax.ShapeDtypeStruct((t, D_MODEL), F32),
                   jax.ShapeDtypeStruct((t, D_PACK), U32)] + r_shape,
        scratch_shapes=r_scratch,
        compiler_params=_params("arbitrary"),
        name=name,
    )(y, w, x, g[None, :], b[None, :], *router_w)


def _rope_table_kernel(pos_ref, freq_ref, c_ref, s1_ref, s2_ref, tok):
    ang = pos_ref[...].astype(F32) * freq_ref[...]
    d = lax.broadcasted_iota(jnp.int32, ang.shape, 1) % ATT_HEAD_DIM
    cos, sin = jnp.cos(ang), jnp.sin(ang)
    half = ROPE_DIM // 2
    tabs = (jnp.where(d < ROPE_DIM, cos, 1.0),
            jnp.where(d < half, -sin, 0.0),
            jnp.where((d >= half) & (d < ROPE_DIM), sin, 0.0))
    rows = ang.shape[0]
    for ti, (tab, out) in enumerate(zip(tabs, (c_ref, s1_ref, s2_ref))):
        tok[ti] = tab
        for grp, (_, dil) in enumerate(ATT_PATTERNS):
            if dil == 1:
                out[grp] = tab
            else:
                n = rows // dil
                for r in range(dil):
                    out[grp, r * n:(r + 1) * n, :] = tok[ti, pl.ds(r, n, stride=dil), :]


def _residue_order(v, dil):
    tail = v.shape[1:]
    return v.reshape(-1, ROW_TILE // dil, dil, *tail).swapaxes(1, 2).reshape(-1, *tail)


def _perm_matrix(dil):
    src = _residue_order(jnp.arange(ROW_TILE, dtype=jnp.int32), dil)
    return (src[:, None] == jnp.arange(ROW_TILE, dtype=jnp.int32)[None, :]).astype(BF16)


def _rope_tables(positions):
    t = positions.size
    half = ROPE_DIM // 2
    inv_freq = ROPE_THETA ** (-jnp.arange(0, ROPE_DIM, 2, dtype=F32) / ROPE_DIM)
    d = jnp.arange(LANES) % ATT_HEAD_DIM
    freq = jnp.where(d < ROPE_DIM, inv_freq[d % half], 0.0).astype(F32)[None, :]
    tab = jax.ShapeDtypeStruct((ATT_N_GROUPS, t, LANES), F32)
    ospec = pl.BlockSpec((ATT_N_GROUPS, ROW_TILE, LANES), lambda i: (0, i, 0))
    return pl.pallas_call(
        _rope_table_kernel,
        grid=(t // ROW_TILE,),
        in_specs=[pl.BlockSpec((ROW_TILE, 1), lambda i: (i, 0)),
                  pl.BlockSpec((1, LANES), lambda i: (0, 0))],
        out_specs=[ospec, ospec, ospec],
        out_shape=[tab, tab, tab],
        scratch_shapes=[pltpu.VMEM((3, ROW_TILE, LANES), F32)],
        compiler_params=_params("parallel"),
        name="rope_tables",
    )(positions.reshape(t, 1), freq)


def _qkv_kernel(x_ref, w_ref, c_ref, s1_ref, s2_ref, o_ref, cols):
    xf = _unpack_rows(x_ref[...])
    rows = xf.shape[0]
    n_col = D_MODEL // LANES
    for c in range(n_col):
        cols[c] = xf[:, c * LANES:(c + 1) * LANES]

    def residue_major(dil):
        n = rows // dil
        strips = [jnp.concatenate([cols[c, pl.ds(r, n, stride=dil), :] for r in range(dil)], axis=0)
                  for c in range(n_col)]
        return jnp.concatenate(strips, axis=1).astype(BF16)

    xs = [xf.astype(BF16) if dil == 1 else residue_major(dil) for _, dil in ATT_PATTERNS]
    reps = ATT_OUT_DIM // LANES
    half = ROPE_DIM // 2
    for grp in range(ATT_N_GROUPS):
        c = jnp.concatenate([c_ref[grp]] * reps, axis=1)
        s1 = jnp.concatenate([s1_ref[grp]] * reps, axis=1)
        s2 = jnp.concatenate([s2_ref[grp]] * reps, axis=1)
        for kind in range(3):
            j = grp * 3 + kind
            sl = slice(j * ATT_OUT_DIM, (j + 1) * ATT_OUT_DIM)
            acc = _dot(xs[grp], w_ref[:, sl])
            if kind < 2:
                up = pltpu.roll(acc, ATT_OUT_DIM - half, 1)
                down = pltpu.roll(acc, half, 1)
                acc = acc * c + up * s1 + down * s2
            if kind == 0:
                acc = acc * (ATT_HEAD_DIM ** -0.5)
            o_ref[:, sl] = acc.astype(o_ref.dtype)


def _qkv_proj(xpk, w, tabs):
    t = xpk.shape[0]
    tab_spec = pl.BlockSpec((ATT_N_GROUPS, ROW_TILE, LANES), lambda i: (0, i, 0))
    return pl.pallas_call(
        _qkv_kernel,
        grid=(t // ROW_TILE,),
        in_specs=[pl.BlockSpec((ROW_TILE, D_PACK), lambda i: (i, 0)),
                  _resident((D_MODEL, ATT_QKV_DIM)), tab_spec, tab_spec, tab_spec],
        out_specs=pl.BlockSpec((ROW_TILE, ATT_QKV_DIM), lambda i: (i, 0)),
        out_shape=jax.ShapeDtypeStruct((t, ATT_QKV_DIM), BF16),
        scratch_shapes=[pltpu.VMEM((D_MODEL // LANES, ROW_TILE, LANES), F32)],
        compiler_params=_params("parallel"),
        name="qkv_rope",
    )(xpk, w, *tabs)


def _attn_kernel(q_ref, kp_ref, kc_ref, vp_ref, vc_ref, o_ref, st_ref, *, chained, problems):
    w = ATT_BLOCK
    i = pl.program_id(2)
    qi = lax.broadcasted_iota(jnp.int32, (w, 2 * w), 0)
    kk = lax.broadcasted_iota(jnp.int32, (w, 2 * w), 1)
    band = (kk >= qi) & (kk <= qi + w)
    lane = lax.broadcasted_iota(jnp.int32, (w, LANES), 1)
    lo_mask = lane < HALF

    def part(ref, j):
        if len(ref.shape) == 2:
            return ref[j * w:(j + 1) * w, :]
        return ref[:, j].reshape(w, ref.shape[-1])

    for j in range(problems):
        if chained and j >= 1:
            k, v, has_prev = kc_ref[(j - 1) * w:(j + 1) * w, :], vc_ref[(j - 1) * w:(j + 1) * w, :], True
        else:
            kprev = kp_ref[...] if chained else part(kp_ref, j)
            vprev = vp_ref[...] if chained else part(vp_ref, j)
            k = jnp.concatenate([kprev, part(kc_ref, j)], axis=0)
            v = jnp.concatenate([vprev, part(vc_ref, j)], axis=0)
            has_prev = False
        valid = band if has_prev else band & (kk >= jnp.where(i > 0, 0, w))
        q = part(q_ref, j)
        stats = jnp.zeros((w, LANES), F32)
        zero = jnp.zeros((), q.dtype)
        parts = []
        for p in range(ATT_HEADS // 2):
            sl = slice(p * LANES, (p + 1) * LANES)
            qp, kp, vp = q[:, sl], k[:, sl], v[:, sl]
            outs = []
            for hh in range(2):
                h = 2 * p + hh
                qm = jnp.where(lo_mask if hh == 0 else ~lo_mask, qp, zero)
                s = jnp.where(valid, _dot_nt(qm, kp), NEG_INF)
                m = jnp.max(s, -1, keepdims=True)
                pr = jnp.exp(s - m)
                l = jnp.sum(pr, -1, keepdims=True)
                outs.append(_dot(pr.astype(v.dtype), vp) / l)
                stats = jnp.where(lane == h, m, stats)
                stats = jnp.where(lane == ATT_HEADS + h, l, stats)
            parts.append(jnp.where(lo_mask, outs[0], outs[1]).astype(o_ref.dtype))
        out = jnp.concatenate(parts, axis=1)
        if len(o_ref.shape) == 2:
            o_ref[j * w:(j + 1) * w, :] = out
            st_ref[j * w:(j + 1) * w, :] = stats
        else:
            o_ref[:, j] = out.reshape(o_ref.shape[0], o_ref.shape[2], o_ref.shape[3])
            st_ref[:, j] = stats.reshape(st_ref.shape[0], st_ref.shape[2], st_ref.shape[3])


def _window_attention(qkv, grp, bsz, seq):
    _, dil = ATT_PATTERNS[grp]
    w = ATT_BLOCK
    t = bsz * seq
    chunk = ROW_TILE // dil
    tiles = w // chunk if chunk < w else 1
    span = dil * w
    nb = seq // span
    col0 = grp * 3
    chained = dil == 1
    g = ATT_PROBLEMS if chained else min(ATT_PROBLEMS, dil)

    if chained:
        grid = (bsz, 1, nb // g)
        per_b = seq // (g * w)

        def spec(width, col, prev):
            if prev:
                return pl.BlockSpec((w, width), lambda b, r, i: (b * g * per_b + jnp.maximum(g * i - 1, 0), col))
            return pl.BlockSpec((g * w, width), lambda b, r, i: (b * per_b + i, col))

        qkv_v, o_shape, st_shape = qkv, (t, ATT_OUT_DIM), (t, LANES)
    elif tiles == 1:
        grid = (bsz, dil // g, nb)
        per_b = seq // (g * w)
        stride = span // (g * w)

        def spec(width, col, prev):
            def imap(b, r, i):
                blk = jnp.maximum(i - 1, 0) if prev else i
                return (b * per_b + blk * stride + r, col)
            return pl.BlockSpec((g * w, width), imap)

        qkv_v, o_shape, st_shape = qkv, (t, ATT_OUT_DIM), (t, LANES)
    else:
        grid = (bsz, dil // g, nb)

        def spec(width, col, prev):
            def imap(b, r, i):
                blk = jnp.maximum(i - 1, 0) if prev else i
                return (b, blk, 0, r, 0, col)
            return pl.BlockSpec((None, None, tiles, g, chunk, width), imap)

        lead = (bsz, nb, tiles, dil, chunk)
        qkv_v, o_shape, st_shape = qkv.reshape(*lead, ATT_QKV_DIM), (*lead, ATT_OUT_DIM), (*lead, LANES)

    o, st = pl.pallas_call(
        functools.partial(_attn_kernel, chained=chained, problems=g),
        grid=grid,
        in_specs=[spec(ATT_OUT_DIM, col0, False), spec(ATT_OUT_DIM, col0 + 1, True),
                  spec(ATT_OUT_DIM, col0 + 1, False), spec(ATT_OUT_DIM, col0 + 2, True),
                  spec(ATT_OUT_DIM, col0 + 2, False)],
        out_specs=[spec(ATT_OUT_DIM, 0, False), spec(LANES, 0, False)],
        out_shape=[jax.ShapeDtypeStruct(o_shape, BF16), jax.ShapeDtypeStruct(st_shape, F32)],
        compiler_params=_params("parallel", "parallel", "arbitrary"),
        name=f"window_attn_d{dil}",
    )(qkv_v, qkv_v, qkv_v, qkv_v, qkv_v)
    return o.reshape(t, ATT_OUT_DIM), st.reshape(t, LANES)


def _merge_proj_ln_kernel(o1_ref, o2_ref, o3_ref, s1_ref, s2_ref, s3_ref, p4t_ref, p16t_ref, w_ref, x_ref,
                          g_ref, b_ref, w2_ref, rb_ref, o_ref, opk_ref, route_ref, cnt_ref, carry):
    rows = o1_ref.shape[0]
    lane = lax.broadcasted_iota(jnp.int32, (rows, LANES), 1)
    lo_mask = lane < HALF

    def to_token_order(pt, val):
        if val.dtype == BF16:
            return _dot(pt, val)
        return sum(_dot(pt, term) for term in _split3(val))

    p4t, p16t = p4t_ref[...], p16t_ref[...]
    outs = [o1_ref[...].astype(F32), to_token_order(p4t, o2_ref[...]), to_token_order(p16t, o3_ref[...])]
    sts = [s1_ref[...], to_token_order(p4t, s2_ref[...]), to_token_order(p16t, s3_ref[...])]
    mx = jnp.maximum(jnp.maximum(sts[0], sts[1]), sts[2])
    wgts = [pltpu.roll(s, LANES - ATT_HEADS, 1) * jnp.exp(s - mx) for s in sts]
    den = wgts[0] + wgts[1] + wgts[2]
    den = jnp.where(lane < ATT_HEADS, den, 1.0)
    coefs = [wg / den for wg in wgts]
    parts = []
    for p in range(ATT_HEADS // 2):
        sl = slice(p * LANES, (p + 1) * LANES)
        acc = jnp.zeros((rows, LANES), F32)
        for gi in range(ATT_N_GROUPS):
            acc = acc + _pair_expand(coefs[gi], 2 * p, lo_mask) * outs[gi][:, sl]
        parts.append(acc.astype(BF16))
    mix = _dot(jnp.concatenate(parts, axis=1), w_ref[...])
    _ln_route_epilogue(DEEPNORM_ALPHA * x_ref[...] + mix, g_ref, b_ref, w2_ref, rb_ref,
                       o_ref, opk_ref, route_ref, cnt_ref, carry)


def _merge_proj_ln(os_, sts, perms_t, w, x, g, b, router_w):
    t = x.shape[0]
    tm = ROW_TILE
    r_in, r_out, r_shape, r_scratch = _route_specs(t)
    ospec = pl.BlockSpec((tm, ATT_OUT_DIM), lambda i: (i, 0))
    sspec = pl.BlockSpec((tm, LANES), lambda i: (i, 0))
    xspec = pl.BlockSpec((tm, D_MODEL), lambda i: (i, 0))
    vspec = pl.BlockSpec((1, D_MODEL), lambda i: (0, 0))
    pspec = _resident((tm, tm))
    return pl.pallas_call(
        _merge_proj_ln_kernel,
        grid=(t // tm,),
        in_specs=[ospec] * 3 + [sspec] * 3 + [pspec, pspec, _resident((ATT_OUT_DIM, D_MODEL)),
                                              xspec, vspec, vspec] + r_in,
        out_specs=[xspec, pl.BlockSpec((tm, D_PACK), lambda i: (i, 0))] + r_out,
        out_shape=[jax.ShapeDtypeStruct((t, D_MODEL), F32), jax.ShapeDtypeStruct((t, D_PACK), U32)] + r_shape,
        scratch_shapes=r_scratch,
        compiler_params=_params("arbitrary"),
        name="attn_merge_proj_ln",
    )(*os_, *sts, perms_t[1], perms_t[2], w, x, g[None, :], b[None, :], *router_w)


def _route_rows(x, w2_ref, b_ref, carry):
    xhi = x.astype(BF16)
    xlo = (x - xhi.astype(F32)).astype(BF16)
    w2 = w2_ref[...]
    hi2 = _dot(xhi, w2)
    logits = hi2[:, :LANES] + hi2[:, LANES:] + _dot(xlo, w2[:, :LANES]) + b_ref[...]
    rows = logits.shape[0]
    lane = lax.broadcasted_iota(jnp.int32, (rows, LANES), 1)
    big = jnp.int32(LANES)

    def top1(vals, mask):
        v = jnp.where(mask, vals, -jnp.inf)
        m = jnp.max(v, -1, keepdims=True)
        idx = jnp.min(jnp.where(v == m, lane, big), -1, keepdims=True)
        return v, m, idx

    gmask = lane < MOE_N_GROUPS
    gv, gm, gidx = top1(logits, gmask)
    g_w = 1.0 / jnp.sum(jnp.exp(gv - gm), -1, keepdims=True)
    e_lo = MOE_N_GROUPS + gidx * MOE_EPG
    emask = (lane >= e_lo) & (lane < e_lo + MOE_EPG)
    ev, m1, i1 = top1(logits, emask)
    zsum = jnp.sum(jnp.exp(ev - m1), -1, keepdims=True)
    _, m2, i2 = top1(logits, emask & (lane != i1))
    p1 = 1.0 / zsum
    p2 = jnp.exp(m2 - m1) / zsum
    tot = p1 + p2
    e1 = i1 - MOE_N_GROUPS
    e2 = i2 - MOE_N_GROUPS

    oh1 = jnp.where(lane == e1, 1.0, 0.0)
    oh2 = jnp.where(lane == e2, 1.0, 0.0)
    oh = oh1 + oh2
    ri = lax.broadcasted_iota(jnp.int32, (rows, rows), 0)
    ci = lax.broadcasted_iota(jnp.int32, (rows, rows), 1)
    strict = jnp.where(ri > ci, 1.0, 0.0).astype(BF16)
    before = _dot(strict, oh.astype(BF16)) + carry[...]
    rank1 = jnp.sum(oh1 * before, -1, keepdims=True)
    rank2 = jnp.sum(oh2 * before, -1, keepdims=True)
    carry[...] = carry[...] + jnp.sum(oh, 0, keepdims=True)

    vals = [e1.astype(F32), e2.astype(F32), g_w * (p1 / tot), g_w * (p2 / tot), rank1, rank2]
    out = jnp.zeros((rows, LANES), F32)
    for j, val in enumerate(vals):
        out = jnp.where(lane == j, val, out)
    return out


def _router_weights(w_rg, b_rg, w_re, b_re):
    n_log = MOE_N_GROUPS + MOE_N_EXPERTS
    w = jnp.pad(jnp.concatenate([w_rg, w_re], axis=1), ((0, 0), (0, LANES - n_log)))
    whi = w.astype(BF16)
    w2 = jnp.concatenate([whi, (w - whi.astype(F32)).astype(BF16)], axis=1)
    bias = jnp.pad(jnp.concatenate([b_rg, b_re]), (0, LANES - n_log))[None, :]
    return w2, bias


def _ln_route_epilogue(r, g_ref, b_ref, w2_ref, rb_ref, o_ref, opk_ref, route_ref, cnt_ref, carry):
    @pl.when(pl.program_id(0) == 0)
    def _():
        carry[...] = jnp.zeros(carry.shape, F32)

    out = _layer_norm(r, g_ref[...], b_ref[...])
    o_ref[...] = out
    opk_ref[...] = _pack_rows(out)
    route_ref[...] = _route_rows(out, w2_ref, rb_ref, carry)
    cnt_ref[...] = carry[...]


def _expert_kernel(meta_ref, x_ref, wg_ref, wu_ref, wd_ref, o_ref, wg16, wu16, wd16):
    i = pl.program_id(0)
    n_blocks = pl.num_programs(0)

    @pl.when(i < meta_ref[n_blocks])
    def _():
        @pl.when((i == 0) | (meta_ref[i] != meta_ref[jnp.maximum(i - 1, 0)]))
        def _():
            wg16[...] = wg_ref[...].astype(BF16)
            wu16[...] = wu_ref[...].astype(BF16)
            wd16[...] = wd_ref[...].astype(BF16)

        rows = lax.broadcasted_iota(jnp.int32, x_ref.shape, 0)
        xw = jnp.where(rows < meta_ref[n_blocks + 1 + i], x_ref[...], jnp.uint32(0))
        x = _unpack_rows(xw).astype(BF16)
        h = _silu(_dot(x, wg16[...])) * _dot(x, wu16[...])
        o_ref[...] = _pack_rows(_dot(h.astype(BF16), wd16[...]))


def _expert_mlp(meta, x_rows, w_gate, w_up, w_down, layer):
    n_rows = x_rows.shape[0]
    n_blocks = n_rows // MOE_ROW_BLOCK
    tb = MOE_ROW_BLOCK
    grid_spec = pltpu.PrefetchScalarGridSpec(
        num_scalar_prefetch=1,
        grid=(n_blocks,),
        in_specs=[pl.BlockSpec((tb, D_PACK), lambda i, meta: (i, 0)),
                  pl.BlockSpec((None, None, D_MODEL, MOE_HIDDEN), lambda i, meta: (layer, meta[i], 0, 0)),
                  pl.BlockSpec((None, None, D_MODEL, MOE_HIDDEN), lambda i, meta: (layer, meta[i], 0, 0)),
                  pl.BlockSpec((None, None, MOE_HIDDEN, D_MODEL), lambda i, meta: (layer, meta[i], 0, 0))],
        out_specs=pl.BlockSpec((tb, D_PACK), lambda i, meta: (i, 0)),
        scratch_shapes=[pltpu.VMEM((D_MODEL, MOE_HIDDEN), BF16), pltpu.VMEM((D_MODEL, MOE_HIDDEN), BF16),
                        pltpu.VMEM((MOE_HIDDEN, D_MODEL), BF16)],
    )
    return pl.pallas_call(
        _expert_kernel,
        grid_spec=grid_spec,
        out_shape=jax.ShapeDtypeStruct((n_rows, D_PACK), U32),
        compiler_params=_params("arbitrary"),
        name="moe_experts",
    )(meta, x_rows, w_gate, w_up, w_down)


def _combine_ln_kernel(y0_ref, y1_ref, r_ref, x_ref, g_ref, b_ref, o_ref, opk_ref):
    route = r_ref[...]
    g0 = route[:, 2:3]
    g1 = route[:, 3:4]
    ffn = g0 * _unpack_rows(y0_ref[...]) + g1 * _unpack_rows(y1_ref[...])
    out = _layer_norm(DEEPNORM_ALPHA * x_ref[...] + ffn, g_ref[...], b_ref[...])
    o_ref[...] = out
    opk_ref[...] = _pack_rows(out)


def _combine_ln(y0, y1, route, x, g, b):
    t = x.shape[0]
    xspec = pl.BlockSpec((ROW_TILE, D_MODEL), lambda i: (i, 0))
    pspec = pl.BlockSpec((ROW_TILE, D_PACK), lambda i: (i, 0))
    vspec = pl.BlockSpec((1, D_MODEL), lambda i: (0, 0))
    return pl.pallas_call(
        _combine_ln_kernel,
        grid=(t // ROW_TILE,),
        in_specs=[pspec, pspec, pl.BlockSpec((ROW_TILE, LANES), lambda i: (i, 0)), xspec, vspec, vspec],
        out_specs=[xspec, pspec],
        out_shape=[jax.ShapeDtypeStruct((t, D_MODEL), F32), jax.ShapeDtypeStruct((t, D_PACK), U32)],
        compiler_params=_params("parallel"),
        name="moe_combine_ln",
    )(y0, y1, route, x, g[None, :], b[None, :])


def _positions_kernel(r_ref, ps_ref, p0_ref, p1_ref):
    route = r_ref[...]
    lane = lax.broadcasted_iota(jnp.int32, route.shape, 1)
    starts = ps_ref[...]
    out = jnp.zeros(route.shape, F32)
    for k in range(MOE_TOP_K):
        eid = route[:, k:k + 1].astype(jnp.int32)
        pos = jnp.sum(jnp.where(lane == eid, starts, 0.0), -1, keepdims=True) + route[:, 4 + k:5 + k]
        out = jnp.where(lane == k, pos, out)
    for c in range(route.shape[0] // LANES):
        tile = out[c * LANES:(c + 1) * LANES, :].T.astype(jnp.int32)
        p0_ref[:, c * LANES:(c + 1) * LANES] = tile[0:1, :]
        p1_ref[:, c * LANES:(c + 1) * LANES] = tile[1:2, :]


def _positions(route, pad_start):
    t = route.shape[0]
    rows = 4 * ROW_TILE
    starts = jnp.pad(pad_start.astype(F32), (0, LANES - MOE_N_EXPERTS))[None, :]
    ospec = pl.BlockSpec((1, rows), lambda i: (0, i))
    return pl.pallas_call(
        _positions_kernel,
        grid=(t // rows,),
        in_specs=[pl.BlockSpec((rows, LANES), lambda i: (i, 0)), pl.BlockSpec((1, LANES), lambda i: (0, 0))],
        out_specs=[ospec, ospec],
        out_shape=[jax.ShapeDtypeStruct((1, t), jnp.int32)] * 2,
        compiler_params=_params("parallel"),
        name="moe_positions",
    )(route, starts)


def _moe(x, xpk, route, cnt, w_gate, w_up, w_down, layer, g, b):
    t = x.shape[0]
    tb = MOE_ROW_BLOCK
    n_assign = t * MOE_TOP_K
    n_blocks = n_assign // tb + MOE_N_EXPERTS
    n_rows = n_blocks * tb
    counts = cnt[0, :MOE_N_EXPERTS].astype(jnp.int32)
    padded = (counts + tb - 1) // tb * tb
    pad_end = jnp.cumsum(padded)
    pos0, pos1 = _positions(route, pad_end - padded)
    block_start = jnp.arange(n_blocks, dtype=jnp.int32) * tb
    block_e = jnp.minimum(jnp.sum((pad_end[None, :] <= block_start[:, None]).astype(jnp.int32), -1),
                          MOE_N_EXPERTS - 1)
    valid = jnp.clip(pad_end[block_e] - padded[block_e] + counts[block_e] - block_start, 0, tb)
    meta = jnp.concatenate([block_e, pad_end[-1:] // tb, valid]).astype(jnp.int32)
    x_rows = _row_scatter(xpk, (pos0, pos1), n_rows)
    y_rows = _expert_mlp(meta, x_rows, w_gate, w_up, w_down, layer)
    y0 = _row_gather(y_rows, pos0)
    y1 = _row_gather(y_rows, pos1)
    return _combine_ln(y0, y1, route, x, g, b)


def _ssd_layer(x, xpk, w_in, conv_w, conv_b, dt_bias, a_log, d_skip, norm_w, w_out, g, b, router_w, bsz, seq):
    w_zxbc = w_in[:, :SSD_D_INNER + SSD_CONV_DIM].astype(BF16)
    w_dt = jnp.pad(w_in[:, SSD_D_INNER + SSD_CONV_DIM:], ((0, 0), (0, LANES - SSD_N_HEADS))).astype(BF16)
    zxbc, dt_raw = _in_proj(xpk, w_zxbc, w_dt, conv_w, conv_b, bsz, seq)
    y = _ssd_scan(zxbc, dt_raw, conv_w, conv_b, dt_bias, a_log, d_skip, norm_w, bsz, seq)
    return _proj_ln(y, w_out.astype(BF16), x, g, b, router_w, "ssd_out_proj_ln")


def _attn_layer(x, xpk, tabs, w_qkv, w_o, g, b, router_w, bsz, seq):
    perms_t = [_perm_matrix(dil).T for _, dil in ATT_PATTERNS]
    qkv = _qkv_proj(xpk, w_qkv.astype(BF16), tabs)
    os_, sts = [], []
    for grp in range(ATT_N_GROUPS):
        o, st = _window_attention(qkv, grp, bsz, seq)
        os_.append(o)
        sts.append(st)
    return _merge_proj_ln(os_, sts, perms_t, w_o.astype(BF16), x, g, b, router_w)


def kernel(x, positions, ssd_w_in, ssd_conv_w, ssd_conv_b, ssd_dt_bias, ssd_a_log, ssd_d, ssd_norm_w, ssd_w_out,
           attn_w_qkv, attn_w_o, ln_g, ln_b, moe_w_router_group, moe_b_router_group, moe_w_router_expert,
           moe_b_router_expert, moe_w_gate, moe_w_up, moe_w_down):
    bsz, seq, d = x.shape
    t = bsz * seq
    h = x.reshape(t, d)
    hpk = h
    tabs = _rope_tables(positions)
    for i in range(DEPTH):
        j = i // N_MIXERS
        router_w = _router_weights(moe_w_router_group[i], moe_b_router_group[i], moe_w_router_expert[i],
                                   moe_b_router_expert[i])
        if i % N_MIXERS == 0:
            h, hpk, route, cnt = _ssd_layer(h, hpk, ssd_w_in[j], ssd_conv_w[j], ssd_conv_b[j], ssd_dt_bias[j],
                                            ssd_a_log[j], ssd_d[j], ssd_norm_w[j], ssd_w_out[j], ln_g[i, 0],
                                            ln_b[i, 0], router_w, bsz, seq)
        else:
            h, hpk, route, cnt = _attn_layer(h, hpk, tabs, attn_w_qkv[j], attn_w_o[j], ln_g[i, 0], ln_b[i, 0],
                                             router_w, bsz, seq)
        h, hpk = _moe(h, hpk, route, cnt, moe_w_gate, moe_w_up, moe_w_down, i, ln_g[i, 1], ln_b[i, 1])
    return h.reshape(bsz, seq, d)
```

```python
import functools

import jax
import jax.numpy as jnp
from jax import lax
from jax.experimental import pallas as pl
from jax.experimental.pallas import tpu as pltpu
from jax.experimental.pallas import tpu_sc as plsc

F32 = jnp.float32
BF16 = jnp.bfloat16
U32 = jnp.uint32

D_MODEL = 1024
D_PACK = D_MODEL // 2
DEPTH = 4
N_MIXERS = 2

SSD_D_INNER = 2048
SSD_HEAD_DIM = 64
SSD_N_HEADS = 32
SSD_N_GROUPS = 4
SSD_D_STATE = 128
SSD_D_CONV = 4
SSD_CHUNK = 128
SSD_GN = SSD_N_GROUPS * SSD_D_STATE
SSD_CONV_DIM = SSD_D_INNER + 2 * SSD_GN
SSD_GROUP_COLS = SSD_D_INNER // SSD_N_GROUPS
IN_PROJ_COLS = 1024
SSD_PAIR_UNROLL = 4

ATT_HEAD_DIM = 64
ATT_HEADS = 8
ATT_PATTERNS = ((128, 1), (512, 4), (2048, 16))
ATT_N_GROUPS = 3
ATT_OUT_DIM = ATT_HEADS * ATT_HEAD_DIM
ATT_QKV_DIM = ATT_N_GROUPS * 3 * ATT_OUT_DIM
ATT_BLOCK = 128
ATT_PROBLEMS = 8
ROPE_THETA = 500000.0
ROPE_DIM = 16

MOE_N_GROUPS = 4
MOE_EPG = 8
MOE_N_EXPERTS = 32
MOE_TOP_K = 2
MOE_HIDDEN = 512
MOE_ROW_BLOCK = 512

DEEPNORM_ALPHA = (2 * DEPTH) ** 0.25
LN_EPS = 1e-5
RMS_EPS = 1e-5
NEG_INF = -1e30

LANES = 128
HALF = LANES // 2
VMEM_LIMIT = 56 * 1024 * 1024

ROW_TILE = 512


def _params(*sem):
    return pltpu.CompilerParams(dimension_semantics=sem, vmem_limit_bytes=VMEM_LIMIT)


def _silu(v):
    h = 0.5 * v
    return h + h * jnp.tanh(h)


def _layer_norm(r, g, b):
    mu = jnp.mean(r, -1, keepdims=True)
    d = r - mu
    var = jnp.mean(d * d, -1, keepdims=True)
    return d * lax.rsqrt(var + LN_EPS) * g + b


def _split3(v):
    hi = v.astype(BF16)
    r1 = v - hi.astype(F32)
    mid = r1.astype(BF16)
    lo = (r1 - mid.astype(F32)).astype(BF16)
    return hi, mid, lo


def _dot(a, b):
    return jnp.dot(a, b, preferred_element_type=F32)


def _dot_nt(a, b):
    return lax.dot_general(a, b, (((1,), (1,)), ((), ())), preferred_element_type=F32)


def _dot_tn(a, b):
    return lax.dot_general(a, b, (((0,), (0,)), ((), ())), preferred_element_type=F32)


def _pack_rows(v):
    r = pltpu.bitcast(v.astype(BF16).astype(F32), U32)
    return r[:, :D_PACK] | (r[:, D_PACK:] >> 16)


def _unpack_rows(p):
    hi = pltpu.bitcast(p & jnp.uint32(0xFFFF0000), F32)
    lo = pltpu.bitcast(p << 16, F32)
    return jnp.concatenate([hi, lo], axis=1)


def _row_gather(data, idx):
    n = idx.shape[1]
    d = data.shape[1]
    window = LANES
    dc = d // 2
    mesh = plsc.VectorSubcoreMesh(core_axis_name="core", subcore_axis_name="subcore")

    @functools.partial(pl.kernel, out_type=jax.ShapeDtypeStruct((n, d), data.dtype), mesh=mesh)
    def gather(x_hbm, i_hbm, o_hbm):
        for c in range(d // dc):
            def body(i_vmem, o_vmem, c=c):
                pltpu.sync_copy(x_hbm.at[i_vmem.at[0], pl.ds(c * dc, dc)], o_vmem)

            pltpu.emit_pipeline(
                body,
                grid=(n // window,),
                in_specs=[pl.BlockSpec((1, window), lambda i: (0, i))],
                out_specs=[pl.BlockSpec((window, dc), lambda i, c=c: (i, c))],
                core_axis_name=("core", "subcore"),
                dimension_semantics=(pltpu.PARALLEL,),
            )(i_hbm, o_hbm)

    return gather(data, idx)


def _row_scatter(data, idxs, n_rows):
    t, d = data.shape
    window = LANES
    dc = d // 2
    mesh = plsc.VectorSubcoreMesh(core_axis_name="core", subcore_axis_name="subcore")

    @functools.partial(pl.kernel, out_type=jax.ShapeDtypeStruct((n_rows, d), data.dtype), mesh=mesh)
    def scatter(x_hbm, *refs):
        o_hbm = refs[-1]
        for i_hbm in refs[:-1]:
            for c in range(d // dc):
                def body(x_vmem, i_vmem, c=c):
                    pltpu.sync_copy(x_vmem, o_hbm.at[i_vmem.at[0], pl.ds(c * dc, dc)])

                pltpu.emit_pipeline(
                    body,
                    grid=(t // window,),
                    in_specs=[pl.BlockSpec((window, dc), lambda i, c=c: (i, c)),
                              pl.BlockSpec((1, window), lambda i: (0, i))],
                    out_specs=[],
                    core_axis_name=("core", "subcore"),
                    dimension_semantics=(pltpu.PARALLEL,),
                )(x_hbm, i_hbm)

    return scatter(data, *idxs)


def _resident(shape):
    return pl.BlockSpec(shape, lambda *_: (0,) * len(shape), pipeline_mode=pl.Buffered(1))


def _in_proj_kernel(x_ref, w_ref, wdt_ref, cw_ref, cb_ref, o_ref, dt_ref, ext, tail):
    rows = x_ref.shape[0]
    tn = IN_PROJ_COLS
    n_z = SSD_D_INNER // tn

    @pl.when(pl.program_id(1) == 0)
    def _():
        tail[...] = jnp.zeros(tail.shape, F32)

    x = x_ref[...]
    x = (_unpack_rows(x) if x.dtype == U32 else x).astype(BF16)
    for n in range(n_z):
        sl = slice(n * tn, (n + 1) * tn)
        o_ref[:, sl] = _dot(x, w_ref[:, sl]).astype(o_ref.dtype)
    for n in range(n_z + SSD_D_INNER // tn, w_ref.shape[1] // tn):
        sl = slice(n * tn, (n + 1) * tn)
        o_ref[:, sl] = _dot(x, w_ref[:, sl]).astype(o_ref.dtype)
    for j in range(SSD_D_INNER // tn):
        sl = slice((n_z + j) * tn, (n_z + j + 1) * tn)
        cl = slice(j * tn, (j + 1) * tn)
        ext[j, 0:8, :] = tail[j]
        ext[j, 8:8 + rows, :] = _dot(x, w_ref[:, sl])
        tail[j] = ext[j, rows:rows + 8, :]
        acc = ext[j, 8:8 + rows, :] * cw_ref[3:4, cl] + cb_ref[:, cl]
        for k in range(SSD_D_CONV - 1):
            acc = acc + ext[j, 5 + k:5 + k + rows, :] * cw_ref[k:k + 1, cl]
        o_ref[:, sl] = _silu(acc).astype(o_ref.dtype)
    dt_ref[...] = _dot(x, wdt_ref[...])


def _in_proj(xpk, w_zxbc, w_dt, conv_w, conv_b, bsz, seq):
    t = xpk.shape[0]
    n = w_zxbc.shape[1]
    tiles = seq // ROW_TILE
    cw = jnp.pad(conv_w[:, :SSD_D_INNER], ((0, 8 - SSD_D_CONV), (0, 0)))
    conv_b = conv_b[:SSD_D_INNER]
    n_conv = SSD_D_INNER // IN_PROJ_COLS
    return pl.pallas_call(
        _in_proj_kernel,
        grid=(bsz, tiles),
        in_specs=[pl.BlockSpec((ROW_TILE, xpk.shape[1]), lambda b, s: (b * tiles + s, 0)),
                  _resident((D_MODEL, n)), _resident((D_MODEL, LANES)),
                  _resident((8, SSD_D_INNER)), _resident((1, SSD_D_INNER))],
        out_specs=[pl.BlockSpec((ROW_TILE, n), lambda b, s: (b * tiles + s, 0)),
                   pl.BlockSpec((ROW_TILE, LANES), lambda b, s: (b * tiles + s, 0))],
        out_shape=[jax.ShapeDtypeStruct((t, n), BF16), jax.ShapeDtypeStruct((t, LANES), F32)],
        scratch_shapes=[pltpu.VMEM((n_conv, ROW_TILE + 8, IN_PROJ_COLS), F32),
                        pltpu.VMEM((n_conv, 8, IN_PROJ_COLS), F32)],
        compiler_params=_params("arbitrary", "arbitrary"),
        name="ssd_in_proj",
    )(xpk, w_zxbc, w_dt, cw, conv_b[None, :])


def _ssd_kernel(z_ref, xs_ref, bc_ref, dt_ref, dtb_ref, alog_ref, dsk_ref, nw_ref, ex_ref, cw_ref, cbias_ref, o_ref,
                state, cs_cols, cs_rows, dt_x, cb_all, y_off, y_grp, xd_all, cd_all, bc_ext, bc_s):
    q = SSD_CHUNK
    pairs = SSD_GROUP_COLS // LANES

    @pl.when(pl.program_id(1) == 0)
    def _():
        state[...] = jnp.zeros(state.shape, F32)
        bc_ext[0:8, :] = jnp.zeros((8, 2 * SSD_GN), F32)

    bc_ext[8:8 + q, :] = bc_ref[...].astype(F32)
    acc = bc_ext[8:8 + q, :] * cw_ref[3:4, :] + cbias_ref[...]
    for k in range(SSD_D_CONV - 1):
        acc = acc + bc_ext[5 + k:5 + k + q, :] * cw_ref[k:k + 1, :]
    bc_s[...] = _silu(acc).astype(BF16)
    bc_ext[0:8, :] = bc_ext[q:q + 8, :]

    pre = dt_ref[...] + dtb_ref[...]
    dt = jnp.maximum(pre, 0.0) + jnp.log(1.0 + jnp.exp(-jnp.abs(pre)))
    a = -jnp.exp(alog_ref[...])
    row = lax.broadcasted_iota(jnp.int32, (q, q), 0)
    col = lax.broadcasted_iota(jnp.int32, (q, q), 1)
    causal = row >= col
    lo_mask = col < HALF
    tri = jnp.where(causal, 1.0, 0.0).astype(BF16)
    cs3 = _dot(tri, jnp.concatenate(_split3(dt * a), axis=1))
    cs = cs3[:, :LANES] + cs3[:, LANES:2 * LANES] + cs3[:, 2 * LANES:]
    cs_rows[...] = cs.T
    for h in range(SSD_N_HEADS):
        cs_cols[h] = jnp.broadcast_to(cs[:, h:h + 1], (q, q))
    dt3 = _dot(jnp.concatenate(_split3(dt), axis=0), ex_ref[...])
    dt_x[...] = dt3[:q] + dt3[q:2 * q] + dt3[2 * q:]
    for g in range(SSD_N_GROUPS):
        bg = bc_s[:, g * SSD_D_STATE:(g + 1) * SSD_D_STATE]
        cg = bc_s[:, SSD_GN + g * SSD_D_STATE:SSD_GN + (g + 1) * SSD_D_STATE]
        cb_all[g] = _dot_nt(cg, bg)
        y_off[:, g * SSD_GROUP_COLS:(g + 1) * SSD_GROUP_COLS] = _dot(cg, state[g].astype(BF16))

    def group_body(g, carry):
        cb = cb_all[g]

        def pair_body(pp, ssq):
            p = g * pairs + pp
            h0 = 2 * p
            x0 = pl.multiple_of(p * LANES, LANES)
            l0 = pl.multiple_of(pp * LANES, LANES)
            cols = (cs_cols[h0], cs_cols[h0 + 1])
            csx = jnp.where(lo_mask, cols[0], cols[1])
            xp = xs_ref[:, pl.ds(x0, LANES)].astype(F32)
            xdt = xp * dt_x[:, pl.ds(x0, LANES)]
            xdt16 = xdt.astype(BF16)
            last = csx[q - 1:q, :]
            halves = []
            for hh in range(2):
                diff = cols[hh] - cs_rows[pl.ds(h0 + hh, 1), :]
                decay = jnp.exp(jnp.where(causal, diff, -jnp.inf))
                halves.append(_dot((cb * decay).astype(BF16), xdt16))
            y = jnp.where(lo_mask, halves[0], halves[1])
            y = y + y_off[:, pl.ds(x0, LANES)] * jnp.exp(csx) + xp * dsk_ref[:, pl.ds(x0, LANES)]
            y = y * _silu(z_ref[:, pl.ds(x0, LANES)].astype(F32))
            y_grp[:, pl.ds(l0, LANES)] = y
            xd_all[:, pl.ds(x0, LANES)] = (xdt * jnp.exp(last - csx)).astype(BF16)
            cd_all[:, pl.ds(x0, LANES)] = jnp.exp(last)
            return ssq + jnp.sum(y * y, -1, keepdims=True)

        ssq = lax.fori_loop(0, pairs, pair_body, jnp.zeros((q, 1), F32), unroll=SSD_PAIR_UNROLL)
        inv = lax.rsqrt(ssq * (1.0 / SSD_GROUP_COLS) + RMS_EPS)
        g0 = pl.multiple_of(g * SSD_GROUP_COLS, SSD_GROUP_COLS)
        o_ref[:, pl.ds(g0, SSD_GROUP_COLS)] = (
            y_grp[...] * inv * nw_ref[:, pl.ds(g0, SSD_GROUP_COLS)]).astype(o_ref.dtype)
        return carry

    lax.fori_loop(0, SSD_N_GROUPS, group_body, 0)

    for g in range(SSD_N_GROUPS):
        gs = slice(g * SSD_GROUP_COLS, (g + 1) * SSD_GROUP_COLS)
        bg = bc_s[:, g * SSD_D_STATE:(g + 1) * SSD_D_STATE]
        state[g] = state[g] * cd_all[:, gs] + _dot_tn(bg, xd_all[:, gs])


def _ssd_scan(zxbc, dt_raw, conv_w, conv_b, dt_bias, a_log, d_skip, norm_w, bsz, seq):
    t = bsz * seq
    nc = seq // SSD_CHUNK
    q = SSD_CHUNK
    pad_h = LANES - SSD_N_HEADS
    dtb = jnp.pad(dt_bias, (0, pad_h))[None, :]
    alog = jnp.pad(a_log, (0, pad_h))[None, :]
    dsk = jnp.repeat(d_skip, SSD_HEAD_DIM)[None, :]
    nw = norm_w[None, :]
    expand = (jnp.arange(LANES)[:, None] == jnp.arange(SSD_D_INNER)[None, :] // SSD_HEAD_DIM).astype(BF16)
    cw_bc = jnp.pad(conv_w[:, SSD_D_INNER:], ((0, 8 - SSD_D_CONV), (0, 0)))
    cb_bc = conv_b[None, SSD_D_INNER:]

    def const(shape):
        return pl.BlockSpec(shape, lambda b, c: (0, 0))

    return pl.pallas_call(
        _ssd_kernel,
        grid=(bsz, nc),
        in_specs=[pl.BlockSpec((q, SSD_D_INNER), lambda b, c: (b * nc + c, 0)),
                  pl.BlockSpec((q, SSD_D_INNER), lambda b, c: (b * nc + c, 1)),
                  pl.BlockSpec((q, 2 * SSD_GN), lambda b, c: (b * nc + c, 4)),
                  pl.BlockSpec((q, LANES), lambda b, c: (b * nc + c, 0)),
                  const((1, LANES)), const((1, LANES)),
                  const((1, SSD_D_INNER)), const((1, SSD_D_INNER)), const((LANES, SSD_D_INNER)),
                  const((8, 2 * SSD_GN)), const((1, 2 * SSD_GN))],
        out_specs=pl.BlockSpec((q, SSD_D_INNER), lambda b, c: (b * nc + c, 0)),
        out_shape=jax.ShapeDtypeStruct((t, SSD_D_INNER), BF16),
        scratch_shapes=[pltpu.VMEM((SSD_N_GROUPS, SSD_D_STATE, SSD_GROUP_COLS), F32),
                        pltpu.VMEM((SSD_N_HEADS, q, q), F32),
                        pltpu.VMEM((LANES, q), F32),
                        pltpu.VMEM((q, SSD_D_INNER), F32),
                        pltpu.VMEM((SSD_N_GROUPS, q, q), F32),
                        pltpu.VMEM((q, SSD_D_INNER), F32),
                        pltpu.VMEM((q, SSD_GROUP_COLS), F32),
                        pltpu.VMEM((q, SSD_D_INNER), BF16),
                        pltpu.VMEM((1, SSD_D_INNER), F32),
                        pltpu.VMEM((q + 8, 2 * SSD_GN), F32),
                        pltpu.VMEM((q, 2 * SSD_GN), BF16)],
        compiler_params=_params("arbitrary", "arbitrary"),
        name="ssd_scan",
    )(zxbc, zxbc, zxbc, dt_raw, dtb, alog, dsk, nw, expand, cw_bc, cb_bc)


def _proj_ln_kernel(y_ref, w_ref, x_ref, g_ref, b_ref, w2_ref, rb_ref, o_ref, opk_ref, route_ref, cnt_ref, carry):
    mix = _dot(y_ref[...].astype(BF16), w_ref[...])
    _ln_route_epilogue(DEEPNORM_ALPHA * x_ref[...] + mix, g_ref, b_ref, w2_ref, rb_ref,
                       o_ref, opk_ref, route_ref, cnt_ref, carry)


def _route_specs(t):
    ins = [_resident((D_MODEL, 2 * LANES)), pl.BlockSpec((1, LANES), lambda i: (0, 0))]
    outs = [pl.BlockSpec((ROW_TILE, LANES), lambda i: (i, 0)), pl.BlockSpec((1, LANES), lambda i: (0, 0))]
    shapes = [jax.ShapeDtypeStruct((t, LANES), F32), jax.ShapeDtypeStruct((1, LANES), F32)]
    return ins, outs, shapes, [pltpu.VMEM((1, LANES), F32)]


def _proj_ln(y, w, x, g, b, router_w, name):
    t, k = y.shape
    r_in, r_out, r_shape, r_scratch = _route_specs(t)
    return pl.pallas_call(
        _proj_ln_kernel,
        grid=(t // ROW_TILE,),
        in_specs=[pl.BlockSpec((ROW_TILE, k), lambda i: (i, 0)),
                  _resident((k, D_MODEL)),
                  pl.BlockSpec((ROW_TILE, D_MODEL), lambda i: (i, 0)),
                  pl.BlockSpec((1, D_MODEL), lambda i: (0, 0)),
                  pl.BlockSpec((1, D_MODEL), lambda i: (0, 0))] + r_in,
        out_specs=[pl.BlockSpec((ROW_TILE, D_MODEL), lambda i: (i, 0)),
                   pl.BlockSpec((ROW_TILE, D_PACK), lambda i: (i, 0))] + r_out,
        out_shape=[jax.ShapeDtypeStruct((t, D_MODEL), F32),
                   jax.ShapeDtypeStruct((t, D_PACK), U32)] + r_shape,
        scratch_shapes=r_scratch,
        compiler_params=_params("arbitrary"),
        name=name,
    )(y, w, x, g[None, :], b[None, :], *router_w)


def _rope_table_kernel(pos_ref, freq_ref, c_ref, s1_ref, s2_ref, tok):
    ang = pos_ref[...].astype(F32) * freq_ref[...]
    d = lax.broadcasted_iota(jnp.int32, ang.shape, 1) % ATT_HEAD_DIM
    cos, sin = jnp.cos(ang), jnp.sin(ang)
    half = ROPE_DIM // 2
    tabs = (jnp.where(d < ROPE_DIM, cos, 1.0),
            jnp.where(d < half, -sin, 0.0),
            jnp.where((d >= half) & (d < ROPE_DIM), sin, 0.0))
    rows = ang.shape[0]
    for ti, (tab, out) in enumerate(zip(tabs, (c_ref, s1_ref, s2_ref))):
        tok[ti] = tab
        for grp, (_, dil) in enumerate(ATT_PATTERNS):
            if dil == 1:
                out[grp] = tab
            else:
                n = rows // dil
                for r in range(dil):
                    out[grp, r * n:(r + 1) * n, :] = tok[ti, pl.ds(r, n, stride=dil), :]


def _rope_tables(positions):
    t = positions.size
    half = ROPE_DIM // 2
    inv_freq = ROPE_THETA ** (-jnp.arange(0, ROPE_DIM, 2, dtype=F32) / ROPE_DIM)
    d = jnp.arange(LANES) % ATT_HEAD_DIM
    freq = jnp.where(d < ROPE_DIM, inv_freq[d % half], 0.0).astype(F32)[None, :]
    tab = jax.ShapeDtypeStruct((ATT_N_GROUPS, t, LANES), F32)
    ospec = pl.BlockSpec((ATT_N_GROUPS, ROW_TILE, LANES), lambda i: (0, i, 0))
    return pl.pallas_call(
        _rope_table_kernel,
        grid=(t // ROW_TILE,),
        in_specs=[pl.BlockSpec((ROW_TILE, 1), lambda i: (i, 0)),
                  pl.BlockSpec((1, LANES), lambda i: (0, 0))],
        out_specs=[ospec, ospec, ospec],
        out_shape=[tab, tab, tab],
        scratch_shapes=[pltpu.VMEM((3, ROW_TILE, LANES), F32)],
        compiler_params=_params("parallel"),
        name="rope_tables",
    )(positions.reshape(t, 1), freq)


def _qkv_kernel(x_ref, w_ref, c_ref, s1_ref, s2_ref, o_ref, cols):
    xf = _unpack_rows(x_ref[...])
    rows = xf.shape[0]
    n_col = D_MODEL // LANES
    for c in range(n_col):
        cols[c] = xf[:, c * LANES:(c + 1) * LANES]

    def residue_major(dil):
        n = rows // dil
        strips = [jnp.concatenate([cols[c, pl.ds(r, n, stride=dil), :] for r in range(dil)], axis=0)
                  for c in range(n_col)]
        return jnp.concatenate(strips, axis=1).astype(BF16)

    xs = [xf.astype(BF16) if dil == 1 else residue_major(dil) for _, dil in ATT_PATTERNS]
    reps = ATT_OUT_DIM // LANES
    half = ROPE_DIM // 2
    for grp in range(ATT_N_GROUPS):
        c = jnp.concatenate([c_ref[grp]] * reps, axis=1)
        s1 = jnp.concatenate([s1_ref[grp]] * reps, axis=1)
        s2 = jnp.concatenate([s2_ref[grp]] * reps, axis=1)
        for kind in range(3):
            j = grp * 3 + kind
            sl = slice(j * ATT_OUT_DIM, (j + 1) * ATT_OUT_DIM)
            acc = _dot(xs[grp], w_ref[:, sl])
            if kind < 2:
                up = pltpu.roll(acc, ATT_OUT_DIM - half, 1)
                down = pltpu.roll(acc, half, 1)
                acc = acc * c + up * s1 + down * s2
            if kind == 0:
                acc = acc * (ATT_HEAD_DIM ** -0.5)
            o_ref[:, sl] = acc.astype(o_ref.dtype)


def _qkv_proj(xpk, w, tabs):
    t = xpk.shape[0]
    tab_spec = pl.BlockSpec((ATT_N_GROUPS, ROW_TILE, LANES), lambda i: (0, i, 0))
    return pl.pallas_call(
        _qkv_kernel,
        grid=(t // ROW_TILE,),
        in_specs=[pl.BlockSpec((ROW_TILE, D_PACK), lambda i: (i, 0)),
                  _resident((D_MODEL, ATT_QKV_DIM)), tab_spec, tab_spec, tab_spec],
        out_specs=pl.BlockSpec((ROW_TILE, ATT_QKV_DIM), lambda i: (i, 0)),
        out_shape=jax.ShapeDtypeStruct((t, ATT_QKV_DIM), BF16),
        scratch_shapes=[pltpu.VMEM((D_MODEL // LANES, ROW_TILE, LANES), F32)],
        compiler_params=_params("parallel"),
        name="qkv_rope",
    )(xpk, w, *tabs)


def _attn_kernel(q_ref, kp_ref, kc_ref, vp_ref, vc_ref, o_ref, st_ref, *, chained, problems):
    w = ATT_BLOCK
    i = pl.program_id(2)
    qi = lax.broadcasted_iota(jnp.int32, (w, 2 * w), 0)
    kk = lax.broadcasted_iota(jnp.int32, (w, 2 * w), 1)
    band = (kk >= qi) & (kk <= qi + w)
    lane = lax.broadcasted_iota(jnp.int32, (w, LANES), 1)
    lo_mask = lane < HALF

    def part(ref, j):
        if len(ref.shape) == 2:
            return ref[j * w:(j + 1) * w, :]
        return ref[:, j].reshape(w, ref.shape[-1])

    for j in range(problems):
        if chained and j >= 1:
            k, v, has_prev = kc_ref[(j - 1) * w:(j + 1) * w, :], vc_ref[(j - 1) * w:(j + 1) * w, :], True
        else:
            kprev = kp_ref[...] if chained else part(kp_ref, j)
            vprev = vp_ref[...] if chained else part(vp_ref, j)
            k = jnp.concatenate([kprev, part(kc_ref, j)], axis=0)
            v = jnp.concatenate([vprev, part(vc_ref, j)], axis=0)
            has_prev = False
        valid = band if has_prev else band & (kk >= jnp.where(i > 0, 0, w))
        q = part(q_ref, j)
        stats = jnp.zeros((w, LANES), F32)
        zero = jnp.zeros((), q.dtype)
        parts = []
        for p in range(ATT_HEADS // 2):
            sl = slice(p * LANES, (p + 1) * LANES)
            qp, kp, vp = q[:, sl], k[:, sl], v[:, sl]
            outs = []
            for hh in range(2):
                h = 2 * p + hh
                qm = jnp.where(lo_mask if hh == 0 else ~lo_mask, qp, zero)
                s = jnp.where(valid, _dot_nt(qm, kp), NEG_INF)
                m = jnp.max(s, -1, keepdims=True)
                pr = jnp.exp(s - m)
                l = jnp.sum(pr, -1, keepdims=True)
                outs.append(_dot(pr.astype(v.dtype), vp) / l)
                stats = jnp.where(lane == h, m, stats)
                stats = jnp.where(lane == ATT_HEADS + h, l, stats)
            parts.append(jnp.where(lo_mask, outs[0], outs[1]).astype(o_ref.dtype))
        out = jnp.concatenate(parts, axis=1)
        if len(o_ref.shape) == 2:
            o_ref[j * w:(j + 1) * w, :] = out
            st_ref[j * w:(j + 1) * w, :] = stats
        else:
            o_ref[:, j] = out.reshape(o_ref.shape[0], o_ref.shape[2], o_ref.shape[3])
            st_ref[:, j] = stats.reshape(st_ref.shape[0], st_ref.shape[2], st_ref.shape[3])


def _window_attention(qkv, grp, bsz, seq):
    _, dil = ATT_PATTERNS[grp]
    w = ATT_BLOCK
    t = bsz * seq
    chunk = ROW_TILE // dil
    tiles = w // chunk if chunk < w else 1
    span = dil * w
    nb = seq // span
    col0 = grp * 3
    chained = dil == 1
    g = ATT_PROBLEMS if chained else min(ATT_PROBLEMS, dil)

    if chained:
        grid = (bsz, 1, nb // g)
        per_b = seq // (g * w)

        def spec(width, col, prev):
            if prev:
                return pl.BlockSpec((w, width), lambda b, r, i: (b * g * per_b + jnp.maximum(g * i - 1, 0), col))
            return pl.BlockSpec((g * w, width), lambda b, r, i: (b * per_b + i, col))

        qkv_v, o_shape, st_shape = qkv, (t, ATT_OUT_DIM), (t, LANES)
    elif tiles == 1:
        grid = (bsz, dil // g, nb)
        per_b = seq // (g * w)
        stride = span // (g * w)

        def spec(width, col, prev):
            def imap(b, r, i):
                blk = jnp.maximum(i - 1, 0) if prev else i
                return (b * per_b + blk * stride + r, col)
            return pl.BlockSpec((g * w, width), imap)

        qkv_v, o_shape, st_shape = qkv, (t, ATT_OUT_DIM), (t, LANES)
    else:
        grid = (bsz, dil // g, nb)

        def spec(width, col, prev):
            def imap(b, r, i):
                blk = jnp.maximum(i - 1, 0) if prev else i
                return (b, blk, 0, r, 0, col)
            return pl.BlockSpec((None, None, tiles, g, chunk, width), imap)

        lead = (bsz, nb, tiles, dil, chunk)
        qkv_v, o_shape, st_shape = qkv.reshape(*lead, ATT_QKV_DIM), (*lead, ATT_OUT_DIM), (*lead, LANES)

    o, st = pl.pallas_call(
        functools.partial(_attn_kernel, chained=chained, problems=g),
        grid=grid,
        in_specs=[spec(ATT_OUT_DIM, col0, False), spec(ATT_OUT_DIM, col0 + 1, True),
                  spec(ATT_OUT_DIM, col0 + 1, False), spec(ATT_OUT_DIM, col0 + 2, True),
                  spec(ATT_OUT_DIM, col0 + 2, False)],
        out_specs=[spec(ATT_OUT_DIM, 0, False), spec(LANES, 0, False)],
        out_shape=[jax.ShapeDtypeStruct(o_shape, BF16), jax.ShapeDtypeStruct(st_shape, F32)],
        compiler_params=_params("parallel", "parallel", "arbitrary"),
        name=f"window_attn_d{dil}",
    )(qkv_v, qkv_v, qkv_v, qkv_v, qkv_v)
    return o.reshape(t, ATT_OUT_DIM), st.reshape(t, LANES)


def _merge_proj_ln_kernel(o1_ref, o2_ref, o3_ref, s1_ref, s2_ref, s3_ref, w_ref, x_ref,
                          g_ref, b_ref, w2_ref, rb_ref, o_ref, opk_ref, route_ref, cnt_ref, carry, tok_o, tok_s):
    rows = o1_ref.shape[0]
    pairs = ATT_HEADS // 2
    lane = lax.broadcasted_iota(jnp.int32, (rows, LANES), 1)

    for gi, (o_ref_g, s_ref_g) in enumerate(((o2_ref, s2_ref), (o3_ref, s3_ref))):
        dil = ATT_PATTERNS[gi + 1][1]
        n = rows // dil
        for r in range(dil):
            tok_s[gi, pl.ds(r, n, stride=dil), :] = s_ref_g[r * n:(r + 1) * n, :]
            for p in range(pairs):
                tok_o[gi, p, pl.ds(r, n, stride=dil), :] = o_ref_g[r * n:(r + 1) * n, p * LANES:(p + 1) * LANES].astype(F32)

    sts = [s1_ref[...], tok_s[0], tok_s[1]]
    mx = jnp.maximum(jnp.maximum(sts[0], sts[1]), sts[2])
    wgts = [pltpu.roll(s, LANES - ATT_HEADS, 1) * jnp.exp(s - mx) for s in sts]
    den = wgts[0] + wgts[1] + wgts[2]
    den = jnp.where(lane < ATT_HEADS, den, 1.0)
    coefs = [wg / den for wg in wgts]
    parts = []
    for p in range(pairs):
        sl = slice(p * LANES, (p + 1) * LANES)
        head_of_lane = jnp.where(lane < HALF, 2 * p, 2 * p + 1)
        acc = jnp.take_along_axis(coefs[0], head_of_lane, axis=1) * o1_ref[:, sl].astype(F32)
        for gi in range(1, ATT_N_GROUPS):
            acc = acc + jnp.take_along_axis(coefs[gi], head_of_lane, axis=1) * tok_o[gi - 1, p]
        parts.append(acc.astype(BF16))
    mix = _dot(jnp.concatenate(parts, axis=1), w_ref[...])
    _ln_route_epilogue(DEEPNORM_ALPHA * x_ref[...] + mix, g_ref, b_ref, w2_ref, rb_ref,
                       o_ref, opk_ref, route_ref, cnt_ref, carry)


def _merge_proj_ln(os_, sts, w, x, g, b, router_w):
    t = x.shape[0]
    tm = ROW_TILE
    r_in, r_out, r_shape, r_scratch = _route_specs(t)
    ospec = pl.BlockSpec((tm, ATT_OUT_DIM), lambda i: (i, 0))
    sspec = pl.BlockSpec((tm, LANES), lambda i: (i, 0))
    xspec = pl.BlockSpec((tm, D_MODEL), lambda i: (i, 0))
    vspec = pl.BlockSpec((1, D_MODEL), lambda i: (0, 0))
    return pl.pallas_call(
        _merge_proj_ln_kernel,
        grid=(t // tm,),
        in_specs=[ospec] * 3 + [sspec] * 3 + [_resident((ATT_OUT_DIM, D_MODEL)), xspec, vspec, vspec] + r_in,
        out_specs=[xspec, pl.BlockSpec((tm, D_PACK), lambda i: (i, 0))] + r_out,
        out_shape=[jax.ShapeDtypeStruct((t, D_MODEL), F32), jax.ShapeDtypeStruct((t, D_PACK), U32)] + r_shape,
        scratch_shapes=r_scratch + [pltpu.VMEM((ATT_N_GROUPS - 1, ATT_HEADS // 2, tm, LANES), F32),
                                    pltpu.VMEM((ATT_N_GROUPS - 1, tm, LANES), F32)],
        compiler_params=_params("arbitrary"),
        name="attn_merge_proj_ln",
    )(*os_, *sts, w, x, g[None, :], b[None, :], *router_w)


def _route_rows(x, w2_ref, b_ref, carry):
    xhi = x.astype(BF16)
    xlo = (x - xhi.astype(F32)).astype(BF16)
    w2 = w2_ref[...]
    hi2 = _dot(xhi, w2)
    logits = hi2[:, :LANES] + hi2[:, LANES:] + _dot(xlo, w2[:, :LANES]) + b_ref[...]
    rows = logits.shape[0]
    lane = lax.broadcasted_iota(jnp.int32, (rows, LANES), 1)
    big = jnp.int32(LANES)

    def top1(vals, mask):
        v = jnp.where(mask, vals, -jnp.inf)
        m = jnp.max(v, -1, keepdims=True)
        idx = jnp.min(jnp.where(v == m, lane, big), -1, keepdims=True)
        return v, m, idx

    gmask = lane < MOE_N_GROUPS
    gv, gm, gidx = top1(logits, gmask)
    g_w = 1.0 / jnp.sum(jnp.exp(gv - gm), -1, keepdims=True)
    e_lo = MOE_N_GROUPS + gidx * MOE_EPG
    emask = (lane >= e_lo) & (lane < e_lo + MOE_EPG)
    ev, m1, i1 = top1(logits, emask)
    zsum = jnp.sum(jnp.exp(ev - m1), -1, keepdims=True)
    _, m2, i2 = top1(logits, emask & (lane != i1))
    p1 = 1.0 / zsum
    p2 = jnp.exp(m2 - m1) / zsum
    tot = p1 + p2
    e1 = i1 - MOE_N_GROUPS
    e2 = i2 - MOE_N_GROUPS

    oh1 = jnp.where(lane == e1, 1.0, 0.0)
    oh2 = jnp.where(lane == e2, 1.0, 0.0)
    oh = oh1 + oh2
    ri = lax.broadcasted_iota(jnp.int32, (rows, rows), 0)
    ci = lax.broadcasted_iota(jnp.int32, (rows, rows), 1)
    strict = jnp.where(ri > ci, 1.0, 0.0).astype(BF16)
    before = _dot(strict, oh.astype(BF16)) + carry[...]
    rank1 = jnp.sum(oh1 * before, -1, keepdims=True)
    rank2 = jnp.sum(oh2 * before, -1, keepdims=True)
    carry[...] = carry[...] + jnp.sum(oh, 0, keepdims=True)

    vals = [e1.astype(F32), e2.astype(F32), g_w * (p1 / tot), g_w * (p2 / tot), rank1, rank2]
    out = jnp.zeros((rows, LANES), F32)
    for j, val in enumerate(vals):
        out = jnp.where(lane == j, val, out)
    return out


def _router_weights(w_rg, b_rg, w_re, b_re):
    n_log = MOE_N_GROUPS + MOE_N_EXPERTS
    w = jnp.pad(jnp.concatenate([w_rg, w_re], axis=1), ((0, 0), (0, LANES - n_log)))
    whi = w.astype(BF16)
    w2 = jnp.concatenate([whi, (w - whi.astype(F32)).astype(BF16)], axis=1)
    bias = jnp.pad(jnp.concatenate([b_rg, b_re]), (0, LANES - n_log))[None, :]
    return w2, bias


def _ln_route_epilogue(r, g_ref, b_ref, w2_ref, rb_ref, o_ref, opk_ref, route_ref, cnt_ref, carry):
    @pl.when(pl.program_id(0) == 0)
    def _():
        carry[...] = jnp.zeros(carry.shape, F32)

    out = _layer_norm(r, g_ref[...], b_ref[...])
    o_ref[...] = out
    opk_ref[...] = _pack_rows(out)
    route_ref[...] = _route_rows(out, w2_ref, rb_ref, carry)
    cnt_ref[...] = carry[...]


def _expert_kernel(meta_ref, x_ref, wg_ref, wu_ref, wd_ref, o_ref, wg16, wu16, wd16):
    i = pl.program_id(0)
    n_blocks = pl.num_programs(0)

    @pl.when(i < meta_ref[n_blocks])
    def _():
        @pl.when((i == 0) | (meta_ref[i] != meta_ref[jnp.maximum(i - 1, 0)]))
        def _():
            wg16[...] = wg_ref[...].astype(BF16)
            wu16[...] = wu_ref[...].astype(BF16)
            wd16[...] = wd_ref[...].astype(BF16)

        rows = lax.broadcasted_iota(jnp.int32, x_ref.shape, 0)
        xw = jnp.where(rows < meta_ref[n_blocks + 1 + i], x_ref[...], jnp.uint32(0))
        x = _unpack_rows(xw).astype(BF16)
        h = _silu(_dot(x, wg16[...])) * _dot(x, wu16[...])
        o_ref[...] = _pack_rows(_dot(h.astype(BF16), wd16[...]))


def _expert_mlp(meta, x_rows, w_gate, w_up, w_down, layer):
    n_rows = x_rows.shape[0]
    n_blocks = n_rows // MOE_ROW_BLOCK
    tb = MOE_ROW_BLOCK
    grid_spec = pltpu.PrefetchScalarGridSpec(
        num_scalar_prefetch=1,
        grid=(n_blocks,),
        in_specs=[pl.BlockSpec((tb, D_PACK), lambda i, meta: (i, 0)),
                  pl.BlockSpec((None, None, D_MODEL, MOE_HIDDEN), lambda i, meta: (layer, meta[i], 0, 0)),
                  pl.BlockSpec((None, None, D_MODEL, MOE_HIDDEN), lambda i, meta: (layer, meta[i], 0, 0)),
                  pl.BlockSpec((None, None, MOE_HIDDEN, D_MODEL), lambda i, meta: (layer, meta[i], 0, 0))],
        out_specs=pl.BlockSpec((tb, D_PACK), lambda i, meta: (i, 0)),
        scratch_shapes=[pltpu.VMEM((D_MODEL, MOE_HIDDEN), BF16), pltpu.VMEM((D_MODEL, MOE_HIDDEN), BF16),
                        pltpu.VMEM((MOE_HIDDEN, D_MODEL), BF16)],
    )
    return pl.pallas_call(
        _expert_kernel,
        grid_spec=grid_spec,
        out_shape=jax.ShapeDtypeStruct((n_rows, D_PACK), U32),
        compiler_params=_params("arbitrary"),
        name="moe_experts",
    )(meta, x_rows, w_gate, w_up, w_down)


def _combine_ln_kernel(y0_ref, y1_ref, r_ref, x_ref, g_ref, b_ref, o_ref, opk_ref):
    route = r_ref[...]
    g0 = route[:, 2:3]
    g1 = route[:, 3:4]
    ffn = g0 * _unpack_rows(y0_ref[...]) + g1 * _unpack_rows(y1_ref[...])
    out = _layer_norm(DEEPNORM_ALPHA * x_ref[...] + ffn, g_ref[...], b_ref[...])
    o_ref[...] = out
    opk_ref[...] = _pack_rows(out)


def _combine_ln(y0, y1, route, x, g, b):
    t = x.shape[0]
    xspec = pl.BlockSpec((ROW_TILE, D_MODEL), lambda i: (i, 0))
    pspec = pl.BlockSpec((ROW_TILE, D_PACK), lambda i: (i, 0))
    vspec = pl.BlockSpec((1, D_MODEL), lambda i: (0, 0))
    return pl.pallas_call(
        _combine_ln_kernel,
        grid=(t // ROW_TILE,),
        in_specs=[pspec, pspec, pl.BlockSpec((ROW_TILE, LANES), lambda i: (i, 0)), xspec, vspec, vspec],
        out_specs=[xspec, pspec],
        out_shape=[jax.ShapeDtypeStruct((t, D_MODEL), F32), jax.ShapeDtypeStruct((t, D_PACK), U32)],
        compiler_params=_params("parallel"),
        name="moe_combine_ln",
    )(y0, y1, route, x, g[None, :], b[None, :])


def _positions_kernel(r_ref, ps_ref, p0_ref, p1_ref):
    route = r_ref[...]
    lane = lax.broadcasted_iota(jnp.int32, route.shape, 1)
    starts = ps_ref[...]
    out = jnp.zeros(route.shape, F32)
    for k in range(MOE_TOP_K):
        eid = route[:, k:k + 1].astype(jnp.int32)
        pos = jnp.sum(jnp.where(lane == eid, starts, 0.0), -1, keepdims=True) + route[:, 4 + k:5 + k]
        out = jnp.where(lane == k, pos, out)
    for c in range(route.shape[0] // LANES):
        tile = out[c * LANES:(c + 1) * LANES, :].T.astype(jnp.int32)
        p0_ref[:, c * LANES:(c + 1) * LANES] = tile[0:1, :]
        p1_ref[:, c * LANES:(c + 1) * LANES] = tile[1:2, :]


def _positions(route, pad_start):
    t = route.shape[0]
    rows = 4 * ROW_TILE
    starts = jnp.pad(pad_start.astype(F32), (0, LANES - MOE_N_EXPERTS))[None, :]
    ospec = pl.BlockSpec((1, rows), lambda i: (0, i))
    return pl.pallas_call(
        _positions_kernel,
        grid=(t // rows,),
        in_specs=[pl.BlockSpec((rows, LANES), lambda i: (i, 0)), pl.BlockSpec((1, LANES), lambda i: (0, 0))],
        out_specs=[ospec, ospec],
        out_shape=[jax.ShapeDtypeStruct((1, t), jnp.int32)] * 2,
        compiler_params=_params("parallel"),
        name="moe_positions",
    )(route, starts)


def _moe(x, xpk, route, cnt, w_gate, w_up, w_down, layer, g, b):
    t = x.shape[0]
    tb = MOE_ROW_BLOCK
    n_assign = t * MOE_TOP_K
    n_blocks = n_assign // tb + MOE_N_EXPERTS
    n_rows = n_blocks * tb
    counts = cnt[0, :MOE_N_EXPERTS].astype(jnp.int32)
    padded = (counts + tb - 1) // tb * tb
    pad_end = jnp.cumsum(padded)
    pos0, pos1 = _positions(route, pad_end - padded)
    block_start = jnp.arange(n_blocks, dtype=jnp.int32) * tb
    block_e = jnp.minimum(jnp.sum((pad_end[None, :] <= block_start[:, None]).astype(jnp.int32), -1),
                          MOE_N_EXPERTS - 1)
    valid = jnp.clip(pad_end[block_e] - padded[block_e] + counts[block_e] - block_start, 0, tb)
    meta = jnp.concatenate([block_e, pad_end[-1:] // tb, valid]).astype(jnp.int32)
    x_rows = _row_scatter(xpk, (pos0, pos1), n_rows)
    y_rows = _expert_mlp(meta, x_rows, w_gate, w_up, w_down, layer)
    y0 = _row_gather(y_rows, pos0)
    y1 = _row_gather(y_rows, pos1)
    return _combine_ln(y0, y1, route, x, g, b)


def _ssd_layer(x, xpk, w_in, conv_w, conv_b, dt_bias, a_log, d_skip, norm_w, w_out, g, b, router_w, bsz, seq):
    w_zxbc = w_in[:, :SSD_D_INNER + SSD_CONV_DIM].astype(BF16)
    w_dt = jnp.pad(w_in[:, SSD_D_INNER + SSD_CONV_DIM:], ((0, 0), (0, LANES - SSD_N_HEADS))).astype(BF16)
    zxbc, dt_raw = _in_proj(xpk, w_zxbc, w_dt, conv_w, conv_b, bsz, seq)
    y = _ssd_scan(zxbc, dt_raw, conv_w, conv_b, dt_bias, a_log, d_skip, norm_w, bsz, seq)
    return _proj_ln(y, w_out.astype(BF16), x, g, b, router_w, "ssd_out_proj_ln")


def _attn_layer(x, xpk, tabs, w_qkv, w_o, g, b, router_w, bsz, seq):
    qkv = _qkv_proj(xpk, w_qkv.astype(BF16), tabs)
    os_, sts = [], []
    for grp in range(ATT_N_GROUPS):
        o, st = _window_attention(qkv, grp, bsz, seq)
        os_.append(o)
        sts.append(st)
    return _merge_proj_ln(os_, sts, w_o.astype(BF16), x, g, b, router_w)


def kernel(x, positions, ssd_w_in, ssd_conv_w, ssd_conv_b, ssd_dt_bias, ssd_a_log, ssd_d, ssd_norm_w, ssd_w_out,
           attn_w_qkv, attn_w_o, ln_g, ln_b, moe_w_router_group, moe_b_router_group, moe_w_router_expert,
           moe_b_router_expert, moe_w_gate, moe_w_up, moe_w_down):
    bsz, seq, d = x.shape
    t = bsz * seq
    h = x.reshape(t, d)
    hpk = h
    tabs = _rope_tables(positions)
    for i in range(DEPTH):
        j = i // N_MIXERS
        router_w = _router_weights(moe_w_router_group[i], moe_b_router_group[i], moe_w_router_expert[i],
                                   moe_b_router_expert[i])
        if i % N_MIXERS == 0:
            h, hpk, route, cnt = _ssd_layer(h, hpk, ssd_w_in[j], ssd_conv_w[j], ssd_conv_b[j], ssd_dt_bias[j],
                                            ssd_a_log[j], ssd_d[j], ssd_norm_w[j], ssd_w_out[j], ln_g[i, 0],
                                            ln_b[i, 0], router_w, bsz, seq)
        else:
            h, hpk, route, cnt = _attn_layer(h, hpk, tabs, attn_w_qkv[j], attn_w_o[j], ln_g[i, 0], ln_b[i, 0],
                                             router_w, bsz, seq)
        h, hpk = _moe(h, hpk, route, cnt, moe_w_gate, moe_w_up, moe_w_down, i, ln_g[i, 1], ln_b[i, 1])
    return h.reshape(bsz, seq, d)
```

```python
import functools

import jax
import jax.numpy as jnp
from jax import lax
from jax.experimental import pallas as pl
from jax.experimental.pallas import tpu as pltpu
from jax.experimental.pallas import tpu_sc as plsc

F32 = jnp.float32
BF16 = jnp.bfloat16
U32 = jnp.uint32

D_MODEL = 1024
D_PACK = D_MODEL // 2
DEPTH = 4
N_MIXERS = 2

SSD_D_INNER = 2048
SSD_HEAD_DIM = 64
SSD_N_HEADS = 32
SSD_N_GROUPS = 4
SSD_D_STATE = 128
SSD_D_CONV = 4
SSD_CHUNK = 128
SSD_GN = SSD_N_GROUPS * SSD_D_STATE
SSD_CONV_DIM = SSD_D_INNER + 2 * SSD_GN
SSD_GROUP_COLS = SSD_D_INNER // SSD_N_GROUPS
IN_PROJ_COLS = 1024
SSD_PAIR_UNROLL = 4

ATT_HEAD_DIM = 64
ATT_HEADS = 8
ATT_PATTERNS = ((128, 1), (512, 4), (2048, 16))
ATT_N_GROUPS = 3
ATT_OUT_DIM = ATT_HEADS * ATT_HEAD_DIM
ATT_QKV_DIM = ATT_N_GROUPS * 3 * ATT_OUT_DIM
ATT_BLOCK = 128
ATT_PROBLEMS = 8
ROPE_THETA = 500000.0
ROPE_DIM = 16

MOE_N_GROUPS = 4
MOE_EPG = 8
MOE_N_EXPERTS = 32
MOE_TOP_K = 2
MOE_HIDDEN = 512
MOE_ROW_BLOCK = 512

DEEPNORM_ALPHA = (2 * DEPTH) ** 0.25
LN_EPS = 1e-5
RMS_EPS = 1e-5
NEG_INF = -1e30

LANES = 128
HALF = LANES // 2
VMEM_LIMIT = 56 * 1024 * 1024

ROW_TILE = 512


def _params(*sem):
    return pltpu.CompilerParams(dimension_semantics=sem, vmem_limit_bytes=VMEM_LIMIT)


def _silu(v):
    h = 0.5 * v
    return h + h * jnp.tanh(h)


def _layer_norm(r, g, b):
    mu = jnp.mean(r, -1, keepdims=True)
    d = r - mu
    var = jnp.mean(d * d, -1, keepdims=True)
    return d * lax.rsqrt(var + LN_EPS) * g + b


def _split3(v):
    hi = v.astype(BF16)
    r1 = v - hi.astype(F32)
    mid = r1.astype(BF16)
    lo = (r1 - mid.astype(F32)).astype(BF16)
    return hi, mid, lo


def _dot(a, b):
    return jnp.dot(a, b, preferred_element_type=F32)


def _dot_nt(a, b):
    return lax.dot_general(a, b, (((1,), (1,)), ((), ())), preferred_element_type=F32)


def _dot_tn(a, b):
    return lax.dot_general(a, b, (((0,), (0,)), ((), ())), preferred_element_type=F32)


def _pack_rows(v):
    r = pltpu.bitcast(v.astype(BF16).astype(F32), U32)
    return r[:, :D_PACK] | (r[:, D_PACK:] >> 16)


def _unpack_rows(p):
    hi = pltpu.bitcast(p & jnp.uint32(0xFFFF0000), F32)
    lo = pltpu.bitcast(p << 16, F32)
    return jnp.concatenate([hi, lo], axis=1)


def _row_gather(data, idx):
    n = idx.shape[1]
    d = data.shape[1]
    window = LANES
    dc = d // 2
    mesh = plsc.VectorSubcoreMesh(core_axis_name="core", subcore_axis_name="subcore")

    @functools.partial(pl.kernel, out_type=jax.ShapeDtypeStruct((n, d), data.dtype), mesh=mesh)
    def gather(x_hbm, i_hbm, o_hbm):
        for c in range(d // dc):
            def body(i_vmem, o_vmem, c=c):
                pltpu.sync_copy(x_hbm.at[i_vmem.at[0], pl.ds(c * dc, dc)], o_vmem)

            pltpu.emit_pipeline(
                body,
                grid=(n // window,),
                in_specs=[pl.BlockSpec((1, window), lambda i: (0, i))],
                out_specs=[pl.BlockSpec((window, dc), lambda i, c=c: (i, c))],
                core_axis_name=("core", "subcore"),
                dimension_semantics=(pltpu.PARALLEL,),
            )(i_hbm, o_hbm)

    return gather(data, idx)


def _row_scatter(data, idxs, n_rows):
    t, d = data.shape
    window = LANES
    dc = d // 2
    mesh = plsc.VectorSubcoreMesh(core_axis_name="core", subcore_axis_name="subcore")

    @functools.partial(pl.kernel, out_type=jax.ShapeDtypeStruct((n_rows, d), data.dtype), mesh=mesh)
    def scatter(x_hbm, *refs):
        o_hbm = refs[-1]
        for i_hbm in refs[:-1]:
            for c in range(d // dc):
                def body(x_vmem, i_vmem, c=c):
                    pltpu.sync_copy(x_vmem, o_hbm.at[i_vmem.at[0], pl.ds(c * dc, dc)])

                pltpu.emit_pipeline(
                    body,
                    grid=(t // window,),
                    in_specs=[pl.BlockSpec((window, dc), lambda i, c=c: (i, c)),
                              pl.BlockSpec((1, window), lambda i: (0, i))],
                    out_specs=[],
                    core_axis_name=("core", "subcore"),
                    dimension_semantics=(pltpu.PARALLEL,),
                )(x_hbm, i_hbm)

    return scatter(data, *idxs)


def _resident(shape):
    return pl.BlockSpec(shape, lambda *_: (0,) * len(shape), pipeline_mode=pl.Buffered(1))


def _in_proj_kernel(x_ref, w_ref, wdt_ref, cw_ref, cb_ref, o_ref, dt_ref, ext, tail):
    rows = x_ref.shape[0]
    tn = IN_PROJ_COLS
    n_z = SSD_D_INNER // tn

    @pl.when(pl.program_id(1) == 0)
    def _():
        tail[...] = jnp.zeros(tail.shape, F32)

    x = x_ref[...]
    x = (_unpack_rows(x) if x.dtype == U32 else x).astype(BF16)
    for n in range(n_z):
        sl = slice(n * tn, (n + 1) * tn)
        o_ref[:, sl] = _dot(x, w_ref[:, sl]).astype(o_ref.dtype)
    for n in range(n_z + SSD_D_INNER // tn, w_ref.shape[1] // tn):
        sl = slice(n * tn, (n + 1) * tn)
        o_ref[:, sl] = _dot(x, w_ref[:, sl]).astype(o_ref.dtype)
    for j in range(SSD_D_INNER // tn):
        sl = slice((n_z + j) * tn, (n_z + j + 1) * tn)
        cl = slice(j * tn, (j + 1) * tn)
        ext[j, 0:8, :] = tail[j]
        ext[j, 8:8 + rows, :] = _dot(x, w_ref[:, sl])
        tail[j] = ext[j, rows:rows + 8, :]
        acc = ext[j, 8:8 + rows, :] * cw_ref[3:4, cl] + cb_ref[:, cl]
        for k in range(SSD_D_CONV - 1):
            acc = acc + ext[j, 5 + k:5 + k + rows, :] * cw_ref[k:k + 1, cl]
        o_ref[:, sl] = _silu(acc).astype(o_ref.dtype)
    dt_ref[...] = _dot(x, wdt_ref[...])


def _in_proj(xpk, w_zxbc, w_dt, conv_w, conv_b, bsz, seq):
    t = xpk.shape[0]
    n = w_zxbc.shape[1]
    tiles = seq // ROW_TILE
    cw = jnp.pad(conv_w[:, :SSD_D_INNER], ((0, 8 - SSD_D_CONV), (0, 0)))
    conv_b = conv_b[:SSD_D_INNER]
    n_conv = SSD_D_INNER // IN_PROJ_COLS
    return pl.pallas_call(
        _in_proj_kernel,
        grid=(bsz, tiles),
        in_specs=[pl.BlockSpec((ROW_TILE, xpk.shape[1]), lambda b, s: (b * tiles + s, 0)),
                  _resident((D_MODEL, n)), _resident((D_MODEL, LANES)),
                  _resident((8, SSD_D_INNER)), _resident((1, SSD_D_INNER))],
        out_specs=[pl.BlockSpec((ROW_TILE, n), lambda b, s: (b * tiles + s, 0)),
                   pl.BlockSpec((ROW_TILE, LANES), lambda b, s: (b * tiles + s, 0))],
        out_shape=[jax.ShapeDtypeStruct((t, n), BF16), jax.ShapeDtypeStruct((t, LANES), F32)],
        scratch_shapes=[pltpu.VMEM((n_conv, ROW_TILE + 8, IN_PROJ_COLS), F32),
                        pltpu.VMEM((n_conv, 8, IN_PROJ_COLS), F32)],
        compiler_params=_params("arbitrary", "arbitrary"),
        name="ssd_in_proj",
    )(xpk, w_zxbc, w_dt, cw, conv_b[None, :])


def _ssd_kernel(z_ref, xs_ref, bc_ref, dt_ref, dtb_ref, alog_ref, dsk_ref, nw_ref, ex_ref, cw_ref, cbias_ref, o_ref,
                state, cs_cols, cs_rows, dt_x, cb_all, y_off, y_grp, xd_all, cd_all, bc_ext, bc_s):
    q = SSD_CHUNK
    pairs = SSD_GROUP_COLS // LANES

    @pl.when(pl.program_id(1) == 0)
    def _():
        state[...] = jnp.zeros(state.shape, F32)
        bc_ext[0:8, :] = jnp.zeros((8, 2 * SSD_GN), F32)

    bc_ext[8:8 + q, :] = bc_ref[...].astype(F32)
    acc = bc_ext[8:8 + q, :] * cw_ref[3:4, :] + cbias_ref[...]
    for k in range(SSD_D_CONV - 1):
        acc = acc + bc_ext[5 + k:5 + k + q, :] * cw_ref[k:k + 1, :]
    bc_s[...] = _silu(acc).astype(BF16)
    bc_ext[0:8, :] = bc_ext[q:q + 8, :]

    pre = dt_ref[...] + dtb_ref[...]
    dt = jnp.maximum(pre, 0.0) + jnp.log(1.0 + jnp.exp(-jnp.abs(pre)))
    a = -jnp.exp(alog_ref[...])
    row = lax.broadcasted_iota(jnp.int32, (q, q), 0)
    col = lax.broadcasted_iota(jnp.int32, (q, q), 1)
    causal = row >= col
    lo_mask = col < HALF
    tri = jnp.where(causal, 1.0, 0.0).astype(BF16)
    cs3 = _dot(tri, jnp.concatenate(_split3(dt * a), axis=1))
    cs = cs3[:, :LANES] + cs3[:, LANES:2 * LANES] + cs3[:, 2 * LANES:]
    cs_rows[...] = cs.T
    for h in range(SSD_N_HEADS):
        cs_cols[h] = jnp.broadcast_to(cs[:, h:h + 1], (q, q))
    dt3 = _dot(jnp.concatenate(_split3(dt), axis=0), ex_ref[...])
    dt_x[...] = dt3[:q] + dt3[q:2 * q] + dt3[2 * q:]
    for g in range(SSD_N_GROUPS):
        bg = bc_s[:, g * SSD_D_STATE:(g + 1) * SSD_D_STATE]
        cg = bc_s[:, SSD_GN + g * SSD_D_STATE:SSD_GN + (g + 1) * SSD_D_STATE]
        cb_all[g] = _dot_nt(cg, bg)
        y_off[:, g * SSD_GROUP_COLS:(g + 1) * SSD_GROUP_COLS] = _dot(cg, state[g].astype(BF16))

    def group_body(g, carry):
        cb = cb_all[g]

        def pair_body(pp, ssq):
            p = g * pairs + pp
            h0 = 2 * p
            x0 = pl.multiple_of(p * LANES, LANES)
            l0 = pl.multiple_of(pp * LANES, LANES)
            cols = (cs_cols[h0], cs_cols[h0 + 1])
            csx = jnp.where(lo_mask, cols[0], cols[1])
            xp = xs_ref[:, pl.ds(x0, LANES)].astype(F32)
            xdt = xp * dt_x[:, pl.ds(x0, LANES)]
            xdt16 = xdt.astype(BF16)
            last = csx[q - 1:q, :]
            halves = []
            for hh in range(2):
                diff = cols[hh] - cs_rows[pl.ds(h0 + hh, 1), :]
                decay = jnp.exp(jnp.where(causal, diff, -jnp.inf))
                halves.append(_dot((cb * decay).astype(BF16), xdt16))
            y = jnp.where(lo_mask, halves[0], halves[1])
            y = y + y_off[:, pl.ds(x0, LANES)] * jnp.exp(csx) + xp * dsk_ref[:, pl.ds(x0, LANES)]
            y = y * _silu(z_ref[:, pl.ds(x0, LANES)].astype(F32))
            y_grp[:, pl.ds(l0, LANES)] = y
            xd_all[:, pl.ds(x0, LANES)] = (xdt * jnp.exp(last - csx)).astype(BF16)
            cd_all[:, pl.ds(x0, LANES)] = jnp.exp(last)
            return ssq + jnp.sum(y * y, -1, keepdims=True)

        ssq = lax.fori_loop(0, pairs, pair_body, jnp.zeros((q, 1), F32), unroll=SSD_PAIR_UNROLL)
        inv = lax.rsqrt(ssq * (1.0 / SSD_GROUP_COLS) + RMS_EPS)
        g0 = pl.multiple_of(g * SSD_GROUP_COLS, SSD_GROUP_COLS)
        o_ref[:, pl.ds(g0, SSD_GROUP_COLS)] = (
            y_grp[...] * inv * nw_ref[:, pl.ds(g0, SSD_GROUP_COLS)]).astype(o_ref.dtype)
        return carry

    lax.fori_loop(0, SSD_N_GROUPS, group_body, 0)

    for g in range(SSD_N_GROUPS):
        gs = slice(g * SSD_GROUP_COLS, (g + 1) * SSD_GROUP_COLS)
        bg = bc_s[:, g * SSD_D_STATE:(g + 1) * SSD_D_STATE]
        state[g] = state[g] * cd_all[:, gs] + _dot_tn(bg, xd_all[:, gs])


def _ssd_scan(zxbc, dt_raw, conv_w, conv_b, dt_bias, a_log, d_skip, norm_w, bsz, seq):
    t = bsz * seq
    nc = seq // SSD_CHUNK
    q = SSD_CHUNK
    pad_h = LANES - SSD_N_HEADS
    dtb = jnp.pad(dt_bias, (0, pad_h))[None, :]
    alog = jnp.pad(a_log, (0, pad_h))[None, :]
    dsk = jnp.repeat(d_skip, SSD_HEAD_DIM)[None, :]
    nw = norm_w[None, :]
    expand = (jnp.arange(LANES)[:, None] == jnp.arange(SSD_D_INNER)[None, :] // SSD_HEAD_DIM).astype(BF16)
    cw_bc = jnp.pad(conv_w[:, SSD_D_INNER:], ((0, 8 - SSD_D_CONV), (0, 0)))
    cb_bc = conv_b[None, SSD_D_INNER:]

    def const(shape):
        return pl.BlockSpec(shape, lambda b, c: (0, 0))

    return pl.pallas_call(
        _ssd_kernel,
        grid=(bsz, nc),
        in_specs=[pl.BlockSpec((q, SSD_D_INNER), lambda b, c: (b * nc + c, 0)),
                  pl.BlockSpec((q, SSD_D_INNER), lambda b, c: (b * nc + c, 1)),
                  pl.BlockSpec((q, 2 * SSD_GN), lambda b, c: (b * nc + c, 4)),
                  pl.BlockSpec((q, LANES), lambda b, c: (b * nc + c, 0)),
                  const((1, LANES)), const((1, LANES)),
                  const((1, SSD_D_INNER)), const((1, SSD_D_INNER)), const((LANES, SSD_D_INNER)),
                  const((8, 2 * SSD_GN)), const((1, 2 * SSD_GN))],
        out_specs=pl.BlockSpec((q, SSD_D_INNER), lambda b, c: (b * nc + c, 0)),
        out_shape=jax.ShapeDtypeStruct((t, SSD_D_INNER), BF16),
        scratch_shapes=[pltpu.VMEM((SSD_N_GROUPS, SSD_D_STATE, SSD_GROUP_COLS), F32),
                        pltpu.VMEM((SSD_N_HEADS, q, q), F32),
                        pltpu.VMEM((LANES, q), F32),
                        pltpu.VMEM((q, SSD_D_INNER), F32),
                        pltpu.VMEM((SSD_N_GROUPS, q, q), F32),
                        pltpu.VMEM((q, SSD_D_INNER), F32),
                        pltpu.VMEM((q, SSD_GROUP_COLS), F32),
                        pltpu.VMEM((q, SSD_D_INNER), BF16),
                        pltpu.VMEM((1, SSD_D_INNER), F32),
                        pltpu.VMEM((q + 8, 2 * SSD_GN), F32),
                        pltpu.VMEM((q, 2 * SSD_GN), BF16)],
        compiler_params=_params("arbitrary", "arbitrary"),
        name="ssd_scan",
    )(zxbc, zxbc, zxbc, dt_raw, dtb, alog, dsk, nw, expand, cw_bc, cb_bc)


def _proj_ln_kernel(y_ref, w_ref, x_ref, g_ref, b_ref, w2_ref, rb_ref, o_ref, opk_ref, route_ref, cnt_ref, carry):
    mix = _dot(y_ref[...].astype(BF16), w_ref[...])
    _ln_route_epilogue(DEEPNORM_ALPHA * x_ref[...] + mix, g_ref, b_ref, w2_ref, rb_ref,
                       o_ref, opk_ref, route_ref, cnt_ref, carry)


def _route_specs(t):
    ins = [_resident((D_MODEL, 2 * LANES)), pl.BlockSpec((1, LANES), lambda i: (0, 0))]
    outs = [pl.BlockSpec((ROW_TILE, LANES), lambda i: (i, 0)), pl.BlockSpec((1, LANES), lambda i: (0, 0))]
    shapes = [jax.ShapeDtypeStruct((t, LANES), F32), jax.ShapeDtypeStruct((1, LANES), F32)]
    return ins, outs, shapes, [pltpu.VMEM((1, LANES), F32)]


def _proj_ln(y, w, x, g, b, router_w, name):
    t, k = y.shape
    r_in, r_out, r_shape, r_scratch = _route_specs(t)
    return pl.pallas_call(
        _proj_ln_kernel,
        grid=(t // ROW_TILE,),
        in_specs=[pl.BlockSpec((ROW_TILE, k), lambda i: (i, 0)),
                  _resident((k, D_MODEL)),
                  pl.BlockSpec((ROW_TILE, D_MODEL), lambda i: (i, 0)),
                  pl.BlockSpec((1, D_MODEL), lambda i: (0, 0)),
                  pl.BlockSpec((1, D_MODEL), lambda i: (0, 0))] + r_in,
        out_specs=[pl.BlockSpec((ROW_TILE, D_MODEL), lambda i: (i, 0)),
                   pl.BlockSpec((ROW_TILE, D_PACK), lambda i: (i, 0))] + r_out,
        out_shape=[jax.ShapeDtypeStruct((t, D_MODEL), F32),
                   jax.ShapeDtypeStruct((t, D_PACK), U32)] + r_shape,
        scratch_shapes=r_scratch,
        compiler_params=_params("arbitrary"),
        name=name,
    )(y, w, x, g[None, :], b[None, :], *router_w)


def _rope_table_kernel(pos_ref, freq_ref, c_ref, s1_ref, s2_ref, tok):
    rows = pos_ref.shape[0]
    hr = rows // 2
    lane = lax.broadcasted_iota(jnp.int32, (hr, LANES), 1)
    left = lane < HALF
    pos2 = jnp.where(left, pos_ref[0:hr, :], pos_ref[hr:rows, :]).astype(F32)
    ang = pos2 * freq_ref[...]
    d = lane % ATT_HEAD_DIM
    cos, sin = jnp.cos(ang), jnp.sin(ang)
    half = ROPE_DIM // 2
    tabs = (jnp.where(d < ROPE_DIM, cos, 1.0),
            jnp.where(d < half, -sin, 0.0),
            jnp.where((d >= half) & (d < ROPE_DIM), sin, 0.0))
    for ti, (tab, out) in enumerate(zip(tabs, (c_ref, s1_ref, s2_ref))):
        swapped = pltpu.roll(tab, HALF, 1)
        tok[ti, 0:hr, :] = jnp.where(left, tab, swapped)
        tok[ti, hr:rows, :] = jnp.where(left, swapped, tab)
        for grp, (_, dil) in enumerate(ATT_PATTERNS):
            if dil == 1:
                out[grp] = tok[ti]
            else:
                n = rows // dil
                for r in range(dil):
                    out[grp, r * n:(r + 1) * n, :] = tok[ti, pl.ds(r, n, stride=dil), :]


def _rope_tables(positions):
    t = positions.size
    half = ROPE_DIM // 2
    inv_freq = ROPE_THETA ** (-jnp.arange(0, ROPE_DIM, 2, dtype=F32) / ROPE_DIM)
    d = jnp.arange(LANES) % ATT_HEAD_DIM
    freq = jnp.where(d < ROPE_DIM, inv_freq[d % half], 0.0).astype(F32)[None, :]
    tab = jax.ShapeDtypeStruct((ATT_N_GROUPS, t, LANES), F32)
    ospec = pl.BlockSpec((ATT_N_GROUPS, ROW_TILE, LANES), lambda i: (0, i, 0))
    return pl.pallas_call(
        _rope_table_kernel,
        grid=(t // ROW_TILE,),
        in_specs=[pl.BlockSpec((ROW_TILE, 1), lambda i: (i, 0)),
                  pl.BlockSpec((1, LANES), lambda i: (0, 0))],
        out_specs=[ospec, ospec, ospec],
        out_shape=[tab, tab, tab],
        scratch_shapes=[pltpu.VMEM((3, ROW_TILE, LANES), F32)],
        compiler_params=_params("parallel"),
        name="rope_tables",
    )(positions.reshape(t, 1), freq)


def _qkv_kernel(x_ref, w_ref, c_ref, s1_ref, s2_ref, o_ref, cols):
    xf = _unpack_rows(x_ref[...])
    rows = xf.shape[0]
    n_col = D_MODEL // LANES
    for c in range(n_col):
        cols[c] = xf[:, c * LANES:(c + 1) * LANES]

    def residue_major(dil):
        n = rows // dil
        strips = [jnp.concatenate([cols[c, pl.ds(r, n, stride=dil), :] for r in range(dil)], axis=0)
                  for c in range(n_col)]
        return jnp.concatenate(strips, axis=1).astype(BF16)

    xs = [xf.astype(BF16) if dil == 1 else residue_major(dil) for _, dil in ATT_PATTERNS]
    reps = ATT_OUT_DIM // LANES
    half = ROPE_DIM // 2
    for grp in range(ATT_N_GROUPS):
        c = jnp.concatenate([c_ref[grp]] * reps, axis=1)
        s1 = jnp.concatenate([s1_ref[grp]] * reps, axis=1)
        s2 = jnp.concatenate([s2_ref[grp]] * reps, axis=1)
        for kind in range(3):
            j = grp * 3 + kind
            sl = slice(j * ATT_OUT_DIM, (j + 1) * ATT_OUT_DIM)
            acc = _dot(xs[grp], w_ref[:, sl])
            if kind < 2:
                up = pltpu.roll(acc, ATT_OUT_DIM - half, 1)
                down = pltpu.roll(acc, half, 1)
                acc = acc * c + up * s1 + down * s2
            if kind == 0:
                acc = acc * (ATT_HEAD_DIM ** -0.5)
            o_ref[:, sl] = acc.astype(o_ref.dtype)


def _qkv_proj(xpk, w, tabs):
    t = xpk.shape[0]
    tab_spec = pl.BlockSpec((ATT_N_GROUPS, ROW_TILE, LANES), lambda i: (0, i, 0))
    return pl.pallas_call(
        _qkv_kernel,
        grid=(t // ROW_TILE,),
        in_specs=[pl.BlockSpec((ROW_TILE, D_PACK), lambda i: (i, 0)),
                  _resident((D_MODEL, ATT_QKV_DIM)), tab_spec, tab_spec, tab_spec],
        out_specs=pl.BlockSpec((ROW_TILE, ATT_QKV_DIM), lambda i: (i, 0)),
        out_shape=jax.ShapeDtypeStruct((t, ATT_QKV_DIM), BF16),
        scratch_shapes=[pltpu.VMEM((D_MODEL // LANES, ROW_TILE, LANES), F32)],
        compiler_params=_params("parallel"),
        name="qkv_rope",
    )(xpk, w, *tabs)


def _attn_kernel(q_ref, kp_ref, kc_ref, vp_ref, vc_ref, o_ref, st_ref, *, chained, problems):
    w = ATT_BLOCK
    i = pl.program_id(2)
    qi = lax.broadcasted_iota(jnp.int32, (w, 2 * w), 0)
    kk = lax.broadcasted_iota(jnp.int32, (w, 2 * w), 1)
    band = (kk >= qi) & (kk <= qi + w)
    lane = lax.broadcasted_iota(jnp.int32, (w, LANES), 1)
    lo_mask = lane < HALF

    def part(ref, j):
        if len(ref.shape) == 2:
            return ref[j * w:(j + 1) * w, :]
        return ref[:, j].reshape(w, ref.shape[-1])

    for j in range(problems):
        if chained and j >= 1:
            k, v, has_prev = kc_ref[(j - 1) * w:(j + 1) * w, :], vc_ref[(j - 1) * w:(j + 1) * w, :], True
        else:
            kprev = kp_ref[...] if chained else part(kp_ref, j)
            vprev = vp_ref[...] if chained else part(vp_ref, j)
            k = jnp.concatenate([kprev, part(kc_ref, j)], axis=0)
            v = jnp.concatenate([vprev, part(vc_ref, j)], axis=0)
            has_prev = False
        valid = band if has_prev else band & (kk >= jnp.where(i > 0, 0, w))
        q = part(q_ref, j)
        stats = jnp.zeros((w, LANES), F32)
        zero = jnp.zeros((), q.dtype)
        parts = []
        for p in range(ATT_HEADS // 2):
            sl = slice(p * LANES, (p + 1) * LANES)
            qp, kp, vp = q[:, sl], k[:, sl], v[:, sl]
            outs = []
            for hh in range(2):
                h = 2 * p + hh
                qm = jnp.where(lo_mask if hh == 0 else ~lo_mask, qp, zero)
                s = jnp.where(valid, _dot_nt(qm, kp), NEG_INF)
                m = jnp.max(s, -1, keepdims=True)
                pr = jnp.exp(s - m)
                l = jnp.sum(pr, -1, keepdims=True)
                outs.append(_dot(pr.astype(v.dtype), vp) / l)
                stats = jnp.where(lane == h, m, stats)
                stats = jnp.where(lane == ATT_HEADS + h, l, stats)
            parts.append(jnp.where(lo_mask, outs[0], outs[1]).astype(o_ref.dtype))
        out = jnp.concatenate(parts, axis=1)
        if len(o_ref.shape) == 2:
            o_ref[j * w:(j + 1) * w, :] = out
            st_ref[j * w:(j + 1) * w, :] = stats
        else:
            o_ref[:, j] = out.reshape(o_ref.shape[0], o_ref.shape[2], o_ref.shape[3])
            st_ref[:, j] = stats.reshape(st_ref.shape[0], st_ref.shape[2], st_ref.shape[3])


def _window_attention(qkv, grp, bsz, seq):
    _, dil = ATT_PATTERNS[grp]
    w = ATT_BLOCK
    t = bsz * seq
    chunk = ROW_TILE // dil
    tiles = w // chunk if chunk < w else 1
    span = dil * w
    nb = seq // span
    col0 = grp * 3
    chained = dil == 1
    g = ATT_PROBLEMS if chained else min(ATT_PROBLEMS, dil)

    if chained:
        grid = (bsz, 1, nb // g)
        per_b = seq // (g * w)

        def spec(width, col, prev):
            if prev:
                return pl.BlockSpec((w, width), lambda b, r, i: (b * g * per_b + jnp.maximum(g * i - 1, 0), col))
            return pl.BlockSpec((g * w, width), lambda b, r, i: (b * per_b + i, col))

        qkv_v, o_shape, st_shape = qkv, (t, ATT_OUT_DIM), (t, LANES)
    elif tiles == 1:
        grid = (bsz, dil // g, nb)
        per_b = seq // (g * w)
        stride = span // (g * w)

        def spec(width, col, prev):
            def imap(b, r, i):
                blk = jnp.maximum(i - 1, 0) if prev else i
                return (b * per_b + blk * stride + r, col)
            return pl.BlockSpec((g * w, width), imap)

        qkv_v, o_shape, st_shape = qkv, (t, ATT_OUT_DIM), (t, LANES)
    else:
        grid = (bsz, dil // g, nb)

        def spec(width, col, prev):
            def imap(b, r, i):
                blk = jnp.maximum(i - 1, 0) if prev else i
                return (b, blk, 0, r, 0, col)
            return pl.BlockSpec((None, None, tiles, g, chunk, width), imap)

        lead = (bsz, nb, tiles, dil, chunk)
        qkv_v, o_shape, st_shape = qkv.reshape(*lead, ATT_QKV_DIM), (*lead, ATT_OUT_DIM), (*lead, LANES)

    o, st = pl.pallas_call(
        functools.partial(_attn_kernel, chained=chained, problems=g),
        grid=grid,
        in_specs=[spec(ATT_OUT_DIM, col0, False), spec(ATT_OUT_DIM, col0 + 1, True),
                  spec(ATT_OUT_DIM, col0 + 1, False), spec(ATT_OUT_DIM, col0 + 2, True),
                  spec(ATT_OUT_DIM, col0 + 2, False)],
        out_specs=[spec(ATT_OUT_DIM, 0, False), spec(LANES, 0, False)],
        out_shape=[jax.ShapeDtypeStruct(o_shape, BF16), jax.ShapeDtypeStruct(st_shape, F32)],
        compiler_params=_params("parallel", "parallel", "arbitrary"),
        name=f"window_attn_d{dil}",
    )(qkv_v, qkv_v, qkv_v, qkv_v, qkv_v)
    return o.reshape(t, ATT_OUT_DIM), st.reshape(t, LANES)


def _merge_proj_ln_kernel(o1_ref, o2_ref, o3_ref, s1_ref, s2_ref, s3_ref, w_ref, x_ref,
                          g_ref, b_ref, w2_ref, rb_ref, o_ref, opk_ref, route_ref, cnt_ref, carry, tok_o, tok_s):
    rows = o1_ref.shape[0]
    pairs = ATT_HEADS // 2
    lane = lax.broadcasted_iota(jnp.int32, (rows, LANES), 1)

    for gi, (o_ref_g, s_ref_g) in enumerate(((o2_ref, s2_ref), (o3_ref, s3_ref))):
        dil = ATT_PATTERNS[gi + 1][1]
        n = rows // dil
        for r in range(dil):
            tok_s[gi, pl.ds(r, n, stride=dil), :] = s_ref_g[r * n:(r + 1) * n, :]
            for p in range(pairs):
                tok_o[gi, p, pl.ds(r, n, stride=dil), :] = o_ref_g[r * n:(r + 1) * n, p * LANES:(p + 1) * LANES].astype(F32)

    sts = [s1_ref[...], tok_s[0], tok_s[1]]
    mx = jnp.maximum(jnp.maximum(sts[0], sts[1]), sts[2])
    wgts = [pltpu.roll(s, LANES - ATT_HEADS, 1) * jnp.exp(s - mx) for s in sts]
    den = wgts[0] + wgts[1] + wgts[2]
    den = jnp.where(lane < ATT_HEADS, den, 1.0)
    coefs = [wg / den for wg in wgts]
    parts = []
    for p in range(pairs):
        sl = slice(p * LANES, (p + 1) * LANES)
        head_of_lane = jnp.where(lane < HALF, 2 * p, 2 * p + 1)
        acc = jnp.take_along_axis(coefs[0], head_of_lane, axis=1) * o1_ref[:, sl].astype(F32)
        for gi in range(1, ATT_N_GROUPS):
            acc = acc + jnp.take_along_axis(coefs[gi], head_of_lane, axis=1) * tok_o[gi - 1, p]
        parts.append(acc.astype(BF16))
    mix = _dot(jnp.concatenate(parts, axis=1), w_ref[...])
    _ln_route_epilogue(DEEPNORM_ALPHA * x_ref[...] + mix, g_ref, b_ref, w2_ref, rb_ref,
                       o_ref, opk_ref, route_ref, cnt_ref, carry)


def _merge_proj_ln(os_, sts, w, x, g, b, router_w):
    t = x.shape[0]
    tm = ROW_TILE
    r_in, r_out, r_shape, r_scratch = _route_specs(t)
    ospec = pl.BlockSpec((tm, ATT_OUT_DIM), lambda i: (i, 0))
    sspec = pl.BlockSpec((tm, LANES), lambda i: (i, 0))
    xspec = pl.BlockSpec((tm, D_MODEL), lambda i: (i, 0))
    vspec = pl.BlockSpec((1, D_MODEL), lambda i: (0, 0))
    return pl.pallas_call(
        _merge_proj_ln_kernel,
        grid=(t // tm,),
        in_specs=[ospec] * 3 + [sspec] * 3 + [_resident((ATT_OUT_DIM, D_MODEL)), xspec, vspec, vspec] + r_in,
        out_specs=[xspec, pl.BlockSpec((tm, D_PACK), lambda i: (i, 0))] + r_out,
        out_shape=[jax.ShapeDtypeStruct((t, D_MODEL), F32), jax.ShapeDtypeStruct((t, D_PACK), U32)] + r_shape,
        scratch_shapes=r_scratch + [pltpu.VMEM((ATT_N_GROUPS - 1, ATT_HEADS // 2, tm, LANES), F32),
                                    pltpu.VMEM((ATT_N_GROUPS - 1, tm, LANES), F32)],
        compiler_params=_params("arbitrary"),
        name="attn_merge_proj_ln",
    )(*os_, *sts, w, x, g[None, :], b[None, :], *router_w)


def _route_rows(x, w2_ref, b_ref, carry):
    xhi = x.astype(BF16)
    xlo = (x - xhi.astype(F32)).astype(BF16)
    w2 = w2_ref[...]
    hi2 = _dot(xhi, w2)
    logits = hi2[:, :LANES] + hi2[:, LANES:] + _dot(xlo, w2[:, :LANES]) + b_ref[...]
    rows = logits.shape[0]
    lane = lax.broadcasted_iota(jnp.int32, (rows, LANES), 1)
    big = jnp.int32(LANES)

    def top1(vals, mask):
        v = jnp.where(mask, vals, -jnp.inf)
        m = jnp.max(v, -1, keepdims=True)
        idx = jnp.min(jnp.where(v == m, lane, big), -1, keepdims=True)
        return v, m, idx

    gmask = lane < MOE_N_GROUPS
    gv, gm, gidx = top1(logits, gmask)
    g_w = 1.0 / jnp.sum(jnp.exp(gv - gm), -1, keepdims=True)
    e_lo = MOE_N_GROUPS + gidx * MOE_EPG
    emask = (lane >= e_lo) & (lane < e_lo + MOE_EPG)
    ev, m1, i1 = top1(logits, emask)
    zsum = jnp.sum(jnp.exp(ev - m1), -1, keepdims=True)
    _, m2, i2 = top1(logits, emask & (lane != i1))
    p1 = 1.0 / zsum
    p2 = jnp.exp(m2 - m1) / zsum
    tot = p1 + p2
    e1 = i1 - MOE_N_GROUPS
    e2 = i2 - MOE_N_GROUPS

    oh1 = jnp.where(lane == e1, 1.0, 0.0)
    oh2 = jnp.where(lane == e2, 1.0, 0.0)
    oh = oh1 + oh2
    ri = lax.broadcasted_iota(jnp.int32, (rows, rows), 0)
    ci = lax.broadcasted_iota(jnp.int32, (rows, rows), 1)
    strict = jnp.where(ri > ci, 1.0, 0.0).astype(BF16)
    before = _dot(strict, oh.astype(BF16)) + carry[...]
    rank1 = jnp.sum(oh1 * before, -1, keepdims=True)
    rank2 = jnp.sum(oh2 * before, -1, keepdims=True)
    carry[...] = carry[...] + jnp.sum(oh, 0, keepdims=True)

    vals = [e1.astype(F32), e2.astype(F32), g_w * (p1 / tot), g_w * (p2 / tot), rank1, rank2]
    out = jnp.zeros((rows, LANES), F32)
    for j, val in enumerate(vals):
        out = jnp.where(lane == j, val, out)
    return out


def _router_weights(w_rg, b_rg, w_re, b_re):
    n_log = MOE_N_GROUPS + MOE_N_EXPERTS
    w = jnp.pad(jnp.concatenate([w_rg, w_re], axis=1), ((0, 0), (0, LANES - n_log)))
    whi = w.astype(BF16)
    w2 = jnp.concatenate([whi, (w - whi.astype(F32)).astype(BF16)], axis=1)
    bias = jnp.pad(jnp.concatenate([b_rg, b_re]), (0, LANES - n_log))[None, :]
    return w2, bias


def _ln_route_epilogue(r, g_ref, b_ref, w2_ref, rb_ref, o_ref, opk_ref, route_ref, cnt_ref, carry):
    @pl.when(pl.program_id(0) == 0)
    def _():
        carry[...] = jnp.zeros(carry.shape, F32)

    out = _layer_norm(r, g_ref[...], b_ref[...])
    o_ref[...] = out
    opk_ref[...] = _pack_rows(out)
    route_ref[...] = _route_rows(out, w2_ref, rb_ref, carry)
    cnt_ref[...] = carry[...]


def _expert_kernel(meta_ref, x_ref, wg_ref, wu_ref, wd_ref, o_ref, wg16, wu16, wd16):
    i = pl.program_id(0)
    n_blocks = pl.num_programs(0)

    @pl.when(i < meta_ref[n_blocks])
    def _():
        @pl.when((i == 0) | (meta_ref[i] != meta_ref[jnp.maximum(i - 1, 0)]))
        def _():
            wg16[...] = wg_ref[...].astype(BF16)
            wu16[...] = wu_ref[...].astype(BF16)
            wd16[...] = wd_ref[...].astype(BF16)

        rows = lax.broadcasted_iota(jnp.int32, x_ref.shape, 0)
        xw = jnp.where(rows < meta_ref[n_blocks + 1 + i], x_ref[...], jnp.uint32(0))
        x = _unpack_rows(xw).astype(BF16)
        h = _silu(_dot(x, wg16[...])) * _dot(x, wu16[...])
        o_ref[...] = _pack_rows(_dot(h.astype(BF16), wd16[...]))


def _expert_mlp(meta, x_rows, w_gate, w_up, w_down, layer):
    n_rows = x_rows.shape[0]
    n_blocks = n_rows // MOE_ROW_BLOCK
    tb = MOE_ROW_BLOCK
    grid_spec = pltpu.PrefetchScalarGridSpec(
        num_scalar_prefetch=1,
        grid=(n_blocks,),
        in_specs=[pl.BlockSpec((tb, D_PACK), lambda i, meta: (i, 0)),
                  pl.BlockSpec((None, None, D_MODEL, MOE_HIDDEN), lambda i, meta: (layer, meta[i], 0, 0)),
                  pl.BlockSpec((None, None, D_MODEL, MOE_HIDDEN), lambda i, meta: (layer, meta[i], 0, 0)),
                  pl.BlockSpec((None, None, MOE_HIDDEN, D_MODEL), lambda i, meta: (layer, meta[i], 0, 0))],
        out_specs=pl.BlockSpec((tb, D_PACK), lambda i, meta: (i, 0)),
        scratch_shapes=[pltpu.VMEM((D_MODEL, MOE_HIDDEN), BF16), pltpu.VMEM((D_MODEL, MOE_HIDDEN), BF16),
                        pltpu.VMEM((MOE_HIDDEN, D_MODEL), BF16)],
    )
    return pl.pallas_call(
        _expert_kernel,
        grid_spec=grid_spec,
        out_shape=jax.ShapeDtypeStruct((n_rows, D_PACK), U32),
        compiler_params=_params("arbitrary"),
        name="moe_experts",
    )(meta, x_rows, w_gate, w_up, w_down)


def _combine_ln_kernel(y0_ref, y1_ref, r_ref, x_ref, g_ref, b_ref, o_ref, opk_ref):
    route = r_ref[...]
    g0 = route[:, 2:3]
    g1 = route[:, 3:4]
    ffn = g0 * _unpack_rows(y0_ref[...]) + g1 * _unpack_rows(y1_ref[...])
    out = _layer_norm(DEEPNORM_ALPHA * x_ref[...] + ffn, g_ref[...], b_ref[...])
    o_ref[...] = out
    opk_ref[...] = _pack_rows(out)


def _combine_ln(y0, y1, route, x, g, b):
    t = x.shape[0]
    rows = 2 * ROW_TILE
    xspec = pl.BlockSpec((rows, D_MODEL), lambda i: (i, 0))
    pspec = pl.BlockSpec((rows, D_PACK), lambda i: (i, 0))
    vspec = pl.BlockSpec((1, D_MODEL), lambda i: (0, 0))
    return pl.pallas_call(
        _combine_ln_kernel,
        grid=(t // rows,),
        in_specs=[pspec, pspec, pl.BlockSpec((rows, LANES), lambda i: (i, 0)), xspec, vspec, vspec],
        out_specs=[xspec, pspec],
        out_shape=[jax.ShapeDtypeStruct((t, D_MODEL), F32), jax.ShapeDtypeStruct((t, D_PACK), U32)],
        compiler_params=_params("parallel"),
        name="moe_combine_ln",
    )(y0, y1, route, x, g[None, :], b[None, :])


def _positions_kernel(r_ref, ps_ref, p0_ref, p1_ref):
    route = r_ref[...]
    lane = lax.broadcasted_iota(jnp.int32, route.shape, 1)
    starts = ps_ref[...]
    out = jnp.zeros(route.shape, F32)
    for k in range(MOE_TOP_K):
        eid = route[:, k:k + 1].astype(jnp.int32)
        pos = jnp.sum(jnp.where(lane == eid, starts, 0.0), -1, keepdims=True) + route[:, 4 + k:5 + k]
        out = jnp.where(lane == k, pos, out)
    for c in range(route.shape[0] // LANES):
        tile = out[c * LANES:(c + 1) * LANES, :].T.astype(jnp.int32)
        p0_ref[:, c * LANES:(c + 1) * LANES] = tile[0:1, :]
        p1_ref[:, c * LANES:(c + 1) * LANES] = tile[1:2, :]


def _positions(route, pad_start):
    t = route.shape[0]
    rows = 4 * ROW_TILE
    starts = jnp.pad(pad_start.astype(F32), (0, LANES - MOE_N_EXPERTS))[None, :]
    ospec = pl.BlockSpec((1, rows), lambda i: (0, i))
    return pl.pallas_call(
        _positions_kernel,
        grid=(t // rows,),
        in_specs=[pl.BlockSpec((rows, LANES), lambda i: (i, 0)), pl.BlockSpec((1, LANES), lambda i: (0, 0))],
        out_specs=[ospec, ospec],
        out_shape=[jax.ShapeDtypeStruct((1, t), jnp.int32)] * 2,
        compiler_params=_params("parallel"),
        name="moe_positions",
    )(route, starts)


def _moe(x, xpk, route, cnt, w_gate, w_up, w_down, layer, g, b):
    t = x.shape[0]
    tb = MOE_ROW_BLOCK
    n_assign = t * MOE_TOP_K
    n_blocks = n_assign // tb + MOE_N_EXPERTS
    n_rows = n_blocks * tb
    counts = cnt[0, :MOE_N_EXPERTS].astype(jnp.int32)
    padded = (counts + tb - 1) // tb * tb
    pad_end = jnp.cumsum(padded)
    pos0, pos1 = _positions(route, pad_end - padded)
    block_start = jnp.arange(n_blocks, dtype=jnp.int32) * tb
    block_e = jnp.minimum(jnp.sum((pad_end[None, :] <= block_start[:, None]).astype(jnp.int32), -1),
                          MOE_N_EXPERTS - 1)
    valid = jnp.clip(pad_end[block_e] - padded[block_e] + counts[block_e] - block_start, 0, tb)
    meta = jnp.concatenate([block_e, pad_end[-1:] // tb, valid]).astype(jnp.int32)
    x_rows = _row_scatter(xpk, (pos0, pos1), n_rows)
    y_rows = _expert_mlp(meta, x_rows, w_gate, w_up, w_down, layer)
    y0 = _row_gather(y_rows, pos0)
    y1 = _row_gather(y_rows, pos1)
    return _combine_ln(y0, y1, route, x, g, b)


def _ssd_layer(x, xpk, w_in, conv_w, conv_b, dt_bias, a_log, d_skip, norm_w, w_out, g, b, router_w, bsz, seq):
    w_zxbc = w_in[:, :SSD_D_INNER + SSD_CONV_DIM].astype(BF16)
    w_dt = jnp.pad(w_in[:, SSD_D_INNER + SSD_CONV_DIM:], ((0, 0), (0, LANES - SSD_N_HEADS))).astype(BF16)
    zxbc, dt_raw = _in_proj(xpk, w_zxbc, w_dt, conv_w, conv_b, bsz, seq)
    y = _ssd_scan(zxbc, dt_raw, conv_w, conv_b, dt_bias, a_log, d_skip, norm_w, bsz, seq)
    return _proj_ln(y, w_out.astype(BF16), x, g, b, router_w, "ssd_out_proj_ln")


def _attn_layer(x, xpk, tabs, w_qkv, w_o, g, b, router_w, bsz, seq):
    qkv = _qkv_proj(xpk, w_qkv.astype(BF16), tabs)
    os_, sts = [], []
    for grp in range(ATT_N_GROUPS):
        o, st = _window_attention(qkv, grp, bsz, seq)
        os_.append(o)
        sts.append(st)
    return _merge_proj_ln(os_, sts, w_o.astype(BF16), x, g, b, router_w)


def kernel(x, positions, ssd_w_in, ssd_conv_w, ssd_conv_b, ssd_dt_bias, ssd_a_log, ssd_d, ssd_norm_w, ssd_w_out,
           attn_w_qkv, attn_w_o, ln_g, ln_b, moe_w_router_group, moe_b_router_group, moe_w_router_expert,
           moe_b_router_expert, moe_w_gate, moe_w_up, moe_w_down):
    bsz, seq, d = x.shape
    t = bsz * seq
    h = x.reshape(t, d)
    hpk = h
    tabs = _rope_tables(positions)
    for i in range(DEPTH):
        j = i // N_MIXERS
        router_w = _router_weights(moe_w_router_group[i], moe_b_router_group[i], moe_w_router_expert[i],
                                   moe_b_router_expert[i])
        if i % N_MIXERS == 0:
            h, hpk, route, cnt = _ssd_layer(h, hpk, ssd_w_in[j], ssd_conv_w[j], ssd_conv_b[j], ssd_dt_bias[j],
                                            ssd_a_log[j], ssd_d[j], ssd_norm_w[j], ssd_w_out[j], ln_g[i, 0],
                                            ln_b[i, 0], router_w, bsz, seq)
        else:
            h, hpk, route, cnt = _attn_layer(h, hpk, tabs, attn_w_qkv[j], attn_w_o[j], ln_g[i, 0], ln_b[i, 0],
                                             router_w, bsz, seq)
        h, hpk = _moe(h, hpk, route, cnt, moe_w_gate, moe_w_up, moe_w_down, i, ln_g[i, 1], ln_b[i, 1])
    return h.reshape(bsz, seq, d)
```

```python
import functools

import jax
import jax.numpy as jnp
from jax import lax
from jax.experimental import pallas as pl
from jax.experimental.pallas import tpu as pltpu
from jax.experimental.pallas import tpu_sc as plsc

F32 = jnp.float32
BF16 = jnp.bfloat16
U32 = jnp.uint32

D_MODEL = 1024
D_PACK = D_MODEL // 2
DEPTH = 4
N_MIXERS = 2

SSD_D_INNER = 2048
SSD_HEAD_DIM = 64
SSD_N_HEADS = 32
SSD_N_GROUPS = 4
SSD_D_STATE = 128
SSD_D_CONV = 4
SSD_CHUNK = 128
SSD_GN = SSD_N_GROUPS * SSD_D_STATE
SSD_CONV_DIM = SSD_D_INNER + 2 * SSD_GN
SSD_GROUP_COLS = SSD_D_INNER // SSD_N_GROUPS
IN_PROJ_COLS = 1024
SSD_PAIR_UNROLL = 4

ATT_HEAD_DIM = 64
ATT_HEADS = 8
ATT_PATTERNS = ((128, 1), (512, 4), (2048, 16))
ATT_N_GROUPS = 3
ATT_OUT_DIM = ATT_HEADS * ATT_HEAD_DIM
ATT_QKV_DIM = ATT_N_GROUPS * 3 * ATT_OUT_DIM
ATT_BLOCK = 128
ATT_PROBLEMS = 8
ROPE_THETA = 500000.0
ROPE_DIM = 16

MOE_N_GROUPS = 4
MOE_EPG = 8
MOE_N_EXPERTS = 32
MOE_TOP_K = 2
MOE_HIDDEN = 512
MOE_ROW_BLOCK = 512

DEEPNORM_ALPHA = (2 * DEPTH) ** 0.25
LN_EPS = 1e-5
RMS_EPS = 1e-5
NEG_INF = -1e30

LANES = 128
HALF = LANES // 2
VMEM_LIMIT = 56 * 1024 * 1024

ROW_TILE = 512


def _params(*sem):
    return pltpu.CompilerParams(dimension_semantics=sem, vmem_limit_bytes=VMEM_LIMIT)


def _silu(v):
    h = 0.5 * v
    return h + h * jnp.tanh(h)


def _layer_norm(r, g, b):
    mu = jnp.mean(r, -1, keepdims=True)
    d = r - mu
    var = jnp.mean(d * d, -1, keepdims=True)
    return d * lax.rsqrt(var + LN_EPS) * g + b


def _split3(v):
    hi = v.astype(BF16)
    r1 = v - hi.astype(F32)
    mid = r1.astype(BF16)
    lo = (r1 - mid.astype(F32)).astype(BF16)
    return hi, mid, lo


def _dot(a, b):
    return jnp.dot(a, b, preferred_element_type=F32)


def _dot_nt(a, b):
    return lax.dot_general(a, b, (((1,), (1,)), ((), ())), preferred_element_type=F32)


def _dot_tn(a, b):
    return lax.dot_general(a, b, (((0,), (0,)), ((), ())), preferred_element_type=F32)


def _pack_rows(v):
    r = pltpu.bitcast(v.astype(BF16).astype(F32), U32)
    return r[:, :D_PACK] | (r[:, D_PACK:] >> 16)


def _unpack_rows(p):
    hi = pltpu.bitcast(p & jnp.uint32(0xFFFF0000), F32)
    lo = pltpu.bitcast(p << 16, F32)
    return jnp.concatenate([hi, lo], axis=1)


def _row_gather(data, idx):
    n = idx.shape[1]
    d = data.shape[1]
    window = LANES
    dc = d // 2
    mesh = plsc.VectorSubcoreMesh(core_axis_name="core", subcore_axis_name="subcore")

    @functools.partial(pl.kernel, out_type=jax.ShapeDtypeStruct((n, d), data.dtype), mesh=mesh)
    def gather(x_hbm, i_hbm, o_hbm):
        for c in range(d // dc):
            def body(i_vmem, o_vmem, c=c):
                pltpu.sync_copy(x_hbm.at[i_vmem.at[0], pl.ds(c * dc, dc)], o_vmem)

            pltpu.emit_pipeline(
                body,
                grid=(n // window,),
                in_specs=[pl.BlockSpec((1, window), lambda i: (0, i))],
                out_specs=[pl.BlockSpec((window, dc), lambda i, c=c: (i, c))],
                core_axis_name=("core", "subcore"),
                dimension_semantics=(pltpu.PARALLEL,),
            )(i_hbm, o_hbm)

    return gather(data, idx)


def _row_scatter(data, idxs, n_rows):
    t, d = data.shape
    window = LANES
    dc = d // 2
    mesh = plsc.VectorSubcoreMesh(core_axis_name="core", subcore_axis_name="subcore")

    @functools.partial(pl.kernel, out_type=jax.ShapeDtypeStruct((n_rows, d), data.dtype), mesh=mesh)
    def scatter(x_hbm, *refs):
        o_hbm = refs[-1]
        for i_hbm in refs[:-1]:
            for c in range(d // dc):
                def body(x_vmem, i_vmem, c=c):
                    pltpu.sync_copy(x_vmem, o_hbm.at[i_vmem.at[0], pl.ds(c * dc, dc)])

                pltpu.emit_pipeline(
                    body,
                    grid=(t // window,),
                    in_specs=[pl.BlockSpec((window, dc), lambda i, c=c: (i, c)),
                              pl.BlockSpec((1, window), lambda i: (0, i))],
                    out_specs=[],
                    core_axis_name=("core", "subcore"),
                    dimension_semantics=(pltpu.PARALLEL,),
                )(x_hbm, i_hbm)

    return scatter(data, *idxs)


def _resident(shape):
    return pl.BlockSpec(shape, lambda *_: (0,) * len(shape), pipeline_mode=pl.Buffered(1))


def _in_proj_kernel(x_ref, w_ref, wdt_ref, cw_ref, cb_ref, o_ref, dt_ref, ext, tail):
    rows = x_ref.shape[0]
    tn = IN_PROJ_COLS
    n_z = SSD_D_INNER // tn

    @pl.when(pl.program_id(1) == 0)
    def _():
        tail[...] = jnp.zeros(tail.shape, F32)

    x = x_ref[...]
    x = (_unpack_rows(x) if x.dtype == U32 else x).astype(BF16)
    for n in range(n_z):
        sl = slice(n * tn, (n + 1) * tn)
        o_ref[:, sl] = _dot(x, w_ref[:, sl]).astype(o_ref.dtype)
    for n in range(n_z + SSD_D_INNER // tn, w_ref.shape[1] // tn):
        sl = slice(n * tn, (n + 1) * tn)
        o_ref[:, sl] = _dot(x, w_ref[:, sl]).astype(o_ref.dtype)
    for j in range(SSD_D_INNER // tn):
        sl = slice((n_z + j) * tn, (n_z + j + 1) * tn)
        cl = slice(j * tn, (j + 1) * tn)
        ext[j, 0:8, :] = tail[j]
        ext[j, 8:8 + rows, :] = _dot(x, w_ref[:, sl])
        tail[j] = ext[j, rows:rows + 8, :]
        acc = ext[j, 8:8 + rows, :] * cw_ref[3:4, cl] + cb_ref[:, cl]
        for k in range(SSD_D_CONV - 1):
            acc = acc + ext[j, 5 + k:5 + k + rows, :] * cw_ref[k:k + 1, cl]
        o_ref[:, sl] = _silu(acc).astype(o_ref.dtype)
    dt_ref[...] = _dot(x, wdt_ref[...])


def _in_proj(xpk, w_zxbc, w_dt, conv_w, conv_b, bsz, seq):
    t = xpk.shape[0]
    n = w_zxbc.shape[1]
    tiles = seq // ROW_TILE
    cw = jnp.pad(conv_w[:, :SSD_D_INNER], ((0, 8 - SSD_D_CONV), (0, 0)))
    conv_b = conv_b[:SSD_D_INNER]
    n_conv = SSD_D_INNER // IN_PROJ_COLS
    return pl.pallas_call(
        _in_proj_kernel,
        grid=(bsz, tiles),
        in_specs=[pl.BlockSpec((ROW_TILE, xpk.shape[1]), lambda b, s: (b * tiles + s, 0)),
                  _resident((D_MODEL, n)), _resident((D_MODEL, LANES)),
                  _resident((8, SSD_D_INNER)), _resident((1, SSD_D_INNER))],
        out_specs=[pl.BlockSpec((ROW_TILE, n), lambda b, s: (b * tiles + s, 0)),
                   pl.BlockSpec((ROW_TILE, LANES), lambda b, s: (b * tiles + s, 0))],
        out_shape=[jax.ShapeDtypeStruct((t, n), BF16), jax.ShapeDtypeStruct((t, LANES), F32)],
        scratch_shapes=[pltpu.VMEM((n_conv, ROW_TILE + 8, IN_PROJ_COLS), F32),
                        pltpu.VMEM((n_conv, 8, IN_PROJ_COLS), F32)],
        compiler_params=_params("arbitrary", "arbitrary"),
        name="ssd_in_proj",
    )(xpk, w_zxbc, w_dt, cw, conv_b[None, :])


def _ssd_kernel(z_ref, xs_ref, bc_ref, dt_ref, dtb_ref, alog_ref, dsk_ref, nw_ref, ex_ref, cw_ref, cbias_ref, o_ref,
                state, cs_cols, cs_rows, dt_x, cb_all, y_off, y_grp, xd_all, cd_all, bc_ext, bc_s):
    q = SSD_CHUNK
    pairs = SSD_GROUP_COLS // LANES

    @pl.when(pl.program_id(1) == 0)
    def _():
        state[...] = jnp.zeros(state.shape, F32)
        bc_ext[0:8, :] = jnp.zeros((8, 2 * SSD_GN), F32)

    bc_ext[8:8 + q, :] = bc_ref[...].astype(F32)
    acc = bc_ext[8:8 + q, :] * cw_ref[3:4, :] + cbias_ref[...]
    for k in range(SSD_D_CONV - 1):
        acc = acc + bc_ext[5 + k:5 + k + q, :] * cw_ref[k:k + 1, :]
    bc_s[...] = _silu(acc).astype(BF16)
    bc_ext[0:8, :] = bc_ext[q:q + 8, :]

    pre = dt_ref[...] + dtb_ref[...]
    dt = jnp.maximum(pre, 0.0) + jnp.log(1.0 + jnp.exp(-jnp.abs(pre)))
    a = -jnp.exp(alog_ref[...])
    row = lax.broadcasted_iota(jnp.int32, (q, q), 0)
    col = lax.broadcasted_iota(jnp.int32, (q, q), 1)
    causal = row >= col
    lo_mask = col < HALF
    tri = jnp.where(causal, 1.0, 0.0).astype(BF16)
    cs3 = _dot(tri, jnp.concatenate(_split3(dt * a), axis=1))
    cs = cs3[:, :LANES] + cs3[:, LANES:2 * LANES] + cs3[:, 2 * LANES:]
    cs_rows[...] = cs.T
    for h in range(SSD_N_HEADS):
        cs_cols[h] = jnp.broadcast_to(cs[:, h:h + 1], (q, q))
    dt3 = _dot(jnp.concatenate(_split3(dt), axis=0), ex_ref[...])
    dt_x[...] = dt3[:q] + dt3[q:2 * q] + dt3[2 * q:]
    for g in range(SSD_N_GROUPS):
        bg = bc_s[:, g * SSD_D_STATE:(g + 1) * SSD_D_STATE]
        cg = bc_s[:, SSD_GN + g * SSD_D_STATE:SSD_GN + (g + 1) * SSD_D_STATE]
        cb_all[g] = _dot_nt(cg, bg)
        y_off[:, g * SSD_GROUP_COLS:(g + 1) * SSD_GROUP_COLS] = _dot(cg, state[g].astype(BF16))

    def group_body(g, carry):
        cb = cb_all[g]

        def pair_body(pp, ssq):
            p = g * pairs + pp
            h0 = 2 * p
            x0 = pl.multiple_of(p * LANES, LANES)
            l0 = pl.multiple_of(pp * LANES, LANES)
            cols = (cs_cols[h0], cs_cols[h0 + 1])
            csx = jnp.where(lo_mask, cols[0], cols[1])
            xp = xs_ref[:, pl.ds(x0, LANES)].astype(F32)
            xdt = xp * dt_x[:, pl.ds(x0, LANES)]
            xdt16 = xdt.astype(BF16)
            last = csx[q - 1:q, :]
            halves = []
            for hh in range(2):
                diff = cols[hh] - cs_rows[pl.ds(h0 + hh, 1), :]
                decay = jnp.exp(jnp.where(causal, diff, -jnp.inf))
                halves.append(_dot((cb * decay).astype(BF16), xdt16))
            y = jnp.where(lo_mask, halves[0], halves[1])
            y = y + y_off[:, pl.ds(x0, LANES)] * jnp.exp(csx) + xp * dsk_ref[:, pl.ds(x0, LANES)]
            y = y * _silu(z_ref[:, pl.ds(x0, LANES)].astype(F32))
            y_grp[:, pl.ds(l0, LANES)] = y
            xd_all[:, pl.ds(x0, LANES)] = (xdt * jnp.exp(last - csx)).astype(BF16)
            cd_all[:, pl.ds(x0, LANES)] = jnp.exp(last)
            return ssq + jnp.sum(y * y, -1, keepdims=True)

        ssq = lax.fori_loop(0, pairs, pair_body, jnp.zeros((q, 1), F32), unroll=SSD_PAIR_UNROLL)
        inv = lax.rsqrt(ssq * (1.0 / SSD_GROUP_COLS) + RMS_EPS)
        g0 = pl.multiple_of(g * SSD_GROUP_COLS, SSD_GROUP_COLS)
        o_ref[:, pl.ds(g0, SSD_GROUP_COLS)] = (
            y_grp[...] * inv * nw_ref[:, pl.ds(g0, SSD_GROUP_COLS)]).astype(o_ref.dtype)
        return carry

    lax.fori_loop(0, SSD_N_GROUPS, group_body, 0)

    for g in range(SSD_N_GROUPS):
        gs = slice(g * SSD_GROUP_COLS, (g + 1) * SSD_GROUP_COLS)
        bg = bc_s[:, g * SSD_D_STATE:(g + 1) * SSD_D_STATE]
        state[g] = state[g] * cd_all[:, gs] + _dot_tn(bg, xd_all[:, gs])


def _ssd_scan(zxbc, dt_raw, conv_w, conv_b, dt_bias, a_log, d_skip, norm_w, bsz, seq):
    t = bsz * seq
    nc = seq // SSD_CHUNK
    q = SSD_CHUNK
    pad_h = LANES - SSD_N_HEADS
    dtb = jnp.pad(dt_bias, (0, pad_h))[None, :]
    alog = jnp.pad(a_log, (0, pad_h))[None, :]
    dsk = jnp.repeat(d_skip, SSD_HEAD_DIM)[None, :]
    nw = norm_w[None, :]
    expand = (jnp.arange(LANES)[:, None] == jnp.arange(SSD_D_INNER)[None, :] // SSD_HEAD_DIM).astype(BF16)
    cw_bc = jnp.pad(conv_w[:, SSD_D_INNER:], ((0, 8 - SSD_D_CONV), (0, 0)))
    cb_bc = conv_b[None, SSD_D_INNER:]

    def const(shape):
        return pl.BlockSpec(shape, lambda b, c: (0, 0))

    return pl.pallas_call(
        _ssd_kernel,
        grid=(bsz, nc),
        in_specs=[pl.BlockSpec((q, SSD_D_INNER), lambda b, c: (b * nc + c, 0)),
                  pl.BlockSpec((q, SSD_D_INNER), lambda b, c: (b * nc + c, 1)),
                  pl.BlockSpec((q, 2 * SSD_GN), lambda b, c: (b * nc + c, 4)),
                  pl.BlockSpec((q, LANES), lambda b, c: (b * nc + c, 0)),
                  const((1, LANES)), const((1, LANES)),
                  const((1, SSD_D_INNER)), const((1, SSD_D_INNER)), const((LANES, SSD_D_INNER)),
                  const((8, 2 * SSD_GN)), const((1, 2 * SSD_GN))],
        out_specs=pl.BlockSpec((q, SSD_D_INNER), lambda b, c: (b * nc + c, 0)),
        out_shape=jax.ShapeDtypeStruct((t, SSD_D_INNER), BF16),
        scratch_shapes=[pltpu.VMEM((SSD_N_GROUPS, SSD_D_STATE, SSD_GROUP_COLS), F32),
                        pltpu.VMEM((SSD_N_HEADS, q, q), F32),
                        pltpu.VMEM((LANES, q), F32),
                        pltpu.VMEM((q, SSD_D_INNER), F32),
                        pltpu.VMEM((SSD_N_GROUPS, q, q), F32),
                        pltpu.VMEM((q, SSD_D_INNER), F32),
                        pltpu.VMEM((q, SSD_GROUP_COLS), F32),
                        pltpu.VMEM((q, SSD_D_INNER), BF16),
                        pltpu.VMEM((1, SSD_D_INNER), F32),
                        pltpu.VMEM((q + 8, 2 * SSD_GN), F32),
                        pltpu.VMEM((q, 2 * SSD_GN), BF16)],
        compiler_params=_params("arbitrary", "arbitrary"),
        name="ssd_scan",
    )(zxbc, zxbc, zxbc, dt_raw, dtb, alog, dsk, nw, expand, cw_bc, cb_bc)


def _proj_ln_kernel(y_ref, w_ref, x_ref, g_ref, b_ref, w2_ref, rb_ref, o_ref, opk_ref, route_ref, cnt_ref, carry):
    mix = _dot(y_ref[...].astype(BF16), w_ref[...])
    _ln_route_epilogue(DEEPNORM_ALPHA * x_ref[...] + mix, g_ref, b_ref, w2_ref, rb_ref,
                       o_ref, opk_ref, route_ref, cnt_ref, carry)


def _route_specs(t):
    ins = [_resident((D_MODEL, 2 * LANES)), pl.BlockSpec((1, LANES), lambda i: (0, 0))]
    outs = [pl.BlockSpec((ROW_TILE, LANES), lambda i: (i, 0)), pl.BlockSpec((1, LANES), lambda i: (0, 0))]
    shapes = [jax.ShapeDtypeStruct((t, LANES), F32), jax.ShapeDtypeStruct((1, LANES), F32)]
    return ins, outs, shapes, [pltpu.VMEM((1, LANES), F32)]


def _proj_ln(y, w, x, g, b, router_w, name):
    t, k = y.shape
    r_in, r_out, r_shape, r_scratch = _route_specs(t)
    return pl.pallas_call(
        _proj_ln_kernel,
        grid=(t // ROW_TILE,),
        in_specs=[pl.BlockSpec((ROW_TILE, k), lambda i: (i, 0)),
                  _resident((k, D_MODEL)),
                  pl.BlockSpec((ROW_TILE, D_MODEL), lambda i: (i, 0)),
                  pl.BlockSpec((1, D_MODEL), lambda i: (0, 0)),
                  pl.BlockSpec((1, D_MODEL), lambda i: (0, 0))] + r_in,
        out_specs=[pl.BlockSpec((ROW_TILE, D_MODEL), lambda i: (i, 0)),
                   pl.BlockSpec((ROW_TILE, D_PACK), lambda i: (i, 0))] + r_out,
        out_shape=[jax.ShapeDtypeStruct((t, D_MODEL), F32),
                   jax.ShapeDtypeStruct((t, D_PACK), U32)] + r_shape,
        scratch_shapes=r_scratch,
        compiler_params=_params("arbitrary"),
        name=name,
    )(y, w, x, g[None, :], b[None, :], *router_w)


def _rope_table_kernel(pos_ref, freq_ref, c_ref, s1_ref, s2_ref, tok):
    rows = pos_ref.shape[0]
    hr = rows // 2
    lane = lax.broadcasted_iota(jnp.int32, (hr, LANES), 1)
    left = lane < HALF
    pos2 = jnp.where(left, pos_ref[0:hr, :], pos_ref[hr:rows, :]).astype(F32)
    ang = pos2 * freq_ref[...]
    d = lane % ATT_HEAD_DIM
    cos, sin = jnp.cos(ang), jnp.sin(ang)
    half = ROPE_DIM // 2
    tabs = (jnp.where(d < ROPE_DIM, cos, 1.0),
            jnp.where(d < half, -sin, 0.0),
            jnp.where((d >= half) & (d < ROPE_DIM), sin, 0.0))
    for ti, (tab, out) in enumerate(zip(tabs, (c_ref, s1_ref, s2_ref))):
        swapped = pltpu.roll(tab, HALF, 1)
        tok[ti, 0:hr, :] = jnp.where(left, tab, swapped)
        tok[ti, hr:rows, :] = jnp.where(left, swapped, tab)
        for grp, (_, dil) in enumerate(ATT_PATTERNS):
            if dil == 1:
                out[grp] = tok[ti]
            else:
                n = rows // dil
                for r in range(dil):
                    out[grp, r * n:(r + 1) * n, :] = tok[ti, pl.ds(r, n, stride=dil), :]


def _rope_tables(positions):
    t = positions.size
    half = ROPE_DIM // 2
    inv_freq = ROPE_THETA ** (-jnp.arange(0, ROPE_DIM, 2, dtype=F32) / ROPE_DIM)
    d = jnp.arange(LANES) % ATT_HEAD_DIM
    freq = jnp.where(d < ROPE_DIM, inv_freq[d % half], 0.0).astype(F32)[None, :]
    tab = jax.ShapeDtypeStruct((ATT_N_GROUPS, t, LANES), F32)
    ospec = pl.BlockSpec((ATT_N_GROUPS, ROW_TILE, LANES), lambda i: (0, i, 0))
    return pl.pallas_call(
        _rope_table_kernel,
        grid=(t // ROW_TILE,),
        in_specs=[pl.BlockSpec((ROW_TILE, 1), lambda i: (i, 0)),
                  pl.BlockSpec((1, LANES), lambda i: (0, 0))],
        out_specs=[ospec, ospec, ospec],
        out_shape=[tab, tab, tab],
        scratch_shapes=[pltpu.VMEM((3, ROW_TILE, LANES), F32)],
        compiler_params=_params("parallel"),
        name="rope_tables",
    )(positions.reshape(t, 1), freq)


def _qkv_kernel(x_ref, w_ref, c_ref, s1_ref, s2_ref, o_ref, cols):
    xf = _unpack_rows(x_ref[...])
    rows = xf.shape[0]
    n_col = D_MODEL // LANES
    for c in range(n_col):
        cols[c] = xf[:, c * LANES:(c + 1) * LANES]

    def residue_major(dil):
        n = rows // dil
        strips = [jnp.concatenate([cols[c, pl.ds(r, n, stride=dil), :] for r in range(dil)], axis=0)
                  for c in range(n_col)]
        return jnp.concatenate(strips, axis=1).astype(BF16)

    xs = [xf.astype(BF16) if dil == 1 else residue_major(dil) for _, dil in ATT_PATTERNS]
    reps = ATT_OUT_DIM // LANES
    half = ROPE_DIM // 2
    for grp in range(ATT_N_GROUPS):
        c = jnp.concatenate([c_ref[grp]] * reps, axis=1)
        s1 = jnp.concatenate([s1_ref[grp]] * reps, axis=1)
        s2 = jnp.concatenate([s2_ref[grp]] * reps, axis=1)
        for kind in range(3):
            j = grp * 3 + kind
            sl = slice(j * ATT_OUT_DIM, (j + 1) * ATT_OUT_DIM)
            acc = _dot(xs[grp], w_ref[:, sl])
            if kind < 2:
                up = pltpu.roll(acc, ATT_OUT_DIM - half, 1)
                down = pltpu.roll(acc, half, 1)
                acc = acc * c + up * s1 + down * s2
            if kind == 0:
                acc = acc * (ATT_HEAD_DIM ** -0.5)
            o_ref[:, sl] = acc.astype(o_ref.dtype)


def _qkv_proj(xpk, w, tabs):
    t = xpk.shape[0]
    tab_spec = pl.BlockSpec((ATT_N_GROUPS, ROW_TILE, LANES), lambda i: (0, i, 0))
    return pl.pallas_call(
        _qkv_kernel,
        grid=(t // ROW_TILE,),
        in_specs=[pl.BlockSpec((ROW_TILE, D_PACK), lambda i: (i, 0)),
                  _resident((D_MODEL, ATT_QKV_DIM)), tab_spec, tab_spec, tab_spec],
        out_specs=pl.BlockSpec((ROW_TILE, ATT_QKV_DIM), lambda i: (i, 0)),
        out_shape=jax.ShapeDtypeStruct((t, ATT_QKV_DIM), BF16),
        scratch_shapes=[pltpu.VMEM((D_MODEL // LANES, ROW_TILE, LANES), F32)],
        compiler_params=_params("parallel"),
        name="qkv_rope",
    )(xpk, w, *tabs)


def _attn_kernel(q_ref, kp_ref, kc_ref, vp_ref, vc_ref, o_ref, st_ref, *, chained, problems):
    w = ATT_BLOCK
    i = pl.program_id(2)
    qi = lax.broadcasted_iota(jnp.int32, (w, 2 * w), 0)
    kk = lax.broadcasted_iota(jnp.int32, (w, 2 * w), 1)
    band = (kk >= qi) & (kk <= qi + w)
    lane = lax.broadcasted_iota(jnp.int32, (w, LANES), 1)
    lo_mask = lane < HALF

    def part(ref, j):
        if len(ref.shape) == 2:
            return ref[j * w:(j + 1) * w, :]
        return ref[:, j].reshape(w, ref.shape[-1])

    for j in range(problems):
        if chained and j >= 1:
            k, v, has_prev = kc_ref[(j - 1) * w:(j + 1) * w, :], vc_ref[(j - 1) * w:(j + 1) * w, :], True
        else:
            kprev = kp_ref[...] if chained else part(kp_ref, j)
            vprev = vp_ref[...] if chained else part(vp_ref, j)
            k = jnp.concatenate([kprev, part(kc_ref, j)], axis=0)
            v = jnp.concatenate([vprev, part(vc_ref, j)], axis=0)
            has_prev = False
        valid = band if has_prev else band & (kk >= jnp.where(i > 0, 0, w))
        q = part(q_ref, j)
        stats = jnp.zeros((w, LANES), F32)
        zero = jnp.zeros((), q.dtype)
        parts = []
        for p in range(ATT_HEADS // 2):
            sl = slice(p * LANES, (p + 1) * LANES)
            qp, kp, vp = q[:, sl], k[:, sl], v[:, sl]
            outs = []
            for hh in range(2):
                h = 2 * p + hh
                qm = jnp.where(lo_mask if hh == 0 else ~lo_mask, qp, zero)
                s = jnp.where(valid, _dot_nt(qm, kp), NEG_INF)
                m = jnp.max(s, -1, keepdims=True)
                pr = jnp.exp(s - m)
                l = jnp.sum(pr, -1, keepdims=True)
                outs.append(_dot(pr.astype(v.dtype), vp) / l)
                stats = jnp.where(lane == h, m, stats)
                stats = jnp.where(lane == ATT_HEADS + h, l, stats)
            parts.append(jnp.where(lo_mask, outs[0], outs[1]).astype(o_ref.dtype))
        out = jnp.concatenate(parts, axis=1)
        if len(o_ref.shape) == 2:
            o_ref[j * w:(j + 1) * w, :] = out
            st_ref[j * w:(j + 1) * w, :] = stats
        else:
            o_ref[:, j] = out.reshape(o_ref.shape[0], o_ref.shape[2], o_ref.shape[3])
            st_ref[:, j] = stats.reshape(st_ref.shape[0], st_ref.shape[2], st_ref.shape[3])


def _window_attention(qkv, grp, bsz, seq):
    _, dil = ATT_PATTERNS[grp]
    w = ATT_BLOCK
    t = bsz * seq
    chunk = ROW_TILE // dil
    tiles = w // chunk if chunk < w else 1
    span = dil * w
    nb = seq // span
    col0 = grp * 3
    chained = dil == 1
    g = ATT_PROBLEMS if chained else min(ATT_PROBLEMS, dil)

    if chained:
        grid = (bsz, 1, nb // g)
        per_b = seq // (g * w)

        def spec(width, col, prev):
            if prev:
                return pl.BlockSpec((w, width), lambda b, r, i: (b * g * per_b + jnp.maximum(g * i - 1, 0), col))
            return pl.BlockSpec((g * w, width), lambda b, r, i: (b * per_b + i, col))

        qkv_v, o_shape, st_shape = qkv, (t, ATT_OUT_DIM), (t, LANES)
    elif tiles == 1:
        grid = (bsz, dil // g, nb)
        per_b = seq // (g * w)
        stride = span // (g * w)

        def spec(width, col, prev):
            def imap(b, r, i):
                blk = jnp.maximum(i - 1, 0) if prev else i
                return (b * per_b + blk * stride + r, col)
            return pl.BlockSpec((g * w, width), imap)

        qkv_v, o_shape, st_shape = qkv, (t, ATT_OUT_DIM), (t, LANES)
    else:
        grid = (bsz, dil // g, nb)

        def spec(width, col, prev):
            def imap(b, r, i):
                blk = jnp.maximum(i - 1, 0) if prev else i
                return (b, blk, 0, r, 0, col)
            return pl.BlockSpec((None, None, tiles, g, chunk, width), imap)

        lead = (bsz, nb, tiles, dil, chunk)
        qkv_v, o_shape, st_shape = qkv.reshape(*lead, ATT_QKV_DIM), (*lead, ATT_OUT_DIM), (*lead, LANES)

    o, st = pl.pallas_call(
        functools.partial(_attn_kernel, chained=chained, problems=g),
        grid=grid,
        in_specs=[spec(ATT_OUT_DIM, col0, False), spec(ATT_OUT_DIM, col0 + 1, True),
                  spec(ATT_OUT_DIM, col0 + 1, False), spec(ATT_OUT_DIM, col0 + 2, True),
                  spec(ATT_OUT_DIM, col0 + 2, False)],
        out_specs=[spec(ATT_OUT_DIM, 0, False), spec(LANES, 0, False)],
        out_shape=[jax.ShapeDtypeStruct(o_shape, BF16), jax.ShapeDtypeStruct(st_shape, F32)],
        compiler_params=_params("parallel", "parallel", "arbitrary"),
        name=f"window_attn_d{dil}",
    )(qkv_v, qkv_v, qkv_v, qkv_v, qkv_v)
    return o.reshape(t, ATT_OUT_DIM), st.reshape(t, LANES)


def _merge_proj_ln_kernel(o1_ref, o2_ref, o3_ref, s1_ref, s2_ref, s3_ref, w_ref, x_ref,
                          g_ref, b_ref, w2_ref, rb_ref, o_ref, opk_ref, route_ref, cnt_ref, carry, tok_o, tok_s):
    rows = o1_ref.shape[0]
    pairs = ATT_HEADS // 2
    lane = lax.broadcasted_iota(jnp.int32, (rows, LANES), 1)

    for gi, (o_ref_g, s_ref_g) in enumerate(((o2_ref, s2_ref), (o3_ref, s3_ref))):
        dil = ATT_PATTERNS[gi + 1][1]
        n = rows // dil
        for r in range(dil):
            tok_s[gi, pl.ds(r, n, stride=dil), :] = s_ref_g[r * n:(r + 1) * n, :]
            for p in range(pairs):
                tok_o[gi, p, pl.ds(r, n, stride=dil), :] = o_ref_g[r * n:(r + 1) * n, p * LANES:(p + 1) * LANES].astype(F32)

    sts = [s1_ref[...], tok_s[0], tok_s[1]]
    mx = jnp.maximum(jnp.maximum(sts[0], sts[1]), sts[2])
    wgts = [pltpu.roll(s, LANES - ATT_HEADS, 1) * jnp.exp(s - mx) for s in sts]
    den = wgts[0] + wgts[1] + wgts[2]
    den = jnp.where(lane < ATT_HEADS, den, 1.0)
    coefs = [wg / den for wg in wgts]
    parts = []
    for p in range(pairs):
        sl = slice(p * LANES, (p + 1) * LANES)
        head_of_lane = jnp.where(lane < HALF, 2 * p, 2 * p + 1)
        acc = jnp.take_along_axis(coefs[0], head_of_lane, axis=1) * o1_ref[:, sl].astype(F32)
        for gi in range(1, ATT_N_GROUPS):
            acc = acc + jnp.take_along_axis(coefs[gi], head_of_lane, axis=1) * tok_o[gi - 1, p]
        parts.append(acc.astype(BF16))
    mix = _dot(jnp.concatenate(parts, axis=1), w_ref[...])
    _ln_route_epilogue(DEEPNORM_ALPHA * x_ref[...] + mix, g_ref, b_ref, w2_ref, rb_ref,
                       o_ref, opk_ref, route_ref, cnt_ref, carry)


def _merge_proj_ln(os_, sts, w, x, g, b, router_w):
    t = x.shape[0]
    tm = ROW_TILE
    r_in, r_out, r_shape, r_scratch = _route_specs(t)
    ospec = pl.BlockSpec((tm, ATT_OUT_DIM), lambda i: (i, 0))
    sspec = pl.BlockSpec((tm, LANES), lambda i: (i, 0))
    xspec = pl.BlockSpec((tm, D_MODEL), lambda i: (i, 0))
    vspec = pl.BlockSpec((1, D_MODEL), lambda i: (0, 0))
    return pl.pallas_call(
        _merge_proj_ln_kernel,
        grid=(t // tm,),
        in_specs=[ospec] * 3 + [sspec] * 3 + [_resident((ATT_OUT_DIM, D_MODEL)), xspec, vspec, vspec] + r_in,
        out_specs=[xspec, pl.BlockSpec((tm, D_PACK), lambda i: (i, 0))] + r_out,
        out_shape=[jax.ShapeDtypeStruct((t, D_MODEL), F32), jax.ShapeDtypeStruct((t, D_PACK), U32)] + r_shape,
        scratch_shapes=r_scratch + [pltpu.VMEM((ATT_N_GROUPS - 1, ATT_HEADS // 2, tm, LANES), F32),
                                    pltpu.VMEM((ATT_N_GROUPS - 1, tm, LANES), F32)],
        compiler_params=_params("arbitrary"),
        name="attn_merge_proj_ln",
    )(*os_, *sts, w, x, g[None, :], b[None, :], *router_w)


def _route_rows(x, w2_ref, b_ref, carry):
    xhi = x.astype(BF16)
    xlo = (x - xhi.astype(F32)).astype(BF16)
    w2 = w2_ref[...]
    hi2 = _dot(xhi, w2)
    logits = hi2[:, :LANES] + hi2[:, LANES:] + _dot(xlo, w2[:, :LANES]) + b_ref[...]
    rows = logits.shape[0]
    lane = lax.broadcasted_iota(jnp.int32, (rows, LANES), 1)
    big = jnp.int32(LANES)

    def top1(vals, mask):
        v = jnp.where(mask, vals, -jnp.inf)
        m = jnp.max(v, -1, keepdims=True)
        idx = jnp.min(jnp.where(v == m, lane, big), -1, keepdims=True)
        return v, m, idx

    gmask = lane < MOE_N_GROUPS
    gv, gm, gidx = top1(logits, gmask)
    g_w = 1.0 / jnp.sum(jnp.exp(gv - gm), -1, keepdims=True)
    e_lo = MOE_N_GROUPS + gidx * MOE_EPG
    emask = (lane >= e_lo) & (lane < e_lo + MOE_EPG)
    ev, m1, i1 = top1(logits, emask)
    zsum = jnp.sum(jnp.exp(ev - m1), -1, keepdims=True)
    _, m2, i2 = top1(logits, emask & (lane != i1))
    p1 = 1.0 / zsum
    p2 = jnp.exp(m2 - m1) / zsum
    tot = p1 + p2
    e1 = i1 - MOE_N_GROUPS
    e2 = i2 - MOE_N_GROUPS

    oh1 = jnp.where(lane == e1, 1.0, 0.0)
    oh2 = jnp.where(lane == e2, 1.0, 0.0)
    oh = oh1 + oh2
    ri = lax.broadcasted_iota(jnp.int32, (rows, rows), 0)
    ci = lax.broadcasted_iota(jnp.int32, (rows, rows), 1)
    strict = jnp.where(ri > ci, 1.0, 0.0).astype(BF16)
    before = _dot(strict, oh.astype(BF16)) + carry[...]
    rank1 = jnp.sum(oh1 * before, -1, keepdims=True)
    rank2 = jnp.sum(oh2 * before, -1, keepdims=True)
    carry[...] = carry[...] + jnp.sum(oh, 0, keepdims=True)

    vals = [e1.astype(F32), e2.astype(F32), g_w * (p1 / tot), g_w * (p2 / tot), rank1, rank2]
    out = jnp.zeros((rows, LANES), F32)
    for j, val in enumerate(vals):
        out = jnp.where(lane == j, val, out)
    return out


def _router_weights(w_rg, b_rg, w_re, b_re):
    n_log = MOE_N_GROUPS + MOE_N_EXPERTS
    w = jnp.pad(jnp.concatenate([w_rg, w_re], axis=1), ((0, 0), (0, LANES - n_log)))
    whi = w.astype(BF16)
    w2 = jnp.concatenate([whi, (w - whi.astype(F32)).astype(BF16)], axis=1)
    bias = jnp.pad(jnp.concatenate([b_rg, b_re]), (0, LANES - n_log))[None, :]
    return w2, bias


def _ln_route_epilogue(r, g_ref, b_ref, w2_ref, rb_ref, o_ref, opk_ref, route_ref, cnt_ref, carry):
    @pl.when(pl.program_id(0) == 0)
    def _():
        carry[...] = jnp.zeros(carry.shape, F32)

    out = _layer_norm(r, g_ref[...], b_ref[...])
    o_ref[...] = out
    opk_ref[...] = _pack_rows(out)
    route_ref[...] = _route_rows(out, w2_ref, rb_ref, carry)
    cnt_ref[...] = carry[...]


def _expert_kernel(meta_ref, x_ref, wg_ref, wu_ref, wd_ref, o_ref, wg16, wu16, wd16):
    i = pl.program_id(0)
    n_blocks = pl.num_programs(0)

    @pl.when(i < meta_ref[n_blocks])
    def _():
        @pl.when((i == 0) | (meta_ref[i] != meta_ref[jnp.maximum(i - 1, 0)]))
        def _():
            wg16[...] = wg_ref[...].astype(BF16)
            wu16[...] = wu_ref[...].astype(BF16)
            wd16[...] = wd_ref[...].astype(BF16)

        rows = lax.broadcasted_iota(jnp.int32, x_ref.shape, 0)
        xw = jnp.where(rows < meta_ref[n_blocks + 1 + i], x_ref[...], jnp.uint32(0))
        x = _unpack_rows(xw).astype(BF16)
        h = _silu(_dot(x, wg16[...])) * _dot(x, wu16[...])
        o_ref[...] = _pack_rows(_dot(h.astype(BF16), wd16[...]))


def _expert_mlp(meta, x_rows, w_gate, w_up, w_down, layer):
    n_rows = x_rows.shape[0]
    n_blocks = n_rows // MOE_ROW_BLOCK
    tb = MOE_ROW_BLOCK
    grid_spec = pltpu.PrefetchScalarGridSpec(
        num_scalar_prefetch=1,
        grid=(n_blocks,),
        in_specs=[pl.BlockSpec((tb, D_PACK), lambda i, meta: (i, 0)),
                  pl.BlockSpec((None, None, D_MODEL, MOE_HIDDEN), lambda i, meta: (layer, meta[i], 0, 0)),
                  pl.BlockSpec((None, None, D_MODEL, MOE_HIDDEN), lambda i, meta: (layer, meta[i], 0, 0)),
                  pl.BlockSpec((None, None, MOE_HIDDEN, D_MODEL), lambda i, meta: (layer, meta[i], 0, 0))],
        out_specs=pl.BlockSpec((tb, D_PACK), lambda i, meta: (i, 0)),
        scratch_shapes=[pltpu.VMEM((D_MODEL, MOE_HIDDEN), BF16), pltpu.VMEM((D_MODEL, MOE_HIDDEN), BF16),
                        pltpu.VMEM((MOE_HIDDEN, D_MODEL), BF16)],
    )
    return pl.pallas_call(
        _expert_kernel,
        grid_spec=grid_spec,
        out_shape=jax.ShapeDtypeStruct((n_rows, D_PACK), U32),
        compiler_params=_params("arbitrary"),
        name="moe_experts",
    )(meta, x_rows, w_gate, w_up, w_down)


def _combine_ln_kernel(y0_ref, y1_ref, r_ref, x_ref, g_ref, b_ref, o_ref, opk_ref):
    route = r_ref[...]
    g0 = route[:, 2:3]
    g1 = route[:, 3:4]
    ffn = g0 * _unpack_rows(y0_ref[...]) + g1 * _unpack_rows(y1_ref[...])
    out = _layer_norm(DEEPNORM_ALPHA * x_ref[...] + ffn, g_ref[...], b_ref[...])
    o_ref[...] = out
    opk_ref[...] = _pack_rows(out)


def _combine_ln(y0, y1, route, x, g, b):
    t = x.shape[0]
    rows = 2 * ROW_TILE
    xspec = pl.BlockSpec((rows, D_MODEL), lambda i: (i, 0))
    pspec = pl.BlockSpec((rows, D_PACK), lambda i: (i, 0))
    vspec = pl.BlockSpec((1, D_MODEL), lambda i: (0, 0))
    return pl.pallas_call(
        _combine_ln_kernel,
        grid=(t // rows,),
        in_specs=[pspec, pspec, pl.BlockSpec((rows, LANES), lambda i: (i, 0)), xspec, vspec, vspec],
        out_specs=[xspec, pspec],
        out_shape=[jax.ShapeDtypeStruct((t, D_MODEL), F32), jax.ShapeDtypeStruct((t, D_PACK), U32)],
        compiler_params=_params("parallel"),
        name="moe_combine_ln",
    )(y0, y1, route, x, g[None, :], b[None, :])


def _positions_kernel(r_ref, ps_ref, p0_ref, p1_ref):
    route = r_ref[...]
    lane = lax.broadcasted_iota(jnp.int32, route.shape, 1)
    starts = ps_ref[...]
    out = jnp.zeros(route.shape, F32)
    for k in range(MOE_TOP_K):
        eid = route[:, k:k + 1].astype(jnp.int32)
        pos = jnp.sum(jnp.where(lane == eid, starts, 0.0), -1, keepdims=True) + route[:, 4 + k:5 + k]
        out = jnp.where(lane == k, pos, out)
    for c in range(route.shape[0] // LANES):
        tile = out[c * LANES:(c + 1) * LANES, :].T.astype(jnp.int32)
        p0_ref[:, c * LANES:(c + 1) * LANES] = tile[0:1, :]
        p1_ref[:, c * LANES:(c + 1) * LANES] = tile[1:2, :]


def _positions(route, pad_start):
    t = route.shape[0]
    rows = 4 * ROW_TILE
    starts = jnp.pad(pad_start.astype(F32), (0, LANES - MOE_N_EXPERTS))[None, :]
    ospec = pl.BlockSpec((1, rows), lambda i: (0, i))
    return pl.pallas_call(
        _positions_kernel,
        grid=(t // rows,),
        in_specs=[pl.BlockSpec((rows, LANES), lambda i: (i, 0)), pl.BlockSpec((1, LANES), lambda i: (0, 0))],
        out_specs=[ospec, ospec],
        out_shape=[jax.ShapeDtypeStruct((1, t), jnp.int32)] * 2,
        compiler_params=_params("parallel"),
        name="moe_positions",
    )(route, starts)


def _moe(x, xpk, route, cnt, w_gate, w_up, w_down, layer, g, b):
    t = x.shape[0]
    tb = MOE_ROW_BLOCK
    n_assign = t * MOE_TOP_K
    n_blocks = n_assign // tb + MOE_N_EXPERTS
    n_rows = n_blocks * tb
    counts = cnt[0, :MOE_N_EXPERTS].astype(jnp.int32)
    padded = (counts + tb - 1) // tb * tb
    pad_end = jnp.cumsum(padded)
    pos0, pos1 = _positions(route, pad_end - padded)
    block_start = jnp.arange(n_blocks, dtype=jnp.int32) * tb
    block_e = jnp.minimum(jnp.sum((pad_end[None, :] <= block_start[:, None]).astype(jnp.int32), -1),
                          MOE_N_EXPERTS - 1)
    pad_start = pad_end - padded
    inside = (pad_start[None, :] <= block_start[:, None]) & (block_start[:, None] < pad_end[None, :])
    real_end = jnp.sum(jnp.where(inside, (pad_start + counts)[None, :], 0), -1)
    valid = jnp.clip(real_end - block_start, 0, tb)
    meta = jnp.concatenate([block_e, pad_end[-1:] // tb, valid]).astype(jnp.int32)
    x_rows = _row_scatter(xpk, (pos0, pos1), n_rows)
    y_rows = _expert_mlp(meta, x_rows, w_gate, w_up, w_down, layer)
    y0 = _row_gather(y_rows, pos0)
    y1 = _row_gather(y_rows, pos1)
    return _combine_ln(y0, y1, route, x, g, b)


def _ssd_layer(x, xpk, w_in, conv_w, conv_b, dt_bias, a_log, d_skip, norm_w, w_out, g, b, router_w, bsz, seq):
    w_zxbc = w_in[:, :SSD_D_INNER + SSD_CONV_DIM].astype(BF16)
    w_dt = jnp.pad(w_in[:, SSD_D_INNER + SSD_CONV_DIM:], ((0, 0), (0, LANES - SSD_N_HEADS))).astype(BF16)
    zxbc, dt_raw = _in_proj(xpk, w_zxbc, w_dt, conv_w, conv_b, bsz, seq)
    y = _ssd_scan(zxbc, dt_raw, conv_w, conv_b, dt_bias, a_log, d_skip, norm_w, bsz, seq)
    return _proj_ln(y, w_out.astype(BF16), x, g, b, router_w, "ssd_out_proj_ln")


def _attn_layer(x, xpk, tabs, w_qkv, w_o, g, b, router_w, bsz, seq):
    qkv = _qkv_proj(xpk, w_qkv.astype(BF16), tabs)
    os_, sts = [], []
    for grp in range(ATT_N_GROUPS):
        o, st = _window_attention(qkv, grp, bsz, seq)
        os_.append(o)
        sts.append(st)
    return _merge_proj_ln(os_, sts, w_o.astype(BF16), x, g, b, router_w)


def kernel(x, positions, ssd_w_in, ssd_conv_w, ssd_conv_b, ssd_dt_bias, ssd_a_log, ssd_d, ssd_norm_w, ssd_w_out,
           attn_w_qkv, attn_w_o, ln_g, ln_b, moe_w_router_group, moe_b_router_group, moe_w_router_expert,
           moe_b_router_expert, moe_w_gate, moe_w_up, moe_w_down):
    bsz, seq, d = x.shape
    t = bsz * seq
    h = x.reshape(t, d)
    hpk = h
    tabs = _rope_tables(positions)
    for i in range(DEPTH):
        j = i // N_MIXERS
        router_w = _router_weights(moe_w_router_group[i], moe_b_router_group[i], moe_w_router_expert[i],
                                   moe_b_router_expert[i])
        if i % N_MIXERS == 0:
            h, hpk, route, cnt = _ssd_layer(h, hpk, ssd_w_in[j], ssd_conv_w[j], ssd_conv_b[j], ssd_dt_bias[j],
                                            ssd_a_log[j], ssd_d[j], ssd_norm_w[j], ssd_w_out[j], ln_g[i, 0],
                                            ln_b[i, 0], router_w, bsz, seq)
        else:
            h, hpk, route, cnt = _attn_layer(h, hpk, tabs, attn_w_qkv[j], attn_w_o[j], ln_g[i, 0], ln_b[i, 0],
                                             router_w, bsz, seq)
        h, hpk = _moe(h, hpk, route, cnt, moe_w_gate, moe_w_up, moe_w_down, i, ln_g[i, 1], ln_b[i, 1])
    return h.reshape(bsz, seq, d)
```
